```python
import math
import jax
import jax.numpy as jnp
from jax import lax
import numpy as np

D_MODEL = 1024
BATCH = 16
SEQ = 4096
DEPTH = 2
DEC_BATCH = 32
DEC_SEQ = 64
PAST_LEN = 4096

CHUNK = 64
HEAD_DIM = 64
N_HEADS = D_MODEL // HEAD_DIM
H_A = N_HEADS // 4
DQK_A = HEAD_DIM // 2
DV_A = HEAD_DIM
H_B = N_HEADS // 4
D_B = HEAD_DIM
B_LEFT_CHUNKS = 8
B_REL_CLIP = 128
H_C = N_HEADS // 2
D_C = HEAD_DIM
CONV_W = 4
T5_BUCKETS = 32
T5_MAX_DIST = 128
Q_BLOCK = 128
N_EXPERTS = 8
TOP_K = 2
D_FF = 2816
NORM_EPS = 1e-6
L2_EPS = 1e-6
NEG_INF = -1e30

A_QK = H_A * 2 * DQK_A
A_V = H_A * DV_A
B_W = H_B * D_B
C_W = H_C * D_C
MIX_WIDTH = A_V + B_W + C_W
IN_SIZES = (A_QK, A_QK, A_V, B_W, B_W, B_W, 3 * C_W, H_C, H_C, C_W)
IN_WIDTH = 2 * A_QK + A_V + 3 * B_W + 4 * C_W + 2 * H_C

kernel_name = 'hybrid_stream_diffattn_band_gdn_moe_step'


def rmsnorm(x, g):
    xf = x.astype(jnp.float32)
    y = xf * lax.rsqrt(jnp.mean(xf * xf, axis=-1, keepdims=True) + NORM_EPS)
    return (y * g.astype(jnp.float32)).astype(x.dtype)


def l2norm(x):
    xf = x.astype(jnp.float32)
    return (xf * lax.rsqrt(jnp.sum(xf * xf, axis=-1, keepdims=True) + L2_EPS)).astype(x.dtype)


def chunk_mask(q_pos, k_pos, left_chunks=None):
    qc = q_pos[:, None] // CHUNK
    kc = k_pos[None, :] // CHUNK
    m = (kc <= qc) & (k_pos[None, :] >= 0)
    if left_chunks is not None:
        m = m & (kc >= qc - left_chunks)
    return m


def t5_bucket(rel):
    nb = T5_BUCKETS // 2
    max_exact = nb // 2
    ret = jnp.where(rel > 0, nb, 0)
    n = jnp.abs(rel)
    nf = jnp.maximum(n, 1).astype(jnp.float32)
    large = max_exact + (jnp.log(nf / max_exact) / math.log(T5_MAX_DIST / max_exact) * (nb - max_exact)).astype(jnp.int32)
    large = jnp.minimum(large, nb - 1)
    return ret + jnp.where(n < max_exact, n, large)


def diff_attention(q, k, v, q_pos, k_pos, t5_bias, lam, lam_init, subln):
    qs = q.reshape(q.shape[:-1] + (2, DQK_A))
    ks = k.reshape(k.shape[:-1] + (2, DQK_A))
    bias = jnp.transpose(t5_bias[t5_bucket(k_pos[None, :] - q_pos[:, None])], (2, 0, 1))
    s = jnp.einsum('bqhmd,bkhmd->bmhqk', qs, ks).astype(jnp.float32) * (DQK_A ** -0.5) + bias.astype(jnp.float32)
    s = jnp.where(chunk_mask(q_pos, k_pos), s, NEG_INF)
    p = jax.nn.softmax(s, axis=-1)
    p = p[:, 0] - lam * p[:, 1]
    o = jnp.einsum('bhqk,bkhd->bqhd', p.astype(v.dtype), v)
    return rmsnorm(o, subln) * (1.0 - lam_init)


def diff_attention_prompt(q, k, v, t5_bias, lam, lam_init, subln):
    bsz, s_len = q.shape[:2]
    nb = s_len // Q_BLOCK
    qb = jnp.moveaxis(q.reshape(bsz, nb, Q_BLOCK, H_A, 2 * DQK_A), 1, 0)
    posb = jnp.arange(s_len, dtype=jnp.int32).reshape(nb, Q_BLOCK)
    k_pos = jnp.arange(s_len, dtype=jnp.int32)
    ob = lax.map(lambda xs: diff_attention(xs[0], k, v, xs[1], k_pos, t5_bias, lam, lam_init, subln), (qb, posb))
    return jnp.moveaxis(ob, 0, 1).reshape(bsz, s_len, H_A, DV_A)


def band_attention(q, k, v, q_pos, k_pos, rel_bias):
    rel = jnp.clip(k_pos[None, :] - q_pos[:, None], -B_REL_CLIP, B_REL_CLIP) + B_REL_CLIP
    bias = jnp.transpose(rel_bias[rel], (2, 0, 1))
    s = jnp.einsum('bqhd,bkhd->bhqk', q, k).astype(jnp.float32) * (D_B ** -0.5) + bias.astype(jnp.float32)
    s = jnp.where(chunk_mask(q_pos, k_pos, B_LEFT_CHUNKS), s, NEG_INF)
    p = jax.nn.softmax(s, axis=-1)
    return jnp.einsum('bhqk,bkhd->bqhd', p.astype(v.dtype), v)


def band_attention_prompt(q, k, v, rel_bias):
    bsz, s_len = q.shape[:2]
    nc = s_len // CHUNK
    pad = B_LEFT_CHUNKS * CHUNK
    band = pad + CHUNK
    kp = jnp.pad(k, ((0, 0), (pad, 0), (0, 0), (0, 0)))
    vp = jnp.pad(v, ((0, 0), (pad, 0), (0, 0), (0, 0)))
    qc = jnp.moveaxis(q.reshape(bsz, nc, CHUNK, H_B, D_B), 1, 0)

    def one_chunk(xs):
        qi, c = xs
        start = c * CHUNK
        kb = lax.dynamic_slice_in_dim(kp, start, band, axis=1)
        vb = lax.dynamic_slice_in_dim(vp, start, band, axis=1)
        q_pos = start + jnp.arange(CHUNK, dtype=jnp.int32)
        k_pos = start - pad + jnp.arange(band, dtype=jnp.int32)
        return band_attention(qi, kb, vb, q_pos, k_pos, rel_bias)

    ob = lax.map(one_chunk, (qc, jnp.arange(nc, dtype=jnp.int32)))
    return jnp.moveaxis(ob, 0, 1).reshape(bsz, s_len, H_B, D_B)


def gdn_chunk(s0, xs):
    q, k, v, log_a, beta = xs
    c = q.shape[2]
    g = jnp.cumsum(log_a, axis=-1)
    idx = jnp.arange(c)
    incl = idx[:, None] >= idx[None, :]
    strict = idx[:, None] > idx[None, :]
    dec = jnp.exp(jnp.where(incl, g[..., :, None] - g[..., None, :], -jnp.inf))
    a_mat = beta[..., :, None] * jnp.where(strict, dec, 0.0) * jnp.einsum('bhid,bhjd->bhij', k, k)
    eg = jnp.exp(g)[..., None]
    rhs = beta[..., None] * (v - eg * jnp.einsum('bhid,bhde->bhie', k, s0))
    u = lax.linalg.triangular_solve(jnp.eye(c, dtype=jnp.float32) + a_mat, rhs,
                                    left_side=True, lower=True, unit_diagonal=True)
    o = eg * jnp.einsum('bhid,bhde->bhie', q, s0) + jnp.einsum('bhij,bhje->bhie', dec * jnp.einsum('bhid,bhjd->bhij', q, k), u)
    g_last = g[..., -1:]
    s_new = jnp.exp(g_last)[..., None] * s0 + jnp.einsum('bhid,bhie->bhde', k * jnp.exp(g_last - g)[..., None], u)
    return s_new, o


def gated_delta(q, k, v, log_a, beta, s0):
    bsz, t = q.shape[:2]
    c = min(CHUNK, t)
    nc = t // c

    def blocks(x):
        x = x.astype(jnp.float32).reshape((bsz, nc, c) + x.shape[2:])
        return jnp.transpose(x, (1, 0, 3, 2) + tuple(range(4, x.ndim)))

    s_fin, o = lax.scan(gdn_chunk, s0.astype(jnp.float32),
                        (blocks(q), blocks(k), blocks(v), blocks(log_a), blocks(beta)))
    o = jnp.transpose(o, (1, 0, 3, 2, 4)).reshape(bsz, t, H_C, D_C)
    return o, s_fin.astype(s0.dtype)


def gdn_mixer(c_qkv, c_beta, c_decay, c_gate, conv_prev, s0, conv_w, a_log, dt_bias, onorm):
    bsz, t = c_qkv.shape[:2]
    x_ext = jnp.concatenate([conv_prev, c_qkv], axis=1)
    conv_new = x_ext[:, -(CONV_W - 1):]
    y = lax.conv_general_dilated(x_ext, conv_w[:, None, :], (1,), 'VALID',
                                 dimension_numbers=('NWC', 'WIO', 'NWC'),
                                 feature_group_count=3 * C_W)
    y = jax.nn.silu(y)
    q, k, v = jnp.split(y, 3, axis=-1)
    q = l2norm(q.reshape(bsz, t, H_C, D_C)) * (D_C ** -0.5)
    k = l2norm(k.reshape(bsz, t, H_C, D_C))
    v = v.reshape(bsz, t, H_C, D_C)
    beta = jax.nn.sigmoid(c_beta.astype(jnp.float32))
    log_a = -jnp.exp(a_log.astype(jnp.float32)) * jax.nn.softplus(c_decay.astype(jnp.float32) + dt_bias.astype(jnp.float32))
    o, s_new = gated_delta(q, k, v, log_a, beta, s0)
    o = rmsnorm(o.astype(c_qkv.dtype), onorm) * jax.nn.silu(c_gate.reshape(bsz, t, H_C, D_C))
    return o.reshape(bsz, t, C_W), conv_new, s_new


def split_projection(h, w_in):
    proj = jnp.einsum('btd,de->bte', h, w_in)
    offsets = []
    acc = 0
    for size in IN_SIZES[:-1]:
        acc += size
        offsets.append(acc)
    return jnp.split(proj, offsets, axis=-1)


def mixer_layer(h, l, past, w_in_l, w_out_l, lam_qk_l, subln_l, t5_bias, rel_bias_l, conv_l, a_log_l, dt_bias_l, onorm_l):
    bsz, t = h.shape[:2]
    aq, ak, av, bq, bk, bv, c_qkv, c_beta, c_decay, c_gate = split_projection(h, w_in_l)
    aq = aq.reshape(bsz, t, H_A, 2 * DQK_A)
    ak = ak.reshape(bsz, t, H_A, 2 * DQK_A)
    av = av.reshape(bsz, t, H_A, DV_A)
    bq = bq.reshape(bsz, t, H_B, D_B)
    bk = bk.reshape(bsz, t, H_B, D_B)
    bv = bv.reshape(bsz, t, H_B, D_B)
    lam_init = 0.8 - 0.6 * math.exp(-0.3 * l)
    lq = lam_qk_l.astype(jnp.float32)
    lam = jnp.exp(jnp.sum(lq[0] * lq[1])) - jnp.exp(jnp.sum(lq[2] * lq[3])) + lam_init
    if past is None:
        oa = diff_attention_prompt(aq, ak, av, t5_bias, lam, lam_init, subln_l)
        ob = band_attention_prompt(bq, bk, bv, rel_bias_l)
        conv_prev = jnp.zeros((bsz, CONV_W - 1, 3 * C_W), h.dtype)
        s0 = jnp.zeros((bsz, H_C, D_C, D_C), h.dtype)
        keep = min(B_LEFT_CHUNKS * CHUNK, t)
        new_bk = bk[:, t - keep:]
        new_bv = bv[:, t - keep:]
    else:
        ck_a, cv_a, ck_b, cv_b, conv_prev, s0 = past
        past_len = ck_a.shape[1]
        q_pos = past_len + jnp.arange(t, dtype=jnp.int32)
        ka = jnp.concatenate([ck_a, ak], axis=1)
        va = jnp.concatenate([cv_a, av], axis=1)
        oa = diff_attention(aq, ka, va, q_pos, jnp.arange(past_len + t, dtype=jnp.int32),
                            t5_bias, lam, lam_init, subln_l)
        nb = ck_b.shape[1]
        kb = jnp.concatenate([ck_b, bk], axis=1)
        vb = jnp.concatenate([cv_b, bv], axis=1)
        k_pos_b = past_len - nb + jnp.arange(nb + t, dtype=jnp.int32)
        ob = band_attention(bq, kb, vb, q_pos, k_pos_b, rel_bias_l)
        new_bk = bk
        new_bv = bv
    oc, conv_new, s_new = gdn_mixer(c_qkv, c_beta, c_decay, c_gate, conv_prev, s0, conv_l, a_log_l, dt_bias_l, onorm_l)
    mixed = jnp.concatenate([oa.reshape(bsz, t, A_V), ob.reshape(bsz, t, B_W), oc], axis=-1)
    out = jnp.einsum('btm,md->btd', mixed, w_out_l)
    return out, (ak, av, new_bk, new_bv, conv_new, s_new)


def swiglu(h, wg, wu, wd):
    a = jax.nn.silu(jnp.einsum('btd,df->btf', h, wg)) * jnp.einsum('btd,df->btf', h, wu)
    return jnp.einsum('btf,fd->btd', a, wd)


def moe_swiglu(h, router, wg, wu, wd):
    logits = jnp.einsum('btd,de->bte', h, router).astype(jnp.float32)
    top_v, top_i = lax.top_k(logits, TOP_K)
    w = jax.nn.softmax(top_v, axis=-1)
    gates = jnp.sum(jax.nn.one_hot(top_i, N_EXPERTS, dtype=jnp.float32) * w[..., None], axis=-2)
    y = jnp.zeros_like(h)
    for e in range(N_EXPERTS):
        y = y + gates[..., e:e + 1].astype(h.dtype) * swiglu(h, wg[e], wu[e], wd[e])
    return y


def channel_mixer(h, l, ffn_gate, ffn_up, ffn_down, moe_router, moe_gate, moe_up, moe_down):
    i = l // 2
    if l % 2 == 0:
        return swiglu(h, ffn_gate[i], ffn_up[i], ffn_down[i])
    return moe_swiglu(h, moe_router[i], moe_gate[i], moe_up[i], moe_down[i])


def setup_inputs(seed: int = 0) -> dict:
    key = jax.random.key(seed)
    ks = jax.random.split(key, 32)

    def nrm(k, shape, scale):
        return jax.random.normal(k, shape, jnp.float32) * scale

    b_cache = min(B_LEFT_CHUNKS * CHUNK, PAST_LEN)
    n_dense = (DEPTH + 1) // 2
    n_moe = DEPTH // 2
    dt = jax.random.uniform(ks[20], (DEPTH, H_C), jnp.float32, minval=0.001, maxval=0.1)
    return {
        'x_prompt': nrm(ks[0], (BATCH, SEQ, D_MODEL), 1.0),
        'x_sample': nrm(ks[1], (DEC_BATCH, DEC_SEQ, D_MODEL), 1.0),
        'cache_a_k': nrm(ks[2], (DEPTH, DEC_BATCH, PAST_LEN, H_A, 2 * DQK_A), 1.0),
        'cache_a_v': nrm(ks[3], (DEPTH, DEC_BATCH, PAST_LEN, H_A, DV_A), 1.0),
        'cache_b_k': nrm(ks[4], (DEPTH, DEC_BATCH, b_cache, H_B, D_B), 1.0),
        'cache_b_v': nrm(ks[5], (DEPTH, DEC_BATCH, b_cache, H_B, D_B), 1.0),
        'cache_c_conv': nrm(ks[6], (DEPTH, DEC_BATCH, CONV_W - 1, 3 * C_W), 1.0),
        'state_c_ssm': nrm(ks[7], (DEPTH, DEC_BATCH, H_C, D_C, D_C), 0.1),
        'w_in': nrm(ks[8], (DEPTH, D_MODEL, IN_WIDTH), D_MODEL ** -0.5),
        'w_out': nrm(ks[9], (DEPTH, MIX_WIDTH, D_MODEL), MIX_WIDTH ** -0.5),
        'norm_mix': 1.0 + nrm(ks[10], (DEPTH, D_MODEL), 0.05),
        'norm_ffn': 1.0 + nrm(ks[11], (DEPTH, D_MODEL), 0.05),
        'norm_final': 1.0 + nrm(ks[12], (D_MODEL,), 0.05),
        'lam_qk': nrm(ks[13], (DEPTH, 4, DQK_A), 0.1),
        'subln_a': 1.0 + nrm(ks[14], (DEPTH, DV_A), 0.05),
        't5_bias': nrm(ks[15], (T5_BUCKETS, H_A), 0.5),
        'rel_bias_b': nrm(ks[16], (DEPTH, 2 * B_REL_CLIP + 1, H_B), 0.5),
        'conv_c': nrm(ks[17], (DEPTH, CONV_W, 3 * C_W), CONV_W ** -0.5),
        'a_log_c': jnp.log(jax.random.uniform(ks[18], (DEPTH, H_C), jnp.float32, minval=1.0, maxval=16.0)),
        'dt_bias_c': jnp.log(jnp.expm1(dt)),
        'onorm_c': 1.0 + nrm(ks[19], (DEPTH, D_C), 0.05),
        'ffn_gate': nrm(ks[21], (n_dense, D_MODEL, D_FF), D_MODEL ** -0.5),
        'ffn_up': nrm(ks[22], (n_dense, D_MODEL, D_FF), D_MODEL ** -0.5),
        'ffn_down': nrm(ks[23], (n_dense, D_FF, D_MODEL), D_FF ** -0.5),
        'moe_router': nrm(ks[24], (n_moe, D_MODEL, N_EXPERTS), D_MODEL ** -0.5),
        'moe_gate': nrm(ks[25], (n_moe, N_EXPERTS, D_MODEL, D_FF), D_MODEL ** -0.5),
        'moe_up': nrm(ks[26], (n_moe, N_EXPERTS, D_MODEL, D_FF), D_MODEL ** -0.5),
        'moe_down': nrm(ks[27], (n_moe, N_EXPERTS, D_FF, D_MODEL), D_FF ** -0.5),
    }


def reference(x_prompt, x_sample, cache_a_k, cache_a_v, cache_b_k, cache_b_v, cache_c_conv, state_c_ssm,
              w_in, w_out, norm_mix, norm_ffn, norm_final, lam_qk, subln_a, t5_bias, rel_bias_b,
              conv_c, a_log_c, dt_bias_c, onorm_c, ffn_gate, ffn_up, ffn_down,
              moe_router, moe_gate, moe_up, moe_down):
    xp = x_prompt
    xs = x_sample
    p_states = [[] for _ in range(6)]
    s_states = [[] for _ in range(6)]
    for l in range(DEPTH):
        layer_w = (w_in[l], w_out[l], lam_qk[l], subln_a[l], t5_bias, rel_bias_b[l],
                   conv_c[l], a_log_c[l], dt_bias_c[l], onorm_c[l])
        mp, stp = mixer_layer(rmsnorm(xp, norm_mix[l]), l, None, *layer_w)
        past = (cache_a_k[l], cache_a_v[l], cache_b_k[l], cache_b_v[l], cache_c_conv[l], state_c_ssm[l])
        ms, sts = mixer_layer(rmsnorm(xs, norm_mix[l]), l, past, *layer_w)
        xp = xp + mp
        xs = xs + ms
        xp = xp + channel_mixer(rmsnorm(xp, norm_ffn[l]), l, ffn_gate, ffn_up, ffn_down, moe_router, moe_gate, moe_up, moe_down)
        xs = xs + channel_mixer(rmsnorm(xs, norm_ffn[l]), l, ffn_gate, ffn_up, ffn_down, moe_router, moe_gate, moe_up, moe_down)
        for i in range(6):
            p_states[i].append(stp[i])
            s_states[i].append(sts[i])
    y_prompt = rmsnorm(xp, norm_final)
    y_sample = rmsnorm(xs, norm_final)
    p_a_k, p_a_v, p_b_k, p_b_v, p_c_conv, p_c_ssm = [jnp.stack(s, axis=0) for s in p_states]
    s_a_k, s_a_v, s_b_k, s_b_v, s_c_conv, s_c_ssm = [jnp.stack(s, axis=0) for s in s_states]
    return (y_prompt, y_sample, p_a_k, p_a_v, p_b_k, p_b_v, p_c_conv, p_c_ssm,
            s_a_k, s_a_v, s_b_k, s_b_v, s_c_conv, s_c_ssm)
```

```python
import functools
import math

import jax
import jax.numpy as jnp
from jax import lax
from jax.experimental import pallas as pl
from jax.experimental.pallas import tpu as pltpu

F32 = jnp.float32
BF16 = jnp.bfloat16

CHUNK = 64
H_A = 4
DQK_A = 32
DV_A = 64
H_B = 4
D_B = 64
B_LEFT_CHUNKS = 8
B_REL_CLIP = 128
H_C = 8
D_C = 64
CONV_W = 4
T5_BUCKETS = 32
T5_MAX_DIST = 128
N_EXPERTS = 8
NORM_EPS = 1e-6
L2_EPS = 1e-6
NEG_INF = -1e30
HEAD_W = 256
C_W = H_C * D_C
BAND = (B_LEFT_CHUNKS + 1) * CHUNK
LANES = 128
VMEM_LIMIT = 56 * 1024 * 1024

A_BLOCK = 256
ROW_TILE = 512
MOE_ROW_TILE = 256
MOE_TOKEN_BLOCK = 256


def _params(sem):
    return pltpu.CompilerParams(dimension_semantics=sem, vmem_limit_bytes=VMEM_LIMIT)


def _rms(x, g):
    ms = jnp.mean(x * x, axis=-1, keepdims=True)
    return x * lax.rsqrt(ms + NORM_EPS) * g


def _sigmoid(x):
    return 1.0 / (1.0 + jnp.exp(-x))


def _silu(x):
    return x * _sigmoid(x)


def _softplus(x):
    return jnp.maximum(x, 0.0) + jnp.log1p(jnp.exp(-jnp.abs(x)))


def _dot(a, b):
    return jnp.dot(a, b, preferred_element_type=F32)


def _dot_nt(a, b):
    return lax.dot_general(a, b, (((1,), (1,)), ((), ())), preferred_element_type=F32)


def _dot_tn(a, b):
    return lax.dot_general(a, b, (((0,), (0,)), ((), ())), preferred_element_type=F32)


def _split3(x):
    x1 = x.astype(BF16)
    r1 = x - x1.astype(F32)
    x2 = r1.astype(BF16)
    x3 = (r1 - x2.astype(F32)).astype(BF16)
    return x1, x2, x3


def _inproj_body(x_ref, g_ref, w_ref, ws_ref, aq_ref, ak_ref, av_ref, bq_ref, bk_ref, bv_ref,
                 cqkv_ref, cgate_ref, csmall_ref):
    h = _rms(x_ref[...], g_ref[...]).astype(BF16)
    col = 0
    for ref in (aq_ref, ak_ref, av_ref, bq_ref, bk_ref, bv_ref, cqkv_ref, cgate_ref):
        width = ref.shape[-1]
        ref[...] = _dot(h, w_ref[:, col:col + width])
        col += width
    csmall_ref[...] = _dot(h, ws_ref[...])


def _inproj(x, g, w_main, w_small):
    n, d = x.shape
    tm = min(ROW_TILE, n)
    widths = (HEAD_W,) * 6 + (3 * C_W, C_W, LANES)
    row = lambda i: (i, 0)
    const = lambda i: (0, 0)
    return pl.pallas_call(
        _inproj_body,
        grid=(n // tm,),
        in_specs=[pl.BlockSpec((tm, d), row), pl.BlockSpec((1, d), const),
                  pl.BlockSpec(w_main.shape, const), pl.BlockSpec(w_small.shape, const)],
        out_specs=[pl.BlockSpec((tm, w), row) for w in widths],
        out_shape=[jax.ShapeDtypeStruct((n, w), F32) for w in widths],
        compiler_params=_params(("parallel",)),
        name="inproj",
    )(x, g, w_main, w_small)


def _diff_out(acc1, l1, acc2, l2, lam, subln, post_scale):
    o = acc1 / l1 - lam * (acc2 / l2)
    ms = jnp.mean(o * o, axis=-1, keepdims=True)
    return o * lax.rsqrt(ms + NORM_EPS) * subln * post_scale


def _attn_a_prompt_body(lam_ref, q_ref, k_ref, v_ref, bd_ref, bs_ref, subln_ref, o_ref, kbf, vbf,
                        *, blk, post_scale):
    i = pl.program_id(1)

    @pl.when(i == 0)
    def _():
        kbf[...] = k_ref[...].astype(BF16)
        vbf[...] = v_ref[...].astype(BF16)

    lam = lam_ref[0]
    q = q_ref[...] * (DQK_A ** -0.5)
    j_diag = pl.multiple_of(i * blk, blk)
    j_sub = pl.multiple_of(jnp.maximum(i - 1, 0) * blk, blk)
    sub_off = jnp.where(i >= 1, 0.0, NEG_INF)
    n_far = jnp.maximum(i - 1, 0)
    outs = []
    for h in range(H_A):
        res = []
        for m in range(2):
            c0 = h * 2 * DQK_A + m * DQK_A
            qhm = q[:, c0:c0 + DQK_A].astype(BF16)

            def scores(j0, c0=c0, qhm=qhm):
                return _dot_nt(qhm, kbf[pl.ds(j0, blk), c0:c0 + DQK_A])

            def update(carry, s, j0, h=h):
                m_run, l_run, acc = carry
                m_new = jnp.maximum(m_run, jnp.max(s, axis=-1, keepdims=True))
                alpha = jnp.exp(m_run - m_new)
                p = jnp.exp(s - m_new)
                l_new = alpha * l_run + jnp.sum(p, axis=-1, keepdims=True)
                vb = vbf[pl.ds(j0, blk), h * DV_A:(h + 1) * DV_A]
                return m_new, l_new, alpha * acc + _dot(p.astype(BF16), vb)

            s = scores(j_diag) + bd_ref[h]
            m0 = jnp.max(s, axis=-1, keepdims=True)
            p = jnp.exp(s - m0)
            carry = (m0, jnp.sum(p, axis=-1, keepdims=True),
                     _dot(p.astype(BF16), vbf[pl.ds(j_diag, blk), h * DV_A:(h + 1) * DV_A]))
            carry = update(carry, scores(j_sub) + bs_ref[h] + sub_off, j_sub)

            def far(jj, carry, scores=scores, update=update):
                j0 = pl.multiple_of(jj * blk, blk)
                return update(carry, scores(j0), j0)

            carry = lax.fori_loop(0, n_far, far, carry)
            res.append(carry)
        (_, l1, a1), (_, l2, a2) = res
        outs.append(_diff_out(a1, l1, a2, l2, lam, subln_ref[...], post_scale))
    o_ref[...] = jnp.concatenate(outs, axis=-1)


def _attn_a_prompt(lam, q, k, v, bias_diag, bias_sub, subln, post_scale):
    b, s, w = q.shape
    blk = min(A_BLOCK, s)
    body = functools.partial(_attn_a_prompt_body, blk=blk, post_scale=post_scale)
    return pl.pallas_call(
        body,
        grid=(b, s // blk),
        in_specs=[pl.BlockSpec(memory_space=pltpu.SMEM),
                  pl.BlockSpec((None, blk, w), lambda bi, i: (bi, i, 0)),
                  pl.BlockSpec((None, s, w), lambda bi, i: (bi, 0, 0)),
                  pl.BlockSpec((None, s, w), lambda bi, i: (bi, 0, 0)),
                  pl.BlockSpec(bias_diag.shape, lambda bi, i: (0, 0, 0)),
                  pl.BlockSpec(bias_sub.shape, lambda bi, i: (0, 0, 0)),
                  pl.BlockSpec((1, DV_A), lambda bi, i: (0, 0))],
        out_specs=pl.BlockSpec((None, blk, w), lambda bi, i: (bi, i, 0)),
        out_shape=jax.ShapeDtypeStruct((b, s, w), F32),
        scratch_shapes=[pltpu.VMEM((s, w), BF16), pltpu.VMEM((s, w), BF16)],
        compiler_params=_params(("parallel", "arbitrary")),
        name="attn_a_prompt",
    )(lam, q, k, v, bias_diag, bias_sub, subln)


def _attn_a_sample_body(lam_ref, q_ref, kc_ref, vc_ref, kn_ref, vn_ref, bc_ref, bn_ref, subln_ref, o_ref,
                        *, post_scale):
    lam = lam_ref[0]
    q = q_ref[...] * (DQK_A ** -0.5)
    kc = kc_ref[...].astype(BF16)
    vc = vc_ref[...].astype(BF16)
    kn = kn_ref[...].astype(BF16)
    vn = vn_ref[...].astype(BF16)
    outs = []
    for h in range(H_A):
        res = []
        for m in range(2):
            c0 = h * 2 * DQK_A + m * DQK_A
            qhm = q[:, c0:c0 + DQK_A].astype(BF16)
            s_c = _dot_nt(qhm, kc[:, c0:c0 + DQK_A]) + bc_ref[h]
            s_n = _dot_nt(qhm, kn[:, c0:c0 + DQK_A]) + bn_ref[h]
            mx = jnp.maximum(jnp.max(s_c, axis=-1, keepdims=True), jnp.max(s_n, axis=-1, keepdims=True))
            p_c = jnp.exp(s_c - mx)
            p_n = jnp.exp(s_n - mx)
            l = jnp.sum(p_c, axis=-1, keepdims=True) + jnp.sum(p_n, axis=-1, keepdims=True)
            acc = (_dot(p_c.astype(BF16), vc[:, h * DV_A:(h + 1) * DV_A])
                   + _dot(p_n.astype(BF16), vn[:, h * DV_A:(h + 1) * DV_A]))
            res.append((l, acc))
        (l1, a1), (l2, a2) = res
        outs.append(_diff_out(a1, l1, a2, l2, lam, subln_ref[...], post_scale))
    o_ref[...] = jnp.concatenate(outs, axis=-1)


def _attn_a_sample(lam, q, kc, vc, kn, vn, bias_c, bias_n, subln, post_scale):
    b, t, w = q.shape
    p = kc.shape[1]
    body = functools.partial(_attn_a_sample_body, post_scale=post_scale)
    per_b = lambda bi: (bi, 0, 0)
    const3 = lambda bi: (0, 0, 0)
    return pl.pallas_call(
        body,
        grid=(b,),
        in_specs=[pl.BlockSpec(memory_space=pltpu.SMEM),
                  pl.BlockSpec((None, t, w), per_b),
                  pl.BlockSpec((None, p, w), per_b), pl.BlockSpec((None, p, w), per_b),
                  pl.BlockSpec((None, t, w), per_b), pl.BlockSpec((None, t, w), per_b),
                  pl.BlockSpec(bias_c.shape, const3), pl.BlockSpec(bias_n.shape, const3),
                  pl.BlockSpec((1, DV_A), lambda bi: (0, 0))],
        out_specs=pl.BlockSpec((None, t, w), per_b),
        out_shape=jax.ShapeDtypeStruct((b, t, w), F32),
        compiler_params=_params(("parallel",)),
        name="attn_a_sample",
    )(lam, q, kc, vc, kn, vn, bias_c, bias_n, subln)


def _band_body(q_ref, k_ref, v_ref, bias_ref, o_ref, kbf, vbf, *, qb, n_invalid):
    i = pl.program_id(1)

    @pl.when(i == 0)
    def _():
        kbf[...] = k_ref[...].astype(BF16)
        vbf[...] = v_ref[...].astype(BF16)

    def chunk(c, carry):
        r0 = pl.multiple_of(c * CHUNK, CHUNK)
        p0 = pl.multiple_of(i * qb + c * CHUNK, CHUNK)
        q = (q_ref[pl.ds(r0, CHUNK), :] * (D_B ** -0.5)).astype(BF16)
        kb = kbf[pl.ds(p0, BAND), :]
        vb = vbf[pl.ds(p0, BAND), :]
        if n_invalid:
            valid = (p0 + lax.broadcasted_iota(jnp.int32, (CHUNK, BAND), 1)) >= n_invalid
        outs = []
        for h in range(H_B):
            sl = slice(h * D_B, (h + 1) * D_B)
            s = _dot_nt(q[:, sl], kb[:, sl]) + bias_ref[h]
            if n_invalid:
                s = jnp.where(valid, s, NEG_INF)
            mx = jnp.max(s, axis=-1, keepdims=True)
            p = jnp.exp(s - mx)
            l = jnp.sum(p, axis=-1, keepdims=True)
            outs.append(_dot(p.astype(BF16), vb[:, sl]) / l)
        o_ref[pl.ds(r0, CHUNK), :] = jnp.concatenate(outs, axis=-1)
        return carry

    lax.fori_loop(0, qb // CHUNK, chunk, 0)


def _band_attn(q, k, v, bias, n_invalid):
    b, t, w = q.shape
    tk = k.shape[1]
    qb = min(8 * CHUNK, t)
    body = functools.partial(_band_body, qb=qb, n_invalid=n_invalid)
    return pl.pallas_call(
        body,
        grid=(b, t // qb),
        in_specs=[pl.BlockSpec((None, qb, w), lambda bi, i: (bi, i, 0)),
                  pl.BlockSpec((None, tk, w), lambda bi, i: (bi, 0, 0)),
                  pl.BlockSpec((None, tk, w), lambda bi, i: (bi, 0, 0)),
                  pl.BlockSpec(bias.shape, lambda bi, i: (0, 0, 0))],
        out_specs=pl.BlockSpec((None, qb, w), lambda bi, i: (bi, i, 0)),
        out_shape=jax.ShapeDtypeStruct((b, t, w), F32),
        scratch_shapes=[pltpu.VMEM((tk, w), BF16), pltpu.VMEM((tk, w), BF16)],
        compiler_params=_params(("parallel", "arbitrary")),
        name="band_attn",
    )(q, k, v, bias)


_SMALL_DECAY0 = H_C


def _gdn_body(cqkv_ref, small_ref, gate_ref, cprev_ref, s0_ref, convw_ref, alog_ref, dtb_ref, onorm_ref,
              oc_ref, sout_ref, xext, yconv, *, tc):
    j = pl.program_id(1)
    pad = 8
    lo = pad - (CONV_W - 1)

    @pl.when(j == 0)
    def _():
        xext[lo:pad, :] = cprev_ref[...]
        sout_ref[...] = s0_ref[...]

    @pl.when(j > 0)
    def _():
        xext[lo:pad, :] = xext[tc + lo:tc + pad, :]

    xext[pad:, :] = cqkv_ref[...]
    y = convw_ref[0:1, :] * xext[lo:lo + tc, :]
    for w in range(1, CONV_W):
        y = y + convw_ref[w:w + 1, :] * xext[lo + w:lo + w + tc, :]
    yconv[...] = _silu(y)

    ii = lax.broadcasted_iota(jnp.int32, (CHUNK, CHUNK), 0)
    jj = lax.broadcasted_iota(jnp.int32, (CHUNK, CHUNK), 1)
    incl = ii >= jj
    strict = ii > jj
    tri = jnp.where(incl, 1.0, 0.0).astype(BF16)
    eye = jnp.where(ii == jj, 1.0, 0.0)
    lane = lax.broadcasted_iota(jnp.int32, (CHUNK, LANES), 1)
    is_dec = (lane >= _SMALL_DECAY0) & (lane < _SMALL_DECAY0 + H_C)
    pick = jnp.where(lax.broadcasted_iota(jnp.int32, (H_C, LANES), 1)
                     == lax.broadcasted_iota(jnp.int32, (H_C, LANES), 0) + _SMALL_DECAY0, 1.0, 0.0).astype(BF16)
    a_vec = jnp.exp(alog_ref[...])

    def chunk(r, carry):
        r0 = pl.multiple_of(r * CHUNK, CHUNK)
        sm = small_ref[pl.ds(r0, CHUNK), :]
        log_a = jnp.where(is_dec, -a_vec * _softplus(sm + dtb_ref[...]), 0.0)
        l1, l2, l3 = _split3(log_a)
        g_full = _dot(tri, l1) + _dot(tri, l2) + _dot(tri, l3)
        g1, g2, g3 = _split3(g_full)
        g_t = _dot_nt(pick, g1) + _dot_nt(pick, g2) + _dot_nt(pick, g3)
        beta_full = _sigmoid(sm)
        yc = yconv[pl.ds(r0, CHUNK), :]
        gt = gate_ref[pl.ds(r0, CHUNK), :]
        outs = []
        for h in range(H_C):
            sl = slice(h * D_C, (h + 1) * D_C)
            qh = yc[:, sl]
            kh = yc[:, C_W + h * D_C:C_W + (h + 1) * D_C]
            vh = yc[:, 2 * C_W + h * D_C:2 * C_W + (h + 1) * D_C]
            qh = qh * lax.rsqrt(jnp.sum(qh * qh, axis=-1, keepdims=True) + L2_EPS) * (D_C ** -0.5)
            kh = kh * lax.rsqrt(jnp.sum(kh * kh, axis=-1, keepdims=True) + L2_EPS)
            g_col = g_full[:, _SMALL_DECAY0 + h:_SMALL_DECAY0 + h + 1]
            g_row = g_t[h:h + 1, :]
            beta = beta_full[:, h:h + 1]
            dec = jnp.where(incl, jnp.exp(jnp.where(incl, g_col - g_row, 0.0)), 0.0)
            qk2 = jnp.concatenate([qh, kh], axis=0).astype(BF16)
            r_k = _dot_nt(qk2, kh.astype(BF16))
            qk, kk = r_k[:CHUNK], r_k[CHUNK:]
            s_h = sout_ref[h]
            r_s = _dot(qk2, s_h.astype(BF16))
            qs, ks = r_s[:CHUNK], r_s[CHUNK:]
            eg = jnp.exp(g_col)
            rhs = beta * (vh - eg * ks)
            nm = -(beta * jnp.where(strict, dec, 0.0) * kk)
            pb = nm.astype(BF16)
            pw = _dot(pb, pb)
            tm_ = eye + nm
            for step in range(1, 6):
                pb = pw.astype(BF16)
                if step < 5:
                    st = _dot(jnp.concatenate([tm_, pw], axis=0).astype(BF16), pb)
                    tm_ = tm_ + st[:CHUNK]
                    pw = st[CHUNK:]
                else:
                    tm_ = tm_ + _dot(tm_.astype(BF16), pb)
            u = _dot(tm_.astype(BF16), rhs.astype(BF16))
            o = eg * qs + _dot((dec * qk).astype(BF16), u.astype(BF16))
            g_last = g_col[CHUNK - 1:CHUNK, :]
            kd = kh * jnp.exp(g_last - g_col)
            sout_ref[h] = jnp.exp(g_last) * s_h + _dot_tn(kd.astype(BF16), u.astype(BF16))
            ms = jnp.mean(o * o, axis=-1, keepdims=True)
            outs.append(o * lax.rsqrt(ms + NORM_EPS) * onorm_ref[...] * _silu(gt[:, sl]))
        oc_ref[pl.ds(r0, CHUNK), :] = jnp.concatenate(outs, axis=-1)
        return carry

    lax.fori_loop(0, tc // CHUNK, chunk, 0)


def _gdn(cqkv, small, gate, conv_prev, s0, conv_w, alog_vec, dtb_vec, onorm):
    b, t, w3 = cqkv.shape
    tc = min(4 * CHUNK, t)
    body = functools.partial(_gdn_body, tc=tc)
    blk = lambda bi, j: (bi, j, 0)
    per_b3 = lambda bi, j: (bi, 0, 0)
    per_b4 = lambda bi, j: (bi, 0, 0, 0)
    const2 = lambda bi, j: (0, 0)
    return pl.pallas_call(
        body,
        grid=(b, t // tc),
        in_specs=[pl.BlockSpec((None, tc, w3), blk),
                  pl.BlockSpec((None, tc, LANES), blk),
                  pl.BlockSpec((None, tc, C_W), blk),
                  pl.BlockSpec((None, CONV_W - 1, w3), per_b3),
                  pl.BlockSpec((None, H_C, D_C, D_C), per_b4),
                  pl.BlockSpec(conv_w.shape, const2),
                  pl.BlockSpec((1, LANES), const2),
                  pl.BlockSpec((1, LANES), const2),
                  pl.BlockSpec((1, D_C), const2)],
        out_specs=[pl.BlockSpec((None, tc, C_W), blk),
                   pl.BlockSpec((None, H_C, D_C, D_C), per_b4)],
        out_shape=[jax.ShapeDtypeStruct((b, t, C_W), F32),
                   jax.ShapeDtypeStruct((b, H_C, D_C, D_C), F32)],
        scratch_shapes=[pltpu.VMEM((tc + 8, w3), F32), pltpu.VMEM((tc, w3), F32)],
        compiler_params=_params(("parallel", "arbitrary")),
        name="gdn",
    )(cqkv, small, gate, conv_prev, s0, conv_w, alog_vec, dtb_vec, onorm)


def _top2_gates(logits):
    lane = lax.broadcasted_iota(jnp.int32, logits.shape, 1).astype(F32)
    low = -3.0e38
    lg = jnp.where(lane < N_EXPERTS, logits, low)
    m1 = jnp.max(lg, axis=-1, keepdims=True)
    i1 = jnp.min(jnp.where(lg == m1, lane, float(LANES)), axis=-1, keepdims=True)
    lg2 = jnp.where(lane == i1, low, lg)
    m2 = jnp.max(lg2, axis=-1, keepdims=True)
    i2 = jnp.min(jnp.where(lg2 == m2, lane, float(LANES)), axis=-1, keepdims=True)
    e2 = jnp.exp(m2 - m1)
    den = 1.0 + e2
    return jnp.where(lane == i1, 1.0 / den, 0.0) + jnp.where(lane == i2, e2 / den, 0.0)


def _outproj_body(x_ref, oa_ref, ob_ref, oc_ref, w_ref, g_ref, *rest, with_router):
    if with_router:
        rhi_ref, rlo_ref, xo_ref, h_ref, gates_ref = rest
    else:
        xo_ref, h_ref = rest
    y = (_dot(oa_ref[...].astype(BF16), w_ref[0:HEAD_W, :])
         + _dot(ob_ref[...].astype(BF16), w_ref[HEAD_W:2 * HEAD_W, :])
         + _dot(oc_ref[...].astype(BF16), w_ref[2 * HEAD_W:, :]))
    x = x_ref[...] + y
    xo_ref[...] = x
    hf = _rms(x, g_ref[...])
    hb = hf.astype(BF16)
    h_ref[...] = hb
    if with_router:
        lo = (hf - hb.astype(F32)).astype(BF16)
        logits = _dot(hb, rhi_ref[...]) + _dot(lo, rhi_ref[...]) + _dot(hb, rlo_ref[...])
        gates_ref[...] = _top2_gates(logits)


def _outproj(x, oa, ob, oc, w_out, g, router=None):
    n, d = x.shape
    tm = min(ROW_TILE, n)
    row = lambda i: (i, 0)
    const = lambda i: (0, 0)
    in_specs = [pl.BlockSpec((tm, d), row), pl.BlockSpec((tm, HEAD_W), row), pl.BlockSpec((tm, HEAD_W), row),
                pl.BlockSpec((tm, C_W), row), pl.BlockSpec(w_out.shape, const), pl.BlockSpec((1, d), const)]
    out_specs = [pl.BlockSpec((tm, d), row), pl.BlockSpec((tm, d), row)]
    out_shape = [jax.ShapeDtypeStruct((n, d), F32), jax.ShapeDtypeStruct((n, d), BF16)]
    args = [x, oa, ob, oc, w_out, g]
    if router is not None:
        in_specs += [pl.BlockSpec(router[0].shape, const), pl.BlockSpec(router[1].shape, const)]
        out_specs.append(pl.BlockSpec((tm, LANES), row))
        out_shape.append(jax.ShapeDtypeStruct((n, LANES), F32))
        args += list(router)
    return pl.pallas_call(
        functools.partial(_outproj_body, with_router=router is not None),
        grid=(n // tm,),
        in_specs=in_specs, out_specs=out_specs, out_shape=out_shape,
        compiler_params=_params(("parallel",)),
        name="outproj",
    )(*args)


def _ffn_body(h_ref, x_ref, wg_ref, wu_ref, wd_ref, o_ref, acc_ref, *, ff_chunk):
    h = h_ref[...]
    d_ff = wg_ref.shape[1]
    acc_ref[...] = x_ref[...]
    for c0 in range(0, d_ff, ff_chunk):
        a = _silu(_dot(h, wg_ref[:, c0:c0 + ff_chunk])) * _dot(h, wu_ref[:, c0:c0 + ff_chunk])
        acc_ref[...] += _dot(a.astype(BF16), wd_ref[c0:c0 + ff_chunk, :])
    o_ref[...] = acc_ref[...]


def _ffn_dense(h, x, wg, wu, wd):
    n, d = x.shape
    d_ff = wg.shape[1]
    tm = min(ROW_TILE, n)
    ff_chunk = 256 if d_ff % 256 == 0 else LANES
    row = lambda i: (i, 0)
    const = lambda i: (0, 0)
    return pl.pallas_call(
        functools.partial(_ffn_body, ff_chunk=ff_chunk),
        grid=(n // tm,),
        in_specs=[pl.BlockSpec((tm, d), row), pl.BlockSpec((tm, d), row),
                  pl.BlockSpec(wg.shape, const), pl.BlockSpec(wu.shape, const), pl.BlockSpec(wd.shape, const)],
        out_specs=pl.BlockSpec((tm, d), row),
        out_shape=jax.ShapeDtypeStruct((n, d), F32),
        scratch_shapes=[pltpu.VMEM((tm, d), F32)],
        compiler_params=_params(("parallel",)),
        name="ffn_dense",
    )(h, x, wg, wu, wd)


def _moe_body(cnt_ref, h_ref, gates_ref, gates_t_ref, wg_ref, wu_ref, wd_ref, y_ref,
              rank_col, rank_row, hs, eo, *, tg, n_f):
    g = pl.program_id(0)
    e = pl.program_id(1)
    f = pl.program_id(2)
    tm = MOE_ROW_TILE
    tb = MOE_TOKEN_BLOCK
    n_tb = tg // tb
    n_tiles = (cnt_ref[g * N_EXPERTS + e] + tm - 1) // tm

    @pl.when((e == 0) & (f == 0))
    def _():
        y_ref[...] = jnp.zeros_like(y_ref)
        ii = lax.broadcasted_iota(jnp.int32, (tb, tb), 0)
        jj = lax.broadcasted_iota(jnp.int32, (tb, tb), 1)
        lower = jnp.where(ii > jj, 1.0, 0.0).astype(BF16)
        upper = jnp.where(ii < jj, 1.0, 0.0).astype(BF16)
        carry_c = jnp.zeros((1, LANES), F32)
        carry_r = jnp.zeros((N_EXPERTS, 1), F32)
        for b in range(n_tb):
            mc = jnp.where(gates_ref[b * tb:(b + 1) * tb, :] > 0.0, 1.0, 0.0)
            rank_col[b * tb:(b + 1) * tb, :] = _dot(lower, mc.astype(BF16)) + carry_c
            carry_c = carry_c + jnp.sum(mc, axis=0, keepdims=True)
            mr = jnp.where(gates_t_ref[:, b * tb:(b + 1) * tb] > 0.0, 1.0, 0.0)
            rank_row[:, b * tb:(b + 1) * tb] = _dot(mr.astype(BF16), upper) + carry_r
            carry_r = carry_r + jnp.sum(mr, axis=1, keepdims=True)

    @pl.when(f == 0)
    def _():
        def gather(t, carry):
            r0 = pl.multiple_of(t * tm, tm)
            rows = (r0 + lax.broadcasted_iota(jnp.int32, (tm, tb), 0)).astype(F32)
            acc = jnp.zeros((tm, h_ref.shape[1]), F32)
            for b in range(n_tb):
                gr = gates_t_ref[pl.ds(e, 1), b * tb:(b + 1) * tb]
                rr = jnp.where(gr > 0.0, rank_row[pl.ds(e, 1), b * tb:(b + 1) * tb], -1.0)
                onehot = jnp.where(rr == rows, 1.0, 0.0).astype(BF16)
                acc = acc + _dot(onehot, h_ref[b * tb:(b + 1) * tb, :])
            hs[pl.ds(r0, tm), :] = acc.astype(BF16)
            return carry

        lax.fori_loop(0, n_tiles, gather, 0)

    def expert(t, carry):
        r0 = pl.multiple_of(t * tm, tm)
        x = hs[pl.ds(r0, tm), :]
        a = _silu(_dot(x, wg_ref[...])) * _dot(x, wu_ref[...])
        part = _dot(a.astype(BF16), wd_ref[...])

        @pl.when(f == 0)
        def _():
            eo[pl.ds(r0, tm), :] = part

        @pl.when(f > 0)
        def _():
            eo[pl.ds(r0, tm), :] += part

        return carry

    lax.fori_loop(0, n_tiles, expert, 0)

    @pl.when(f == n_f - 1)
    def _():
        lane = lax.broadcasted_iota(jnp.int32, (tb, LANES), 1)

        def scatter(t, carry):
            r0 = pl.multiple_of(t * tm, tm)
            cols = (r0 + lax.broadcasted_iota(jnp.int32, (tb, tm), 1)).astype(F32)
            out = eo[pl.ds(r0, tm), :].astype(BF16)
            for b in range(n_tb):
                sel = lane == e
                gc = jnp.sum(jnp.where(sel, gates_ref[b * tb:(b + 1) * tb, :], 0.0), axis=-1, keepdims=True)
                rc = jnp.sum(jnp.where(sel, rank_col[b * tb:(b + 1) * tb, :], 0.0), axis=-1, keepdims=True)
                rc = jnp.where(gc > 0.0, rc, -1.0)
                weighted = jnp.where(rc == cols, gc, 0.0).astype(BF16)
                y_ref[b * tb:(b + 1) * tb, :] += _dot(weighted, out)
            return carry

        lax.fori_loop(0, n_tiles, scatter, 0)


def _moe(counts, h, gates, gates_t, wg, wu, wd, tg):
    n, d = h.shape
    n_e, _, d_ff = wg.shape
    n_f = 2
    ffh = d_ff // n_f
    body = functools.partial(_moe_body, tg=tg, n_f=n_f)
    grid_spec = pltpu.PrefetchScalarGridSpec(
        num_scalar_prefetch=1,
        grid=(n // tg, n_e, n_f),
        in_specs=[pl.BlockSpec((tg, d), lambda g, e, f, c: (g, 0)),
                  pl.BlockSpec((tg, LANES), lambda g, e, f, c: (g, 0)),
                  pl.BlockSpec((N_EXPERTS, tg), lambda g, e, f, c: (0, g)),
                  pl.BlockSpec((None, d, ffh), lambda g, e, f, c: (e, 0, f)),
                  pl.BlockSpec((None, d, ffh), lambda g, e, f, c: (e, 0, f)),
                  pl.BlockSpec((None, ffh, d), lambda g, e, f, c: (e, f, 0))],
        out_specs=pl.BlockSpec((tg, d), lambda g, e, f, c: (g, 0)),
        scratch_shapes=[pltpu.VMEM((tg, LANES), F32), pltpu.VMEM((N_EXPERTS, tg), F32),
                        pltpu.VMEM((tg, d), BF16), pltpu.VMEM((tg, d), F32)],
    )
    return pl.pallas_call(
        body,
        grid_spec=grid_spec,
        out_shape=jax.ShapeDtypeStruct((n, d), F32),
        compiler_params=_params(("parallel", "arbitrary", "arbitrary")),
        name="moe",
    )(counts, h, gates, gates_t, wg, wu, wd)


def _final_body(x_ref, y_ref, g_ref, o_ref):
    o_ref[...] = _rms(x_ref[...] + y_ref[...], g_ref[...])


def _final_norm(x, y, g):
    n, d = x.shape
    tm = min(ROW_TILE, n)
    row = lambda i: (i, 0)
    return pl.pallas_call(
        _final_body,
        grid=(n // tm,),
        in_specs=[pl.BlockSpec((tm, d), row), pl.BlockSpec((tm, d), row), pl.BlockSpec((1, d), lambda i: (0, 0))],
        out_specs=pl.BlockSpec((tm, d), row),
        out_shape=jax.ShapeDtypeStruct((n, d), F32),
        compiler_params=_params(("parallel",)),
        name="final_norm",
    )(x, y, g)


def _t5_bucket(rel):
    nb = T5_BUCKETS // 2
    max_exact = nb // 2
    ret = jnp.where(rel > 0, nb, 0)
    n = jnp.abs(rel)
    nf = jnp.maximum(n, 1).astype(F32)
    large = max_exact + (jnp.log(nf / max_exact) / math.log(T5_MAX_DIST / max_exact) * (nb - max_exact)).astype(jnp.int32)
    large = jnp.minimum(large, nb - 1)
    return ret + jnp.where(n < max_exact, n, large)


def _t5_table(t5_bias, q_pos, k_pos):
    bias = jnp.transpose(t5_bias[_t5_bucket(k_pos[None, :] - q_pos[:, None])], (2, 0, 1)).astype(F32)
    mask = (k_pos[None, :] // CHUNK) <= (q_pos[:, None] // CHUNK)
    return bias, mask[None]


def _band_table(rel_bias):
    qi = jnp.arange(CHUNK, dtype=jnp.int32)
    kj = jnp.arange(BAND, dtype=jnp.int32) - B_LEFT_CHUNKS * CHUNK
    rel = jnp.clip(kj[None, :] - qi[:, None], -B_REL_CLIP, B_REL_CLIP) + B_REL_CLIP
    return jnp.transpose(rel_bias[rel], (2, 0, 1)).astype(F32)


def _lane_vec(v, offset):
    return jnp.zeros((1, LANES), F32).at[0, offset:offset + v.shape[0]].set(v.astype(F32))


def _moe_group(n):
    for tg in (1024, 512, 256):
        if n % tg == 0:
            return tg
    raise ValueError(f"token count {n} is not a multiple of 256")


def kernel(x_prompt, x_sample, cache_a_k, cache_a_v, cache_b_k, cache_b_v, cache_c_conv, state_c_ssm, w_in, w_out, norm_mix, norm_ffn, norm_final, lam_qk, subln_a, t5_bias, rel_bias_b, conv_c, a_log_c, dt_bias_c, onorm_c, ffn_gate, ffn_up, ffn_down, moe_router, moe_gate, moe_up, moe_down):
    depth = w_in.shape[0]
    bp, sp, d = x_prompt.shape
    bs, ss, _ = x_sample.shape
    past = cache_a_k.shape[2]
    nb_cache = cache_b_k.shape[2]
    assert ss == CHUNK and nb_cache == B_LEFT_CHUNKS * CHUNK and sp >= B_LEFT_CHUNKS * CHUNK

    xp = x_prompt.reshape(bp * sp, d)
    xs = x_sample.reshape(bs * ss, d)
    blk = min(A_BLOCK, sp)

    pos_blk = jnp.arange(blk, dtype=jnp.int32)
    far_bias = t5_bias[_t5_bucket(jnp.int32(-(blk + 1)))].astype(F32)
    bd, md = _t5_table(t5_bias, pos_blk, pos_blk)
    bias_diag = jnp.where(md, bd - far_bias[:, None, None], NEG_INF)
    bsub, _ = _t5_table(t5_bias, blk + pos_blk, pos_blk)
    bias_sub = bsub - far_bias[:, None, None]
    q_pos_s = past + jnp.arange(ss, dtype=jnp.int32)
    bfull, mfull = _t5_table(t5_bias, q_pos_s, jnp.arange(past + ss, dtype=jnp.int32))
    bias_s = jnp.where(mfull, bfull, NEG_INF)
    bias_s_cache, bias_s_new = bias_s[:, :, :past], bias_s[:, :, past:]

    zeros_conv = jnp.zeros((bp, CONV_W - 1, 3 * C_W), F32)
    zeros_state = jnp.zeros((bp, H_C, D_C, D_C), F32)

    p_states = [[] for _ in range(6)]
    s_states = [[] for _ in range(6)]
    yp = ys = None
    for l in range(depth):
        if yp is not None:
            xp, xs, yp, ys = xp + yp, xs + ys, None, None
        w = w_in[l]
        n_main = 6 * HEAD_W + 3 * C_W
        w_main = jnp.concatenate([w[:, :n_main], w[:, n_main + 2 * H_C:]], axis=1).astype(BF16)
        w_small = jnp.zeros((d, LANES), F32).at[:, :2 * H_C].set(w[:, n_main:n_main + 2 * H_C]).astype(BF16)
        w_out_l = w_out[l].astype(BF16)
        g_mix = norm_mix[l].reshape(1, d)
        g_ffn = norm_ffn[l].reshape(1, d)
        lam_init = 0.8 - 0.6 * math.exp(-0.3 * l)
        lq = lam_qk[l].astype(F32)
        lam = (jnp.exp(jnp.sum(lq[0] * lq[1])) - jnp.exp(jnp.sum(lq[2] * lq[3])) + lam_init).reshape(1)
        subln = subln_a[l].reshape(1, DV_A)
        band_bias = _band_table(rel_bias_b[l])
        alog_vec = _lane_vec(a_log_c[l], _SMALL_DECAY0)
        dtb_vec = _lane_vec(dt_bias_c[l], _SMALL_DECAY0)
        onorm = onorm_c[l].reshape(1, D_C)
        is_moe = l % 2 == 1
        if is_moe:
            r = jnp.zeros((d, LANES), F32).at[:, :N_EXPERTS].set(moe_router[l // 2])
            r_hi = r.astype(BF16)
            router = (r_hi, (r - r_hi.astype(F32)).astype(BF16))
            e_wg, e_wu, e_wd = (moe_gate[l // 2].astype(BF16), moe_up[l // 2].astype(BF16),
                                moe_down[l // 2].astype(BF16))
        else:
            router = None
            f_wg, f_wu, f_wd = (ffn_gate[l // 2].astype(BF16), ffn_up[l // 2].astype(BF16),
                                ffn_down[l // 2].astype(BF16))

        new_x = []
        for is_prompt, x in ((True, xp), (False, xs)):
            b, t = (bp, sp) if is_prompt else (bs, ss)
            aq, ak, av, bq, bk, bv, cqkv, cgate, csmall = _inproj(x, g_mix, w_main, w_small)
            r3 = lambda a: a.reshape(b, t, a.shape[-1])
            if is_prompt:
                oa = _attn_a_prompt(lam, r3(aq), r3(ak), r3(av), bias_diag, bias_sub, subln, 1.0 - lam_init)
                pad = ((0, 0), (B_LEFT_CHUNKS * CHUNK, 0), (0, 0))
                ob = _band_attn(r3(bq), jnp.pad(r3(bk), pad), jnp.pad(r3(bv), pad), band_bias,
                                B_LEFT_CHUNKS * CHUNK)
                conv_prev, s0 = zeros_conv, zeros_state
            else:
                oa = _attn_a_sample(lam, r3(aq), cache_a_k[l].reshape(b, past, HEAD_W),
                                    cache_a_v[l].reshape(b, past, HEAD_W), r3(ak), r3(av),
                                    bias_s_cache, bias_s_new, subln, 1.0 - lam_init)
                kb = jnp.concatenate([cache_b_k[l].reshape(b, nb_cache, HEAD_W), r3(bk)], axis=1)
                vb = jnp.concatenate([cache_b_v[l].reshape(b, nb_cache, HEAD_W), r3(bv)], axis=1)
                ob = _band_attn(r3(bq), kb, vb, band_bias, 0)
                conv_prev, s0 = cache_c_conv[l], state_c_ssm[l]
            oc, s_new = _gdn(r3(cqkv), r3(csmall), r3(cgate), conv_prev, s0, conv_c[l], alog_vec, dtb_vec, onorm)
            n = b * t
            res = _outproj(x, oa.reshape(n, HEAD_W), ob.reshape(n, HEAD_W), oc.reshape(n, C_W), w_out_l, g_ffn,
                           router)
            if is_moe:
                x_new, h2, gates = res
                tg = _moe_group(n)
                counts = jnp.sum((gates[:, :N_EXPERTS] > 0.0).reshape(n // tg, tg, N_EXPERTS), axis=1,
                                 dtype=jnp.int32).reshape(-1)
                y_moe = _moe(counts, h2, gates, gates[:, :N_EXPERTS].T, e_wg, e_wu, e_wd, tg)
                new_x.append((x_new, y_moe))
            else:
                x_new, h2 = res
                new_x.append((_ffn_dense(h2, x_new, f_wg, f_wu, f_wd), None))
            keep = min(B_LEFT_CHUNKS * CHUNK, t)
            states = (ak.reshape(b, t, H_A, 2 * DQK_A), av.reshape(b, t, H_A, DV_A),
                      bk.reshape(b, t, H_B, D_B)[:, t - keep:], bv.reshape(b, t, H_B, D_B)[:, t - keep:],
                      r3(cqkv)[:, t - (CONV_W - 1):], s_new)
            for i in range(6):
                (p_states if is_prompt else s_states)[i].append(states[i])
        (xp, yp), (xs, ys) = new_x

    g_fin = norm_final.reshape(1, d)
    if yp is None:
        yp, ys = jnp.zeros_like(xp), jnp.zeros_like(xs)
    y_prompt = _final_norm(xp, yp, g_fin).reshape(bp, sp, d)
    y_sample = _final_norm(xs, ys, g_fin).reshape(bs, ss, d)
    p_out = [jnp.stack(s, axis=0) for s in p_states]
    s_out = [jnp.stack(s, axis=0) for s in s_states]
    return (y_prompt, y_sample, *p_out, *s_out)
```

```python
import functools
import math

import jax
import jax.numpy as jnp
from jax import lax
from jax.experimental import pallas as pl
from jax.experimental.pallas import tpu as pltpu

F32 = jnp.float32
BF16 = jnp.bfloat16

CHUNK = 64
H_A = 4
DQK_A = 32
DV_A = 64
H_B = 4
D_B = 64
B_LEFT_CHUNKS = 8
B_REL_CLIP = 128
H_C = 8
D_C = 64
CONV_W = 4
T5_BUCKETS = 32
T5_MAX_DIST = 128
N_EXPERTS = 8
NORM_EPS = 1e-6
L2_EPS = 1e-6
NEG_INF = -1e30
HEAD_W = 256
C_W = H_C * D_C
BAND = (B_LEFT_CHUNKS + 1) * CHUNK
LANES = 128
VMEM_LIMIT = 56 * 1024 * 1024

A_BLOCK = 256
ROW_TILE = 512
MOE_ROW_TILE = 256
MOE_TOKEN_BLOCK = 256


def _params(sem):
    return pltpu.CompilerParams(dimension_semantics=sem, vmem_limit_bytes=VMEM_LIMIT)


def _rms(x, g):
    ms = jnp.mean(x * x, axis=-1, keepdims=True)
    return x * lax.rsqrt(ms + NORM_EPS) * g


def _sigmoid(x):
    return 1.0 / (1.0 + jnp.exp(-x))


def _silu(x):
    return x * _sigmoid(x)


def _softplus(x):
    return jnp.maximum(x, 0.0) + jnp.log1p(jnp.exp(-jnp.abs(x)))


def _dot(a, b):
    return jnp.dot(a, b, preferred_element_type=F32)


def _dot_nt(a, b):
    return lax.dot_general(a, b, (((1,), (1,)), ((), ())), preferred_element_type=F32)


def _dot_tn(a, b):
    return lax.dot_general(a, b, (((0,), (0,)), ((), ())), preferred_element_type=F32)


def _split3(x):
    x1 = x.astype(BF16)
    r1 = x - x1.astype(F32)
    x2 = r1.astype(BF16)
    x3 = (r1 - x2.astype(F32)).astype(BF16)
    return x1, x2, x3


def _inproj_body(x_ref, g_ref, w_ref, ws_ref, aq_ref, ak_ref, av_ref, bq_ref, bk_ref, bv_ref,
                 cqkv_ref, cgate_ref, csmall_ref):
    h = _rms(x_ref[...], g_ref[...]).astype(BF16)
    col = 0
    for ref in (aq_ref, ak_ref, av_ref, bq_ref, bk_ref, bv_ref, cqkv_ref, cgate_ref):
        width = ref.shape[-1]
        ref[...] = _dot(h, w_ref[:, col:col + width])
        col += width
    csmall_ref[...] = _dot(h, ws_ref[...])


def _inproj(x, g, w_main, w_small):
    n, d = x.shape
    tm = min(ROW_TILE, n)
    widths = (HEAD_W,) * 6 + (3 * C_W, C_W, LANES)
    row = lambda i: (i, 0)
    const = lambda i: (0, 0)
    return pl.pallas_call(
        _inproj_body,
        grid=(n // tm,),
        in_specs=[pl.BlockSpec((tm, d), row), pl.BlockSpec((1, d), const),
                  pl.BlockSpec(w_main.shape, const), pl.BlockSpec(w_small.shape, const)],
        out_specs=[pl.BlockSpec((tm, w), row) for w in widths],
        out_shape=[jax.ShapeDtypeStruct((n, w), F32) for w in widths],
        compiler_params=_params(("parallel",)),
        name="inproj",
    )(x, g, w_main, w_small)


def _diff_out(acc1, l1, acc2, l2, lam, subln, post_scale):
    o = acc1 / l1 - lam * (acc2 / l2)
    ms = jnp.mean(o * o, axis=-1, keepdims=True)
    return o * lax.rsqrt(ms + NORM_EPS) * subln * post_scale


A_ACC_ROWS = DV_A + 16
A_GROUPS = LANES // DQK_A


def _attn_a_prompt_body(lam_ref, q_ref, k_ref, v_ref, bd_ref, bs_ref, subln_ref, o_ref,
                        kbf, vte, wq, m_scr, acc_scr, *, blk, post_scale):
    i = pl.program_id(1)
    n_blk = k_ref.shape[0] // blk
    n_pair = H_A // 2

    @pl.when(i == 0)
    def _():
        ones_rows = jnp.where(lax.broadcasted_iota(jnp.int32, (A_ACC_ROWS - DV_A, blk), 0) == 0, 1.0, 0.0)
        for jb in range(n_blk):
            kbf[jb] = k_ref[jb * blk:(jb + 1) * blk, :].astype(BF16)
            vt = v_ref[jb * blk:(jb + 1) * blk, :].T
            for h in range(H_A):
                vte[jb, h, 0:DV_A, :] = vt[h * DV_A:(h + 1) * DV_A, :].astype(BF16)
                vte[jb, h, DV_A:, :] = ones_rows.astype(BF16)

    lam = lam_ref[0]
    qt = (q_ref[...] * (DQK_A ** -0.5)).T
    grp = lax.broadcasted_iota(jnp.int32, (LANES, blk), 0) // DQK_A
    for p in range(n_pair):
        qtp = qt[p * LANES:(p + 1) * LANES, :]
        wq[p] = jnp.concatenate([jnp.where(grp == g, qtp, 0.0) for g in range(A_GROUPS)], axis=1).astype(BF16)
    m_scr[...] = jnp.full(m_scr.shape, NEG_INF, F32)
    acc_scr[...] = jnp.zeros(acc_scr.shape, F32)

    def tiles(jb, bias_ref, off):
        for p in range(n_pair):
            s_all = _dot(kbf[jb, :, p * LANES:(p + 1) * LANES], wq[p])
            for gi in range(A_GROUPS):
                h = 2 * p + gi // 2
                g = p * A_GROUPS + gi
                s = s_all[:, gi * blk:(gi + 1) * blk]
                if bias_ref is not None:
                    s = s + bias_ref[h]
                if off is not None:
                    s = s + off
                m_old = m_scr[g:g + 1, :]
                m_new = jnp.maximum(m_old, jnp.max(s, axis=0, keepdims=True))
                alpha = jnp.exp(m_old - m_new)
                pt = jnp.exp(s - m_new).astype(BF16)
                acc_scr[g] = alpha * acc_scr[g] + _dot(vte[jb, h], pt)
                m_scr[g:g + 1, :] = m_new

    tiles(i, bd_ref, None)
    tiles(jnp.maximum(i - 1, 0), bs_ref, jnp.where(i >= 1, 0.0, NEG_INF))

    def far(jb, carry):
        tiles(jb, None, None)
        return carry

    lax.fori_loop(0, jnp.maximum(i - 1, 0), far, 0)

    outs = []
    for h in range(H_A):
        g1 = (h // 2) * A_GROUPS + (h % 2) * 2
        a1 = acc_scr[g1]
        a2 = acc_scr[g1 + 1]
        o = a1[:DV_A] / a1[DV_A:DV_A + 1] - lam * (a2[:DV_A] / a2[DV_A:DV_A + 1])
        ms = jnp.mean(o * o, axis=0, keepdims=True)
        outs.append(o * lax.rsqrt(ms + NORM_EPS) * subln_ref[...] * post_scale)
    o_ref[...] = jnp.concatenate(outs, axis=0).T


def _attn_a_prompt(lam, q, k, v, bias_diag_t, bias_sub_t, subln_col, post_scale):
    b, s, w = q.shape
    blk = min(A_BLOCK, s)
    n_blk = s // blk
    body = functools.partial(_attn_a_prompt_body, blk=blk, post_scale=post_scale)
    return pl.pallas_call(
        body,
        grid=(b, n_blk),
        in_specs=[pl.BlockSpec(memory_space=pltpu.SMEM),
                  pl.BlockSpec((None, blk, w), lambda bi, i: (bi, i, 0)),
                  pl.BlockSpec((None, s, w), lambda bi, i: (bi, 0, 0)),
                  pl.BlockSpec((None, s, w), lambda bi, i: (bi, 0, 0)),
                  pl.BlockSpec(bias_diag_t.shape, lambda bi, i: (0, 0, 0)),
                  pl.BlockSpec(bias_sub_t.shape, lambda bi, i: (0, 0, 0)),
                  pl.BlockSpec((DV_A, 1), lambda bi, i: (0, 0))],
        out_specs=pl.BlockSpec((None, blk, w), lambda bi, i: (bi, i, 0)),
        out_shape=jax.ShapeDtypeStruct((b, s, w), F32),
        scratch_shapes=[pltpu.VMEM((n_blk, blk, w), BF16),
                        pltpu.VMEM((n_blk, H_A, A_ACC_ROWS, blk), BF16),
                        pltpu.VMEM((H_A // 2, LANES, A_GROUPS * blk), BF16),
                        pltpu.VMEM((H_A * 2, blk), F32),
                        pltpu.VMEM((H_A * 2, A_ACC_ROWS, blk), F32)],
        compiler_params=_params(("parallel", "arbitrary")),
        name="attn_a_prompt",
    )(lam, q, k, v, bias_diag_t, bias_sub_t, subln_col)


def _attn_a_sample_body(lam_ref, q_ref, kc_ref, vc_ref, kn_ref, vn_ref, bc_ref, bn_ref, subln_ref, o_ref,
                        *, post_scale):
    lam = lam_ref[0]
    q = q_ref[...] * (DQK_A ** -0.5)
    kc = kc_ref[...].astype(BF16)
    vc = vc_ref[...].astype(BF16)
    kn = kn_ref[...].astype(BF16)
    vn = vn_ref[...].astype(BF16)
    outs = []
    for h in range(H_A):
        res = []
        for m in range(2):
            c0 = h * 2 * DQK_A + m * DQK_A
            qhm = q[:, c0:c0 + DQK_A].astype(BF16)
            s_c = _dot_nt(qhm, kc[:, c0:c0 + DQK_A]) + bc_ref[h]
            s_n = _dot_nt(qhm, kn[:, c0:c0 + DQK_A]) + bn_ref[h]
            mx = jnp.maximum(jnp.max(s_c, axis=-1, keepdims=True), jnp.max(s_n, axis=-1, keepdims=True))
            p_c = jnp.exp(s_c - mx)
            p_n = jnp.exp(s_n - mx)
            l = jnp.sum(p_c, axis=-1, keepdims=True) + jnp.sum(p_n, axis=-1, keepdims=True)
            acc = (_dot(p_c.astype(BF16), vc[:, h * DV_A:(h + 1) * DV_A])
                   + _dot(p_n.astype(BF16), vn[:, h * DV_A:(h + 1) * DV_A]))
            res.append((l, acc))
        (l1, a1), (l2, a2) = res
        outs.append(_diff_out(a1, l1, a2, l2, lam, subln_ref[...], post_scale))
    o_ref[...] = jnp.concatenate(outs, axis=-1)


def _attn_a_sample(lam, q, kc, vc, kn, vn, bias_c, bias_n, subln, post_scale):
    b, t, w = q.shape
    p = kc.shape[1]
    body = functools.partial(_attn_a_sample_body, post_scale=post_scale)
    per_b = lambda bi: (bi, 0, 0)
    const3 = lambda bi: (0, 0, 0)
    return pl.pallas_call(
        body,
        grid=(b,),
        in_specs=[pl.BlockSpec(memory_space=pltpu.SMEM),
                  pl.BlockSpec((None, t, w), per_b),
                  pl.BlockSpec((None, p, w), per_b), pl.BlockSpec((None, p, w), per_b),
                  pl.BlockSpec((None, t, w), per_b), pl.BlockSpec((None, t, w), per_b),
                  pl.BlockSpec(bias_c.shape, const3), pl.BlockSpec(bias_n.shape, const3),
                  pl.BlockSpec((1, DV_A), lambda bi: (0, 0))],
        out_specs=pl.BlockSpec((None, t, w), per_b),
        out_shape=jax.ShapeDtypeStruct((b, t, w), F32),
        compiler_params=_params(("parallel",)),
        name="attn_a_sample",
    )(lam, q, kc, vc, kn, vn, bias_c, bias_n, subln)


def _band_body(q_ref, k_ref, v_ref, bias_ref, o_ref, kbf, vbf, *, qb, n_invalid):
    i = pl.program_id(1)

    @pl.when(i == 0)
    def _():
        kbf[...] = k_ref[...].astype(BF16)
        vbf[...] = v_ref[...].astype(BF16)

    def chunk(c, carry):
        r0 = pl.multiple_of(c * CHUNK, CHUNK)
        p0 = pl.multiple_of(i * qb + c * CHUNK, CHUNK)
        q = (q_ref[pl.ds(r0, CHUNK), :] * (D_B ** -0.5)).astype(BF16)
        kb = kbf[pl.ds(p0, BAND), :]
        vb = vbf[pl.ds(p0, BAND), :]
        if n_invalid:
            valid = (p0 + lax.broadcasted_iota(jnp.int32, (CHUNK, BAND), 1)) >= n_invalid
        outs = []
        for h in range(H_B):
            sl = slice(h * D_B, (h + 1) * D_B)
            s = _dot_nt(q[:, sl], kb[:, sl]) + bias_ref[h]
            if n_invalid:
                s = jnp.where(valid, s, NEG_INF)
            mx = jnp.max(s, axis=-1, keepdims=True)
            p = jnp.exp(s - mx)
            l = jnp.sum(p, axis=-1, keepdims=True)
            outs.append(_dot(p.astype(BF16), vb[:, sl]) / l)
        o_ref[pl.ds(r0, CHUNK), :] = jnp.concatenate(outs, axis=-1)
        return carry

    lax.fori_loop(0, qb // CHUNK, chunk, 0)


def _band_attn(q, k, v, bias, n_invalid):
    b, t, w = q.shape
    tk = k.shape[1]
    qb = min(8 * CHUNK, t)
    body = functools.partial(_band_body, qb=qb, n_invalid=n_invalid)
    return pl.pallas_call(
        body,
        grid=(b, t // qb),
        in_specs=[pl.BlockSpec((None, qb, w), lambda bi, i: (bi, i, 0)),
                  pl.BlockSpec((None, tk, w), lambda bi, i: (bi, 0, 0)),
                  pl.BlockSpec((None, tk, w), lambda bi, i: (bi, 0, 0)),
                  pl.BlockSpec(bias.shape, lambda bi, i: (0, 0, 0))],
        out_specs=pl.BlockSpec((None, qb, w), lambda bi, i: (bi, i, 0)),
        out_shape=jax.ShapeDtypeStruct((b, t, w), F32),
        scratch_shapes=[pltpu.VMEM((tk, w), BF16), pltpu.VMEM((tk, w), BF16)],
        compiler_params=_params(("parallel", "arbitrary")),
        name="band_attn",
    )(q, k, v, bias)


_SMALL_DECAY0 = H_C


def _split2(x):
    hi = x.astype(BF16)
    return hi, (x - hi.astype(F32)).astype(BF16)


def _gdn_body(cqkv_ref, small_ref, gate_ref, cprev_ref, s0_ref, convw_ref, alog_ref, dtb_ref, onorm_ref,
              tri_ref, bo_ref, eb_ref, eg_ref, oc_ref, sout_ref, xext, s_scr, *, tc):
    j = pl.program_id(1)
    n_ch = tc // CHUNK
    n_pair = H_C // 2
    pad = 8
    lo = pad - (CONV_W - 1)

    @pl.when(j == 0)
    def _():
        xext[lo:pad, :] = cprev_ref[...]
        z = jnp.zeros((D_C, D_C), F32)
        for p in range(n_pair):
            s_scr[p] = jnp.concatenate([jnp.concatenate([s0_ref[2 * p], z], axis=1),
                                        jnp.concatenate([z, s0_ref[2 * p + 1]], axis=1)], axis=0)

    @pl.when(j > 0)
    def _():
        xext[lo:pad, :] = xext[tc + lo:tc + pad, :]

    xext[pad:, :] = cqkv_ref[...]
    y = convw_ref[0:1, :] * xext[lo:lo + tc, :]
    for w in range(1, CONV_W):
        y = y + convw_ref[w:w + 1, :] * xext[lo + w:lo + w + tc, :]
    y = _silu(y)

    bo = bo_ref[...]

    def head_sumsq(x):
        hi, lo_ = _split2(x * x)
        return jnp.concatenate([_dot(hi[:, t * LANES:(t + 1) * LANES], bo) + _dot(lo_[:, t * LANES:(t + 1) * LANES], bo)
                                for t in range(n_pair)], axis=1)

    yq, yk, yv = y[:, :C_W], y[:, C_W:2 * C_W], y[:, 2 * C_W:]
    qn = yq * lax.rsqrt(head_sumsq(yq) + L2_EPS) * (D_C ** -0.5)
    kn = yk * lax.rsqrt(head_sumsq(yk) + L2_EPS)

    sm = small_ref[...]
    lane_s = lax.broadcasted_iota(jnp.int32, (tc, LANES), 1)
    is_dec = (lane_s >= _SMALL_DECAY0) & (lane_s < _SMALL_DECAY0 + H_C)
    log_a = jnp.where(is_dec, -jnp.exp(alog_ref[...]) * _softplus(sm + dtb_ref[...]), 0.0)
    tri = tri_ref[...]
    g_full = sum(_dot(tri, part) for part in _split3(log_a))
    gc = sum(_dot(part, eg_ref[...]) for part in _split3(g_full))
    bc = sum(_dot(part, eb_ref[...]) for part in _split3(_sigmoid(sm)))
    egc = jnp.exp(gc)
    xk = bc * egc * kn
    vb = bc * yv
    qe = egc * qn

    ii = lax.broadcasted_iota(jnp.int32, (CHUNK, LANES), 0)
    ln = lax.broadcasted_iota(jnp.int32, (CHUNK, LANES), 1)
    jn = ln % D_C
    incl2 = ii >= jn
    strict2 = ii > jn
    eye2 = ii == jn
    left = ln < D_C
    eye2f = jnp.where(eye2, 1.0, 0.0)
    r128 = lax.broadcasted_iota(jnp.int32, (LANES, LANES), 0)
    c128 = lax.broadcasted_iota(jnp.int32, (LANES, LANES), 1)
    on_diag_blocks = (r128 < D_C) == (c128 < D_C)
    eye128 = jnp.where(r128 == c128, 1.0, 0.0).astype(BF16)

    def bdiag(x):
        return jnp.concatenate([jnp.where(left, x, 0.0), jnp.where(left, 0.0, x)], axis=0)

    pre = []
    for r in range(n_ch):
        rows = slice(r * CHUNK, (r + 1) * CHUNK)
        g_last = gc[(r + 1) * CHUNK - 1:(r + 1) * CHUNK, :]
        kd_r = kn[rows] * jnp.exp(g_last - gc[rows])
        eg_last = jnp.exp(g_last)
        per_pair = []
        for p in range(n_pair):
            tile = slice(p * LANES, (p + 1) * LANES)
            gcp = gc[rows, tile]
            g_row = jnp.sum(jnp.where(eye2, gcp, 0.0), axis=0, keepdims=True)
            dec = jnp.where(incl2, jnp.exp(jnp.where(incl2, gcp - g_row, 0.0)), 0.0)
            qp = qn[rows, tile]
            kp = kn[rows, tile]
            r_k = _dot_nt(jnp.concatenate([qp, kp], axis=0).astype(BF16), bdiag(kp).astype(BF16))
            qk, kk = r_k[:CHUNK], r_k[CHUNK:]
            nm = -(bc[rows, tile] * jnp.where(strict2, dec, 0.0) * kk)
            pw = _dot(nm.astype(BF16), bdiag(nm).astype(BF16))
            tm_ = eye2f + nm
            for _ in range(4):
                st = _dot(jnp.concatenate([tm_, pw], axis=0).astype(BF16), bdiag(pw).astype(BF16))
                tm_ = tm_ + st[:CHUNK]
                pw = st[CHUNK:]
            tm_ = tm_ + _dot(tm_.astype(BF16), bdiag(pw).astype(BF16))
            wu = _dot(tm_.astype(BF16),
                      jnp.concatenate([bdiag(xk[rows, tile]), bdiag(vb[rows, tile])], axis=1).astype(BF16))
            kdt = _dot_nt(eye128, bdiag(kd_r[:, tile]).astype(BF16)).astype(BF16)
            per_pair.append(dict(w=wu[:, :LANES].astype(BF16), u1=wu[:, LANES:], qkd=(dec * qk).astype(BF16),
                                 kdt=kdt, qe=qe[rows, tile].astype(BF16), eg_last=eg_last[:, tile]))
        pre.append(per_pair)

    states = [s_scr[p] for p in range(n_pair)]
    o_rows = []
    for r in range(n_ch):
        o_tiles = []
        for p in range(n_pair):
            c = pre[r][p]
            res = _dot(jnp.concatenate([c["qe"], c["w"]], axis=0), states[p].astype(BF16))
            u = c["u1"] - res[CHUNK:]
            o_tiles.append(res[:CHUNK] + _dot(c["qkd"], bdiag(u).astype(BF16)))
            upd = _dot(c["kdt"], jnp.concatenate([u, u], axis=0).astype(BF16))
            states[p] = c["eg_last"] * states[p] + jnp.where(on_diag_blocks, upd, 0.0)
        o_rows.append(jnp.concatenate(o_tiles, axis=1))
    for p in range(n_pair):
        s_scr[p] = states[p]
    o = jnp.concatenate(o_rows, axis=0) if n_ch > 1 else o_rows[0]
    ms = head_sumsq(o) * (1.0 / D_C)
    oc_ref[...] = o * lax.rsqrt(ms + NORM_EPS) * onorm_ref[...] * _silu(gate_ref[...])

    @pl.when(j == pl.num_programs(1) - 1)
    def _():
        for p in range(n_pair):
            sout_ref[2 * p] = states[p][:D_C, :D_C]
            sout_ref[2 * p + 1] = states[p][D_C:, D_C:]


def _gdn_constants(tc):
    i = jnp.arange(tc, dtype=jnp.int32)
    tri = ((i[None, :] <= i[:, None]) & (i[None, :] // CHUNK == i[:, None] // CHUNK)).astype(BF16)
    l = jnp.arange(LANES, dtype=jnp.int32)
    block_ones = (l[:, None] // D_C == l[None, :] // D_C).astype(BF16)
    col_head = jnp.arange(C_W, dtype=jnp.int32)[None, :] // D_C
    e_beta = (l[:, None] == col_head).astype(BF16)
    e_g = (l[:, None] == col_head + _SMALL_DECAY0).astype(BF16)
    return tri, block_ones, e_beta, e_g


def _gdn(cqkv, small, gate, conv_prev, s0, conv_w, alog_vec, dtb_vec, onorm_tiled):
    b, t, w3 = cqkv.shape
    tc = min(4 * CHUNK, t)
    consts = _gdn_constants(tc)
    body = functools.partial(_gdn_body, tc=tc)
    blk = lambda bi, j: (bi, j, 0)
    per_b3 = lambda bi, j: (bi, 0, 0)
    per_b4 = lambda bi, j: (bi, 0, 0, 0)
    const2 = lambda bi, j: (0, 0)
    return pl.pallas_call(
        body,
        grid=(b, t // tc),
        in_specs=[pl.BlockSpec((None, tc, w3), blk),
                  pl.BlockSpec((None, tc, LANES), blk),
                  pl.BlockSpec((None, tc, C_W), blk),
                  pl.BlockSpec((None, CONV_W - 1, w3), per_b3),
                  pl.BlockSpec((None, H_C, D_C, D_C), per_b4),
                  pl.BlockSpec(conv_w.shape, const2),
                  pl.BlockSpec((1, LANES), const2),
                  pl.BlockSpec((1, LANES), const2),
                  pl.BlockSpec((1, C_W), const2)]
                 + [pl.BlockSpec(c.shape, const2) for c in consts],
        out_specs=[pl.BlockSpec((None, tc, C_W), blk),
                   pl.BlockSpec((None, H_C, D_C, D_C), per_b4)],
        out_shape=[jax.ShapeDtypeStruct((b, t, C_W), F32),
                   jax.ShapeDtypeStruct((b, H_C, D_C, D_C), F32)],
        scratch_shapes=[pltpu.VMEM((tc + 8, w3), F32), pltpu.VMEM((H_C // 2, LANES, LANES), F32)],
        compiler_params=_params(("parallel", "arbitrary")),
        name="gdn",
    )(cqkv, small, gate, conv_prev, s0, conv_w, alog_vec, dtb_vec, onorm_tiled, *consts)


def _top2_gates(logits):
    lane = lax.broadcasted_iota(jnp.int32, logits.shape, 1).astype(F32)
    low = -3.0e38
    lg = jnp.where(lane < N_EXPERTS, logits, low)
    m1 = jnp.max(lg, axis=-1, keepdims=True)
    i1 = jnp.min(jnp.where(lg == m1, lane, float(LANES)), axis=-1, keepdims=True)
    lg2 = jnp.where(lane == i1, low, lg)
    m2 = jnp.max(lg2, axis=-1, keepdims=True)
    i2 = jnp.min(jnp.where(lg2 == m2, lane, float(LANES)), axis=-1, keepdims=True)
    e2 = jnp.exp(m2 - m1)
    den = 1.0 + e2
    return jnp.where(lane == i1, 1.0 / den, 0.0) + jnp.where(lane == i2, e2 / den, 0.0)


def _outproj_body(x_ref, oa_ref, ob_ref, oc_ref, w_ref, g_ref, *rest, with_router):
    if with_router:
        rhi_ref, rlo_ref, xo_ref, h_ref, gates_ref = rest
    else:
        xo_ref, h_ref = rest
    y = (_dot(oa_ref[...].astype(BF16), w_ref[0:HEAD_W, :])
         + _dot(ob_ref[...].astype(BF16), w_ref[HEAD_W:2 * HEAD_W, :])
         + _dot(oc_ref[...].astype(BF16), w_ref[2 * HEAD_W:, :]))
    x = x_ref[...] + y
    xo_ref[...] = x
    hf = _rms(x, g_ref[...])
    hb = hf.astype(BF16)
    h_ref[...] = hb
    if with_router:
        lo = (hf - hb.astype(F32)).astype(BF16)
        logits = _dot(hb, rhi_ref[...]) + _dot(lo, rhi_ref[...]) + _dot(hb, rlo_ref[...])
        gates_ref[...] = _top2_gates(logits)


def _outproj(x, oa, ob, oc, w_out, g, router=None):
    n, d = x.shape
    tm = min(ROW_TILE, n)
    row = lambda i: (i, 0)
    const = lambda i: (0, 0)
    in_specs = [pl.BlockSpec((tm, d), row), pl.BlockSpec((tm, HEAD_W), row), pl.BlockSpec((tm, HEAD_W), row),
                pl.BlockSpec((tm, C_W), row), pl.BlockSpec(w_out.shape, const), pl.BlockSpec((1, d), const)]
    out_specs = [pl.BlockSpec((tm, d), row), pl.BlockSpec((tm, d), row)]
    out_shape = [jax.ShapeDtypeStruct((n, d), F32), jax.ShapeDtypeStruct((n, d), BF16)]
    args = [x, oa, ob, oc, w_out, g]
    if router is not None:
        in_specs += [pl.BlockSpec(router[0].shape, const), pl.BlockSpec(router[1].shape, const)]
        out_specs.append(pl.BlockSpec((tm, LANES), row))
        out_shape.append(jax.ShapeDtypeStruct((n, LANES), F32))
        args += list(router)
    return pl.pallas_call(
        functools.partial(_outproj_body, with_router=router is not None),
        grid=(n // tm,),
        in_specs=in_specs, out_specs=out_specs, out_shape=out_shape,
        compiler_params=_params(("parallel",)),
        name="outproj",
    )(*args)


def _ffn_body(h_ref, x_ref, wg_ref, wu_ref, wd_ref, o_ref, acc_ref, *, ff_chunk):
    h = h_ref[...]
    d_ff = wg_ref.shape[1]
    acc_ref[...] = x_ref[...]
    for c0 in range(0, d_ff, ff_chunk):
        a = _silu(_dot(h, wg_ref[:, c0:c0 + ff_chunk])) * _dot(h, wu_ref[:, c0:c0 + ff_chunk])
        acc_ref[...] += _dot(a.astype(BF16), wd_ref[c0:c0 + ff_chunk, :])
    o_ref[...] = acc_ref[...]


def _ffn_dense(h, x, wg, wu, wd):
    n, d = x.shape
    d_ff = wg.shape[1]
    tm = min(ROW_TILE, n)
    ff_chunk = 256 if d_ff % 256 == 0 else LANES
    row = lambda i: (i, 0)
    const = lambda i: (0, 0)
    return pl.pallas_call(
        functools.partial(_ffn_body, ff_chunk=ff_chunk),
        grid=(n // tm,),
        in_specs=[pl.BlockSpec((tm, d), row), pl.BlockSpec((tm, d), row),
                  pl.BlockSpec(wg.shape, const), pl.BlockSpec(wu.shape, const), pl.BlockSpec(wd.shape, const)],
        out_specs=pl.BlockSpec((tm, d), row),
        out_shape=jax.ShapeDtypeStruct((n, d), F32),
        scratch_shapes=[pltpu.VMEM((tm, d), F32)],
        compiler_params=_params(("parallel",)),
        name="ffn_dense",
    )(h, x, wg, wu, wd)


def _moe_body(cnt_ref, h_ref, gates_ref, gates_t_ref, wg_ref, wu_ref, wd_ref, y_ref,
              rank_col, rank_row, hs, eo, *, tg, n_f):
    g = pl.program_id(0)
    e = pl.program_id(1)
    f = pl.program_id(2)
    tm = MOE_ROW_TILE
    tb = MOE_TOKEN_BLOCK
    n_tb = tg // tb
    n_tiles = (cnt_ref[g * N_EXPERTS + e] + tm - 1) // tm

    @pl.when((e == 0) & (f == 0))
    def _():
        y_ref[...] = jnp.zeros_like(y_ref)
        ii = lax.broadcasted_iota(jnp.int32, (tb, tb), 0)
        jj = lax.broadcasted_iota(jnp.int32, (tb, tb), 1)
        lower = jnp.where(ii > jj, 1.0, 0.0).astype(BF16)
        upper = jnp.where(ii < jj, 1.0, 0.0).astype(BF16)
        carry_c = jnp.zeros((1, LANES), F32)
        carry_r = jnp.zeros((N_EXPERTS, 1), F32)
        for b in range(n_tb):
            mc = jnp.where(gates_ref[b * tb:(b + 1) * tb, :] > 0.0, 1.0, 0.0)
            rank_col[b * tb:(b + 1) * tb, :] = _dot(lower, mc.astype(BF16)) + carry_c
            carry_c = carry_c + jnp.sum(mc, axis=0, keepdims=True)
            mr = jnp.where(gates_t_ref[:, b * tb:(b + 1) * tb] > 0.0, 1.0, 0.0)
            rank_row[:, b * tb:(b + 1) * tb] = _dot(mr.astype(BF16), upper) + carry_r
            carry_r = carry_r + jnp.sum(mr, axis=1, keepdims=True)

    @pl.when(f == 0)
    def _():
        def gather(t, carry):
            r0 = pl.multiple_of(t * tm, tm)
            rows = (r0 + lax.broadcasted_iota(jnp.int32, (tm, tb), 0)).astype(F32)
            acc = jnp.zeros((tm, h_ref.shape[1]), F32)
            for b in range(n_tb):
                gr = gates_t_ref[pl.ds(e, 1), b * tb:(b + 1) * tb]
                rr = jnp.where(gr > 0.0, rank_row[pl.ds(e, 1), b * tb:(b + 1) * tb], -1.0)
                onehot = jnp.where(rr == rows, 1.0, 0.0).astype(BF16)
                acc = acc + _dot(onehot, h_ref[b * tb:(b + 1) * tb, :])
            hs[pl.ds(r0, tm), :] = acc.astype(BF16)
            return carry

        lax.fori_loop(0, n_tiles, gather, 0)

    def expert(t, carry):
        r0 = pl.multiple_of(t * tm, tm)
        x = hs[pl.ds(r0, tm), :]
        a = _silu(_dot(x, wg_ref[...])) * _dot(x, wu_ref[...])
        part = _dot(a.astype(BF16), wd_ref[...])

        @pl.when(f == 0)
        def _():
            eo[pl.ds(r0, tm), :] = part

        @pl.when(f > 0)
        def _():
            eo[pl.ds(r0, tm), :] += part

        return carry

    lax.fori_loop(0, n_tiles, expert, 0)

    @pl.when(f == n_f - 1)
    def _():
        lane = lax.broadcasted_iota(jnp.int32, (tb, LANES), 1)

        def scatter(t, carry):
            r0 = pl.multiple_of(t * tm, tm)
            cols = (r0 + lax.broadcasted_iota(jnp.int32, (tb, tm), 1)).astype(F32)
            out = eo[pl.ds(r0, tm), :].astype(BF16)
            for b in range(n_tb):
                sel = lane == e
                gc = jnp.sum(jnp.where(sel, gates_ref[b * tb:(b + 1) * tb, :], 0.0), axis=-1, keepdims=True)
                rc = jnp.sum(jnp.where(sel, rank_col[b * tb:(b + 1) * tb, :], 0.0), axis=-1, keepdims=True)
                rc = jnp.where(gc > 0.0, rc, -1.0)
                weighted = jnp.where(rc == cols, gc, 0.0).astype(BF16)
                y_ref[b * tb:(b + 1) * tb, :] += _dot(weighted, out)
            return carry

        lax.fori_loop(0, n_tiles, scatter, 0)


def _moe(counts, h, gates, gates_t, wg, wu, wd, tg):
    n, d = h.shape
    n_e, _, d_ff = wg.shape
    n_f = 2
    ffh = d_ff // n_f
    body = functools.partial(_moe_body, tg=tg, n_f=n_f)
    grid_spec = pltpu.PrefetchScalarGridSpec(
        num_scalar_prefetch=1,
        grid=(n // tg, n_e, n_f),
        in_specs=[pl.BlockSpec((tg, d), lambda g, e, f, c: (g, 0)),
                  pl.BlockSpec((tg, LANES), lambda g, e, f, c: (g, 0)),
                  pl.BlockSpec((N_EXPERTS, tg), lambda g, e, f, c: (0, g)),
                  pl.BlockSpec((None, d, ffh), lambda g, e, f, c: (e, 0, f)),
                  pl.BlockSpec((None, d, ffh), lambda g, e, f, c: (e, 0, f)),
                  pl.BlockSpec((None, ffh, d), lambda g, e, f, c: (e, f, 0))],
        out_specs=pl.BlockSpec((tg, d), lambda g, e, f, c: (g, 0)),
        scratch_shapes=[pltpu.VMEM((tg, LANES), F32), pltpu.VMEM((N_EXPERTS, tg), F32),
                        pltpu.VMEM((tg, d), BF16), pltpu.VMEM((tg, d), F32)],
    )
    return pl.pallas_call(
        body,
        grid_spec=grid_spec,
        out_shape=jax.ShapeDtypeStruct((n, d), F32),
        compiler_params=_params(("parallel", "arbitrary", "arbitrary")),
        name="moe",
    )(counts, h, gates, gates_t, wg, wu, wd)


def _final_body(x_ref, y_ref, g_ref, o_ref):
    o_ref[...] = _rms(x_ref[...] + y_ref[...], g_ref[...])


def _final_norm(x, y, g):
    n, d = x.shape
    tm = min(ROW_TILE, n)
    row = lambda i: (i, 0)
    return pl.pallas_call(
        _final_body,
        grid=(n // tm,),
        in_specs=[pl.BlockSpec((tm, d), row), pl.BlockSpec((tm, d), row), pl.BlockSpec((1, d), lambda i: (0, 0))],
        out_specs=pl.BlockSpec((tm, d), row),
        out_shape=jax.ShapeDtypeStruct((n, d), F32),
        compiler_params=_params(("parallel",)),
        name="final_norm",
    )(x, y, g)


def _t5_bucket(rel):
    nb = T5_BUCKETS // 2
    max_exact = nb // 2
    ret = jnp.where(rel > 0, nb, 0)
    n = jnp.abs(rel)
    nf = jnp.maximum(n, 1).astype(F32)
    large = max_exact + (jnp.log(nf / max_exact) / math.log(T5_MAX_DIST / max_exact) * (nb - max_exact)).astype(jnp.int32)
    large = jnp.minimum(large, nb - 1)
    return ret + jnp.where(n < max_exact, n, large)


def _t5_table(t5_bias, q_pos, k_pos):
    bias = jnp.transpose(t5_bias[_t5_bucket(k_pos[None, :] - q_pos[:, None])], (2, 0, 1)).astype(F32)
    mask = (k_pos[None, :] // CHUNK) <= (q_pos[:, None] // CHUNK)
    return bias, mask[None]


def _band_table(rel_bias):
    qi = jnp.arange(CHUNK, dtype=jnp.int32)
    kj = jnp.arange(BAND, dtype=jnp.int32) - B_LEFT_CHUNKS * CHUNK
    rel = jnp.clip(kj[None, :] - qi[:, None], -B_REL_CLIP, B_REL_CLIP) + B_REL_CLIP
    return jnp.transpose(rel_bias[rel], (2, 0, 1)).astype(F32)


def _lane_vec(v, offset):
    return jnp.zeros((1, LANES), F32).at[0, offset:offset + v.shape[0]].set(v.astype(F32))


def _moe_group(n):
    for tg in (1024, 512, 256):
        if n % tg == 0:
            return tg
    raise ValueError(f"token count {n} is not a multiple of 256")


def kernel(x_prompt, x_sample, cache_a_k, cache_a_v, cache_b_k, cache_b_v, cache_c_conv, state_c_ssm, w_in, w_out, norm_mix, norm_ffn, norm_final, lam_qk, subln_a, t5_bias, rel_bias_b, conv_c, a_log_c, dt_bias_c, onorm_c, ffn_gate, ffn_up, ffn_down, moe_router, moe_gate, moe_up, moe_down):
    depth = w_in.shape[0]
    bp, sp, d = x_prompt.shape
    bs, ss, _ = x_sample.shape
    past = cache_a_k.shape[2]
    nb_cache = cache_b_k.shape[2]
    assert ss == CHUNK and nb_cache == B_LEFT_CHUNKS * CHUNK and sp >= B_LEFT_CHUNKS * CHUNK

    xp = x_prompt.reshape(bp * sp, d)
    xs = x_sample.reshape(bs * ss, d)
    blk = min(A_BLOCK, sp)

    pos_blk = jnp.arange(blk, dtype=jnp.int32)
    far_bias = t5_bias[_t5_bucket(jnp.int32(-(blk + 1)))].astype(F32)
    bd, md = _t5_table(t5_bias, pos_blk, pos_blk)
    bias_diag = jnp.where(md, bd - far_bias[:, None, None], NEG_INF)
    bsub, _ = _t5_table(t5_bias, blk + pos_blk, pos_blk)
    bias_sub = bsub - far_bias[:, None, None]
    q_pos_s = past + jnp.arange(ss, dtype=jnp.int32)
    bfull, mfull = _t5_table(t5_bias, q_pos_s, jnp.arange(past + ss, dtype=jnp.int32))
    bias_s = jnp.where(mfull, bfull, NEG_INF)
    bias_s_cache, bias_s_new = bias_s[:, :, :past], bias_s[:, :, past:]

    zeros_conv = jnp.zeros((bp, CONV_W - 1, 3 * C_W), F32)
    zeros_state = jnp.zeros((bp, H_C, D_C, D_C), F32)

    p_states = [[] for _ in range(6)]
    s_states = [[] for _ in range(6)]
    yp = ys = None
    for l in range(depth):
        if yp is not None:
            xp, xs, yp, ys = xp + yp, xs + ys, None, None
        w = w_in[l]
        n_main = 6 * HEAD_W + 3 * C_W
        w_main = jnp.concatenate([w[:, :n_main], w[:, n_main + 2 * H_C:]], axis=1).astype(BF16)
        w_small = jnp.zeros((d, LANES), F32).at[:, :2 * H_C].set(w[:, n_main:n_main + 2 * H_C]).astype(BF16)
        w_out_l = w_out[l].astype(BF16)
        g_mix = norm_mix[l].reshape(1, d)
        g_ffn = norm_ffn[l].reshape(1, d)
        lam_init = 0.8 - 0.6 * math.exp(-0.3 * l)
        lq = lam_qk[l].astype(F32)
        lam = (jnp.exp(jnp.sum(lq[0] * lq[1])) - jnp.exp(jnp.sum(lq[2] * lq[3])) + lam_init).reshape(1)
        subln = subln_a[l].reshape(1, DV_A)
        band_bias = _band_table(rel_bias_b[l])
        alog_vec = _lane_vec(a_log_c[l], _SMALL_DECAY0)
        dtb_vec = _lane_vec(dt_bias_c[l], _SMALL_DECAY0)
        onorm = jnp.tile(onorm_c[l].astype(F32), H_C).reshape(1, C_W)
        is_moe = l % 2 == 1
        if is_moe:
            r = jnp.zeros((d, LANES), F32).at[:, :N_EXPERTS].set(moe_router[l // 2])
            r_hi = r.astype(BF16)
            router = (r_hi, (r - r_hi.astype(F32)).astype(BF16))
            e_wg, e_wu, e_wd = (moe_gate[l // 2].astype(BF16), moe_up[l // 2].astype(BF16),
                                moe_down[l // 2].astype(BF16))
        else:
            router = None
            f_wg, f_wu, f_wd = (ffn_gate[l // 2].astype(BF16), ffn_up[l // 2].astype(BF16),
                                ffn_down[l // 2].astype(BF16))

        new_x = []
        for is_prompt, x in ((True, xp), (False, xs)):
            b, t = (bp, sp) if is_prompt else (bs, ss)
            aq, ak, av, bq, bk, bv, cqkv, cgate, csmall = _inproj(x, g_mix, w_main, w_small)
            r3 = lambda a: a.reshape(b, t, a.shape[-1])
            if is_prompt:
                oa = _attn_a_prompt(lam, r3(aq), r3(ak), r3(av), jnp.swapaxes(bias_diag, 1, 2),
                                    jnp.swapaxes(bias_sub, 1, 2), subln.reshape(DV_A, 1), 1.0 - lam_init)
                pad = ((0, 0), (B_LEFT_CHUNKS * CHUNK, 0), (0, 0))
                ob = _band_attn(r3(bq), jnp.pad(r3(bk), pad), jnp.pad(r3(bv), pad), band_bias,
                                B_LEFT_CHUNKS * CHUNK)
                conv_prev, s0 = zeros_conv, zeros_state
            else:
                oa = _attn_a_sample(lam, r3(aq), cache_a_k[l].reshape(b, past, HEAD_W),
                                    cache_a_v[l].reshape(b, past, HEAD_W), r3(ak), r3(av),
                                    bias_s_cache, bias_s_new, subln, 1.0 - lam_init)
                kb = jnp.concatenate([cache_b_k[l].reshape(b, nb_cache, HEAD_W), r3(bk)], axis=1)
                vb = jnp.concatenate([cache_b_v[l].reshape(b, nb_cache, HEAD_W), r3(bv)], axis=1)
                ob = _band_attn(r3(bq), kb, vb, band_bias, 0)
                conv_prev, s0 = cache_c_conv[l], state_c_ssm[l]
            oc, s_new = _gdn(r3(cqkv), r3(csmall), r3(cgate), conv_prev, s0, conv_c[l], alog_vec, dtb_vec, onorm)
            n = b * t
            res = _outproj(x, oa.reshape(n, HEAD_W), ob.reshape(n, HEAD_W), oc.reshape(n, C_W), w_out_l, g_ffn,
                           router)
            if is_moe:
                x_new, h2, gates = res
                tg = _moe_group(n)
                counts = jnp.sum((gates[:, :N_EXPERTS] > 0.0).reshape(n // tg, tg, N_EXPERTS), axis=1,
                                 dtype=jnp.int32).reshape(-1)
                y_moe = _moe(counts, h2, gates, gates[:, :N_EXPERTS].T, e_wg, e_wu, e_wd, tg)
                new_x.append((x_new, y_moe))
            else:
                x_new, h2 = res
                new_x.append((_ffn_dense(h2, x_new, f_wg, f_wu, f_wd), None))
            keep = min(B_LEFT_CHUNKS * CHUNK, t)
            states = (ak.reshape(b, t, H_A, 2 * DQK_A), av.reshape(b, t, H_A, DV_A),
                      bk.reshape(b, t, H_B, D_B)[:, t - keep:], bv.reshape(b, t, H_B, D_B)[:, t - keep:],
                      r3(cqkv)[:, t - (CONV_W - 1):], s_new)
            for i in range(6):
                (p_states if is_prompt else s_states)[i].append(states[i])
        (xp, yp), (xs, ys) = new_x

    g_fin = norm_final.reshape(1, d)
    if yp is None:
        yp, ys = jnp.zeros_like(xp), jnp.zeros_like(xs)
    y_prompt = _final_norm(xp, yp, g_fin).reshape(bp, sp, d)
    y_sample = _final_norm(xs, ys, g_fin).reshape(bs, ss, d)
    p_out = [jnp.stack(s, axis=0) for s in p_states]
    s_out = [jnp.stack(s, axis=0) for s in s_states]
    return (y_prompt, y_sample, *p_out, *s_out)
```

```python
import functools
import math

import jax
import jax.numpy as jnp
from jax import lax
from jax.experimental import pallas as pl
from jax.experimental.pallas import tpu as pltpu

F32 = jnp.float32
BF16 = jnp.bfloat16

CHUNK = 64
H_A = 4
DQK_A = 32
DV_A = 64
H_B = 4
D_B = 64
B_LEFT_CHUNKS = 8
B_REL_CLIP = 128
H_C = 8
D_C = 64
CONV_W = 4
T5_BUCKETS = 32
T5_MAX_DIST = 128
N_EXPERTS = 8
NORM_EPS = 1e-6
L2_EPS = 1e-6
NEG_INF = -1e30
HEAD_W = 256
C_W = H_C * D_C
BAND = (B_LEFT_CHUNKS + 1) * CHUNK
LANES = 128
VMEM_LIMIT = 56 * 1024 * 1024

A_BLOCK = 256
ROW_TILE = 512
MOE_ROW_TILE = 256
MOE_TOKEN_BLOCK = 256


def _params(sem):
    return pltpu.CompilerParams(dimension_semantics=sem, vmem_limit_bytes=VMEM_LIMIT)


def _rms(x, g):
    ms = jnp.mean(x * x, axis=-1, keepdims=True)
    return x * lax.rsqrt(ms + NORM_EPS) * g


def _sigmoid(x):
    return 1.0 / (1.0 + jnp.exp(-x))


def _silu(x):
    return x * _sigmoid(x)


def _softplus(x):
    return jnp.maximum(x, 0.0) + jnp.log1p(jnp.exp(-jnp.abs(x)))


def _dot(a, b):
    return jnp.dot(a, b, preferred_element_type=F32)


def _dot_nt(a, b):
    return lax.dot_general(a, b, (((1,), (1,)), ((), ())), preferred_element_type=F32)


def _dot_tn(a, b):
    return lax.dot_general(a, b, (((0,), (0,)), ((), ())), preferred_element_type=F32)


def _split3(x):
    x1 = x.astype(BF16)
    r1 = x - x1.astype(F32)
    x2 = r1.astype(BF16)
    x3 = (r1 - x2.astype(F32)).astype(BF16)
    return x1, x2, x3


def _inproj_body(x_ref, g_ref, w_ref, ws_ref, aq_ref, ak_ref, av_ref, bq_ref, bk_ref, bv_ref,
                 cqkv_ref, cgate_ref, csmall_ref):
    h = _rms(x_ref[...], g_ref[...]).astype(BF16)
    col = 0
    for ref in (aq_ref, ak_ref, av_ref, bq_ref, bk_ref, bv_ref, cqkv_ref, cgate_ref):
        width = ref.shape[-1]
        ref[...] = _dot(h, w_ref[:, col:col + width])
        col += width
    csmall_ref[...] = _dot(h, ws_ref[...])


def _inproj(x, g, w_main, w_small):
    n, d = x.shape
    tm = min(ROW_TILE, n)
    widths = (HEAD_W,) * 6 + (3 * C_W, C_W, LANES)
    row = lambda i: (i, 0)
    const = lambda i: (0, 0)
    return pl.pallas_call(
        _inproj_body,
        grid=(n // tm,),
        in_specs=[pl.BlockSpec((tm, d), row), pl.BlockSpec((1, d), const),
                  pl.BlockSpec(w_main.shape, const), pl.BlockSpec(w_small.shape, const)],
        out_specs=[pl.BlockSpec((tm, w), row) for w in widths],
        out_shape=[jax.ShapeDtypeStruct((n, w), F32) for w in widths],
        compiler_params=_params(("parallel",)),
        name="inproj",
    )(x, g, w_main, w_small)


def _diff_out(acc1, l1, acc2, l2, lam, subln, post_scale):
    o = acc1 / l1 - lam * (acc2 / l2)
    ms = jnp.mean(o * o, axis=-1, keepdims=True)
    return o * lax.rsqrt(ms + NORM_EPS) * subln * post_scale


A_ACC_ROWS = DV_A + 16
A_GROUPS = LANES // DQK_A


def _attn_a_prompt_body(lam_ref, q_ref, k_ref, v_ref, bd_ref, bs_ref, subln_ref, o_ref,
                        kbf, vte, wq, m_scr, acc_scr, *, blk, post_scale):
    i = pl.program_id(1)
    n_blk = k_ref.shape[0] // blk
    n_pair = H_A // 2

    @pl.when(i == 0)
    def _():
        ones_rows = jnp.where(lax.broadcasted_iota(jnp.int32, (A_ACC_ROWS - DV_A, blk), 0) == 0, 1.0, 0.0)
        for jb in range(n_blk):
            kbf[jb] = k_ref[jb * blk:(jb + 1) * blk, :].astype(BF16)
            vt = v_ref[jb * blk:(jb + 1) * blk, :].T
            for h in range(H_A):
                vte[jb, h, 0:DV_A, :] = vt[h * DV_A:(h + 1) * DV_A, :].astype(BF16)
                vte[jb, h, DV_A:, :] = ones_rows.astype(BF16)

    lam = lam_ref[0]
    qt = (q_ref[...] * (DQK_A ** -0.5)).T
    grp = lax.broadcasted_iota(jnp.int32, (LANES, blk), 0) // DQK_A
    for p in range(n_pair):
        qtp = qt[p * LANES:(p + 1) * LANES, :]
        wq[p] = jnp.concatenate([jnp.where(grp == g, qtp, 0.0) for g in range(A_GROUPS)], axis=1).astype(BF16)
    m_scr[...] = jnp.full(m_scr.shape, NEG_INF, F32)
    acc_scr[...] = jnp.zeros(acc_scr.shape, F32)

    def tiles(jb, bias_ref, off):
        s_alls = [_dot(kbf[jb, :, p * LANES:(p + 1) * LANES], wq[p]) for p in range(n_pair)]
        alphas, pts = [], []
        for g in range(n_pair * A_GROUPS):
            p, gi = divmod(g, A_GROUPS)
            s = s_alls[p][:, gi * blk:(gi + 1) * blk]
            if bias_ref is not None:
                s = s + bias_ref[2 * p + gi // 2]
            if off is not None:
                s = s + off
            m_old = m_scr[g:g + 1, :]
            m_new = jnp.maximum(m_old, jnp.max(s, axis=0, keepdims=True))
            alphas.append(jnp.exp(m_old - m_new))
            pts.append(jnp.exp(s - m_new).astype(BF16))
            m_scr[g:g + 1, :] = m_new
        pvs = [_dot(vte[jb, 2 * (g // A_GROUPS) + (g % A_GROUPS) // 2], pts[g]) for g in range(n_pair * A_GROUPS)]
        for g in range(n_pair * A_GROUPS):
            acc_scr[g] = alphas[g] * acc_scr[g] + pvs[g]

    tiles(i, bd_ref, None)
    tiles(jnp.maximum(i - 1, 0), bs_ref, jnp.where(i >= 1, 0.0, NEG_INF))

    def far(jb, carry):
        tiles(jb, None, None)
        return carry

    lax.fori_loop(0, jnp.maximum(i - 1, 0), far, 0)

    outs = []
    for h in range(H_A):
        g1 = (h // 2) * A_GROUPS + (h % 2) * 2
        a1 = acc_scr[g1]
        a2 = acc_scr[g1 + 1]
        o = a1[:DV_A] / a1[DV_A:DV_A + 1] - lam * (a2[:DV_A] / a2[DV_A:DV_A + 1])
        ms = jnp.mean(o * o, axis=0, keepdims=True)
        outs.append(o * lax.rsqrt(ms + NORM_EPS) * subln_ref[...] * post_scale)
    o_ref[...] = jnp.concatenate(outs, axis=0).T


def _attn_a_prompt(lam, q, k, v, bias_diag_t, bias_sub_t, subln_col, post_scale):
    b, s, w = q.shape
    blk = min(A_BLOCK, s)
    n_blk = s // blk
    body = functools.partial(_attn_a_prompt_body, blk=blk, post_scale=post_scale)
    return pl.pallas_call(
        body,
        grid=(b, n_blk),
        in_specs=[pl.BlockSpec(memory_space=pltpu.SMEM),
                  pl.BlockSpec((None, blk, w), lambda bi, i: (bi, i, 0)),
                  pl.BlockSpec((None, s, w), lambda bi, i: (bi, 0, 0)),
                  pl.BlockSpec((None, s, w), lambda bi, i: (bi, 0, 0)),
                  pl.BlockSpec(bias_diag_t.shape, lambda bi, i: (0, 0, 0)),
                  pl.BlockSpec(bias_sub_t.shape, lambda bi, i: (0, 0, 0)),
                  pl.BlockSpec((DV_A, 1), lambda bi, i: (0, 0))],
        out_specs=pl.BlockSpec((None, blk, w), lambda bi, i: (bi, i, 0)),
        out_shape=jax.ShapeDtypeStruct((b, s, w), F32),
        scratch_shapes=[pltpu.VMEM((n_blk, blk, w), BF16),
                        pltpu.VMEM((n_blk, H_A, A_ACC_ROWS, blk), BF16),
                        pltpu.VMEM((H_A // 2, LANES, A_GROUPS * blk), BF16),
                        pltpu.VMEM((H_A * 2, blk), F32),
                        pltpu.VMEM((H_A * 2, A_ACC_ROWS, blk), F32)],
        compiler_params=_params(("parallel", "arbitrary")),
        name="attn_a_prompt",
    )(lam, q, k, v, bias_diag_t, bias_sub_t, subln_col)


def _attn_a_sample_body(lam_ref, q_ref, kc_ref, vc_ref, kn_ref, vn_ref, bc_ref, bn_ref, subln_ref, o_ref,
                        *, post_scale):
    lam = lam_ref[0]
    q = q_ref[...] * (DQK_A ** -0.5)
    kc = kc_ref[...].astype(BF16)
    vc = vc_ref[...].astype(BF16)
    kn = kn_ref[...].astype(BF16)
    vn = vn_ref[...].astype(BF16)
    outs = []
    for h in range(H_A):
        res = []
        for m in range(2):
            c0 = h * 2 * DQK_A + m * DQK_A
            qhm = q[:, c0:c0 + DQK_A].astype(BF16)
            s_c = _dot_nt(qhm, kc[:, c0:c0 + DQK_A]) + bc_ref[h]
            s_n = _dot_nt(qhm, kn[:, c0:c0 + DQK_A]) + bn_ref[h]
            mx = jnp.maximum(jnp.max(s_c, axis=-1, keepdims=True), jnp.max(s_n, axis=-1, keepdims=True))
            p_c = jnp.exp(s_c - mx)
            p_n = jnp.exp(s_n - mx)
            l = jnp.sum(p_c, axis=-1, keepdims=True) + jnp.sum(p_n, axis=-1, keepdims=True)
            acc = (_dot(p_c.astype(BF16), vc[:, h * DV_A:(h + 1) * DV_A])
                   + _dot(p_n.astype(BF16), vn[:, h * DV_A:(h + 1) * DV_A]))
            res.append((l, acc))
        (l1, a1), (l2, a2) = res
        outs.append(_diff_out(a1, l1, a2, l2, lam, subln_ref[...], post_scale))
    o_ref[...] = jnp.concatenate(outs, axis=-1)


def _attn_a_sample(lam, q, kc, vc, kn, vn, bias_c, bias_n, subln, post_scale):
    b, t, w = q.shape
    p = kc.shape[1]
    body = functools.partial(_attn_a_sample_body, post_scale=post_scale)
    per_b = lambda bi: (bi, 0, 0)
    const3 = lambda bi: (0, 0, 0)
    return pl.pallas_call(
        body,
        grid=(b,),
        in_specs=[pl.BlockSpec(memory_space=pltpu.SMEM),
                  pl.BlockSpec((None, t, w), per_b),
                  pl.BlockSpec((None, p, w), per_b), pl.BlockSpec((None, p, w), per_b),
                  pl.BlockSpec((None, t, w), per_b), pl.BlockSpec((None, t, w), per_b),
                  pl.BlockSpec(bias_c.shape, const3), pl.BlockSpec(bias_n.shape, const3),
                  pl.BlockSpec((1, DV_A), lambda bi: (0, 0))],
        out_specs=pl.BlockSpec((None, t, w), per_b),
        out_shape=jax.ShapeDtypeStruct((b, t, w), F32),
        compiler_params=_params(("parallel",)),
        name="attn_a_sample",
    )(lam, q, kc, vc, kn, vn, bias_c, bias_n, subln)


def _band_body(q_ref, k_ref, v_ref, bias_ref, o_ref, kbf, vbf, *, qb, n_invalid):
    i = pl.program_id(1)

    @pl.when(i == 0)
    def _():
        kbf[...] = k_ref[...].astype(BF16)
        vbf[...] = v_ref[...].astype(BF16)

    n_chunk = qb // CHUNK
    unroll = 2 if n_chunk % 2 == 0 else 1
    heads = [slice(h * D_B, (h + 1) * D_B) for h in range(H_B)]

    def chunks(cc, carry):
        r0 = [pl.multiple_of((cc * unroll + k) * CHUNK, CHUNK) for k in range(unroll)]
        p0 = [pl.multiple_of(i * qb + (cc * unroll + k) * CHUNK, CHUNK) for k in range(unroll)]
        q = [(q_ref[pl.ds(r0[k], CHUNK), :] * (D_B ** -0.5)).astype(BF16) for k in range(unroll)]
        kb = [kbf[pl.ds(p0[k], BAND), :] for k in range(unroll)]
        vb = [vbf[pl.ds(p0[k], BAND), :] for k in range(unroll)]
        s = [[_dot_nt(q[k][:, sl], kb[k][:, sl]) for sl in heads] for k in range(unroll)]
        p, l = [], []
        for k in range(unroll):
            if n_invalid:
                valid = (p0[k] + lax.broadcasted_iota(jnp.int32, (CHUNK, BAND), 1)) >= n_invalid
            pk, lk = [], []
            for h in range(H_B):
                sh = s[k][h] + bias_ref[h]
                if n_invalid:
                    sh = jnp.where(valid, sh, NEG_INF)
                e = jnp.exp(sh - jnp.max(sh, axis=-1, keepdims=True))
                lk.append(jnp.sum(e, axis=-1, keepdims=True))
                pk.append(e.astype(BF16))
            p.append(pk)
            l.append(lk)
        o = [[_dot(p[k][h], vb[k][:, heads[h]]) for h in range(H_B)] for k in range(unroll)]
        for k in range(unroll):
            o_ref[pl.ds(r0[k], CHUNK), :] = jnp.concatenate([o[k][h] / l[k][h] for h in range(H_B)], axis=-1)
        return carry

    lax.fori_loop(0, n_chunk // unroll, chunks, 0)


def _band_attn(q, k, v, bias, n_invalid):
    b, t, w = q.shape
    tk = k.shape[1]
    qb = min(8 * CHUNK, t)
    body = functools.partial(_band_body, qb=qb, n_invalid=n_invalid)
    return pl.pallas_call(
        body,
        grid=(b, t // qb),
        in_specs=[pl.BlockSpec((None, qb, w), lambda bi, i: (bi, i, 0)),
                  pl.BlockSpec((None, tk, w), lambda bi, i: (bi, 0, 0)),
                  pl.BlockSpec((None, tk, w), lambda bi, i: (bi, 0, 0)),
                  pl.BlockSpec(bias.shape, lambda bi, i: (0, 0, 0))],
        out_specs=pl.BlockSpec((None, qb, w), lambda bi, i: (bi, i, 0)),
        out_shape=jax.ShapeDtypeStruct((b, t, w), F32),
        scratch_shapes=[pltpu.VMEM((tk, w), BF16), pltpu.VMEM((tk, w), BF16)],
        compiler_params=_params(("parallel", "arbitrary")),
        name="band_attn",
    )(q, k, v, bias)


_SMALL_DECAY0 = H_C


def _split2(x):
    hi = x.astype(BF16)
    return hi, (x - hi.astype(F32)).astype(BF16)


def _gdn_body(cqkv_ref, small_ref, gate_ref, cprev_ref, s0_ref, convw_ref, alog_ref, dtb_ref, onorm_ref,
              tri_ref, bo_ref, eb_ref, eg_ref, oc_ref, sout_ref, xext, s_scr, *, tc):
    j = pl.program_id(1)
    n_ch = tc // CHUNK
    n_pair = H_C // 2
    pad = 8
    lo = pad - (CONV_W - 1)

    @pl.when(j == 0)
    def _():
        xext[lo:pad, :] = cprev_ref[...]
        z = jnp.zeros((D_C, D_C), F32)
        for p in range(n_pair):
            s_scr[p] = jnp.concatenate([jnp.concatenate([s0_ref[2 * p], z], axis=1),
                                        jnp.concatenate([z, s0_ref[2 * p + 1]], axis=1)], axis=0)

    @pl.when(j > 0)
    def _():
        xext[lo:pad, :] = xext[tc + lo:tc + pad, :]

    xext[pad:, :] = cqkv_ref[...]
    y = convw_ref[0:1, :] * xext[lo:lo + tc, :]
    for w in range(1, CONV_W):
        y = y + convw_ref[w:w + 1, :] * xext[lo + w:lo + w + tc, :]
    y = _silu(y)

    bo = bo_ref[...]

    def head_sumsq(x):
        hi, lo_ = _split2(x * x)
        return jnp.concatenate([_dot(hi[:, t * LANES:(t + 1) * LANES], bo) + _dot(lo_[:, t * LANES:(t + 1) * LANES], bo)
                                for t in range(n_pair)], axis=1)

    yq, yk, yv = y[:, :C_W], y[:, C_W:2 * C_W], y[:, 2 * C_W:]
    qn = yq * lax.rsqrt(head_sumsq(yq) + L2_EPS) * (D_C ** -0.5)
    kn = yk * lax.rsqrt(head_sumsq(yk) + L2_EPS)

    sm = small_ref[...]
    lane_s = lax.broadcasted_iota(jnp.int32, (tc, LANES), 1)
    is_dec = (lane_s >= _SMALL_DECAY0) & (lane_s < _SMALL_DECAY0 + H_C)
    log_a = jnp.where(is_dec, -jnp.exp(alog_ref[...]) * _softplus(sm + dtb_ref[...]), 0.0)
    tri = tri_ref[...]
    g_full = sum(_dot(tri, part) for part in _split3(log_a))
    gc = sum(_dot(part, eg_ref[...]) for part in _split3(g_full))
    bc = sum(_dot(part, eb_ref[...]) for part in _split3(_sigmoid(sm)))
    egc = jnp.exp(gc)
    xk = bc * egc * kn
    vb = bc * yv
    qe = egc * qn

    ii = lax.broadcasted_iota(jnp.int32, (CHUNK, LANES), 0)
    ln = lax.broadcasted_iota(jnp.int32, (CHUNK, LANES), 1)
    jn = ln % D_C
    incl2 = ii >= jn
    strict2 = ii > jn
    eye2 = ii == jn
    left = ln < D_C
    eye2f = jnp.where(eye2, 1.0, 0.0)
    r128 = lax.broadcasted_iota(jnp.int32, (LANES, LANES), 0)
    c128 = lax.broadcasted_iota(jnp.int32, (LANES, LANES), 1)
    on_diag_blocks = (r128 < D_C) == (c128 < D_C)
    eye128 = jnp.where(r128 == c128, 1.0, 0.0).astype(BF16)

    def bdiag(x):
        return jnp.concatenate([jnp.where(left, x, 0.0), jnp.where(left, 0.0, x)], axis=0)

    units = [(r, p) for r in range(n_ch) for p in range(n_pair)]

    def rows_of(r):
        return slice(r * CHUNK, (r + 1) * CHUNK)

    def tile_of(p):
        return slice(p * LANES, (p + 1) * LANES)

    g_last = [gc[(r + 1) * CHUNK - 1:(r + 1) * CHUNK, :] for r in range(n_ch)]
    kd = [kn[rows_of(r)] * jnp.exp(g_last[r] - gc[rows_of(r)]) for r in range(n_ch)]
    eg_last = [jnp.exp(g) for g in g_last]
    dec, r_k, kdt = {}, {}, {}
    for r, p in units:
        gcp = gc[rows_of(r), tile_of(p)]
        g_row = jnp.sum(jnp.where(eye2, gcp, 0.0), axis=0, keepdims=True)
        dec[r, p] = jnp.where(incl2, jnp.exp(jnp.where(incl2, gcp - g_row, 0.0)), 0.0)
        kp = kn[rows_of(r), tile_of(p)]
        r_k[r, p] = _dot_nt(jnp.concatenate([qn[rows_of(r), tile_of(p)], kp], axis=0).astype(BF16),
                            bdiag(kp).astype(BF16))
    for r, p in units:
        kdt[r, p] = _dot_nt(eye128, bdiag(kd[r][:, tile_of(p)]).astype(BF16)).astype(BF16)
    nm = {u: -(bc[rows_of(u[0]), tile_of(u[1])] * jnp.where(strict2, dec[u], 0.0) * r_k[u][CHUNK:]) for u in units}
    pw = {u: _dot(nm[u].astype(BF16), bdiag(nm[u]).astype(BF16)) for u in units}
    tm_ = {u: eye2f + nm[u] for u in units}
    for _ in range(4):
        st = {u: _dot(jnp.concatenate([tm_[u], pw[u]], axis=0).astype(BF16), bdiag(pw[u]).astype(BF16))
              for u in units}
        tm_ = {u: tm_[u] + st[u][:CHUNK] for u in units}
        pw = {u: st[u][CHUNK:] for u in units}
    fin = {u: _dot(tm_[u].astype(BF16), bdiag(pw[u]).astype(BF16)) for u in units}
    tm_ = {u: tm_[u] + fin[u] for u in units}
    wu = {(r, p): _dot(tm_[r, p].astype(BF16),
                       jnp.concatenate([bdiag(xk[rows_of(r), tile_of(p)]), bdiag(vb[rows_of(r), tile_of(p)])],
                                       axis=1).astype(BF16))
          for r, p in units}
    w_b = {u: wu[u][:, :LANES].astype(BF16) for u in units}
    qkd = {u: (dec[u] * r_k[u][:CHUNK]).astype(BF16) for u in units}

    states = [s_scr[p] for p in range(n_pair)]
    o_rows = []
    for r in range(n_ch):
        pairs = range(n_pair)
        res = [_dot(jnp.concatenate([qe[rows_of(r), tile_of(p)].astype(BF16), w_b[r, p]], axis=0),
                    states[p].astype(BF16)) for p in pairs]
        u = [wu[r, p][:, LANES:] - res[p][CHUNK:] for p in pairs]
        upd = [_dot(kdt[r, p], jnp.concatenate([u[p], u[p]], axis=0).astype(BF16)) for p in pairs]
        o_inner = [_dot(qkd[r, p], bdiag(u[p]).astype(BF16)) for p in pairs]
        states = [eg_last[r][:, tile_of(p)] * states[p] + jnp.where(on_diag_blocks, upd[p], 0.0) for p in pairs]
        o_rows.append(jnp.concatenate([res[p][:CHUNK] + o_inner[p] for p in pairs], axis=1))
    for p in range(n_pair):
        s_scr[p] = states[p]
    o = jnp.concatenate(o_rows, axis=0) if n_ch > 1 else o_rows[0]
    ms = head_sumsq(o) * (1.0 / D_C)
    oc_ref[...] = o * lax.rsqrt(ms + NORM_EPS) * onorm_ref[...] * _silu(gate_ref[...])

    @pl.when(j == pl.num_programs(1) - 1)
    def _():
        for p in range(n_pair):
            sout_ref[2 * p] = states[p][:D_C, :D_C]
            sout_ref[2 * p + 1] = states[p][D_C:, D_C:]


def _gdn_constants(tc):
    i = jnp.arange(tc, dtype=jnp.int32)
    tri = ((i[None, :] <= i[:, None]) & (i[None, :] // CHUNK == i[:, None] // CHUNK)).astype(BF16)
    l = jnp.arange(LANES, dtype=jnp.int32)
    block_ones = (l[:, None] // D_C == l[None, :] // D_C).astype(BF16)
    col_head = jnp.arange(C_W, dtype=jnp.int32)[None, :] // D_C
    e_beta = (l[:, None] == col_head).astype(BF16)
    e_g = (l[:, None] == col_head + _SMALL_DECAY0).astype(BF16)
    return tri, block_ones, e_beta, e_g


def _gdn(cqkv, small, gate, conv_prev, s0, conv_w, alog_vec, dtb_vec, onorm_tiled):
    b, t, w3 = cqkv.shape
    tc = min(4 * CHUNK, t)
    consts = _gdn_constants(tc)
    body = functools.partial(_gdn_body, tc=tc)
    blk = lambda bi, j: (bi, j, 0)
    per_b3 = lambda bi, j: (bi, 0, 0)
    per_b4 = lambda bi, j: (bi, 0, 0, 0)
    const2 = lambda bi, j: (0, 0)
    return pl.pallas_call(
        body,
        grid=(b, t // tc),
        in_specs=[pl.BlockSpec((None, tc, w3), blk),
                  pl.BlockSpec((None, tc, LANES), blk),
                  pl.BlockSpec((None, tc, C_W), blk),
                  pl.BlockSpec((None, CONV_W - 1, w3), per_b3),
                  pl.BlockSpec((None, H_C, D_C, D_C), per_b4),
                  pl.BlockSpec(conv_w.shape, const2),
                  pl.BlockSpec((1, LANES), const2),
                  pl.BlockSpec((1, LANES), const2),
                  pl.BlockSpec((1, C_W), const2)]
                 + [pl.BlockSpec(c.shape, const2) for c in consts],
        out_specs=[pl.BlockSpec((None, tc, C_W), blk),
                   pl.BlockSpec((None, H_C, D_C, D_C), per_b4)],
        out_shape=[jax.ShapeDtypeStruct((b, t, C_W), F32),
                   jax.ShapeDtypeStruct((b, H_C, D_C, D_C), F32)],
        scratch_shapes=[pltpu.VMEM((tc + 8, w3), F32), pltpu.VMEM((H_C // 2, LANES, LANES), F32)],
        compiler_params=_params(("parallel", "arbitrary")),
        name="gdn",
    )(cqkv, small, gate, conv_prev, s0, conv_w, alog_vec, dtb_vec, onorm_tiled, *consts)


def _top2_gates(logits):
    lane = lax.broadcasted_iota(jnp.int32, logits.shape, 1).astype(F32)
    low = -3.0e38
    lg = jnp.where(lane < N_EXPERTS, logits, low)
    m1 = jnp.max(lg, axis=-1, keepdims=True)
    i1 = jnp.min(jnp.where(lg == m1, lane, float(LANES)), axis=-1, keepdims=True)
    lg2 = jnp.where(lane == i1, low, lg)
    m2 = jnp.max(lg2, axis=-1, keepdims=True)
    i2 = jnp.min(jnp.where(lg2 == m2, lane, float(LANES)), axis=-1, keepdims=True)
    e2 = jnp.exp(m2 - m1)
    den = 1.0 + e2
    return jnp.where(lane == i1, 1.0 / den, 0.0) + jnp.where(lane == i2, e2 / den, 0.0)


def _outproj_body(x_ref, oa_ref, ob_ref, oc_ref, w_ref, g_ref, *rest, with_router):
    if with_router:
        rhi_ref, rlo_ref, xo_ref, h_ref, gates_ref = rest
    else:
        xo_ref, h_ref = rest
    y = (_dot(oa_ref[...].astype(BF16), w_ref[0:HEAD_W, :])
         + _dot(ob_ref[...].astype(BF16), w_ref[HEAD_W:2 * HEAD_W, :])
         + _dot(oc_ref[...].astype(BF16), w_ref[2 * HEAD_W:, :]))
    x = x_ref[...] + y
    xo_ref[...] = x
    hf = _rms(x, g_ref[...])
    hb = hf.astype(BF16)
    h_ref[...] = hb
    if with_router:
        lo = (hf - hb.astype(F32)).astype(BF16)
        logits = _dot(hb, rhi_ref[...]) + _dot(lo, rhi_ref[...]) + _dot(hb, rlo_ref[...])
        gates_ref[...] = _top2_gates(logits)


def _outproj(x, oa, ob, oc, w_out, g, router=None):
    n, d = x.shape
    tm = min(ROW_TILE, n)
    row = lambda i: (i, 0)
    const = lambda i: (0, 0)
    in_specs = [pl.BlockSpec((tm, d), row), pl.BlockSpec((tm, HEAD_W), row), pl.BlockSpec((tm, HEAD_W), row),
                pl.BlockSpec((tm, C_W), row), pl.BlockSpec(w_out.shape, const), pl.BlockSpec((1, d), const)]
    out_specs = [pl.BlockSpec((tm, d), row), pl.BlockSpec((tm, d), row)]
    out_shape = [jax.ShapeDtypeStruct((n, d), F32), jax.ShapeDtypeStruct((n, d), BF16)]
    args = [x, oa, ob, oc, w_out, g]
    if router is not None:
        in_specs += [pl.BlockSpec(router[0].shape, const), pl.BlockSpec(router[1].shape, const)]
        out_specs.append(pl.BlockSpec((tm, LANES), row))
        out_shape.append(jax.ShapeDtypeStruct((n, LANES), F32))
        args += list(router)
    return pl.pallas_call(
        functools.partial(_outproj_body, with_router=router is not None),
        grid=(n // tm,),
        in_specs=in_specs, out_specs=out_specs, out_shape=out_shape,
        compiler_params=_params(("parallel",)),
        name="outproj",
    )(*args)


def _ffn_body(h_ref, x_ref, wg_ref, wu_ref, wd_ref, o_ref, acc_ref, *, ff_chunk):
    h = h_ref[...]
    d_ff = wg_ref.shape[1]
    acc_ref[...] = x_ref[...]
    for c0 in range(0, d_ff, ff_chunk):
        a = _silu(_dot(h, wg_ref[:, c0:c0 + ff_chunk])) * _dot(h, wu_ref[:, c0:c0 + ff_chunk])
        acc_ref[...] += _dot(a.astype(BF16), wd_ref[c0:c0 + ff_chunk, :])
    o_ref[...] = acc_ref[...]


def _ffn_dense(h, x, wg, wu, wd):
    n, d = x.shape
    d_ff = wg.shape[1]
    tm = min(ROW_TILE, n)
    ff_chunk = 256 if d_ff % 256 == 0 else LANES
    row = lambda i: (i, 0)
    const = lambda i: (0, 0)
    return pl.pallas_call(
        functools.partial(_ffn_body, ff_chunk=ff_chunk),
        grid=(n // tm,),
        in_specs=[pl.BlockSpec((tm, d), row), pl.BlockSpec((tm, d), row),
                  pl.BlockSpec(wg.shape, const), pl.BlockSpec(wu.shape, const), pl.BlockSpec(wd.shape, const)],
        out_specs=pl.BlockSpec((tm, d), row),
        out_shape=jax.ShapeDtypeStruct((n, d), F32),
        scratch_shapes=[pltpu.VMEM((tm, d), F32)],
        compiler_params=_params(("parallel",)),
        name="ffn_dense",
    )(h, x, wg, wu, wd)


def _moe_body(pre_ref, h_ref, gates_ref, gates_t_ref, wg_ref, wu_ref, wd_ref, y_ref,
              rank_col, rank_row, hs, eo, gacc, *, tg, n_f):
    g = pl.program_id(0)
    e = pl.program_id(1)
    f = pl.program_id(2)
    tm_full = MOE_ROW_TILE
    tm_tail = MOE_ROW_TILE // 2
    tb = MOE_TOKEN_BLOCK
    n_tb = tg // tb
    pre0 = (g * N_EXPERTS + e) * (n_tb + 1)
    count = pre_ref[pre0 + n_tb]
    n_full = (count + tm_tail - 1) // tm_full
    has_tail = count > n_full * tm_full

    def for_tiles(fn):
        def full(t, carry):
            fn(pl.multiple_of(t * tm_full, tm_full), tm_full)
            return carry

        lax.fori_loop(0, n_full, full, 0)

        @pl.when(has_tail)
        def _():
            fn(pl.multiple_of(n_full * tm_full, tm_tail), tm_tail)

    def overlaps(r0, tm, b):
        return (pre_ref[pre0 + b] < r0 + tm) & (pre_ref[pre0 + b + 1] > r0)

    @pl.when((e == 0) & (f == 0))
    def _():
        y_ref[...] = jnp.zeros_like(y_ref)
        ii = lax.broadcasted_iota(jnp.int32, (tb, tb), 0)
        jj = lax.broadcasted_iota(jnp.int32, (tb, tb), 1)
        lower = jnp.where(ii > jj, 1.0, 0.0).astype(BF16)
        upper = jnp.where(ii < jj, 1.0, 0.0).astype(BF16)
        carry_c = jnp.zeros((1, LANES), F32)
        carry_r = jnp.zeros((N_EXPERTS, 1), F32)
        for b in range(n_tb):
            mc = jnp.where(gates_ref[b * tb:(b + 1) * tb, :] > 0.0, 1.0, 0.0)
            rank_col[b * tb:(b + 1) * tb, :] = _dot(lower, mc.astype(BF16)) + carry_c
            carry_c = carry_c + jnp.sum(mc, axis=0, keepdims=True)
            mr = jnp.where(gates_t_ref[:, b * tb:(b + 1) * tb] > 0.0, 1.0, 0.0)
            rank_row[:, b * tb:(b + 1) * tb] = _dot(mr.astype(BF16), upper) + carry_r
            carry_r = carry_r + jnp.sum(mr, axis=1, keepdims=True)

    def gather(r0, tm):
        rows = (r0 + lax.broadcasted_iota(jnp.int32, (tm, tb), 0)).astype(F32)
        gacc[0:tm, :] = jnp.zeros((tm, gacc.shape[1]), F32)
        for b in range(n_tb):
            @pl.when(overlaps(r0, tm, b))
            def _(b=b):
                gr = gates_t_ref[pl.ds(e, 1), b * tb:(b + 1) * tb]
                rr = jnp.where(gr > 0.0, rank_row[pl.ds(e, 1), b * tb:(b + 1) * tb], -1.0)
                onehot = jnp.where(rr == rows, 1.0, 0.0).astype(BF16)
                gacc[0:tm, :] += _dot(onehot, h_ref[b * tb:(b + 1) * tb, :])
        hs[pl.ds(r0, tm), :] = gacc[0:tm, :].astype(BF16)

    @pl.when(f == 0)
    def _():
        for_tiles(gather)

    def expert(r0, tm):
        x = hs[pl.ds(r0, tm), :]
        a = _silu(_dot(x, wg_ref[...])) * _dot(x, wu_ref[...])
        part = _dot(a.astype(BF16), wd_ref[...])

        @pl.when(f == 0)
        def _():
            eo[pl.ds(r0, tm), :] = part

        @pl.when(f > 0)
        def _():
            eo[pl.ds(r0, tm), :] += part

    for_tiles(expert)

    def scatter(r0, tm):
        lane = lax.broadcasted_iota(jnp.int32, (tb, LANES), 1)
        cols = (r0 + lax.broadcasted_iota(jnp.int32, (tb, tm), 1)).astype(F32)
        out = eo[pl.ds(r0, tm), :].astype(BF16)
        for b in range(n_tb):
            @pl.when(overlaps(r0, tm, b))
            def _(b=b):
                sel = lane == e
                gc = jnp.sum(jnp.where(sel, gates_ref[b * tb:(b + 1) * tb, :], 0.0), axis=-1, keepdims=True)
                rc = jnp.sum(jnp.where(sel, rank_col[b * tb:(b + 1) * tb, :], 0.0), axis=-1, keepdims=True)
                rc = jnp.where(gc > 0.0, rc, -1.0)
                weighted = jnp.where(rc == cols, gc, 0.0).astype(BF16)
                y_ref[b * tb:(b + 1) * tb, :] += _dot(weighted, out)

    @pl.when(f == n_f - 1)
    def _():
        for_tiles(scatter)


def _moe_prefix(gates, tg):
    n = gates.shape[0]
    n_tb = tg // MOE_TOKEN_BLOCK
    routed = (gates[:, :N_EXPERTS] > 0.0).astype(jnp.int32)
    per_block = routed.reshape(n // tg, n_tb, MOE_TOKEN_BLOCK, N_EXPERTS).sum(axis=2)
    run = jnp.cumsum(per_block, axis=1)
    pre = jnp.concatenate([jnp.zeros_like(run[:, :1]), run], axis=1)
    return jnp.transpose(pre, (0, 2, 1)).reshape(-1)


def _moe(h, gates, wg, wu, wd, tg):
    n, d = h.shape
    n_e, _, d_ff = wg.shape
    n_f = 2
    ffh = d_ff // n_f
    body = functools.partial(_moe_body, tg=tg, n_f=n_f)
    once = pl.Buffered(1)
    grid_spec = pltpu.PrefetchScalarGridSpec(
        num_scalar_prefetch=1,
        grid=(n // tg, n_e, n_f),
        in_specs=[pl.BlockSpec((tg, d), lambda g, e, f, c: (g, 0), pipeline_mode=once),
                  pl.BlockSpec((tg, LANES), lambda g, e, f, c: (g, 0), pipeline_mode=once),
                  pl.BlockSpec((N_EXPERTS, tg), lambda g, e, f, c: (0, g), pipeline_mode=once),
                  pl.BlockSpec((None, d, ffh), lambda g, e, f, c: (e, 0, f)),
                  pl.BlockSpec((None, d, ffh), lambda g, e, f, c: (e, 0, f)),
                  pl.BlockSpec((None, ffh, d), lambda g, e, f, c: (e, f, 0))],
        out_specs=pl.BlockSpec((tg, d), lambda g, e, f, c: (g, 0), pipeline_mode=once),
        scratch_shapes=[pltpu.VMEM((tg, LANES), F32), pltpu.VMEM((N_EXPERTS, tg), F32),
                        pltpu.VMEM((tg, d), BF16), pltpu.VMEM((tg, d), F32),
                        pltpu.VMEM((MOE_ROW_TILE, d), F32)],
    )
    return pl.pallas_call(
        body,
        grid_spec=grid_spec,
        out_shape=jax.ShapeDtypeStruct((n, d), F32),
        compiler_params=_params(("parallel", "arbitrary", "arbitrary")),
        name="moe",
    )(_moe_prefix(gates, tg), h, gates, gates[:, :N_EXPERTS].T, wg, wu, wd)


def _final_body(x_ref, y_ref, g_ref, o_ref):
    o_ref[...] = _rms(x_ref[...] + y_ref[...], g_ref[...])


def _final_norm(x, y, g):
    n, d = x.shape
    tm = min(ROW_TILE, n)
    row = lambda i: (i, 0)
    return pl.pallas_call(
        _final_body,
        grid=(n // tm,),
        in_specs=[pl.BlockSpec((tm, d), row), pl.BlockSpec((tm, d), row), pl.BlockSpec((1, d), lambda i: (0, 0))],
        out_specs=pl.BlockSpec((tm, d), row),
        out_shape=jax.ShapeDtypeStruct((n, d), F32),
        compiler_params=_params(("parallel",)),
        name="final_norm",
    )(x, y, g)


def _t5_bucket(rel):
    nb = T5_BUCKETS // 2
    max_exact = nb // 2
    ret = jnp.where(rel > 0, nb, 0)
    n = jnp.abs(rel)
    nf = jnp.maximum(n, 1).astype(F32)
    large = max_exact + (jnp.log(nf / max_exact) / math.log(T5_MAX_DIST / max_exact) * (nb - max_exact)).astype(jnp.int32)
    large = jnp.minimum(large, nb - 1)
    return ret + jnp.where(n < max_exact, n, large)


def _t5_table(t5_bias, q_pos, k_pos):
    bias = _lookup(t5_bias, _t5_bucket(k_pos[None, :] - q_pos[:, None]))
    mask = (k_pos[None, :] // CHUNK) <= (q_pos[:, None] // CHUNK)
    return bias, mask[None]


def _lookup(table, idx):
    onehot = jax.nn.one_hot(idx, table.shape[0], dtype=F32)
    return jnp.einsum("qkn,nh->hqk", onehot, table.astype(F32), precision=lax.Precision.HIGHEST)


def _band_table(rel_bias):
    qi = jnp.arange(CHUNK, dtype=jnp.int32)
    kj = jnp.arange(BAND, dtype=jnp.int32) - B_LEFT_CHUNKS * CHUNK
    rel = jnp.clip(kj[None, :] - qi[:, None], -B_REL_CLIP, B_REL_CLIP) + B_REL_CLIP
    return _lookup(rel_bias, rel)


def _lane_vec(v, offset):
    return jnp.zeros((1, LANES), F32).at[0, offset:offset + v.shape[0]].set(v.astype(F32))


def _moe_group(n):
    for tg in (2048, 1024, 512, 256):
        if n % tg == 0:
            return tg
    raise ValueError(f"token count {n} is not a multiple of 256")


def kernel(x_prompt, x_sample, cache_a_k, cache_a_v, cache_b_k, cache_b_v, cache_c_conv, state_c_ssm, w_in, w_out, norm_mix, norm_ffn, norm_final, lam_qk, subln_a, t5_bias, rel_bias_b, conv_c, a_log_c, dt_bias_c, onorm_c, ffn_gate, ffn_up, ffn_down, moe_router, moe_gate, moe_up, moe_down):
    depth = w_in.shape[0]
    bp, sp, d = x_prompt.shape
    bs, ss, _ = x_sample.shape
    past = cache_a_k.shape[2]
    nb_cache = cache_b_k.shape[2]
    assert ss == CHUNK and nb_cache == B_LEFT_CHUNKS * CHUNK and sp >= B_LEFT_CHUNKS * CHUNK

    xp = x_prompt.reshape(bp * sp, d)
    xs = x_sample.reshape(bs * ss, d)
    blk = min(A_BLOCK, sp)

    pos_blk = jnp.arange(blk, dtype=jnp.int32)
    far_bias = t5_bias[_t5_bucket(jnp.int32(-(blk + 1)))].astype(F32)
    bd, md = _t5_table(t5_bias, pos_blk, pos_blk)
    bias_diag = jnp.where(md, bd - far_bias[:, None, None], NEG_INF)
    bsub, _ = _t5_table(t5_bias, blk + pos_blk, pos_blk)
    bias_sub = bsub - far_bias[:, None, None]
    q_pos_s = past + jnp.arange(ss, dtype=jnp.int32)
    bfull, mfull = _t5_table(t5_bias, q_pos_s, jnp.arange(past + ss, dtype=jnp.int32))
    bias_s = jnp.where(mfull, bfull, NEG_INF)
    bias_s_cache, bias_s_new = bias_s[:, :, :past], bias_s[:, :, past:]

    zeros_conv = jnp.zeros((bp, CONV_W - 1, 3 * C_W), F32)
    zeros_state = jnp.zeros((bp, H_C, D_C, D_C), F32)

    p_states = [[] for _ in range(6)]
    s_states = [[] for _ in range(6)]
    yp = ys = None
    for l in range(depth):
        if yp is not None:
            xp, xs, yp, ys = xp + yp, xs + ys, None, None
        w = w_in[l]
        n_main = 6 * HEAD_W + 3 * C_W
        w_main = jnp.concatenate([w[:, :n_main], w[:, n_main + 2 * H_C:]], axis=1).astype(BF16)
        w_small = jnp.zeros((d, LANES), F32).at[:, :2 * H_C].set(w[:, n_main:n_main + 2 * H_C]).astype(BF16)
        w_out_l = w_out[l].astype(BF16)
        g_mix = norm_mix[l].reshape(1, d)
        g_ffn = norm_ffn[l].reshape(1, d)
        lam_init = 0.8 - 0.6 * math.exp(-0.3 * l)
        lq = lam_qk[l].astype(F32)
        lam = (jnp.exp(jnp.sum(lq[0] * lq[1])) - jnp.exp(jnp.sum(lq[2] * lq[3])) + lam_init).reshape(1)
        subln = subln_a[l].reshape(1, DV_A)
        band_bias = _band_table(rel_bias_b[l])
        alog_vec = _lane_vec(a_log_c[l], _SMALL_DECAY0)
        dtb_vec = _lane_vec(dt_bias_c[l], _SMALL_DECAY0)
        onorm = jnp.tile(onorm_c[l].astype(F32), H_C).reshape(1, C_W)
        is_moe = l % 2 == 1
        if is_moe:
            r = jnp.zeros((d, LANES), F32).at[:, :N_EXPERTS].set(moe_router[l // 2])
            r_hi = r.astype(BF16)
            router = (r_hi, (r - r_hi.astype(F32)).astype(BF16))
            e_wg, e_wu, e_wd = (moe_gate[l // 2].astype(BF16), moe_up[l // 2].astype(BF16),
                                moe_down[l // 2].astype(BF16))
        else:
            router = None
            f_wg, f_wu, f_wd = (ffn_gate[l // 2].astype(BF16), ffn_up[l // 2].astype(BF16),
                                ffn_down[l // 2].astype(BF16))

        new_x = []
        for is_prompt, x in ((True, xp), (False, xs)):
            b, t = (bp, sp) if is_prompt else (bs, ss)
            aq, ak, av, bq, bk, bv, cqkv, cgate, csmall = _inproj(x, g_mix, w_main, w_small)
            r3 = lambda a: a.reshape(b, t, a.shape[-1])
            if is_prompt:
                oa = _attn_a_prompt(lam, r3(aq), r3(ak), r3(av), jnp.swapaxes(bias_diag, 1, 2),
                                    jnp.swapaxes(bias_sub, 1, 2), subln.reshape(DV_A, 1), 1.0 - lam_init)
                pad = ((0, 0), (B_LEFT_CHUNKS * CHUNK, 0), (0, 0))
                ob = _band_attn(r3(bq), jnp.pad(r3(bk), pad), jnp.pad(r3(bv), pad), band_bias,
                                B_LEFT_CHUNKS * CHUNK)
                conv_prev, s0 = zeros_conv, zeros_state
            else:
                oa = _attn_a_sample(lam, r3(aq), cache_a_k[l].reshape(b, past, HEAD_W),
                                    cache_a_v[l].reshape(b, past, HEAD_W), r3(ak), r3(av),
                                    bias_s_cache, bias_s_new, subln, 1.0 - lam_init)
                kb = jnp.concatenate([cache_b_k[l].reshape(b, nb_cache, HEAD_W), r3(bk)], axis=1)
                vb = jnp.concatenate([cache_b_v[l].reshape(b, nb_cache, HEAD_W), r3(bv)], axis=1)
                ob = _band_attn(r3(bq), kb, vb, band_bias, 0)
                conv_prev, s0 = cache_c_conv[l], state_c_ssm[l]
            oc, s_new = _gdn(r3(cqkv), r3(csmall), r3(cgate), conv_prev, s0, conv_c[l], alog_vec, dtb_vec, onorm)
            n = b * t
            res = _outproj(x, oa.reshape(n, HEAD_W), ob.reshape(n, HEAD_W), oc.reshape(n, C_W), w_out_l, g_ffn,
                           router)
            if is_moe:
                x_new, h2, gates = res
                y_moe = _moe(h2, gates, e_wg, e_wu, e_wd, _moe_group(n))
                new_x.append((x_new, y_moe))
            else:
                x_new, h2 = res
                new_x.append((_ffn_dense(h2, x_new, f_wg, f_wu, f_wd), None))
            keep = min(B_LEFT_CHUNKS * CHUNK, t)
            states = (ak.reshape(b, t, H_A, 2 * DQK_A), av.reshape(b, t, H_A, DV_A),
                      bk.reshape(b, t, H_B, D_B)[:, t - keep:], bv.reshape(b, t, H_B, D_B)[:, t - keep:],
                      r3(cqkv)[:, t - (CONV_W - 1):], s_new)
            for i in range(6):
                (p_states if is_prompt else s_states)[i].append(states[i])
        (xp, yp), (xs, ys) = new_x

    g_fin = norm_final.reshape(1, d)
    if yp is None:
        yp, ys = jnp.zeros_like(xp), jnp.zeros_like(xs)
    y_prompt = _final_norm(xp, yp, g_fin).reshape(bp, sp, d)
    y_sample = _final_norm(xs, ys, g_fin).reshape(bs, ss, d)
    p_out = [jnp.stack(s, axis=0) for s in p_states]
    s_out = [jnp.stack(s, axis=0) for s in s_states]
    return (y_prompt, y_sample, *p_out, *s_out)
```

```python
import functools
import math

import jax
import jax.numpy as jnp
from jax import lax
from jax.experimental import pallas as pl
from jax.experimental.pallas import tpu as pltpu

F32 = jnp.float32
BF16 = jnp.bfloat16

CHUNK = 64
H_A = 4
DQK_A = 32
DV_A = 64
H_B = 4
D_B = 64
B_LEFT_CHUNKS = 8
B_REL_CLIP = 128
H_C = 8
D_C = 64
CONV_W = 4
T5_BUCKETS = 32
T5_MAX_DIST = 128
N_EXPERTS = 8
NORM_EPS = 1e-6
L2_EPS = 1e-6
NEG_INF = -1e30
HEAD_W = 256
C_W = H_C * D_C
BAND = (B_LEFT_CHUNKS + 1) * CHUNK
LANES = 128
LOG2E = math.log2(math.e)
VMEM_LIMIT = 56 * 1024 * 1024

A_BLOCK = 256
ROW_TILE = 512
GDN_BATCH_BLOCK = 2
MOE_ROW_TILE = 256
MOE_TOKEN_BLOCK = 256


def _params(sem):
    return pltpu.CompilerParams(dimension_semantics=sem, vmem_limit_bytes=VMEM_LIMIT)


def _rms(x, g):
    ms = jnp.mean(x * x, axis=-1, keepdims=True)
    return x * lax.rsqrt(ms + NORM_EPS) * g


def _sigmoid(x):
    return 1.0 / (1.0 + jnp.exp(-x))


def _silu(x):
    return x * _sigmoid(x)


def _softplus(x):
    return jnp.maximum(x, 0.0) + jnp.log1p(jnp.exp(-jnp.abs(x)))


def _dot(a, b):
    return jnp.dot(a, b, preferred_element_type=F32)


def _dot_nt(a, b):
    return lax.dot_general(a, b, (((1,), (1,)), ((), ())), preferred_element_type=F32)


def _dot_tn(a, b):
    return lax.dot_general(a, b, (((0,), (0,)), ((), ())), preferred_element_type=F32)


def _split3(x):
    x1 = x.astype(BF16)
    r1 = x - x1.astype(F32)
    x2 = r1.astype(BF16)
    x3 = (r1 - x2.astype(F32)).astype(BF16)
    return x1, x2, x3


def _inproj_body(x_ref, g_ref, w_ref, ws_ref, aq_ref, ak_ref, av_ref, bq_ref, bk_ref, bv_ref,
                 cqkv_ref, cgate_ref, csmall_ref):
    h = _rms(x_ref[...], g_ref[...]).astype(BF16)
    col = 0
    for ref in (aq_ref, ak_ref, av_ref, bq_ref, bk_ref, bv_ref, cqkv_ref, cgate_ref):
        width = ref.shape[-1]
        ref[...] = _dot(h, w_ref[:, col:col + width])
        col += width
    csmall_ref[...] = _dot(h, ws_ref[...])


def _inproj(x, g, w_main, w_small):
    n, d = x.shape
    tm = min(ROW_TILE, n)
    widths = (HEAD_W,) * 6 + (3 * C_W, C_W, LANES)
    row = lambda i: (i, 0)
    const = lambda i: (0, 0)
    return pl.pallas_call(
        _inproj_body,
        grid=(n // tm,),
        in_specs=[pl.BlockSpec((tm, d), row), pl.BlockSpec((1, d), const),
                  pl.BlockSpec(w_main.shape, const), pl.BlockSpec(w_small.shape, const)],
        out_specs=[pl.BlockSpec((tm, w), row) for w in widths],
        out_shape=[jax.ShapeDtypeStruct((n, w), F32) for w in widths],
        compiler_params=_params(("parallel",)),
        name="inproj",
    )(x, g, w_main, w_small)


def _diff_out(acc1, l1, acc2, l2, lam, subln, post_scale):
    o = acc1 / l1 - lam * (acc2 / l2)
    ms = jnp.mean(o * o, axis=-1, keepdims=True)
    return o * lax.rsqrt(ms + NORM_EPS) * subln * post_scale


A_ACC_ROWS = DV_A + 16
A_GROUPS = LANES // DQK_A


def _attn_a_prompt_body(lam_ref, q_ref, k_ref, v_ref, bd_ref, bs_ref, subln_ref, o_ref,
                        kbf, vte, wq, m_scr, acc_scr, s_even, s_odd, *, blk, post_scale):
    i = pl.program_id(1)
    n_blk = k_ref.shape[0] // blk
    n_pair = H_A // 2

    @pl.when(i == 0)
    def _():
        ones_rows = jnp.where(lax.broadcasted_iota(jnp.int32, (A_ACC_ROWS - DV_A, blk), 0) == 0, 1.0, 0.0)
        for jb in range(n_blk):
            kbf[jb] = k_ref[jb * blk:(jb + 1) * blk, :].astype(BF16)
            vt = v_ref[jb * blk:(jb + 1) * blk, :].T
            for h in range(H_A):
                vte[jb, h, 0:DV_A, :] = vt[h * DV_A:(h + 1) * DV_A, :].astype(BF16)
                vte[jb, h, DV_A:, :] = ones_rows.astype(BF16)

    lam = lam_ref[0]
    qt = (q_ref[...] * (DQK_A ** -0.5 * LOG2E)).T
    grp = lax.broadcasted_iota(jnp.int32, (LANES, blk), 0) // DQK_A
    for p in range(n_pair):
        qtp = qt[p * LANES:(p + 1) * LANES, :]
        wq[p] = jnp.concatenate([jnp.where(grp == g, qtp, 0.0) for g in range(A_GROUPS)], axis=1).astype(BF16)
    m_scr[...] = jnp.full(m_scr.shape, NEG_INF, F32)
    acc_scr[...] = jnp.zeros(acc_scr.shape, F32)

    def scores(jb):
        return [_dot(kbf[jb, :, p * LANES:(p + 1) * LANES], wq[p]) for p in range(n_pair)]

    def consume(tile_of, jb, bias_ref, off):
        groups = range(n_pair * A_GROUPS)
        alphas, pts = [], []
        for g in groups:
            p, gi = divmod(g, A_GROUPS)
            s = tile_of(p, gi)
            if bias_ref is not None:
                s = s + bias_ref[2 * p + gi // 2]
            if off is not None:
                s = s + off
            m_old = m_scr[g:g + 1, :]
            m_new = jnp.maximum(m_old, jnp.max(s, axis=0, keepdims=True))
            alphas.append(jnp.exp2(m_old - m_new))
            pts.append(jnp.exp2(s - m_new).astype(BF16))
            m_scr[g:g + 1, :] = m_new
        pvs = [_dot(vte[jb, 2 * (g // A_GROUPS) + (g % A_GROUPS) // 2], pts[g]) for g in groups]
        for g in groups:
            acc_scr[g] = alphas[g] * acc_scr[g] + pvs[g]

    def tiles(jb, bias_ref, off):
        s_alls = scores(jb)
        consume(lambda p, gi: s_alls[p][:, gi * blk:(gi + 1) * blk], jb, bias_ref, off)

    tiles(i, bd_ref, None)
    tiles(jnp.maximum(i - 1, 0), bs_ref, jnp.where(i >= 1, 0.0, NEG_INF))

    n_far = jnp.maximum(i - 1, 0)

    def put_scores(dst, jb):
        s_alls = scores(jnp.minimum(jb, n_far - 1))
        for p in range(n_pair):
            dst[p] = s_alls[p]

    def from_scratch(src):
        return lambda p, gi: src[p, :, gi * blk:(gi + 1) * blk]

    @pl.when(n_far > 0)
    def _():
        put_scores(s_even, 0)

    def far(t, carry):
        jb = 2 * t
        put_scores(s_odd, jb + 1)
        consume(from_scratch(s_even), jb, None, None)

        @pl.when(jb + 1 < n_far)
        def _():
            put_scores(s_even, jb + 2)
            consume(from_scratch(s_odd), jb + 1, None, None)

        return carry

    lax.fori_loop(0, (n_far + 1) // 2, far, 0)

    outs = []
    for h in range(H_A):
        g1 = (h // 2) * A_GROUPS + (h % 2) * 2
        a1 = acc_scr[g1]
        a2 = acc_scr[g1 + 1]
        o = a1[:DV_A] / a1[DV_A:DV_A + 1] - lam * (a2[:DV_A] / a2[DV_A:DV_A + 1])
        ms = jnp.mean(o * o, axis=0, keepdims=True)
        outs.append(o * lax.rsqrt(ms + NORM_EPS) * subln_ref[...] * post_scale)
    o_ref[...] = jnp.concatenate(outs, axis=0).T


def _attn_a_prompt(lam, q, k, v, bias_diag_t, bias_sub_t, subln_col, post_scale):
    b, s, w = q.shape
    blk = min(A_BLOCK, s)
    n_blk = s // blk
    body = functools.partial(_attn_a_prompt_body, blk=blk, post_scale=post_scale)
    return pl.pallas_call(
        body,
        grid=(b, n_blk),
        in_specs=[pl.BlockSpec(memory_space=pltpu.SMEM),
                  pl.BlockSpec((None, blk, w), lambda bi, i: (bi, i, 0)),
                  pl.BlockSpec((None, s, w), lambda bi, i: (bi, 0, 0)),
                  pl.BlockSpec((None, s, w), lambda bi, i: (bi, 0, 0)),
                  pl.BlockSpec(bias_diag_t.shape, lambda bi, i: (0, 0, 0)),
                  pl.BlockSpec(bias_sub_t.shape, lambda bi, i: (0, 0, 0)),
                  pl.BlockSpec((DV_A, 1), lambda bi, i: (0, 0))],
        out_specs=pl.BlockSpec((None, blk, w), lambda bi, i: (bi, i, 0)),
        out_shape=jax.ShapeDtypeStruct((b, s, w), F32),
        scratch_shapes=[pltpu.VMEM((n_blk, blk, w), BF16),
                        pltpu.VMEM((n_blk, H_A, A_ACC_ROWS, blk), BF16),
                        pltpu.VMEM((H_A // 2, LANES, A_GROUPS * blk), BF16),
                        pltpu.VMEM((H_A * 2, blk), F32),
                        pltpu.VMEM((H_A * 2, A_ACC_ROWS, blk), F32),
                        pltpu.VMEM((H_A // 2, blk, A_GROUPS * blk), F32),
                        pltpu.VMEM((H_A // 2, blk, A_GROUPS * blk), F32)],
        compiler_params=_params(("parallel", "arbitrary")),
        name="attn_a_prompt",
    )(lam, q, k, v, bias_diag_t, bias_sub_t, subln_col)


def _attn_a_sample_body(lam_ref, q_ref, kc_ref, vc_ref, kn_ref, vn_ref, bc_ref, bn_ref, subln_ref, o_ref,
                        *, post_scale):
    lam = lam_ref[0]
    q = q_ref[...] * (DQK_A ** -0.5)
    kc = kc_ref[...].astype(BF16)
    vc = vc_ref[...].astype(BF16)
    kn = kn_ref[...].astype(BF16)
    vn = vn_ref[...].astype(BF16)
    outs = []
    for h in range(H_A):
        res = []
        for m in range(2):
            c0 = h * 2 * DQK_A + m * DQK_A
            qhm = q[:, c0:c0 + DQK_A].astype(BF16)
            s_c = _dot_nt(qhm, kc[:, c0:c0 + DQK_A]) + bc_ref[h]
            s_n = _dot_nt(qhm, kn[:, c0:c0 + DQK_A]) + bn_ref[h]
            mx = jnp.maximum(jnp.max(s_c, axis=-1, keepdims=True), jnp.max(s_n, axis=-1, keepdims=True))
            p_c = jnp.exp(s_c - mx)
            p_n = jnp.exp(s_n - mx)
            l = jnp.sum(p_c, axis=-1, keepdims=True) + jnp.sum(p_n, axis=-1, keepdims=True)
            acc = (_dot(p_c.astype(BF16), vc[:, h * DV_A:(h + 1) * DV_A])
                   + _dot(p_n.astype(BF16), vn[:, h * DV_A:(h + 1) * DV_A]))
            res.append((l, acc))
        (l1, a1), (l2, a2) = res
        outs.append(_diff_out(a1, l1, a2, l2, lam, subln_ref[...], post_scale))
    o_ref[...] = jnp.concatenate(outs, axis=-1)


def _attn_a_sample(lam, q, kc, vc, kn, vn, bias_c, bias_n, subln, post_scale):
    b, t, w = q.shape
    p = kc.shape[1]
    body = functools.partial(_attn_a_sample_body, post_scale=post_scale)
    per_b = lambda bi: (bi, 0, 0)
    const3 = lambda bi: (0, 0, 0)
    return pl.pallas_call(
        body,
        grid=(b,),
        in_specs=[pl.BlockSpec(memory_space=pltpu.SMEM),
                  pl.BlockSpec((None, t, w), per_b),
                  pl.BlockSpec((None, p, w), per_b), pl.BlockSpec((None, p, w), per_b),
                  pl.BlockSpec((None, t, w), per_b), pl.BlockSpec((None, t, w), per_b),
                  pl.BlockSpec(bias_c.shape, const3), pl.BlockSpec(bias_n.shape, const3),
                  pl.BlockSpec((1, DV_A), lambda bi: (0, 0))],
        out_specs=pl.BlockSpec((None, t, w), per_b),
        out_shape=jax.ShapeDtypeStruct((b, t, w), F32),
        compiler_params=_params(("parallel",)),
        name="attn_a_sample",
    )(lam, q, kc, vc, kn, vn, bias_c, bias_n, subln)


B_PAIR = 2 * CHUNK
B_UNION = BAND + CHUNK
B_UNION_BLOCKS = B_UNION // LANES
B_ACC_ROWS = D_B + 16


def _band_body(q_ref, k_ref, v_ref, bias_ref, o_ref, kbf, vte, *, n_grp, n_invalid):
    i = pl.program_id(1)
    n_kblk = k_ref.shape[0] // LANES

    @pl.when(i == 0)
    def _():
        ones_rows = jnp.where(lax.broadcasted_iota(jnp.int32, (B_ACC_ROWS - D_B, LANES), 0) == 0, 1.0, 0.0)
        for jb in range(n_kblk):
            kbf[jb] = k_ref[jb * LANES:(jb + 1) * LANES, :].astype(BF16)
            vt = v_ref[jb * LANES:(jb + 1) * LANES, :].T
            for h in range(H_B):
                vte[jb, h, 0:D_B, :] = vt[h * D_B:(h + 1) * D_B, :].astype(BF16)
                vte[jb, h, D_B:, :] = ones_rows.astype(BF16)

    top = lax.broadcasted_iota(jnp.int32, (LANES, B_PAIR), 0) < D_B
    key_row = lax.broadcasted_iota(jnp.int32, (B_UNION, B_PAIR), 0)

    def group(gl, carry):
        g = i * n_grp + gl
        r0 = pl.multiple_of(gl * B_PAIR, B_PAIR)
        qt = (q_ref[pl.ds(r0, B_PAIR), :] * (D_B ** -0.5 * LOG2E)).T
        kun = jnp.concatenate([kbf[g + t] for t in range(B_UNION_BLOCKS)], axis=0)
        s_pairs = []
        for p in range(H_B // 2):
            qtp = qt[p * LANES:(p + 1) * LANES, :]
            w = jnp.concatenate([jnp.where(top, qtp, 0.0), jnp.where(top, 0.0, qtp)], axis=1).astype(BF16)
            s_pairs.append(_dot(kun[:, p * LANES:(p + 1) * LANES], w))
        pts = []
        for h in range(H_B):
            s = s_pairs[h // 2][:, (h % 2) * B_PAIR:(h % 2 + 1) * B_PAIR] + bias_ref[h]
            if n_invalid:
                s = jnp.where(g * B_PAIR + key_row >= n_invalid, s, NEG_INF)
            pts.append(jnp.exp2(s - jnp.max(s, axis=0, keepdims=True)).astype(BF16))
        accs = [_dot(jnp.concatenate([vte[g + t, h] for t in range(B_UNION_BLOCKS)], axis=1), pts[h])
                for h in range(H_B)]
        ot = jnp.concatenate([a[:D_B] / a[D_B:D_B + 1] for a in accs], axis=0)
        o_ref[pl.ds(r0, B_PAIR), :] = ot.T
        return carry

    lax.fori_loop(0, n_grp, group, 0)


def _band_attn(q, k, v, bias_t, n_invalid):
    b, t, w = q.shape
    tq = -(-t // B_PAIR) * B_PAIR
    if tq != t:
        q = jnp.pad(q, ((0, 0), (0, tq - t), (0, 0)))
    tk = tq + B_LEFT_CHUNKS * CHUNK
    if k.shape[1] != tk:
        k = jnp.pad(k, ((0, 0), (0, tk - k.shape[1]), (0, 0)))
        v = jnp.pad(v, ((0, 0), (0, tk - v.shape[1]), (0, 0)))
    qb = min(8 * CHUNK, tq)
    n_kblk = tk // LANES
    body = functools.partial(_band_body, n_grp=qb // B_PAIR, n_invalid=n_invalid)
    out = pl.pallas_call(
        body,
        grid=(b, tq // qb),
        in_specs=[pl.BlockSpec((None, qb, w), lambda bi, i: (bi, i, 0)),
                  pl.BlockSpec((None, tk, w), lambda bi, i: (bi, 0, 0)),
                  pl.BlockSpec((None, tk, w), lambda bi, i: (bi, 0, 0)),
                  pl.BlockSpec(bias_t.shape, lambda bi, i: (0, 0, 0))],
        out_specs=pl.BlockSpec((None, qb, w), lambda bi, i: (bi, i, 0)),
        out_shape=jax.ShapeDtypeStruct((b, tq, w), F32),
        scratch_shapes=[pltpu.VMEM((n_kblk, LANES, w), BF16),
                        pltpu.VMEM((n_kblk, H_B, B_ACC_ROWS, LANES), BF16)],
        compiler_params=_params(("parallel", "arbitrary")),
        name="band_attn",
    )(q, k, v, bias_t)
    return out[:, :t] if tq != t else out


_SMALL_DECAY0 = H_C


def _split2(x):
    hi = x.astype(BF16)
    return hi, (x - hi.astype(F32)).astype(BF16)


def _gdn_body(cqkv_ref, small_ref, gate_ref, cprev_ref, s0_ref, convw_ref, alog_ref, dtb_ref, onorm_ref,
              tri_ref, bo_ref, eb_ref, eg_ref, oc_ref, sout_ref, xext, s_scr, *, tc):
    j = pl.program_id(1)
    n_bb = cqkv_ref.shape[0]
    n_ch = tc // CHUNK
    n_pair = H_C // 2
    pad = 8
    lo = pad - (CONV_W - 1)

    @pl.when(j == 0)
    def _():
        z = jnp.zeros((D_C, D_C), F32)
        for bb in range(n_bb):
            xext[bb, lo:pad, :] = cprev_ref[bb]
            for p in range(n_pair):
                s_scr[bb, p] = jnp.concatenate([jnp.concatenate([s0_ref[bb, 2 * p], z], axis=1),
                                                jnp.concatenate([z, s0_ref[bb, 2 * p + 1]], axis=1)], axis=0)

    @pl.when(j > 0)
    def _():
        for bb in range(n_bb):
            xext[bb, lo:pad, :] = xext[bb, tc + lo:tc + pad, :]

    bo = bo_ref[...]
    tri = tri_ref[...]
    lane_s = lax.broadcasted_iota(jnp.int32, (tc, LANES), 1)
    is_dec = (lane_s >= _SMALL_DECAY0) & (lane_s < _SMALL_DECAY0 + H_C)

    def head_sumsq(x):
        hi, lo_ = _split2(x * x)
        return jnp.concatenate([_dot(hi[:, t * LANES:(t + 1) * LANES], bo) + _dot(lo_[:, t * LANES:(t + 1) * LANES], bo)
                                for t in range(n_pair)], axis=1)

    qn, kn, gc, bc, xk, vb, qe = [], [], [], [], [], [], []
    for bb in range(n_bb):
        xext[bb, pad:, :] = cqkv_ref[bb]
        y = convw_ref[0:1, :] * xext[bb, lo:lo + tc, :]
        for w in range(1, CONV_W):
            y = y + convw_ref[w:w + 1, :] * xext[bb, lo + w:lo + w + tc, :]
        y = _silu(y)
        yq, yk, yv = y[:, :C_W], y[:, C_W:2 * C_W], y[:, 2 * C_W:]
        qn.append(yq * lax.rsqrt(head_sumsq(yq) + L2_EPS) * (D_C ** -0.5))
        kn.append(yk * lax.rsqrt(head_sumsq(yk) + L2_EPS))
        sm = small_ref[bb]
        log_a = jnp.where(is_dec, -jnp.exp(alog_ref[...]) * _softplus(sm + dtb_ref[...]), 0.0)
        g_full = sum(_dot(tri, part) for part in _split3(log_a))
        gc.append(sum(_dot(part, eg_ref[...]) for part in _split3(g_full)))
        bc.append(sum(_dot(part, eb_ref[...]) for part in _split3(_sigmoid(sm))))
        egc = jnp.exp(gc[bb])
        xk.append(bc[bb] * egc * kn[bb])
        vb.append(bc[bb] * yv)
        qe.append(egc * qn[bb])

    ii = lax.broadcasted_iota(jnp.int32, (CHUNK, LANES), 0)
    ln = lax.broadcasted_iota(jnp.int32, (CHUNK, LANES), 1)
    jn = ln % D_C
    incl2 = ii >= jn
    strict2 = ii > jn
    eye2 = ii == jn
    left = ln < D_C
    eye2f = jnp.where(eye2, 1.0, 0.0)
    r128 = lax.broadcasted_iota(jnp.int32, (LANES, LANES), 0)
    c128 = lax.broadcasted_iota(jnp.int32, (LANES, LANES), 1)
    on_diag_blocks = (r128 < D_C) == (c128 < D_C)
    eye128 = jnp.where(r128 == c128, 1.0, 0.0).astype(BF16)

    keep_left = jnp.where(left, 1.0, 0.0).astype(BF16)
    keep_right = jnp.where(left, 0.0, 1.0).astype(BF16)

    def bdiag(x):
        xb = x.astype(BF16)
        return jnp.concatenate([xb * keep_left, xb * keep_right], axis=0)

    units = [(bb, r, p) for bb in range(n_bb) for r in range(n_ch) for p in range(n_pair)]
    lanes_bp = [(bb, p) for bb in range(n_bb) for p in range(n_pair)]

    def rows_of(r):
        return slice(r * CHUNK, (r + 1) * CHUNK)

    def tile_of(p):
        return slice(p * LANES, (p + 1) * LANES)

    def sub(x, u):
        bb, r, p = u
        return x[bb][rows_of(r), tile_of(p)]

    g_last = {(bb, r): gc[bb][(r + 1) * CHUNK - 1:(r + 1) * CHUNK, :] for bb in range(n_bb) for r in range(n_ch)}
    kd = {br: kn[br[0]][rows_of(br[1])] * jnp.exp(g - gc[br[0]][rows_of(br[1])]) for br, g in g_last.items()}
    eg_last = {br: jnp.exp(g) for br, g in g_last.items()}
    dec, r_k, kdt = {}, {}, {}
    for u in units:
        gcp = sub(gc, u)
        g_row = jnp.sum(jnp.where(eye2, gcp, 0.0), axis=0, keepdims=True)
        dec[u] = jnp.where(incl2, jnp.exp(jnp.where(incl2, gcp - g_row, 0.0)), 0.0)
        kp = sub(kn, u)
        r_k[u] = _dot_nt(jnp.concatenate([sub(qn, u), kp], axis=0).astype(BF16), bdiag(kp).astype(BF16))
    for bb, r, p in units:
        kdt[bb, r, p] = _dot_nt(eye128, bdiag(kd[bb, r][:, tile_of(p)]).astype(BF16)).astype(BF16)
    nm = {u: -(sub(bc, u) * jnp.where(strict2, dec[u], 0.0) * r_k[u][CHUNK:]) for u in units}
    pw = {u: _dot(nm[u].astype(BF16), bdiag(nm[u]).astype(BF16)) for u in units}
    tm_ = {u: eye2f + nm[u] for u in units}
    for _ in range(4):
        st = {u: _dot(jnp.concatenate([tm_[u], pw[u]], axis=0).astype(BF16), bdiag(pw[u]).astype(BF16))
              for u in units}
        tm_ = {u: tm_[u] + st[u][:CHUNK] for u in units}
        pw = {u: st[u][CHUNK:] for u in units}
    fin = {u: _dot(tm_[u].astype(BF16), bdiag(pw[u]).astype(BF16)) for u in units}
    tm_ = {u: tm_[u] + fin[u] for u in units}
    wu = {u: _dot(tm_[u].astype(BF16),
                  jnp.concatenate([bdiag(sub(xk, u)), bdiag(sub(vb, u))], axis=1).astype(BF16)) for u in units}
    w_b = {u: wu[u][:, :LANES].astype(BF16) for u in units}
    qkd = {u: (dec[u] * r_k[u][:CHUNK]).astype(BF16) for u in units}

    states = {bp: s_scr[bp[0], bp[1]] for bp in lanes_bp}
    o_tiles = {}
    for r in range(n_ch):
        res = {(bb, p): _dot(jnp.concatenate([sub(qe, (bb, r, p)).astype(BF16), w_b[bb, r, p]], axis=0),
                             states[bb, p].astype(BF16)) for bb, p in lanes_bp}
        u = {(bb, p): wu[bb, r, p][:, LANES:] - res[bb, p][CHUNK:] for bb, p in lanes_bp}
        upd = {(bb, p): _dot(kdt[bb, r, p], jnp.concatenate([u[bb, p], u[bb, p]], axis=0).astype(BF16))
               for bb, p in lanes_bp}
        o_inner = {(bb, p): _dot(qkd[bb, r, p], bdiag(u[bb, p]).astype(BF16)) for bb, p in lanes_bp}
        states = {(bb, p): eg_last[bb, r][:, tile_of(p)] * states[bb, p] + jnp.where(on_diag_blocks, upd[bb, p], 0.0)
                  for bb, p in lanes_bp}
        for bb, p in lanes_bp:
            o_tiles[bb, r, p] = res[bb, p][:CHUNK] + o_inner[bb, p]
    for bb, p in lanes_bp:
        s_scr[bb, p] = states[bb, p]
    for bb in range(n_bb):
        o = jnp.concatenate([jnp.concatenate([o_tiles[bb, r, p] for p in range(n_pair)], axis=1)
                             for r in range(n_ch)], axis=0)
        ms = head_sumsq(o) * (1.0 / D_C)
        oc_ref[bb] = o * lax.rsqrt(ms + NORM_EPS) * onorm_ref[...] * _silu(gate_ref[bb])

    @pl.when(j == pl.num_programs(1) - 1)
    def _():
        for bb, p in lanes_bp:
            sout_ref[bb, 2 * p] = states[bb, p][:D_C, :D_C]
            sout_ref[bb, 2 * p + 1] = states[bb, p][D_C:, D_C:]


def _gdn_constants(tc):
    i = jnp.arange(tc, dtype=jnp.int32)
    tri = ((i[None, :] <= i[:, None]) & (i[None, :] // CHUNK == i[:, None] // CHUNK)).astype(BF16)
    l = jnp.arange(LANES, dtype=jnp.int32)
    block_ones = (l[:, None] // D_C == l[None, :] // D_C).astype(BF16)
    col_head = jnp.arange(C_W, dtype=jnp.int32)[None, :] // D_C
    e_beta = (l[:, None] == col_head).astype(BF16)
    e_g = (l[:, None] == col_head + _SMALL_DECAY0).astype(BF16)
    return tri, block_ones, e_beta, e_g


def _gdn(cqkv, small, gate, conv_prev, s0, conv_w, alog_vec, dtb_vec, onorm_tiled):
    b, t, w3 = cqkv.shape
    tc = min(4 * CHUNK, t)
    nb = GDN_BATCH_BLOCK if b % GDN_BATCH_BLOCK == 0 else 1
    consts = _gdn_constants(tc)
    body = functools.partial(_gdn_body, tc=tc)
    blk = lambda bi, j: (bi, j, 0)
    per_b3 = lambda bi, j: (bi, 0, 0)
    per_b4 = lambda bi, j: (bi, 0, 0, 0)
    const2 = lambda bi, j: (0, 0)
    return pl.pallas_call(
        body,
        grid=(b // nb, t // tc),
        in_specs=[pl.BlockSpec((nb, tc, w3), blk),
                  pl.BlockSpec((nb, tc, LANES), blk),
                  pl.BlockSpec((nb, tc, C_W), blk),
                  pl.BlockSpec((nb, CONV_W - 1, w3), per_b3),
                  pl.BlockSpec((nb, H_C, D_C, D_C), per_b4),
                  pl.BlockSpec(conv_w.shape, const2),
                  pl.BlockSpec((1, LANES), const2),
                  pl.BlockSpec((1, LANES), const2),
                  pl.BlockSpec((1, C_W), const2)]
                 + [pl.BlockSpec(c.shape, const2) for c in consts],
        out_specs=[pl.BlockSpec((nb, tc, C_W), blk),
                   pl.BlockSpec((nb, H_C, D_C, D_C), per_b4)],
        out_shape=[jax.ShapeDtypeStruct((b, t, C_W), F32),
                   jax.ShapeDtypeStruct((b, H_C, D_C, D_C), F32)],
        scratch_shapes=[pltpu.VMEM((nb, tc + 8, w3), F32), pltpu.VMEM((nb, H_C // 2, LANES, LANES), F32)],
        compiler_params=_params(("parallel", "arbitrary")),
        name="gdn",
    )(cqkv, small, gate, conv_prev, s0, conv_w, alog_vec, dtb_vec, onorm_tiled, *consts)


def _top2_gates(logits):
    lane = lax.broadcasted_iota(jnp.int32, logits.shape, 1).astype(F32)
    low = -3.0e38
    lg = jnp.where(lane < N_EXPERTS, logits, low)
    m1 = jnp.max(lg, axis=-1, keepdims=True)
    i1 = jnp.min(jnp.where(lg == m1, lane, float(LANES)), axis=-1, keepdims=True)
    lg2 = jnp.where(lane == i1, low, lg)
    m2 = jnp.max(lg2, axis=-1, keepdims=True)
    i2 = jnp.min(jnp.where(lg2 == m2, lane, float(LANES)), axis=-1, keepdims=True)
    e2 = jnp.exp(m2 - m1)
    den = 1.0 + e2
    return jnp.where(lane == i1, 1.0 / den, 0.0) + jnp.where(lane == i2, e2 / den, 0.0)


def _outproj_body(x_ref, oa_ref, ob_ref, oc_ref, w_ref, g_ref, *rest, with_router):
    if with_router:
        rhi_ref, rlo_ref, xo_ref, h_ref, gates_ref = rest
    else:
        xo_ref, h_ref = rest
    y = (_dot(oa_ref[...].astype(BF16), w_ref[0:HEAD_W, :])
         + _dot(ob_ref[...].astype(BF16), w_ref[HEAD_W:2 * HEAD_W, :])
         + _dot(oc_ref[...].astype(BF16), w_ref[2 * HEAD_W:, :]))
    x = x_ref[...] + y
    xo_ref[...] = x
    hf = _rms(x, g_ref[...])
    hb = hf.astype(BF16)
    h_ref[...] = hb
    if with_router:
        lo = (hf - hb.astype(F32)).astype(BF16)
        logits = _dot(hb, rhi_ref[...]) + _dot(lo, rhi_ref[...]) + _dot(hb, rlo_ref[...])
        gates_ref[...] = _top2_gates(logits)


def _outproj(x, oa, ob, oc, w_out, g, router=None):
    n, d = x.shape
    tm = min(ROW_TILE, n)
    row = lambda i: (i, 0)
    const = lambda i: (0, 0)
    in_specs = [pl.BlockSpec((tm, d), row), pl.BlockSpec((tm, HEAD_W), row), pl.BlockSpec((tm, HEAD_W), row),
                pl.BlockSpec((tm, C_W), row), pl.BlockSpec(w_out.shape, const), pl.BlockSpec((1, d), const)]
    out_specs = [pl.BlockSpec((tm, d), row), pl.BlockSpec((tm, d), row)]
    out_shape = [jax.ShapeDtypeStruct((n, d), F32), jax.ShapeDtypeStruct((n, d), BF16)]
    args = [x, oa, ob, oc, w_out, g]
    if router is not None:
        in_specs += [pl.BlockSpec(router[0].shape, const), pl.BlockSpec(router[1].shape, const)]
        out_specs.append(pl.BlockSpec((tm, LANES), row))
        out_shape.append(jax.ShapeDtypeStruct((n, LANES), F32))
        args += list(router)
    return pl.pallas_call(
        functools.partial(_outproj_body, with_router=router is not None),
        grid=(n // tm,),
        in_specs=in_specs, out_specs=out_specs, out_shape=out_shape,
        compiler_params=_params(("parallel",)),
        name="outproj",
    )(*args)


def _ffn_body(h_ref, x_ref, wg_ref, wu_ref, wd_ref, o_ref, acc_ref, *, ff_chunk):
    h = h_ref[...]
    d_ff = wg_ref.shape[1]
    acc_ref[...] = x_ref[...]
    for c0 in range(0, d_ff, ff_chunk):
        a = _silu(_dot(h, wg_ref[:, c0:c0 + ff_chunk])) * _dot(h, wu_ref[:, c0:c0 + ff_chunk])
        acc_ref[...] += _dot(a.astype(BF16), wd_ref[c0:c0 + ff_chunk, :])
    o_ref[...] = acc_ref[...]


def _ffn_dense(h, x, wg, wu, wd):
    n, d = x.shape
    d_ff = wg.shape[1]
    tm = min(ROW_TILE, n)
    ff_chunk = 256 if d_ff % 256 == 0 else LANES
    row = lambda i: (i, 0)
    const = lambda i: (0, 0)
    return pl.pallas_call(
        functools.partial(_ffn_body, ff_chunk=ff_chunk),
        grid=(n // tm,),
        in_specs=[pl.BlockSpec((tm, d), row), pl.BlockSpec((tm, d), row),
                  pl.BlockSpec(wg.shape, const), pl.BlockSpec(wu.shape, const), pl.BlockSpec(wd.shape, const)],
        out_specs=pl.BlockSpec((tm, d), row),
        out_shape=jax.ShapeDtypeStruct((n, d), F32),
        scratch_shapes=[pltpu.VMEM((tm, d), F32)],
        compiler_params=_params(("parallel",)),
        name="ffn_dense",
    )(h, x, wg, wu, wd)


def _moe_body(pre_ref, h_ref, gates_ref, gates_t_ref, wg_ref, wu_ref, wd_ref, y_ref,
              rank_col, rank_row, hs, eo, gacc, *, tg, n_f):
    g = pl.program_id(0)
    e = pl.program_id(1)
    f = pl.program_id(2)
    tm_full = MOE_ROW_TILE
    tm_tail = MOE_ROW_TILE // 2
    tb = MOE_TOKEN_BLOCK
    n_tb = tg // tb
    pre0 = (g * N_EXPERTS + e) * (n_tb + 1)
    count = pre_ref[pre0 + n_tb]
    n_full = (count + tm_tail - 1) // tm_full
    has_tail = count > n_full * tm_full

    def for_tiles(fn):
        def full(t, carry):
            fn(pl.multiple_of(t * tm_full, tm_full), tm_full)
            return carry

        lax.fori_loop(0, n_full, full, 0)

        @pl.when(has_tail)
        def _():
            fn(pl.multiple_of(n_full * tm_full, tm_tail), tm_tail)

    def overlaps(r0, tm, b):
        return (pre_ref[pre0 + b] < r0 + tm) & (pre_ref[pre0 + b + 1] > r0)

    @pl.when((e == 0) & (f == 0))
    def _():
        y_ref[...] = jnp.zeros_like(y_ref)
        ii = lax.broadcasted_iota(jnp.int32, (tb, tb), 0)
        jj = lax.broadcasted_iota(jnp.int32, (tb, tb), 1)
        lower = jnp.where(ii > jj, 1.0, 0.0).astype(BF16)
        upper = jnp.where(ii < jj, 1.0, 0.0).astype(BF16)
        carry_c = jnp.zeros((1, LANES), F32)
        carry_r = jnp.zeros((N_EXPERTS, 1), F32)
        for b in range(n_tb):
            mc = jnp.where(gates_ref[b * tb:(b + 1) * tb, :] > 0.0, 1.0, 0.0)
            rank_col[b * tb:(b + 1) * tb, :] = _dot(lower, mc.astype(BF16)) + carry_c
            carry_c = carry_c + jnp.sum(mc, axis=0, keepdims=True)
            mr = jnp.where(gates_t_ref[:, b * tb:(b + 1) * tb] > 0.0, 1.0, 0.0)
            rank_row[:, b * tb:(b + 1) * tb] = _dot(mr.astype(BF16), upper) + carry_r
            carry_r = carry_r + jnp.sum(mr, axis=1, keepdims=True)

    def gather(r0, tm):
        rows = (r0 + lax.broadcasted_iota(jnp.int32, (tm, tb), 0)).astype(F32)
        gacc[0:tm, :] = jnp.zeros((tm, gacc.shape[1]), F32)
        for b in range(n_tb):
            @pl.when(overlaps(r0, tm, b))
            def _(b=b):
                gr = gates_t_ref[pl.ds(e, 1), b * tb:(b + 1) * tb]
                rr = jnp.where(gr > 0.0, rank_row[pl.ds(e, 1), b * tb:(b + 1) * tb], -1.0)
                onehot = jnp.where(rr == rows, 1.0, 0.0).astype(BF16)
                gacc[0:tm, :] += _dot(onehot, h_ref[b * tb:(b + 1) * tb, :])
        hs[pl.ds(r0, tm), :] = gacc[0:tm, :].astype(BF16)

    @pl.when(f == 0)
    def _():
        for_tiles(gather)

    def expert(r0, tm):
        x = hs[pl.ds(r0, tm), :]
        a = _silu(_dot(x, wg_ref[...])) * _dot(x, wu_ref[...])
        part = _dot(a.astype(BF16), wd_ref[...])

        @pl.when(f == 0)
        def _():
            eo[pl.ds(r0, tm), :] = part

        @pl.when(f > 0)
        def _():
            eo[pl.ds(r0, tm), :] += part

    for_tiles(expert)

    def scatter(r0, tm):
        lane = lax.broadcasted_iota(jnp.int32, (tb, LANES), 1)
        cols = (r0 + lax.broadcasted_iota(jnp.int32, (tb, tm), 1)).astype(F32)
        out = eo[pl.ds(r0, tm), :].astype(BF16)
        for b in range(n_tb):
            @pl.when(overlaps(r0, tm, b))
            def _(b=b):
                sel = lane == e
                gc = jnp.sum(jnp.where(sel, gates_ref[b * tb:(b + 1) * tb, :], 0.0), axis=-1, keepdims=True)
                rc = jnp.sum(jnp.where(sel, rank_col[b * tb:(b + 1) * tb, :], 0.0), axis=-1, keepdims=True)
                rc = jnp.where(gc > 0.0, rc, -1.0)
                weighted = jnp.where(rc == cols, gc, 0.0).astype(BF16)
                y_ref[b * tb:(b + 1) * tb, :] += _dot(weighted, out)

    @pl.when(f == n_f - 1)
    def _():
        for_tiles(scatter)


def _moe_prefix(gates, tg):
    n = gates.shape[0]
    n_tb = tg // MOE_TOKEN_BLOCK
    routed = (gates[:, :N_EXPERTS] > 0.0).astype(jnp.int32)
    per_block = routed.reshape(n // tg, n_tb, MOE_TOKEN_BLOCK, N_EXPERTS).sum(axis=2)
    run = jnp.cumsum(per_block, axis=1)
    pre = jnp.concatenate([jnp.zeros_like(run[:, :1]), run], axis=1)
    return jnp.transpose(pre, (0, 2, 1)).reshape(-1)


def _moe(h, gates, wg, wu, wd, tg):
    n, d = h.shape
    n_e, _, d_ff = wg.shape
    n_f = 2
    ffh = d_ff // n_f
    body = functools.partial(_moe_body, tg=tg, n_f=n_f)
    once = pl.Buffered(1)
    grid_spec = pltpu.PrefetchScalarGridSpec(
        num_scalar_prefetch=1,
        grid=(n // tg, n_e, n_f),
        in_specs=[pl.BlockSpec((tg, d), lambda g, e, f, c: (g, 0), pipeline_mode=once),
                  pl.BlockSpec((tg, LANES), lambda g, e, f, c: (g, 0), pipeline_mode=once),
                  pl.BlockSpec((N_EXPERTS, tg), lambda g, e, f, c: (0, g), pipeline_mode=once),
                  pl.BlockSpec((None, d, ffh), lambda g, e, f, c: (e, 0, f)),
                  pl.BlockSpec((None, d, ffh), lambda g, e, f, c: (e, 0, f)),
                  pl.BlockSpec((None, ffh, d), lambda g, e, f, c: (e, f, 0))],
        out_specs=pl.BlockSpec((tg, d), lambda g, e, f, c: (g, 0), pipeline_mode=once),
        scratch_shapes=[pltpu.VMEM((tg, LANES), F32), pltpu.VMEM((N_EXPERTS, tg), F32),
                        pltpu.VMEM((tg, d), BF16), pltpu.VMEM((tg, d), F32),
                        pltpu.VMEM((MOE_ROW_TILE, d), F32)],
    )
    return pl.pallas_call(
        body,
        grid_spec=grid_spec,
        out_shape=jax.ShapeDtypeStruct((n, d), F32),
        compiler_params=_params(("parallel", "arbitrary", "arbitrary")),
        name="moe",
    )(_moe_prefix(gates, tg), h, gates, gates[:, :N_EXPERTS].T, wg, wu, wd)


def _final_body(x_ref, y_ref, g_ref, o_ref):
    o_ref[...] = _rms(x_ref[...] + y_ref[...], g_ref[...])


def _final_norm(x, y, g):
    n, d = x.shape
    tm = min(ROW_TILE, n)
    row = lambda i: (i, 0)
    return pl.pallas_call(
        _final_body,
        grid=(n // tm,),
        in_specs=[pl.BlockSpec((tm, d), row), pl.BlockSpec((tm, d), row), pl.BlockSpec((1, d), lambda i: (0, 0))],
        out_specs=pl.BlockSpec((tm, d), row),
        out_shape=jax.ShapeDtypeStruct((n, d), F32),
        compiler_params=_params(("parallel",)),
        name="final_norm",
    )(x, y, g)


def _t5_bucket(rel):
    nb = T5_BUCKETS // 2
    max_exact = nb // 2
    ret = jnp.where(rel > 0, nb, 0)
    n = jnp.abs(rel)
    nf = jnp.maximum(n, 1).astype(F32)
    large = max_exact + (jnp.log(nf / max_exact) / math.log(T5_MAX_DIST / max_exact) * (nb - max_exact)).astype(jnp.int32)
    large = jnp.minimum(large, nb - 1)
    return ret + jnp.where(n < max_exact, n, large)


def _t5_table(t5_bias, q_pos, k_pos):
    bias = _lookup(t5_bias, _t5_bucket(k_pos[None, :] - q_pos[:, None]))
    mask = (k_pos[None, :] // CHUNK) <= (q_pos[:, None] // CHUNK)
    return bias, mask[None]


def _lookup(table, idx):
    onehot = jax.nn.one_hot(idx, table.shape[0], dtype=F32)
    return jnp.einsum("qkn,nh->hqk", onehot, table.astype(F32), precision=lax.Precision.HIGHEST)


def _band_table(rel_bias):
    qi = jnp.arange(CHUNK, dtype=jnp.int32)
    kj = jnp.arange(BAND, dtype=jnp.int32) - B_LEFT_CHUNKS * CHUNK
    rel = jnp.clip(kj[None, :] - qi[:, None], -B_REL_CLIP, B_REL_CLIP) + B_REL_CLIP
    base = jnp.swapaxes(_lookup(rel_bias, rel), 1, 2) * LOG2E
    halves = [jnp.pad(base, ((0, 0), (c * CHUNK, (1 - c) * CHUNK), (0, 0)), constant_values=NEG_INF)
              for c in range(2)]
    return jnp.concatenate(halves, axis=2)


def _lane_vec(v, offset):
    return jnp.zeros((1, LANES), F32).at[0, offset:offset + v.shape[0]].set(v.astype(F32))


def _moe_group(n):
    for tg in (2048, 1024, 512, 256):
        if n % tg == 0:
            return tg
    raise ValueError(f"token count {n} is not a multiple of 256")


def kernel(x_prompt, x_sample, cache_a_k, cache_a_v, cache_b_k, cache_b_v, cache_c_conv, state_c_ssm, w_in, w_out, norm_mix, norm_ffn, norm_final, lam_qk, subln_a, t5_bias, rel_bias_b, conv_c, a_log_c, dt_bias_c, onorm_c, ffn_gate, ffn_up, ffn_down, moe_router, moe_gate, moe_up, moe_down):
    depth = w_in.shape[0]
    bp, sp, d = x_prompt.shape
    bs, ss, _ = x_sample.shape
    past = cache_a_k.shape[2]
    nb_cache = cache_b_k.shape[2]
    assert ss == CHUNK and nb_cache == B_LEFT_CHUNKS * CHUNK and sp >= B_LEFT_CHUNKS * CHUNK

    xp = x_prompt.reshape(bp * sp, d)
    xs = x_sample.reshape(bs * ss, d)
    blk = min(A_BLOCK, sp)

    pos_blk = jnp.arange(blk, dtype=jnp.int32)
    far_bias = t5_bias[_t5_bucket(jnp.int32(-(blk + 1)))].astype(F32)
    bd, md = _t5_table(t5_bias, pos_blk, pos_blk)
    bias_diag = jnp.where(md, bd - far_bias[:, None, None], NEG_INF)
    bsub, _ = _t5_table(t5_bias, blk + pos_blk, pos_blk)
    bias_sub = bsub - far_bias[:, None, None]
    q_pos_s = past + jnp.arange(ss, dtype=jnp.int32)
    bfull, mfull = _t5_table(t5_bias, q_pos_s, jnp.arange(past + ss, dtype=jnp.int32))
    bias_s = jnp.where(mfull, bfull, NEG_INF)
    bias_s_cache, bias_s_new = bias_s[:, :, :past], bias_s[:, :, past:]

    zeros_conv = jnp.zeros((bp, CONV_W - 1, 3 * C_W), F32)
    zeros_state = jnp.zeros((bp, H_C, D_C, D_C), F32)

    p_states = [[] for _ in range(6)]
    s_states = [[] for _ in range(6)]
    yp = ys = None
    for l in range(depth):
        if yp is not None:
            xp, xs, yp, ys = xp + yp, xs + ys, None, None
        w = w_in[l]
        n_main = 6 * HEAD_W + 3 * C_W
        w_main = jnp.concatenate([w[:, :n_main], w[:, n_main + 2 * H_C:]], axis=1).astype(BF16)
        w_small = jnp.zeros((d, LANES), F32).at[:, :2 * H_C].set(w[:, n_main:n_main + 2 * H_C]).astype(BF16)
        w_out_l = w_out[l].astype(BF16)
        g_mix = norm_mix[l].reshape(1, d)
        g_ffn = norm_ffn[l].reshape(1, d)
        lam_init = 0.8 - 0.6 * math.exp(-0.3 * l)
        lq = lam_qk[l].astype(F32)
        lam = (jnp.exp(jnp.sum(lq[0] * lq[1])) - jnp.exp(jnp.sum(lq[2] * lq[3])) + lam_init).reshape(1)
        subln = subln_a[l].reshape(1, DV_A)
        band_bias = _band_table(rel_bias_b[l])
        alog_vec = _lane_vec(a_log_c[l], _SMALL_DECAY0)
        dtb_vec = _lane_vec(dt_bias_c[l], _SMALL_DECAY0)
        onorm = jnp.tile(onorm_c[l].astype(F32), H_C).reshape(1, C_W)
        is_moe = l % 2 == 1
        if is_moe:
            r = jnp.zeros((d, LANES), F32).at[:, :N_EXPERTS].set(moe_router[l // 2])
            r_hi = r.astype(BF16)
            router = (r_hi, (r - r_hi.astype(F32)).astype(BF16))
            e_wg, e_wu, e_wd = (moe_gate[l // 2].astype(BF16), moe_up[l // 2].astype(BF16),
                                moe_down[l // 2].astype(BF16))
        else:
            router = None
            f_wg, f_wu, f_wd = (ffn_gate[l // 2].astype(BF16), ffn_up[l // 2].astype(BF16),
                                ffn_down[l // 2].astype(BF16))

        new_x = []
        for is_prompt, x in ((True, xp), (False, xs)):
            b, t = (bp, sp) if is_prompt else (bs, ss)
            aq, ak, av, bq, bk, bv, cqkv, cgate, csmall = _inproj(x, g_mix, w_main, w_small)
            r3 = lambda a: a.reshape(b, t, a.shape[-1])
            if is_prompt:
                oa = _attn_a_prompt(lam, r3(aq), r3(ak), r3(av), jnp.swapaxes(bias_diag, 1, 2) * LOG2E,
                                    jnp.swapaxes(bias_sub, 1, 2) * LOG2E, subln.reshape(DV_A, 1), 1.0 - lam_init)
                pad = ((0, 0), (B_LEFT_CHUNKS * CHUNK, 0), (0, 0))
                ob = _band_attn(r3(bq), jnp.pad(r3(bk), pad), jnp.pad(r3(bv), pad), band_bias,
                                B_LEFT_CHUNKS * CHUNK)
                conv_prev, s0 = zeros_conv, zeros_state
            else:
                oa = _attn_a_sample(lam, r3(aq), cache_a_k[l].reshape(b, past, HEAD_W),
                                    cache_a_v[l].reshape(b, past, HEAD_W), r3(ak), r3(av),
                                    bias_s_cache, bias_s_new, subln, 1.0 - lam_init)
                kb = jnp.concatenate([cache_b_k[l].reshape(b, nb_cache, HEAD_W), r3(bk)], axis=1)
                vb = jnp.concatenate([cache_b_v[l].reshape(b, nb_cache, HEAD_W), r3(bv)], axis=1)
                ob = _band_attn(r3(bq), kb, vb, band_bias, 0)
                conv_prev, s0 = cache_c_conv[l], state_c_ssm[l]
            oc, s_new = _gdn(r3(cqkv), r3(csmall), r3(cgate), conv_prev, s0, conv_c[l], alog_vec, dtb_vec, onorm)
            n = b * t
            res = _outproj(x, oa.reshape(n, HEAD_W), ob.reshape(n, HEAD_W), oc.reshape(n, C_W), w_out_l, g_ffn,
                           router)
            if is_moe:
                x_new, h2, gates = res
                y_moe = _moe(h2, gates, e_wg, e_wu, e_wd, _moe_group(n))
                new_x.append((x_new, y_moe))
            else:
                x_new, h2 = res
                new_x.append((_ffn_dense(h2, x_new, f_wg, f_wu, f_wd), None))
            keep = min(B_LEFT_CHUNKS * CHUNK, t)
            states = (ak.reshape(b, t, H_A, 2 * DQK_A), av.reshape(b, t, H_A, DV_A),
                      bk.reshape(b, t, H_B, D_B)[:, t - keep:], bv.reshape(b, t, H_B, D_B)[:, t - keep:],
                      r3(cqkv)[:, t - (CONV_W - 1):], s_new)
            for i in range(6):
                (p_states if is_prompt else s_states)[i].append(states[i])
        (xp, yp), (xs, ys) = new_x

    g_fin = norm_final.reshape(1, d)
    if yp is None:
        yp, ys = jnp.zeros_like(xp), jnp.zeros_like(xs)
    y_prompt = _final_norm(xp, yp, g_fin).reshape(bp, sp, d)
    y_sample = _final_norm(xs, ys, g_fin).reshape(bs, ss, d)
    p_out = [jnp.stack(s, axis=0) for s in p_states]
    s_out = [jnp.stack(s, axis=0) for s in s_states]
    return (y_prompt, y_sample, *p_out, *s_out)
```

```python
import functools
import math

import jax
import jax.numpy as jnp
from jax import lax
from jax.experimental import pallas as pl
from jax.experimental.pallas import tpu as pltpu

F32 = jnp.float32
BF16 = jnp.bfloat16

CHUNK = 64
H_A = 4
DQK_A = 32
DV_A = 64
H_B = 4
D_B = 64
B_LEFT_CHUNKS = 8
B_REL_CLIP = 128
H_C = 8
D_C = 64
CONV_W = 4
T5_BUCKETS = 32
T5_MAX_DIST = 128
N_EXPERTS = 8
NORM_EPS = 1e-6
L2_EPS = 1e-6
NEG_INF = -1e30
HEAD_W = 256
C_W = H_C * D_C
BAND = (B_LEFT_CHUNKS + 1) * CHUNK
LANES = 128
LOG2E = math.log2(math.e)
VMEM_LIMIT = 56 * 1024 * 1024

A_BLOCK = 256
ROW_TILE = 512
GDN_BATCH_BLOCK = 2
MOE_ROW_TILE = 256
MOE_TOKEN_BLOCK = 256


def _params(sem):
    return pltpu.CompilerParams(dimension_semantics=sem, vmem_limit_bytes=VMEM_LIMIT)


def _rms(x, g):
    ms = jnp.mean(x * x, axis=-1, keepdims=True)
    return x * lax.rsqrt(ms + NORM_EPS) * g


def _sigmoid(x):
    return 1.0 / (1.0 + jnp.exp(-x))


def _silu(x):
    return x * _sigmoid(x)


def _softplus(x):
    return jnp.maximum(x, 0.0) + jnp.log1p(jnp.exp(-jnp.abs(x)))


def _dot(a, b):
    return jnp.dot(a, b, preferred_element_type=F32)


def _dot_nt(a, b):
    return lax.dot_general(a, b, (((1,), (1,)), ((), ())), preferred_element_type=F32)


def _dot_tn(a, b):
    return lax.dot_general(a, b, (((0,), (0,)), ((), ())), preferred_element_type=F32)


def _split3(x):
    x1 = x.astype(BF16)
    r1 = x - x1.astype(F32)
    x2 = r1.astype(BF16)
    x3 = (r1 - x2.astype(F32)).astype(BF16)
    return x1, x2, x3


def _inproj_body(x_ref, g_ref, w_ref, ws_ref, aq_ref, ak_ref, av_ref, bq_ref, bk_ref, bv_ref,
                 cqkv_ref, cgate_ref, csmall_ref):
    h = _rms(x_ref[...], g_ref[...]).astype(BF16)
    col = 0
    for ref in (aq_ref, ak_ref, av_ref, bq_ref, bk_ref, bv_ref, cqkv_ref, cgate_ref):
        width = ref.shape[-1]
        ref[...] = _dot(h, w_ref[:, col:col + width])
        col += width
    csmall_ref[...] = _dot(h, ws_ref[...])


def _inproj(x, g, w_main, w_small):
    n, d = x.shape
    tm = min(ROW_TILE, n)
    widths = (HEAD_W,) * 6 + (3 * C_W, C_W, LANES)
    row = lambda i: (i, 0)
    const = lambda i: (0, 0)
    return pl.pallas_call(
        _inproj_body,
        grid=(n // tm,),
        in_specs=[pl.BlockSpec((tm, d), row), pl.BlockSpec((1, d), const),
                  pl.BlockSpec(w_main.shape, const), pl.BlockSpec(w_small.shape, const)],
        out_specs=[pl.BlockSpec((tm, w), row) for w in widths],
        out_shape=[jax.ShapeDtypeStruct((n, w), F32) for w in widths],
        compiler_params=_params(("parallel",)),
        name="inproj",
    )(x, g, w_main, w_small)


A_ACC_ROWS = DV_A + 16
A_GROUPS = LANES // DQK_A


def _attn_a_prompt_body(lam_ref, q_ref, k_ref, v_ref, bd_ref, bs_ref, subln_ref, o_ref,
                        kbf, vte, wq, m_scr, acc_scr, s_even, s_odd, *, blk, post_scale):
    i = pl.program_id(1)
    n_blk = k_ref.shape[0] // blk
    n_pair = H_A // 2

    @pl.when(i == 0)
    def _():
        ones_rows = jnp.where(lax.broadcasted_iota(jnp.int32, (A_ACC_ROWS - DV_A, blk), 0) == 0, 1.0, 0.0)
        for jb in range(n_blk):
            kbf[jb] = k_ref[jb * blk:(jb + 1) * blk, :].astype(BF16)
            vt = v_ref[jb * blk:(jb + 1) * blk, :].T
            for h in range(H_A):
                vte[jb, h, 0:DV_A, :] = vt[h * DV_A:(h + 1) * DV_A, :].astype(BF16)
                vte[jb, h, DV_A:, :] = ones_rows.astype(BF16)

    lam = lam_ref[0]
    qt = (q_ref[...] * (DQK_A ** -0.5 * LOG2E)).T
    grp = lax.broadcasted_iota(jnp.int32, (LANES, blk), 0) // DQK_A
    for p in range(n_pair):
        qtp = qt[p * LANES:(p + 1) * LANES, :]
        wq[p] = jnp.concatenate([jnp.where(grp == g, qtp, 0.0) for g in range(A_GROUPS)], axis=1).astype(BF16)
    m_scr[...] = jnp.full(m_scr.shape, NEG_INF, F32)
    acc_scr[...] = jnp.zeros(acc_scr.shape, F32)

    def scores(jb):
        return [_dot(kbf[jb, :, p * LANES:(p + 1) * LANES], wq[p]) for p in range(n_pair)]

    def consume(tile_of, jb, bias_ref, off):
        groups = range(n_pair * A_GROUPS)
        alphas, pts = [], []
        for g in groups:
            p, gi = divmod(g, A_GROUPS)
            s = tile_of(p, gi)
            if bias_ref is not None:
                s = s + bias_ref[2 * p + gi // 2]
            if off is not None:
                s = s + off
            m_old = m_scr[g:g + 1, :]
            m_new = jnp.maximum(m_old, jnp.max(s, axis=0, keepdims=True))
            alphas.append(jnp.exp2(m_old - m_new))
            pts.append(jnp.exp2(s - m_new).astype(BF16))
            m_scr[g:g + 1, :] = m_new
        pvs = [_dot(vte[jb, 2 * (g // A_GROUPS) + (g % A_GROUPS) // 2], pts[g]) for g in groups]
        for g in groups:
            acc_scr[g] = alphas[g] * acc_scr[g] + pvs[g]

    def tiles(jb, bias_ref, off):
        s_alls = scores(jb)
        consume(lambda p, gi: s_alls[p][:, gi * blk:(gi + 1) * blk], jb, bias_ref, off)

    tiles(i, bd_ref, None)
    tiles(jnp.maximum(i - 1, 0), bs_ref, jnp.where(i >= 1, 0.0, NEG_INF))

    n_far = jnp.maximum(i - 1, 0)

    def put_scores(dst, jb):
        s_alls = scores(jnp.minimum(jb, n_far - 1))
        for p in range(n_pair):
            dst[p] = s_alls[p]

    def from_scratch(src):
        return lambda p, gi: src[p, :, gi * blk:(gi + 1) * blk]

    @pl.when(n_far > 0)
    def _():
        put_scores(s_even, 0)

    def far(t, carry):
        jb = 2 * t
        put_scores(s_odd, jb + 1)
        consume(from_scratch(s_even), jb, None, None)

        @pl.when(jb + 1 < n_far)
        def _():
            put_scores(s_even, jb + 2)
            consume(from_scratch(s_odd), jb + 1, None, None)

        return carry

    lax.fori_loop(0, (n_far + 1) // 2, far, 0)

    outs = []
    for h in range(H_A):
        g1 = (h // 2) * A_GROUPS + (h % 2) * 2
        a1 = acc_scr[g1]
        a2 = acc_scr[g1 + 1]
        o = a1[:DV_A] / a1[DV_A:DV_A + 1] - lam * (a2[:DV_A] / a2[DV_A:DV_A + 1])
        ms = jnp.mean(o * o, axis=0, keepdims=True)
        outs.append(o * lax.rsqrt(ms + NORM_EPS) * subln_ref[...] * post_scale)
    o_ref[...] = jnp.concatenate(outs, axis=0).T


def _attn_a_prompt(lam, q, k, v, bias_diag_t, bias_sub_t, subln_col, post_scale):
    b, s, w = q.shape
    blk = min(A_BLOCK, s)
    n_blk = s // blk
    body = functools.partial(_attn_a_prompt_body, blk=blk, post_scale=post_scale)
    return pl.pallas_call(
        body,
        grid=(b, n_blk),
        in_specs=[pl.BlockSpec(memory_space=pltpu.SMEM),
                  pl.BlockSpec((None, blk, w), lambda bi, i: (bi, i, 0)),
                  pl.BlockSpec((None, s, w), lambda bi, i: (bi, 0, 0)),
                  pl.BlockSpec((None, s, w), lambda bi, i: (bi, 0, 0)),
                  pl.BlockSpec(bias_diag_t.shape, lambda bi, i: (0, 0, 0)),
                  pl.BlockSpec(bias_sub_t.shape, lambda bi, i: (0, 0, 0)),
                  pl.BlockSpec((DV_A, 1), lambda bi, i: (0, 0))],
        out_specs=pl.BlockSpec((None, blk, w), lambda bi, i: (bi, i, 0)),
        out_shape=jax.ShapeDtypeStruct((b, s, w), F32),
        scratch_shapes=[pltpu.VMEM((n_blk, blk, w), BF16),
                        pltpu.VMEM((n_blk, H_A, A_ACC_ROWS, blk), BF16),
                        pltpu.VMEM((H_A // 2, LANES, A_GROUPS * blk), BF16),
                        pltpu.VMEM((H_A * 2, blk), F32),
                        pltpu.VMEM((H_A * 2, A_ACC_ROWS, blk), F32),
                        pltpu.VMEM((H_A // 2, blk, A_GROUPS * blk), F32),
                        pltpu.VMEM((H_A // 2, blk, A_GROUPS * blk), F32)],
        compiler_params=_params(("parallel", "arbitrary")),
        name="attn_a_prompt",
    )(lam, q, k, v, bias_diag_t, bias_sub_t, subln_col)


def _attn_a_sample_body(lam_ref, q_ref, kc_ref, vc_ref, kn_ref, vn_ref, bc_ref, bn_ref, subln_ref, o_ref, vte_c,
                        *, post_scale):
    t = q_ref.shape[0]
    p_len = kc_ref.shape[0]
    n_pair = H_A // 2
    lam = lam_ref[0]

    ones_rows = jnp.where(lax.broadcasted_iota(jnp.int32, (A_ACC_ROWS - DV_A, LANES), 0) == 0, 1.0, 0.0).astype(BF16)
    for jb in range(p_len // LANES):
        cols = slice(jb * LANES, (jb + 1) * LANES)
        vt = vc_ref[cols, :].T
        for h in range(H_A):
            vte_c[h, 0:DV_A, cols] = vt[h * DV_A:(h + 1) * DV_A, :].astype(BF16)
            vte_c[h, DV_A:, cols] = ones_rows
    row_pad = jnp.zeros((LANES - t, q_ref.shape[1]), F32)
    vt_n = jnp.concatenate([vn_ref[...], row_pad], axis=0).T
    vte_n = [jnp.concatenate([vt_n[h * DV_A:(h + 1) * DV_A, :].astype(BF16), ones_rows], axis=0) for h in range(H_A)]
    kn_p = jnp.concatenate([kn_ref[...], row_pad], axis=0).astype(BF16)

    qt = jnp.concatenate([q_ref[...] * (DQK_A ** -0.5 * LOG2E), row_pad], axis=0).T
    grp = lax.broadcasted_iota(jnp.int32, (LANES, t), 0) // DQK_A
    s_c, s_n = [], []
    for p in range(n_pair):
        qtp = qt[p * LANES:(p + 1) * LANES, 0:t]
        w = jnp.concatenate([jnp.where(grp == g, qtp, 0.0) for g in range(A_GROUPS)], axis=1).astype(BF16)
        s_c.append(_dot(kc_ref[:, p * LANES:(p + 1) * LANES].astype(BF16), w))
        s_n.append(_dot(kn_p[:, p * LANES:(p + 1) * LANES], w))
    pcs, pns = [], []
    for h in range(H_A):
        cols = slice((h % 2) * 2 * t, (h % 2 + 1) * 2 * t)
        sc = s_c[h // 2][:, cols] + jnp.concatenate([bc_ref[h], bc_ref[h]], axis=1)
        sn = s_n[h // 2][:, cols] + jnp.concatenate([bn_ref[h], bn_ref[h]], axis=1)
        mx = jnp.maximum(jnp.max(sc, axis=0, keepdims=True), jnp.max(sn, axis=0, keepdims=True))
        pcs.append(jnp.exp2(sc - mx).astype(BF16))
        pns.append(jnp.exp2(sn - mx).astype(BF16))
    accs = [_dot(vte_c[h], pcs[h]) + _dot(vte_n[h], pns[h]) for h in range(H_A)]
    outs = []
    for a in accs:
        on = a[:DV_A] / a[DV_A:DV_A + 1]
        o = on[:, 0:t] - lam * on[:, t:2 * t]
        ms = jnp.mean(o * o, axis=0, keepdims=True)
        outs.append(o * lax.rsqrt(ms + NORM_EPS) * subln_ref[...] * post_scale)
    ot = jnp.concatenate(outs, axis=0)
    ot = jnp.concatenate([ot, jnp.zeros((ot.shape[0], LANES - t), F32)], axis=1)
    o_ref[...] = ot.T[0:t, :]


def _attn_a_sample(lam, q, kc, vc, kn, vn, bias_c_t, bias_n_t, subln_col, post_scale):
    b, t, w = q.shape
    p = kc.shape[1]
    assert 2 * t == LANES and p % LANES == 0
    body = functools.partial(_attn_a_sample_body, post_scale=post_scale)
    per_b = lambda bi: (bi, 0, 0)
    const3 = lambda bi: (0, 0, 0)
    return pl.pallas_call(
        body,
        grid=(b,),
        in_specs=[pl.BlockSpec(memory_space=pltpu.SMEM),
                  pl.BlockSpec((None, t, w), per_b),
                  pl.BlockSpec((None, p, w), per_b), pl.BlockSpec((None, p, w), per_b),
                  pl.BlockSpec((None, t, w), per_b), pl.BlockSpec((None, t, w), per_b),
                  pl.BlockSpec(bias_c_t.shape, const3, pipeline_mode=pl.Buffered(1)),
                  pl.BlockSpec(bias_n_t.shape, const3),
                  pl.BlockSpec((DV_A, 1), lambda bi: (0, 0))],
        out_specs=pl.BlockSpec((None, t, w), per_b),
        out_shape=jax.ShapeDtypeStruct((b, t, w), F32),
        scratch_shapes=[pltpu.VMEM((H_A, A_ACC_ROWS, p), BF16)],
        compiler_params=_params(("parallel",)),
        name="attn_a_sample",
    )(lam, q, kc, vc, kn, vn, bias_c_t, bias_n_t, subln_col)


B_PAIR = 2 * CHUNK
B_UNION = BAND + CHUNK
B_UNION_BLOCKS = B_UNION // LANES
B_ACC_ROWS = D_B + 16


def _band_body(q_ref, k_ref, v_ref, bias_ref, o_ref, kbf, vte, *, n_grp, n_invalid):
    i = pl.program_id(1)
    n_kblk = k_ref.shape[0] // LANES

    @pl.when(i == 0)
    def _():
        ones_rows = jnp.where(lax.broadcasted_iota(jnp.int32, (B_ACC_ROWS - D_B, LANES), 0) == 0, 1.0, 0.0)
        for jb in range(n_kblk):
            kbf[jb] = k_ref[jb * LANES:(jb + 1) * LANES, :].astype(BF16)
            vt = v_ref[jb * LANES:(jb + 1) * LANES, :].T
            for h in range(H_B):
                vte[jb, h, 0:D_B, :] = vt[h * D_B:(h + 1) * D_B, :].astype(BF16)
                vte[jb, h, D_B:, :] = ones_rows.astype(BF16)

    top = lax.broadcasted_iota(jnp.int32, (LANES, B_PAIR), 0) < D_B
    key_row = lax.broadcasted_iota(jnp.int32, (B_UNION, B_PAIR), 0)

    def group(gl, carry):
        g = i * n_grp + gl
        r0 = pl.multiple_of(gl * B_PAIR, B_PAIR)
        qt = (q_ref[pl.ds(r0, B_PAIR), :] * (D_B ** -0.5 * LOG2E)).T
        blocks = [jnp.maximum(g + t - n_invalid // LANES, 0) for t in range(B_UNION_BLOCKS)]
        kun = jnp.concatenate([kbf[jb] for jb in blocks], axis=0)
        s_pairs = []
        for p in range(H_B // 2):
            qtp = qt[p * LANES:(p + 1) * LANES, :]
            w = jnp.concatenate([jnp.where(top, qtp, 0.0), jnp.where(top, 0.0, qtp)], axis=1).astype(BF16)
            s_pairs.append(_dot(kun[:, p * LANES:(p + 1) * LANES], w))
        pts = []
        for h in range(H_B):
            s = s_pairs[h // 2][:, (h % 2) * B_PAIR:(h % 2 + 1) * B_PAIR] + bias_ref[h]
            if n_invalid:
                s = jnp.where(g * B_PAIR + key_row >= n_invalid, s, NEG_INF)
            pts.append(jnp.exp2(s - jnp.max(s, axis=0, keepdims=True)).astype(BF16))
        accs = [_dot(jnp.concatenate([vte[jb, h] for jb in blocks], axis=1), pts[h]) for h in range(H_B)]
        ot = jnp.concatenate([a[:D_B] / a[D_B:D_B + 1] for a in accs], axis=0)
        o_ref[pl.ds(r0, B_PAIR), :] = ot.T
        return carry

    lax.fori_loop(0, n_grp, group, 0)


def _band_attn(q, k, v, bias_t, n_invalid):
    b, t, w = q.shape
    assert n_invalid % LANES == 0
    tq = -(-t // B_PAIR) * B_PAIR
    if tq != t:
        q = jnp.pad(q, ((0, 0), (0, tq - t), (0, 0)))
    tk = tq + B_LEFT_CHUNKS * CHUNK - n_invalid
    if k.shape[1] != tk:
        k = jnp.pad(k, ((0, 0), (0, tk - k.shape[1]), (0, 0)))
        v = jnp.pad(v, ((0, 0), (0, tk - v.shape[1]), (0, 0)))
    qb = min(8 * CHUNK, tq)
    n_kblk = tk // LANES
    body = functools.partial(_band_body, n_grp=qb // B_PAIR, n_invalid=n_invalid)
    out = pl.pallas_call(
        body,
        grid=(b, tq // qb),
        in_specs=[pl.BlockSpec((None, qb, w), lambda bi, i: (bi, i, 0)),
                  pl.BlockSpec((None, tk, w), lambda bi, i: (bi, 0, 0)),
                  pl.BlockSpec((None, tk, w), lambda bi, i: (bi, 0, 0)),
                  pl.BlockSpec(bias_t.shape, lambda bi, i: (0, 0, 0))],
        out_specs=pl.BlockSpec((None, qb, w), lambda bi, i: (bi, i, 0)),
        out_shape=jax.ShapeDtypeStruct((b, tq, w), F32),
        scratch_shapes=[pltpu.VMEM((n_kblk, LANES, w), BF16),
                        pltpu.VMEM((n_kblk, H_B, B_ACC_ROWS, LANES), BF16)],
        compiler_params=_params(("parallel", "arbitrary")),
        name="band_attn",
    )(q, k, v, bias_t)
    return out[:, :t] if tq != t else out


_SMALL_DECAY0 = H_C


def _split2(x):
    hi = x.astype(BF16)
    return hi, (x - hi.astype(F32)).astype(BF16)


def _gdn_body(cqkv_ref, small_ref, gate_ref, cprev_ref, s0_ref, convw_ref, alog_ref, dtb_ref, onorm_ref,
              tri_ref, bo_ref, eb_ref, eg_ref, oc_ref, sout_ref, xext, s_scr, *, tc):
    j = pl.program_id(1)
    n_bb = cqkv_ref.shape[0]
    n_ch = tc // CHUNK
    n_pair = H_C // 2
    pad = 8
    lo = pad - (CONV_W - 1)

    @pl.when(j == 0)
    def _():
        z = jnp.zeros((D_C, D_C), F32)
        for bb in range(n_bb):
            xext[bb, lo:pad, :] = cprev_ref[bb]
            for p in range(n_pair):
                s_scr[bb, p] = jnp.concatenate([jnp.concatenate([s0_ref[bb, 2 * p], z], axis=1),
                                                jnp.concatenate([z, s0_ref[bb, 2 * p + 1]], axis=1)], axis=0)

    @pl.when(j > 0)
    def _():
        for bb in range(n_bb):
            xext[bb, lo:pad, :] = xext[bb, tc + lo:tc + pad, :]

    bo = bo_ref[...]
    tri = tri_ref[...]
    lane_s = lax.broadcasted_iota(jnp.int32, (tc, LANES), 1)
    is_dec = (lane_s >= _SMALL_DECAY0) & (lane_s < _SMALL_DECAY0 + H_C)

    def head_sumsq(x):
        hi, lo_ = _split2(x * x)
        return jnp.concatenate([_dot(hi[:, t * LANES:(t + 1) * LANES], bo) + _dot(lo_[:, t * LANES:(t + 1) * LANES], bo)
                                for t in range(n_pair)], axis=1)

    qn, kn, gc, bc, xk, vb, qe = [], [], [], [], [], [], []
    for bb in range(n_bb):
        xext[bb, pad:, :] = cqkv_ref[bb]
        y = convw_ref[0:1, :] * xext[bb, lo:lo + tc, :]
        for w in range(1, CONV_W):
            y = y + convw_ref[w:w + 1, :] * xext[bb, lo + w:lo + w + tc, :]
        y = _silu(y)
        yq, yk, yv = y[:, :C_W], y[:, C_W:2 * C_W], y[:, 2 * C_W:]
        qn.append(yq * lax.rsqrt(head_sumsq(yq) + L2_EPS) * (D_C ** -0.5))
        kn.append(yk * lax.rsqrt(head_sumsq(yk) + L2_EPS))
        sm = small_ref[bb]
        log_a = jnp.where(is_dec, -jnp.exp(alog_ref[...]) * _softplus(sm + dtb_ref[...]), 0.0)
        g_full = sum(_dot(tri, part) for part in _split3(log_a))
        gc.append(sum(_dot(part, eg_ref[...]) for part in _split3(g_full)))
        bc.append(sum(_dot(part, eb_ref[...]) for part in _split3(_sigmoid(sm))))
        egc = jnp.exp(gc[bb])
        xk.append(bc[bb] * egc * kn[bb])
        vb.append(bc[bb] * yv)
        qe.append(egc * qn[bb])

    ii = lax.broadcasted_iota(jnp.int32, (CHUNK, LANES), 0)
    ln = lax.broadcasted_iota(jnp.int32, (CHUNK, LANES), 1)
    jn = ln % D_C
    incl2 = ii >= jn
    strict2 = ii > jn
    eye2 = ii == jn
    left = ln < D_C
    eye2f = jnp.where(eye2, 1.0, 0.0)
    r128 = lax.broadcasted_iota(jnp.int32, (LANES, LANES), 0)
    c128 = lax.broadcasted_iota(jnp.int32, (LANES, LANES), 1)
    on_diag_blocks = (r128 < D_C) == (c128 < D_C)
    eye128 = jnp.where(r128 == c128, 1.0, 0.0).astype(BF16)

    keep_left = jnp.where(left, 1.0, 0.0).astype(BF16)
    keep_right = jnp.where(left, 0.0, 1.0).astype(BF16)

    def bdiag(x):
        xb = x.astype(BF16)
        return jnp.concatenate([xb * keep_left, xb * keep_right], axis=0)

    units = [(bb, r, p) for bb in range(n_bb) for r in range(n_ch) for p in range(n_pair)]
    lanes_bp = [(bb, p) for bb in range(n_bb) for p in range(n_pair)]

    def rows_of(r):
        return slice(r * CHUNK, (r + 1) * CHUNK)

    def tile_of(p):
        return slice(p * LANES, (p + 1) * LANES)

    def sub(x, u):
        bb, r, p = u
        return x[bb][rows_of(r), tile_of(p)]

    g_last = {(bb, r): gc[bb][(r + 1) * CHUNK - 1:(r + 1) * CHUNK, :] for bb in range(n_bb) for r in range(n_ch)}
    kd = {br: kn[br[0]][rows_of(br[1])] * jnp.exp(g - gc[br[0]][rows_of(br[1])]) for br, g in g_last.items()}
    eg_last = {br: jnp.exp(g) for br, g in g_last.items()}
    dec, r_k, kdt = {}, {}, {}
    for u in units:
        gcp = sub(gc, u)
        g_row = jnp.sum(jnp.where(eye2, gcp, 0.0), axis=0, keepdims=True)
        dec[u] = jnp.where(incl2, jnp.exp(jnp.where(incl2, gcp - g_row, 0.0)), 0.0)
        kp = sub(kn, u)
        r_k[u] = _dot_nt(jnp.concatenate([sub(qn, u), kp], axis=0).astype(BF16), bdiag(kp).astype(BF16))
    for bb, r, p in units:
        kdt[bb, r, p] = _dot_nt(eye128, bdiag(kd[bb, r][:, tile_of(p)]).astype(BF16)).astype(BF16)
    nm = {u: -(sub(bc, u) * jnp.where(strict2, dec[u], 0.0) * r_k[u][CHUNK:]) for u in units}
    pw = {u: _dot(nm[u].astype(BF16), bdiag(nm[u]).astype(BF16)) for u in units}
    tm_ = {u: eye2f + nm[u] for u in units}
    for _ in range(4):
        st = {u: _dot(jnp.concatenate([tm_[u], pw[u]], axis=0).astype(BF16), bdiag(pw[u]).astype(BF16))
              for u in units}
        tm_ = {u: tm_[u] + st[u][:CHUNK] for u in units}
        pw = {u: st[u][CHUNK:] for u in units}
    fin = {u: _dot(tm_[u].astype(BF16), bdiag(pw[u]).astype(BF16)) for u in units}
    tm_ = {u: tm_[u] + fin[u] for u in units}
    wu = {u: _dot(tm_[u].astype(BF16),
                  jnp.concatenate([bdiag(sub(xk, u)), bdiag(sub(vb, u))], axis=1).astype(BF16)) for u in units}
    w_b = {u: wu[u][:, :LANES].astype(BF16) for u in units}
    qkd = {u: (dec[u] * r_k[u][:CHUNK]).astype(BF16) for u in units}

    states = {bp: s_scr[bp[0], bp[1]] for bp in lanes_bp}
    o_tiles = {}
    for r in range(n_ch):
        res = {(bb, p): _dot(jnp.concatenate([sub(qe, (bb, r, p)).astype(BF16), w_b[bb, r, p]], axis=0),
                             states[bb, p].astype(BF16)) for bb, p in lanes_bp}
        u = {(bb, p): wu[bb, r, p][:, LANES:] - res[bb, p][CHUNK:] for bb, p in lanes_bp}
        upd = {(bb, p): _dot(kdt[bb, r, p], jnp.concatenate([u[bb, p], u[bb, p]], axis=0).astype(BF16))
               for bb, p in lanes_bp}
        o_inner = {(bb, p): _dot(qkd[bb, r, p], bdiag(u[bb, p]).astype(BF16)) for bb, p in lanes_bp}
        states = {(bb, p): eg_last[bb, r][:, tile_of(p)] * states[bb, p] + jnp.where(on_diag_blocks, upd[bb, p], 0.0)
                  for bb, p in lanes_bp}
        for bb, p in lanes_bp:
            o_tiles[bb, r, p] = res[bb, p][:CHUNK] + o_inner[bb, p]
    for bb, p in lanes_bp:
        s_scr[bb, p] = states[bb, p]
    for bb in range(n_bb):
        o = jnp.concatenate([jnp.concatenate([o_tiles[bb, r, p] for p in range(n_pair)], axis=1)
                             for r in range(n_ch)], axis=0)
        ms = head_sumsq(o) * (1.0 / D_C)
        oc_ref[bb] = o * lax.rsqrt(ms + NORM_EPS) * onorm_ref[...] * _silu(gate_ref[bb])

    @pl.when(j == pl.num_programs(1) - 1)
    def _():
        for bb, p in lanes_bp:
            sout_ref[bb, 2 * p] = states[bb, p][:D_C, :D_C]
            sout_ref[bb, 2 * p + 1] = states[bb, p][D_C:, D_C:]


def _gdn_constants(tc):
    i = jnp.arange(tc, dtype=jnp.int32)
    tri = ((i[None, :] <= i[:, None]) & (i[None, :] // CHUNK == i[:, None] // CHUNK)).astype(BF16)
    l = jnp.arange(LANES, dtype=jnp.int32)
    block_ones = (l[:, None] // D_C == l[None, :] // D_C).astype(BF16)
    col_head = jnp.arange(C_W, dtype=jnp.int32)[None, :] // D_C
    e_beta = (l[:, None] == col_head).astype(BF16)
    e_g = (l[:, None] == col_head + _SMALL_DECAY0).astype(BF16)
    return tri, block_ones, e_beta, e_g


def _gdn(cqkv, small, gate, conv_prev, s0, conv_w, alog_vec, dtb_vec, onorm_tiled):
    b, t, w3 = cqkv.shape
    tc = min(4 * CHUNK, t)
    nb = GDN_BATCH_BLOCK if b % GDN_BATCH_BLOCK == 0 else 1
    consts = _gdn_constants(tc)
    body = functools.partial(_gdn_body, tc=tc)
    blk = lambda bi, j: (bi, j, 0)
    per_b3 = lambda bi, j: (bi, 0, 0)
    per_b4 = lambda bi, j: (bi, 0, 0, 0)
    const2 = lambda bi, j: (0, 0)
    return pl.pallas_call(
        body,
        grid=(b // nb, t // tc),
        in_specs=[pl.BlockSpec((nb, tc, w3), blk),
                  pl.BlockSpec((nb, tc, LANES), blk),
                  pl.BlockSpec((nb, tc, C_W), blk),
                  pl.BlockSpec((nb, CONV_W - 1, w3), per_b3),
                  pl.BlockSpec((nb, H_C, D_C, D_C), per_b4),
                  pl.BlockSpec(conv_w.shape, const2),
                  pl.BlockSpec((1, LANES), const2),
                  pl.BlockSpec((1, LANES), const2),
                  pl.BlockSpec((1, C_W), const2)]
                 + [pl.BlockSpec(c.shape, const2) for c in consts],
        out_specs=[pl.BlockSpec((nb, tc, C_W), blk),
                   pl.BlockSpec((nb, H_C, D_C, D_C), per_b4)],
        out_shape=[jax.ShapeDtypeStruct((b, t, C_W), F32),
                   jax.ShapeDtypeStruct((b, H_C, D_C, D_C), F32)],
        scratch_shapes=[pltpu.VMEM((nb, tc + 8, w3), F32), pltpu.VMEM((nb, H_C // 2, LANES, LANES), F32)],
        compiler_params=_params(("parallel", "arbitrary")),
        name="gdn",
    )(cqkv, small, gate, conv_prev, s0, conv_w, alog_vec, dtb_vec, onorm_tiled, *consts)


def _top2_gates(logits):
    lane = lax.broadcasted_iota(jnp.int32, logits.shape, 1).astype(F32)
    low = -3.0e38
    lg = jnp.where(lane < N_EXPERTS, logits, low)
    m1 = jnp.max(lg, axis=-1, keepdims=True)
    i1 = jnp.min(jnp.where(lg == m1, lane, float(LANES)), axis=-1, keepdims=True)
    lg2 = jnp.where(lane == i1, low, lg)
    m2 = jnp.max(lg2, axis=-1, keepdims=True)
    i2 = jnp.min(jnp.where(lg2 == m2, lane, float(LANES)), axis=-1, keepdims=True)
    e2 = jnp.exp(m2 - m1)
    den = 1.0 + e2
    return jnp.where(lane == i1, 1.0 / den, 0.0) + jnp.where(lane == i2, e2 / den, 0.0)


def _outproj_body(x_ref, oa_ref, ob_ref, oc_ref, w_ref, g_ref, *rest, with_router):
    if with_router:
        rhi_ref, rlo_ref, xo_ref, h_ref, gates_ref = rest
    else:
        xo_ref, h_ref = rest
    y = (_dot(oa_ref[...].astype(BF16), w_ref[0:HEAD_W, :])
         + _dot(ob_ref[...].astype(BF16), w_ref[HEAD_W:2 * HEAD_W, :])
         + _dot(oc_ref[...].astype(BF16), w_ref[2 * HEAD_W:, :]))
    x = x_ref[...] + y
    xo_ref[...] = x
    hf = _rms(x, g_ref[...])
    hb = hf.astype(BF16)
    h_ref[...] = hb
    if with_router:
        lo = (hf - hb.astype(F32)).astype(BF16)
        logits = _dot(hb, rhi_ref[...]) + _dot(lo, rhi_ref[...]) + _dot(hb, rlo_ref[...])
        gates_ref[...] = _top2_gates(logits)


def _outproj(x, oa, ob, oc, w_out, g, router=None):
    n, d = x.shape
    tm = min(ROW_TILE, n)
    row = lambda i: (i, 0)
    const = lambda i: (0, 0)
    in_specs = [pl.BlockSpec((tm, d), row), pl.BlockSpec((tm, HEAD_W), row), pl.BlockSpec((tm, HEAD_W), row),
                pl.BlockSpec((tm, C_W), row), pl.BlockSpec(w_out.shape, const), pl.BlockSpec((1, d), const)]
    out_specs = [pl.BlockSpec((tm, d), row), pl.BlockSpec((tm, d), row)]
    out_shape = [jax.ShapeDtypeStruct((n, d), F32), jax.ShapeDtypeStruct((n, d), BF16)]
    args = [x, oa, ob, oc, w_out, g]
    if router is not None:
        in_specs += [pl.BlockSpec(router[0].shape, const), pl.BlockSpec(router[1].shape, const)]
        out_specs.append(pl.BlockSpec((tm, LANES), row))
        out_shape.append(jax.ShapeDtypeStruct((n, LANES), F32))
        args += list(router)
    return pl.pallas_call(
        functools.partial(_outproj_body, with_router=router is not None),
        grid=(n // tm,),
        in_specs=in_specs, out_specs=out_specs, out_shape=out_shape,
        compiler_params=_params(("parallel",)),
        name="outproj",
    )(*args)


def _ffn_body(h_ref, x_ref, wg_ref, wu_ref, wd_ref, o_ref, acc_ref, *, ff_chunk):
    h = h_ref[...]
    d_ff = wg_ref.shape[1]
    acc_ref[...] = x_ref[...]
    for c0 in range(0, d_ff, ff_chunk):
        a = _silu(_dot(h, wg_ref[:, c0:c0 + ff_chunk])) * _dot(h, wu_ref[:, c0:c0 + ff_chunk])
        acc_ref[...] += _dot(a.astype(BF16), wd_ref[c0:c0 + ff_chunk, :])
    o_ref[...] = acc_ref[...]


def _ffn_dense(h, x, wg, wu, wd):
    n, d = x.shape
    d_ff = wg.shape[1]
    tm = min(ROW_TILE, n)
    ff_chunk = 256 if d_ff % 256 == 0 else LANES
    row = lambda i: (i, 0)
    const = lambda i: (0, 0)
    return pl.pallas_call(
        functools.partial(_ffn_body, ff_chunk=ff_chunk),
        grid=(n // tm,),
        in_specs=[pl.BlockSpec((tm, d), row), pl.BlockSpec((tm, d), row),
                  pl.BlockSpec(wg.shape, const), pl.BlockSpec(wu.shape, const), pl.BlockSpec(wd.shape, const)],
        out_specs=pl.BlockSpec((tm, d), row),
        out_shape=jax.ShapeDtypeStruct((n, d), F32),
        scratch_shapes=[pltpu.VMEM((tm, d), F32)],
        compiler_params=_params(("parallel",)),
        name="ffn_dense",
    )(h, x, wg, wu, wd)


def _moe_body(pre_ref, h_ref, gates_ref, gates_t_ref, wg_ref, wu_ref, wd_ref, *rest, tg, n_f, final_norm):
    if final_norm:
        x_ref, gfin_ref, y_ref, rank_col, rank_row, hs, eo, gacc = rest
    else:
        y_ref, rank_col, rank_row, hs, eo, gacc = rest
    g = pl.program_id(0)
    e = pl.program_id(1)
    f = pl.program_id(2)
    tm_full = MOE_ROW_TILE
    tm_tail = MOE_ROW_TILE // 2
    tb = MOE_TOKEN_BLOCK
    n_tb = tg // tb
    pre0 = (g * N_EXPERTS + e) * (n_tb + 1)
    count = pre_ref[pre0 + n_tb]
    n_full = (count + tm_tail - 1) // tm_full
    has_tail = count > n_full * tm_full

    def for_tiles(fn):
        def full(t, carry):
            fn(pl.multiple_of(t * tm_full, tm_full), tm_full)
            return carry

        lax.fori_loop(0, n_full, full, 0)

        @pl.when(has_tail)
        def _():
            fn(pl.multiple_of(n_full * tm_full, tm_tail), tm_tail)

    def overlaps(r0, tm, b):
        return (pre_ref[pre0 + b] < r0 + tm) & (pre_ref[pre0 + b + 1] > r0)

    @pl.when((e == 0) & (f == 0))
    def _():
        y_ref[...] = jnp.zeros_like(y_ref)
        ii = lax.broadcasted_iota(jnp.int32, (tb, tb), 0)
        jj = lax.broadcasted_iota(jnp.int32, (tb, tb), 1)
        lower = jnp.where(ii > jj, 1.0, 0.0).astype(BF16)
        upper = jnp.where(ii < jj, 1.0, 0.0).astype(BF16)
        carry_c = jnp.zeros((1, LANES), F32)
        carry_r = jnp.zeros((N_EXPERTS, 1), F32)
        for b in range(n_tb):
            mc = jnp.where(gates_ref[b * tb:(b + 1) * tb, :] > 0.0, 1.0, 0.0)
            rank_col[b * tb:(b + 1) * tb, :] = _dot(lower, mc.astype(BF16)) + carry_c
            carry_c = carry_c + jnp.sum(mc, axis=0, keepdims=True)
            mr = jnp.where(gates_t_ref[:, b * tb:(b + 1) * tb] > 0.0, 1.0, 0.0)
            rank_row[:, b * tb:(b + 1) * tb] = _dot(mr.astype(BF16), upper) + carry_r
            carry_r = carry_r + jnp.sum(mr, axis=1, keepdims=True)

    def gather(r0, tm):
        rows = (r0 + lax.broadcasted_iota(jnp.int32, (tm, tb), 0)).astype(F32)
        gacc[0:tm, :] = jnp.zeros((tm, gacc.shape[1]), F32)
        for b in range(n_tb):
            @pl.when(overlaps(r0, tm, b))
            def _(b=b):
                gr = gates_t_ref[pl.ds(e, 1), b * tb:(b + 1) * tb]
                rr = jnp.where(gr > 0.0, rank_row[pl.ds(e, 1), b * tb:(b + 1) * tb], -1.0)
                onehot = jnp.where(rr == rows, 1.0, 0.0).astype(BF16)
                gacc[0:tm, :] += _dot(onehot, h_ref[b * tb:(b + 1) * tb, :])
        hs[pl.ds(r0, tm), :] = gacc[0:tm, :].astype(BF16)

    @pl.when(f == 0)
    def _():
        for_tiles(gather)

    def expert(r0, tm):
        x = hs[pl.ds(r0, tm), :]
        a = _silu(_dot(x, wg_ref[...])) * _dot(x, wu_ref[...])
        part = _dot(a.astype(BF16), wd_ref[...])

        @pl.when(f == 0)
        def _():
            eo[pl.ds(r0, tm), :] = part

        @pl.when(f > 0)
        def _():
            eo[pl.ds(r0, tm), :] += part

    for_tiles(expert)

    def scatter(r0, tm):
        lane = lax.broadcasted_iota(jnp.int32, (tb, LANES), 1)
        cols = (r0 + lax.broadcasted_iota(jnp.int32, (tb, tm), 1)).astype(F32)
        out = eo[pl.ds(r0, tm), :].astype(BF16)
        for b in range(n_tb):
            @pl.when(overlaps(r0, tm, b))
            def _(b=b):
                sel = lane == e
                gc = jnp.sum(jnp.where(sel, gates_ref[b * tb:(b + 1) * tb, :], 0.0), axis=-1, keepdims=True)
                rc = jnp.sum(jnp.where(sel, rank_col[b * tb:(b + 1) * tb, :], 0.0), axis=-1, keepdims=True)
                rc = jnp.where(gc > 0.0, rc, -1.0)
                weighted = jnp.where(rc == cols, gc, 0.0).astype(BF16)
                y_ref[b * tb:(b + 1) * tb, :] += _dot(weighted, out)

    @pl.when(f == n_f - 1)
    def _():
        for_tiles(scatter)

    if final_norm:
        @pl.when((e == pl.num_programs(1) - 1) & (f == n_f - 1))
        def _():
            for b in range(n_tb):
                rows = slice(b * tb, (b + 1) * tb)
                y_ref[rows, :] = _rms(x_ref[rows, :] + y_ref[rows, :], gfin_ref[...])


def _moe_prefix(gates, tg):
    n = gates.shape[0]
    n_tb = tg // MOE_TOKEN_BLOCK
    routed = (gates[:, :N_EXPERTS] > 0.0).astype(jnp.int32)
    per_block = routed.reshape(n // tg, n_tb, MOE_TOKEN_BLOCK, N_EXPERTS).sum(axis=2)
    run = jnp.cumsum(per_block, axis=1)
    pre = jnp.concatenate([jnp.zeros_like(run[:, :1]), run], axis=1)
    return jnp.transpose(pre, (0, 2, 1)).reshape(-1)


def _moe(h, gates, wg, wu, wd, tg, x=None, g_final=None):
    n, d = h.shape
    n_e, _, d_ff = wg.shape
    n_f = 2
    ffh = d_ff // n_f
    final_norm = x is not None
    body = functools.partial(_moe_body, tg=tg, n_f=n_f, final_norm=final_norm)
    once = pl.Buffered(1)
    extra_specs, extra_args = [], []
    if final_norm:
        extra_specs = [pl.BlockSpec((tg, d), lambda g, e, f, c: (g, 0), pipeline_mode=once),
                       pl.BlockSpec((1, d), lambda g, e, f, c: (0, 0))]
        extra_args = [x, g_final]
    grid_spec = pltpu.PrefetchScalarGridSpec(
        num_scalar_prefetch=1,
        grid=(n // tg, n_e, n_f),
        in_specs=[pl.BlockSpec((tg, d), lambda g, e, f, c: (g, 0), pipeline_mode=once),
                  pl.BlockSpec((tg, LANES), lambda g, e, f, c: (g, 0), pipeline_mode=once),
                  pl.BlockSpec((N_EXPERTS, tg), lambda g, e, f, c: (0, g), pipeline_mode=once),
                  pl.BlockSpec((None, d, ffh), lambda g, e, f, c: (e, 0, f)),
                  pl.BlockSpec((None, d, ffh), lambda g, e, f, c: (e, 0, f)),
                  pl.BlockSpec((None, ffh, d), lambda g, e, f, c: (e, f, 0))] + extra_specs,
        out_specs=pl.BlockSpec((tg, d), lambda g, e, f, c: (g, 0), pipeline_mode=once),
        scratch_shapes=[pltpu.VMEM((tg, LANES), F32), pltpu.VMEM((N_EXPERTS, tg), F32),
                        pltpu.VMEM((tg, d), BF16), pltpu.VMEM((tg, d), F32),
                        pltpu.VMEM((MOE_ROW_TILE, d), F32)],
    )
    return pl.pallas_call(
        body,
        grid_spec=grid_spec,
        out_shape=jax.ShapeDtypeStruct((n, d), F32),
        compiler_params=_params(("parallel", "arbitrary", "arbitrary")),
        name="moe",
    )(_moe_prefix(gates, tg), h, gates, gates[:, :N_EXPERTS].T, wg, wu, wd, *extra_args)


def _final_body(x_ref, y_ref, g_ref, o_ref):
    o_ref[...] = _rms(x_ref[...] + y_ref[...], g_ref[...])


def _final_norm(x, y, g):
    n, d = x.shape
    tm = min(ROW_TILE, n)
    row = lambda i: (i, 0)
    return pl.pallas_call(
        _final_body,
        grid=(n // tm,),
        in_specs=[pl.BlockSpec((tm, d), row), pl.BlockSpec((tm, d), row), pl.BlockSpec((1, d), lambda i: (0, 0))],
        out_specs=pl.BlockSpec((tm, d), row),
        out_shape=jax.ShapeDtypeStruct((n, d), F32),
        compiler_params=_params(("parallel",)),
        name="final_norm",
    )(x, y, g)


def _t5_bucket(rel):
    nb = T5_BUCKETS // 2
    max_exact = nb // 2
    ret = jnp.where(rel > 0, nb, 0)
    n = jnp.abs(rel)
    nf = jnp.maximum(n, 1).astype(F32)
    large = max_exact + (jnp.log(nf / max_exact) / math.log(T5_MAX_DIST / max_exact) * (nb - max_exact)).astype(jnp.int32)
    large = jnp.minimum(large, nb - 1)
    return ret + jnp.where(n < max_exact, n, large)


def _t5_table(t5_bias, q_pos, k_pos):
    bias = _lookup(t5_bias, _t5_bucket(k_pos[None, :] - q_pos[:, None]))
    mask = (k_pos[None, :] // CHUNK) <= (q_pos[:, None] // CHUNK)
    return bias, mask[None]


def _lookup(table, idx):
    onehot = jax.nn.one_hot(idx, table.shape[0], dtype=F32)
    return jnp.einsum("qkn,nh->hqk", onehot, table.astype(F32), precision=lax.Precision.HIGHEST)


def _band_table(rel_bias):
    qi = jnp.arange(CHUNK, dtype=jnp.int32)
    kj = jnp.arange(BAND, dtype=jnp.int32) - B_LEFT_CHUNKS * CHUNK
    rel = jnp.clip(kj[None, :] - qi[:, None], -B_REL_CLIP, B_REL_CLIP) + B_REL_CLIP
    base = jnp.swapaxes(_lookup(rel_bias, rel), 1, 2) * LOG2E
    halves = [jnp.pad(base, ((0, 0), (c * CHUNK, (1 - c) * CHUNK), (0, 0)), constant_values=NEG_INF)
              for c in range(2)]
    return jnp.concatenate(halves, axis=2)


def _lane_vec(v, offset):
    return jnp.zeros((1, LANES), F32).at[0, offset:offset + v.shape[0]].set(v.astype(F32))


def _moe_group(n):
    for tg in (2048, 1024, 512, 256):
        if n % tg == 0:
            return tg
    raise ValueError(f"token count {n} is not a multiple of 256")


def kernel(x_prompt, x_sample, cache_a_k, cache_a_v, cache_b_k, cache_b_v, cache_c_conv, state_c_ssm, w_in, w_out, norm_mix, norm_ffn, norm_final, lam_qk, subln_a, t5_bias, rel_bias_b, conv_c, a_log_c, dt_bias_c, onorm_c, ffn_gate, ffn_up, ffn_down, moe_router, moe_gate, moe_up, moe_down):
    depth = w_in.shape[0]
    bp, sp, d = x_prompt.shape
    bs, ss, _ = x_sample.shape
    past = cache_a_k.shape[2]
    nb_cache = cache_b_k.shape[2]
    assert ss == CHUNK and nb_cache == B_LEFT_CHUNKS * CHUNK and sp >= B_LEFT_CHUNKS * CHUNK

    xp = x_prompt.reshape(bp * sp, d)
    xs = x_sample.reshape(bs * ss, d)
    blk = min(A_BLOCK, sp)

    pos_blk = jnp.arange(blk, dtype=jnp.int32)
    far_bias = t5_bias[_t5_bucket(jnp.int32(-(blk + 1)))].astype(F32)
    bd, md = _t5_table(t5_bias, pos_blk, pos_blk)
    bias_diag = jnp.where(md, bd - far_bias[:, None, None], NEG_INF)
    bsub, _ = _t5_table(t5_bias, blk + pos_blk, pos_blk)
    bias_sub = bsub - far_bias[:, None, None]
    q_pos_s = past + jnp.arange(ss, dtype=jnp.int32)
    bfull, mfull = _t5_table(t5_bias, q_pos_s, jnp.arange(past + ss, dtype=jnp.int32))
    bias_s = jnp.where(mfull, bfull, NEG_INF)
    bias_s_t = jnp.swapaxes(bias_s, 1, 2) * LOG2E
    bias_s_cache = bias_s_t[:, :past]
    bias_s_new = jnp.pad(bias_s_t[:, past:], ((0, 0), (0, LANES - ss), (0, 0)), constant_values=NEG_INF)

    zeros_conv = jnp.zeros((bp, CONV_W - 1, 3 * C_W), F32)
    zeros_state = jnp.zeros((bp, H_C, D_C, D_C), F32)

    p_states = [[] for _ in range(6)]
    s_states = [[] for _ in range(6)]
    yp = ys = None
    g_fin = norm_final.reshape(1, d)
    finals = []
    for l in range(depth):
        if yp is not None:
            xp, xs, yp, ys = xp + yp, xs + ys, None, None
        w = w_in[l]
        n_main = 6 * HEAD_W + 3 * C_W
        w_main = jnp.concatenate([w[:, :n_main], w[:, n_main + 2 * H_C:]], axis=1).astype(BF16)
        w_small = jnp.zeros((d, LANES), F32).at[:, :2 * H_C].set(w[:, n_main:n_main + 2 * H_C]).astype(BF16)
        w_out_l = w_out[l].astype(BF16)
        g_mix = norm_mix[l].reshape(1, d)
        g_ffn = norm_ffn[l].reshape(1, d)
        lam_init = 0.8 - 0.6 * math.exp(-0.3 * l)
        lq = lam_qk[l].astype(F32)
        lam = (jnp.exp(jnp.sum(lq[0] * lq[1])) - jnp.exp(jnp.sum(lq[2] * lq[3])) + lam_init).reshape(1)
        subln = subln_a[l].reshape(1, DV_A)
        band_bias = _band_table(rel_bias_b[l])
        alog_vec = _lane_vec(a_log_c[l], _SMALL_DECAY0)
        dtb_vec = _lane_vec(dt_bias_c[l], _SMALL_DECAY0)
        onorm = jnp.tile(onorm_c[l].astype(F32), H_C).reshape(1, C_W)
        is_moe = l % 2 == 1
        if is_moe:
            r = jnp.zeros((d, LANES), F32).at[:, :N_EXPERTS].set(moe_router[l // 2])
            r_hi = r.astype(BF16)
            router = (r_hi, (r - r_hi.astype(F32)).astype(BF16))
            e_wg, e_wu, e_wd = (moe_gate[l // 2].astype(BF16), moe_up[l // 2].astype(BF16),
                                moe_down[l // 2].astype(BF16))
        else:
            router = None
            f_wg, f_wu, f_wd = (ffn_gate[l // 2].astype(BF16), ffn_up[l // 2].astype(BF16),
                                ffn_down[l // 2].astype(BF16))

        new_x = []
        for is_prompt, x in ((True, xp), (False, xs)):
            b, t = (bp, sp) if is_prompt else (bs, ss)
            aq, ak, av, bq, bk, bv, cqkv, cgate, csmall = _inproj(x, g_mix, w_main, w_small)
            r3 = lambda a: a.reshape(b, t, a.shape[-1])
            if is_prompt:
                oa = _attn_a_prompt(lam, r3(aq), r3(ak), r3(av), jnp.swapaxes(bias_diag, 1, 2) * LOG2E,
                                    jnp.swapaxes(bias_sub, 1, 2) * LOG2E, subln.reshape(DV_A, 1), 1.0 - lam_init)
                ob = _band_attn(r3(bq), r3(bk), r3(bv), band_bias, B_LEFT_CHUNKS * CHUNK)
                conv_prev, s0 = zeros_conv, zeros_state
            else:
                oa = _attn_a_sample(lam, r3(aq), cache_a_k[l].reshape(b, past, HEAD_W),
                                    cache_a_v[l].reshape(b, past, HEAD_W), r3(ak), r3(av),
                                    bias_s_cache, bias_s_new, subln.reshape(DV_A, 1), 1.0 - lam_init)
                kb = jnp.concatenate([cache_b_k[l].reshape(b, nb_cache, HEAD_W), r3(bk)], axis=1)
                vb = jnp.concatenate([cache_b_v[l].reshape(b, nb_cache, HEAD_W), r3(bv)], axis=1)
                ob = _band_attn(r3(bq), kb, vb, band_bias, 0)
                conv_prev, s0 = cache_c_conv[l], state_c_ssm[l]
            oc, s_new = _gdn(r3(cqkv), r3(csmall), r3(cgate), conv_prev, s0, conv_c[l], alog_vec, dtb_vec, onorm)
            n = b * t
            res = _outproj(x, oa.reshape(n, HEAD_W), ob.reshape(n, HEAD_W), oc.reshape(n, C_W), w_out_l, g_ffn,
                           router)
            if is_moe:
                x_new, h2, gates = res
                if l == depth - 1:
                    finals.append(_moe(h2, gates, e_wg, e_wu, e_wd, _moe_group(n), x=x_new, g_final=g_fin))
                    new_x.append((x_new, None))
                else:
                    new_x.append((x_new, _moe(h2, gates, e_wg, e_wu, e_wd, _moe_group(n))))
            else:
                x_new, h2 = res
                new_x.append((_ffn_dense(h2, x_new, f_wg, f_wu, f_wd), None))
            keep = min(B_LEFT_CHUNKS * CHUNK, t)
            states = (ak.reshape(b, t, H_A, 2 * DQK_A), av.reshape(b, t, H_A, DV_A),
                      bk.reshape(b, t, H_B, D_B)[:, t - keep:], bv.reshape(b, t, H_B, D_B)[:, t - keep:],
                      r3(cqkv)[:, t - (CONV_W - 1):], s_new)
            for i in range(6):
                (p_states if is_prompt else s_states)[i].append(states[i])
        (xp, yp), (xs, ys) = new_x

    if finals:
        y_prompt, y_sample = finals[0].reshape(bp, sp, d), finals[1].reshape(bs, ss, d)
    else:
        if yp is None:
            yp, ys = jnp.zeros_like(xp), jnp.zeros_like(xs)
        y_prompt = _final_norm(xp, yp, g_fin).reshape(bp, sp, d)
        y_sample = _final_norm(xs, ys, g_fin).reshape(bs, ss, d)
    p_out = [jnp.stack(s, axis=0) for s in p_states]
    s_out = [jnp.stack(s, axis=0) for s in s_states]
    return (y_prompt, y_sample, *p_out, *s_out)
```

```python
import functools
import math

import jax
import jax.numpy as jnp
from jax import lax
from jax.experimental import pallas as pl
from jax.experimental.pallas import tpu as pltpu

F32 = jnp.float32
BF16 = jnp.bfloat16

CHUNK = 64
H_A = 4
DQK_A = 32
DV_A = 64
H_B = 4
D_B = 64
B_LEFT_CHUNKS = 8
B_REL_CLIP = 128
H_C = 8
D_C = 64
CONV_W = 4
T5_BUCKETS = 32
T5_MAX_DIST = 128
N_EXPERTS = 8
NORM_EPS = 1e-6
L2_EPS = 1e-6
NEG_INF = -1e30
HEAD_W = 256
C_W = H_C * D_C
BAND = (B_LEFT_CHUNKS + 1) * CHUNK
LANES = 128
LOG2E = math.log2(math.e)
VMEM_LIMIT = 56 * 1024 * 1024

A_BLOCK = 256
ROW_TILE = 512
GDN_BATCH_BLOCK = 2
GDN_LEAF = 8
MOE_TILES = (512, 256, 128, 64)
MOE_TOKEN_BLOCK = 256
MOE_SCATTER_SLAB = 512


def _params(sem):
    return pltpu.CompilerParams(dimension_semantics=sem, vmem_limit_bytes=VMEM_LIMIT)


def _rms(x, g):
    ms = jnp.mean(x * x, axis=-1, keepdims=True)
    return x * lax.rsqrt(ms + NORM_EPS) * g


def _sigmoid(x):
    return 1.0 / (1.0 + jnp.exp(-x))


def _silu(x):
    return x * _sigmoid(x)


def _softplus(x):
    return jnp.maximum(x, 0.0) + jnp.log1p(jnp.exp(-jnp.abs(x)))


def _dot(a, b):
    return jnp.dot(a, b, preferred_element_type=F32)


def _dot_nt(a, b):
    return lax.dot_general(a, b, (((1,), (1,)), ((), ())), preferred_element_type=F32)


def _dot_tn(a, b):
    return lax.dot_general(a, b, (((0,), (0,)), ((), ())), preferred_element_type=F32)


def _split3(x):
    x1 = x.astype(BF16)
    r1 = x - x1.astype(F32)
    x2 = r1.astype(BF16)
    x3 = (r1 - x2.astype(F32)).astype(BF16)
    return x1, x2, x3


def _inproj_body(x_ref, g_ref, w_ref, ws_ref, aq_ref, ak_ref, av_ref, bq_ref, bk_ref, bv_ref,
                 cqkv_ref, cgate_ref, csmall_ref):
    h = _rms(x_ref[...], g_ref[...]).astype(BF16)
    col = 0
    for ref in (aq_ref, ak_ref, av_ref, bq_ref, bk_ref, bv_ref, cqkv_ref, cgate_ref):
        width = ref.shape[-1]
        ref[...] = _dot(h, w_ref[:, col:col + width])
        col += width
    csmall_ref[...] = _dot(h, ws_ref[...])


def _inproj(x, g, w_main, w_small):
    n, d = x.shape
    tm = min(ROW_TILE, n)
    widths = (HEAD_W,) * 6 + (3 * C_W, C_W, LANES)
    row = lambda i: (i, 0)
    const = lambda i: (0, 0)
    return pl.pallas_call(
        _inproj_body,
        grid=(n // tm,),
        in_specs=[pl.BlockSpec((tm, d), row), pl.BlockSpec((1, d), const),
                  pl.BlockSpec(w_main.shape, const), pl.BlockSpec(w_small.shape, const)],
        out_specs=[pl.BlockSpec((tm, w), row) for w in widths],
        out_shape=[jax.ShapeDtypeStruct((n, w), F32) for w in widths],
        compiler_params=_params(("parallel",)),
        name="inproj",
    )(x, g, w_main, w_small)


A_ACC_ROWS = DV_A + 16
A_GROUPS = LANES // DQK_A


def _attn_a_prompt_body(lam_ref, q_ref, k_ref, v_ref, bd_ref, bs_ref, subln_ref, o_ref,
                        kbf, vte, wq, m_scr, acc_scr, s_even, s_odd, *, blk, post_scale):
    i = pl.program_id(1)
    n_blk = k_ref.shape[0] // blk
    n_pair = H_A // 2

    @pl.when(i == 0)
    def _():
        ones_rows = jnp.where(lax.broadcasted_iota(jnp.int32, (A_ACC_ROWS - DV_A, blk), 0) == 0, 1.0, 0.0)
        for jb in range(n_blk):
            kbf[jb] = k_ref[jb * blk:(jb + 1) * blk, :].astype(BF16)
            vt = v_ref[jb * blk:(jb + 1) * blk, :].T
            for h in range(H_A):
                vte[jb, h, 0:DV_A, :] = vt[h * DV_A:(h + 1) * DV_A, :].astype(BF16)
                vte[jb, h, DV_A:, :] = ones_rows.astype(BF16)

    lam = lam_ref[0]
    qt = (q_ref[...] * (DQK_A ** -0.5 * LOG2E)).T
    grp = lax.broadcasted_iota(jnp.int32, (LANES, blk), 0) // DQK_A
    for p in range(n_pair):
        qtp = qt[p * LANES:(p + 1) * LANES, :]
        wq[p] = jnp.concatenate([jnp.where(grp == g, qtp, 0.0) for g in range(A_GROUPS)], axis=1).astype(BF16)
    m_scr[...] = jnp.full(m_scr.shape, NEG_INF, F32)
    acc_scr[...] = jnp.zeros(acc_scr.shape, F32)

    def scores(jb):
        return [_dot(kbf[jb, :, p * LANES:(p + 1) * LANES], wq[p]) for p in range(n_pair)]

    def consume(tile_of, jb, bias_ref, off):
        groups = range(n_pair * A_GROUPS)
        alphas, pts = [], []
        for g in groups:
            p, gi = divmod(g, A_GROUPS)
            s = tile_of(p, gi)
            if bias_ref is not None:
                s = s + bias_ref[2 * p + gi // 2]
            if off is not None:
                s = s + off
            m_old = m_scr[g:g + 1, :]
            m_new = jnp.maximum(m_old, jnp.max(s, axis=0, keepdims=True))
            alphas.append(jnp.exp2(m_old - m_new))
            pts.append(jnp.exp2(s - m_new).astype(BF16))
            m_scr[g:g + 1, :] = m_new
        pvs = [_dot(vte[jb, 2 * (g // A_GROUPS) + (g % A_GROUPS) // 2], pts[g]) for g in groups]
        for g in groups:
            acc_scr[g] = alphas[g] * acc_scr[g] + pvs[g]

    def tiles(jb, bias_ref, off):
        s_alls = scores(jb)
        consume(lambda p, gi: s_alls[p][:, gi * blk:(gi + 1) * blk], jb, bias_ref, off)

    tiles(i, bd_ref, None)
    tiles(jnp.maximum(i - 1, 0), bs_ref, jnp.where(i >= 1, 0.0, NEG_INF))

    n_far = jnp.maximum(i - 1, 0)

    def put_scores(dst, jb):
        s_alls = scores(jnp.minimum(jb, n_far - 1))
        for p in range(n_pair):
            dst[p] = s_alls[p]

    def from_scratch(src):
        return lambda p, gi: src[p, :, gi * blk:(gi + 1) * blk]

    @pl.when(n_far > 0)
    def _():
        put_scores(s_even, 0)

    def far(t, carry):
        jb = 2 * t
        put_scores(s_odd, jb + 1)
        consume(from_scratch(s_even), jb, None, None)

        @pl.when(jb + 1 < n_far)
        def _():
            put_scores(s_even, jb + 2)
            consume(from_scratch(s_odd), jb + 1, None, None)

        return carry

    lax.fori_loop(0, (n_far + 1) // 2, far, 0)

    outs = []
    for h in range(H_A):
        g1 = (h // 2) * A_GROUPS + (h % 2) * 2
        a1 = acc_scr[g1]
        a2 = acc_scr[g1 + 1]
        o = a1[:DV_A] / a1[DV_A:DV_A + 1] - lam * (a2[:DV_A] / a2[DV_A:DV_A + 1])
        ms = jnp.mean(o * o, axis=0, keepdims=True)
        outs.append(o * lax.rsqrt(ms + NORM_EPS) * subln_ref[...] * post_scale)
    o_ref[...] = jnp.concatenate(outs, axis=0).T


def _attn_a_prompt(lam, q, k, v, bias_diag_t, bias_sub_t, subln_col, post_scale):
    b, s, w = q.shape
    blk = min(A_BLOCK, s)
    n_blk = s // blk
    body = functools.partial(_attn_a_prompt_body, blk=blk, post_scale=post_scale)
    return pl.pallas_call(
        body,
        grid=(b, n_blk),
        in_specs=[pl.BlockSpec(memory_space=pltpu.SMEM),
                  pl.BlockSpec((None, blk, w), lambda bi, i: (bi, i, 0)),
                  pl.BlockSpec((None, s, w), lambda bi, i: (bi, 0, 0)),
                  pl.BlockSpec((None, s, w), lambda bi, i: (bi, 0, 0)),
                  pl.BlockSpec(bias_diag_t.shape, lambda bi, i: (0, 0, 0)),
                  pl.BlockSpec(bias_sub_t.shape, lambda bi, i: (0, 0, 0)),
                  pl.BlockSpec((DV_A, 1), lambda bi, i: (0, 0))],
        out_specs=pl.BlockSpec((None, blk, w), lambda bi, i: (bi, i, 0)),
        out_shape=jax.ShapeDtypeStruct((b, s, w), F32),
        scratch_shapes=[pltpu.VMEM((n_blk, blk, w), BF16),
                        pltpu.VMEM((n_blk, H_A, A_ACC_ROWS, blk), BF16),
                        pltpu.VMEM((H_A // 2, LANES, A_GROUPS * blk), BF16),
                        pltpu.VMEM((H_A * 2, blk), F32),
                        pltpu.VMEM((H_A * 2, A_ACC_ROWS, blk), F32),
                        pltpu.VMEM((H_A // 2, blk, A_GROUPS * blk), F32),
                        pltpu.VMEM((H_A // 2, blk, A_GROUPS * blk), F32)],
        compiler_params=_params(("parallel", "arbitrary")),
        name="attn_a_prompt",
    )(lam, q, k, v, bias_diag_t, bias_sub_t, subln_col)


def _attn_a_sample_body(lam_ref, q_ref, kc_ref, vc_ref, kn_ref, vn_ref, bc_ref, bn_ref, subln_ref, o_ref, vte_c,
                        *, post_scale):
    t = q_ref.shape[0]
    p_len = kc_ref.shape[0]
    n_pair = H_A // 2
    lam = lam_ref[0]

    ones_rows = jnp.where(lax.broadcasted_iota(jnp.int32, (A_ACC_ROWS - DV_A, LANES), 0) == 0, 1.0, 0.0).astype(BF16)
    for jb in range(p_len // LANES):
        cols = slice(jb * LANES, (jb + 1) * LANES)
        vt = vc_ref[cols, :].T
        for h in range(H_A):
            vte_c[h, 0:DV_A, cols] = vt[h * DV_A:(h + 1) * DV_A, :].astype(BF16)
            vte_c[h, DV_A:, cols] = ones_rows
    row_pad = jnp.zeros((LANES - t, q_ref.shape[1]), F32)
    vt_n = jnp.concatenate([vn_ref[...], row_pad], axis=0).T
    vte_n = [jnp.concatenate([vt_n[h * DV_A:(h + 1) * DV_A, :].astype(BF16), ones_rows], axis=0) for h in range(H_A)]
    kn_p = jnp.concatenate([kn_ref[...], row_pad], axis=0).astype(BF16)

    qt = jnp.concatenate([q_ref[...] * (DQK_A ** -0.5 * LOG2E), row_pad], axis=0).T
    grp = lax.broadcasted_iota(jnp.int32, (LANES, t), 0) // DQK_A
    s_c, s_n = [], []
    for p in range(n_pair):
        qtp = qt[p * LANES:(p + 1) * LANES, 0:t]
        w = jnp.concatenate([jnp.where(grp == g, qtp, 0.0) for g in range(A_GROUPS)], axis=1).astype(BF16)
        s_c.append(_dot(kc_ref[:, p * LANES:(p + 1) * LANES].astype(BF16), w))
        s_n.append(_dot(kn_p[:, p * LANES:(p + 1) * LANES], w))
    pcs, pns = [], []
    for h in range(H_A):
        cols = slice((h % 2) * 2 * t, (h % 2 + 1) * 2 * t)
        sc = s_c[h // 2][:, cols] + jnp.concatenate([bc_ref[h], bc_ref[h]], axis=1)
        sn = s_n[h // 2][:, cols] + jnp.concatenate([bn_ref[h], bn_ref[h]], axis=1)
        mx = jnp.maximum(jnp.max(sc, axis=0, keepdims=True), jnp.max(sn, axis=0, keepdims=True))
        pcs.append(jnp.exp2(sc - mx).astype(BF16))
        pns.append(jnp.exp2(sn - mx).astype(BF16))
    accs = [_dot(vte_c[h], pcs[h]) + _dot(vte_n[h], pns[h]) for h in range(H_A)]
    outs = []
    for a in accs:
        on = a[:DV_A] / a[DV_A:DV_A + 1]
        o = on[:, 0:t] - lam * on[:, t:2 * t]
        ms = jnp.mean(o * o, axis=0, keepdims=True)
        outs.append(o * lax.rsqrt(ms + NORM_EPS) * subln_ref[...] * post_scale)
    ot = jnp.concatenate(outs, axis=0)
    ot = jnp.concatenate([ot, jnp.zeros((ot.shape[0], LANES - t), F32)], axis=1)
    o_ref[...] = ot.T[0:t, :]


def _attn_a_sample(lam, q, kc, vc, kn, vn, bias_c_t, bias_n_t, subln_col, post_scale):
    b, t, w = q.shape
    p = kc.shape[1]
    assert 2 * t == LANES and p % LANES == 0
    body = functools.partial(_attn_a_sample_body, post_scale=post_scale)
    per_b = lambda bi: (bi, 0, 0)
    const3 = lambda bi: (0, 0, 0)
    return pl.pallas_call(
        body,
        grid=(b,),
        in_specs=[pl.BlockSpec(memory_space=pltpu.SMEM),
                  pl.BlockSpec((None, t, w), per_b),
                  pl.BlockSpec((None, p, w), per_b), pl.BlockSpec((None, p, w), per_b),
                  pl.BlockSpec((None, t, w), per_b), pl.BlockSpec((None, t, w), per_b),
                  pl.BlockSpec(bias_c_t.shape, const3, pipeline_mode=pl.Buffered(1)),
                  pl.BlockSpec(bias_n_t.shape, const3),
                  pl.BlockSpec((DV_A, 1), lambda bi: (0, 0))],
        out_specs=pl.BlockSpec((None, t, w), per_b),
        out_shape=jax.ShapeDtypeStruct((b, t, w), F32),
        scratch_shapes=[pltpu.VMEM((H_A, A_ACC_ROWS, p), BF16)],
        compiler_params=_params(("parallel",)),
        name="attn_a_sample",
    )(lam, q, kc, vc, kn, vn, bias_c_t, bias_n_t, subln_col)


B_PAIR = 2 * CHUNK
B_UNION = BAND + CHUNK
B_UNION_BLOCKS = B_UNION // LANES
B_ACC_ROWS = D_B + 16


def _band_body(q_ref, k_ref, v_ref, bias_ref, o_ref, kbf, vte, *, n_grp, n_invalid):
    i = pl.program_id(1)
    n_kblk = k_ref.shape[0] // LANES

    @pl.when(i == 0)
    def _():
        ones_rows = jnp.where(lax.broadcasted_iota(jnp.int32, (B_ACC_ROWS - D_B, LANES), 0) == 0, 1.0, 0.0)
        for jb in range(n_kblk):
            kbf[jb] = k_ref[jb * LANES:(jb + 1) * LANES, :].astype(BF16)
            vt = v_ref[jb * LANES:(jb + 1) * LANES, :].T
            for h in range(H_B):
                vte[jb, h, 0:D_B, :] = vt[h * D_B:(h + 1) * D_B, :].astype(BF16)
                vte[jb, h, D_B:, :] = ones_rows.astype(BF16)

    top = lax.broadcasted_iota(jnp.int32, (LANES, B_PAIR), 0) < D_B
    key_row = lax.broadcasted_iota(jnp.int32, (B_UNION, B_PAIR), 0)

    def group(gl, carry):
        g = i * n_grp + gl
        r0 = pl.multiple_of(gl * B_PAIR, B_PAIR)
        qt = (q_ref[pl.ds(r0, B_PAIR), :] * (D_B ** -0.5 * LOG2E)).T
        blocks = [jnp.maximum(g + t - n_invalid // LANES, 0) for t in range(B_UNION_BLOCKS)]
        kun = jnp.concatenate([kbf[jb] for jb in blocks], axis=0)
        s_pairs = []
        for p in range(H_B // 2):
            qtp = qt[p * LANES:(p + 1) * LANES, :]
            w = jnp.concatenate([jnp.where(top, qtp, 0.0), jnp.where(top, 0.0, qtp)], axis=1).astype(BF16)
            s_pairs.append(_dot(kun[:, p * LANES:(p + 1) * LANES], w))
        pts = []
        for h in range(H_B):
            s = s_pairs[h // 2][:, (h % 2) * B_PAIR:(h % 2 + 1) * B_PAIR] + bias_ref[h]
            if n_invalid:
                s = jnp.where(g * B_PAIR + key_row >= n_invalid, s, NEG_INF)
            pts.append(jnp.exp2(s - jnp.max(s, axis=0, keepdims=True)).astype(BF16))
        accs = [_dot(jnp.concatenate([vte[jb, h] for jb in blocks], axis=1), pts[h]) for h in range(H_B)]
        ot = jnp.concatenate([a[:D_B] / a[D_B:D_B + 1] for a in accs], axis=0)
        o_ref[pl.ds(r0, B_PAIR), :] = ot.T
        return carry

    lax.fori_loop(0, n_grp, group, 0)


def _band_attn(q, k, v, bias_t, n_invalid):
    b, t, w = q.shape
    assert n_invalid % LANES == 0
    tq = -(-t // B_PAIR) * B_PAIR
    if tq != t:
        q = jnp.pad(q, ((0, 0), (0, tq - t), (0, 0)))
    tk = tq + B_LEFT_CHUNKS * CHUNK - n_invalid
    if k.shape[1] != tk:
        k = jnp.pad(k, ((0, 0), (0, tk - k.shape[1]), (0, 0)))
        v = jnp.pad(v, ((0, 0), (0, tk - v.shape[1]), (0, 0)))
    qb = min(8 * CHUNK, tq)
    n_kblk = tk // LANES
    body = functools.partial(_band_body, n_grp=qb // B_PAIR, n_invalid=n_invalid)
    out = pl.pallas_call(
        body,
        grid=(b, tq // qb),
        in_specs=[pl.BlockSpec((None, qb, w), lambda bi, i: (bi, i, 0)),
                  pl.BlockSpec((None, tk, w), lambda bi, i: (bi, 0, 0)),
                  pl.BlockSpec((None, tk, w), lambda bi, i: (bi, 0, 0)),
                  pl.BlockSpec(bias_t.shape, lambda bi, i: (0, 0, 0))],
        out_specs=pl.BlockSpec((None, qb, w), lambda bi, i: (bi, i, 0)),
        out_shape=jax.ShapeDtypeStruct((b, tq, w), F32),
        scratch_shapes=[pltpu.VMEM((n_kblk, LANES, w), BF16),
                        pltpu.VMEM((n_kblk, H_B, B_ACC_ROWS, LANES), BF16)],
        compiler_params=_params(("parallel", "arbitrary")),
        name="band_attn",
    )(q, k, v, bias_t)
    return out[:, :t] if tq != t else out


_SMALL_DECAY0 = H_C


def _split2(x):
    hi = x.astype(BF16)
    return hi, (x - hi.astype(F32)).astype(BF16)


def _gdn_body(cqkv_ref, small_ref, gate_ref, cprev_ref, s0_ref, convw_ref, alog_ref, dtb_ref, onorm_ref,
              tri_ref, bo_ref, eb_ref, eg_ref, oc_ref, sout_ref, xext, s_scr, *, tc):
    j = pl.program_id(1)
    n_bb = cqkv_ref.shape[0]
    n_ch = tc // CHUNK
    n_pair = H_C // 2
    pad = 8
    lo = pad - (CONV_W - 1)

    @pl.when(j == 0)
    def _():
        z = jnp.zeros((D_C, D_C), F32)
        for bb in range(n_bb):
            xext[bb, lo:pad, :] = cprev_ref[bb]
            for p in range(n_pair):
                s_scr[bb, p] = jnp.concatenate([jnp.concatenate([s0_ref[bb, 2 * p], z], axis=1),
                                                jnp.concatenate([z, s0_ref[bb, 2 * p + 1]], axis=1)], axis=0)

    @pl.when(j > 0)
    def _():
        for bb in range(n_bb):
            xext[bb, lo:pad, :] = xext[bb, tc + lo:tc + pad, :]

    bo = bo_ref[...]
    tri = tri_ref[...]
    lane_s = lax.broadcasted_iota(jnp.int32, (tc, LANES), 1)
    is_dec = (lane_s >= _SMALL_DECAY0) & (lane_s < _SMALL_DECAY0 + H_C)

    def head_sumsq(x):
        hi, lo_ = _split2(x * x)
        return jnp.concatenate([_dot(hi[:, t * LANES:(t + 1) * LANES], bo) + _dot(lo_[:, t * LANES:(t + 1) * LANES], bo)
                                for t in range(n_pair)], axis=1)

    qn, kn, gc, bc, xk, vb, qe = [], [], [], [], [], [], []
    for bb in range(n_bb):
        xext[bb, pad:, :] = cqkv_ref[bb]
        y = convw_ref[0:1, :] * xext[bb, lo:lo + tc, :]
        for w in range(1, CONV_W):
            y = y + convw_ref[w:w + 1, :] * xext[bb, lo + w:lo + w + tc, :]
        y = _silu(y)
        yq, yk, yv = y[:, :C_W], y[:, C_W:2 * C_W], y[:, 2 * C_W:]
        qn.append(yq * lax.rsqrt(head_sumsq(yq) + L2_EPS) * (D_C ** -0.5))
        kn.append(yk * lax.rsqrt(head_sumsq(yk) + L2_EPS))
        sm = small_ref[bb]
        log_a = jnp.where(is_dec, -jnp.exp(alog_ref[...]) * _softplus(sm + dtb_ref[...]), 0.0)
        g_full = sum(_dot(tri, part) for part in _split3(log_a))
        gc.append(sum(_dot(part, eg_ref[...]) for part in _split3(g_full)))
        bc.append(sum(_dot(part, eb_ref[...]) for part in _split3(_sigmoid(sm))))
        egc = jnp.exp(gc[bb])
        xk.append(bc[bb] * egc * kn[bb])
        vb.append(bc[bb] * yv)
        qe.append(egc * qn[bb])

    ii = lax.broadcasted_iota(jnp.int32, (CHUNK, LANES), 0)
    ln = lax.broadcasted_iota(jnp.int32, (CHUNK, LANES), 1)
    jn = ln % D_C
    incl2 = ii >= jn
    strict2 = ii > jn
    eye2 = ii == jn
    left = ln < D_C
    eye2f = jnp.where(eye2, 1.0, 0.0)

    def same_block(n):
        return (ii // n) == (jn // n)
    r128 = lax.broadcasted_iota(jnp.int32, (LANES, LANES), 0)
    c128 = lax.broadcasted_iota(jnp.int32, (LANES, LANES), 1)
    on_diag_blocks = (r128 < D_C) == (c128 < D_C)
    eye128 = jnp.where(r128 == c128, 1.0, 0.0).astype(BF16)

    keep_left = jnp.where(left, 1.0, 0.0).astype(BF16)
    keep_right = jnp.where(left, 0.0, 1.0).astype(BF16)

    def bdiag(x):
        xb = x.astype(BF16)
        return jnp.concatenate([xb * keep_left, xb * keep_right], axis=0)

    units = [(bb, r, p) for bb in range(n_bb) for r in range(n_ch) for p in range(n_pair)]
    lanes_bp = [(bb, p) for bb in range(n_bb) for p in range(n_pair)]

    def rows_of(r):
        return slice(r * CHUNK, (r + 1) * CHUNK)

    def tile_of(p):
        return slice(p * LANES, (p + 1) * LANES)

    def sub(x, u):
        bb, r, p = u
        return x[bb][rows_of(r), tile_of(p)]

    g_last = {(bb, r): gc[bb][(r + 1) * CHUNK - 1:(r + 1) * CHUNK, :] for bb in range(n_bb) for r in range(n_ch)}
    kd = {br: kn[br[0]][rows_of(br[1])] * jnp.exp(g - gc[br[0]][rows_of(br[1])]) for br, g in g_last.items()}
    eg_last = {br: jnp.exp(g) for br, g in g_last.items()}
    dec, r_k, kdt = {}, {}, {}
    for u in units:
        gcp = sub(gc, u)
        g_row = jnp.sum(jnp.where(eye2, gcp, 0.0), axis=0, keepdims=True)
        dec[u] = jnp.where(incl2, jnp.exp(jnp.where(incl2, gcp - g_row, 0.0)), 0.0)
        kp = sub(kn, u)
        r_k[u] = _dot_nt(jnp.concatenate([sub(qn, u), kp], axis=0).astype(BF16), bdiag(kp).astype(BF16))
    for bb, r, p in units:
        kdt[bb, r, p] = _dot_nt(eye128, bdiag(kd[bb, r][:, tile_of(p)]).astype(BF16)).astype(BF16)
    a_mat = {u: sub(bc, u) * jnp.where(strict2, dec[u], 0.0) * r_k[u][CHUNK:] for u in units}
    leaf = GDN_LEAF
    n_leaf = {u: jnp.where(strict2 & same_block(leaf), -a_mat[u], 0.0) for u in units}
    pw = {u: _dot(n_leaf[u].astype(BF16), bdiag(n_leaf[u])) for u in units}
    tm_ = {u: eye2f + n_leaf[u] for u in units}
    st = {u: _dot(jnp.concatenate([tm_[u], pw[u]], axis=0).astype(BF16), bdiag(pw[u])) for u in units}
    tm_ = {u: tm_[u] + st[u][:CHUNK] for u in units}
    fin = {u: _dot(tm_[u].astype(BF16), bdiag(st[u][CHUNK:])) for u in units}
    tm_ = {u: tm_[u] + fin[u] for u in units}
    size = leaf
    while size < CHUNK:
        coupling = strict2 & same_block(2 * size) & jnp.logical_not(same_block(size))
        cd = {u: _dot(jnp.where(coupling, a_mat[u], 0.0).astype(BF16), bdiag(tm_[u])) for u in units}
        dcd = {u: _dot(tm_[u].astype(BF16), bdiag(cd[u])) for u in units}
        tm_ = {u: tm_[u] - dcd[u] for u in units}
        size *= 2
    wu = {u: _dot(tm_[u].astype(BF16),
                  jnp.concatenate([bdiag(sub(xk, u)), bdiag(sub(vb, u))], axis=1).astype(BF16)) for u in units}
    w_b = {u: wu[u][:, :LANES].astype(BF16) for u in units}
    qkd = {u: (dec[u] * r_k[u][:CHUNK]).astype(BF16) for u in units}

    states = {bp: s_scr[bp[0], bp[1]] for bp in lanes_bp}
    o_tiles = {}
    for r in range(n_ch):
        res = {(bb, p): _dot(jnp.concatenate([sub(qe, (bb, r, p)).astype(BF16), w_b[bb, r, p]], axis=0),
                             states[bb, p].astype(BF16)) for bb, p in lanes_bp}
        u = {(bb, p): wu[bb, r, p][:, LANES:] - res[bb, p][CHUNK:] for bb, p in lanes_bp}
        upd = {(bb, p): _dot(kdt[bb, r, p], jnp.concatenate([u[bb, p], u[bb, p]], axis=0).astype(BF16))
               for bb, p in lanes_bp}
        o_inner = {(bb, p): _dot(qkd[bb, r, p], bdiag(u[bb, p]).astype(BF16)) for bb, p in lanes_bp}
        states = {(bb, p): eg_last[bb, r][:, tile_of(p)] * states[bb, p] + jnp.where(on_diag_blocks, upd[bb, p], 0.0)
                  for bb, p in lanes_bp}
        for bb, p in lanes_bp:
            o_tiles[bb, r, p] = res[bb, p][:CHUNK] + o_inner[bb, p]
    for bb, p in lanes_bp:
        s_scr[bb, p] = states[bb, p]
    for bb in range(n_bb):
        o = jnp.concatenate([jnp.concatenate([o_tiles[bb, r, p] for p in range(n_pair)], axis=1)
                             for r in range(n_ch)], axis=0)
        ms = head_sumsq(o) * (1.0 / D_C)
        oc_ref[bb] = o * lax.rsqrt(ms + NORM_EPS) * onorm_ref[...] * _silu(gate_ref[bb])

    @pl.when(j == pl.num_programs(1) - 1)
    def _():
        for bb, p in lanes_bp:
            sout_ref[bb, 2 * p] = states[bb, p][:D_C, :D_C]
            sout_ref[bb, 2 * p + 1] = states[bb, p][D_C:, D_C:]


def _gdn_constants(tc):
    i = jnp.arange(tc, dtype=jnp.int32)
    tri = ((i[None, :] <= i[:, None]) & (i[None, :] // CHUNK == i[:, None] // CHUNK)).astype(BF16)
    l = jnp.arange(LANES, dtype=jnp.int32)
    block_ones = (l[:, None] // D_C == l[None, :] // D_C).astype(BF16)
    col_head = jnp.arange(C_W, dtype=jnp.int32)[None, :] // D_C
    e_beta = (l[:, None] == col_head).astype(BF16)
    e_g = (l[:, None] == col_head + _SMALL_DECAY0).astype(BF16)
    return tri, block_ones, e_beta, e_g


def _gdn(cqkv, small, gate, conv_prev, s0, conv_w, alog_vec, dtb_vec, onorm_tiled):
    b, t, w3 = cqkv.shape
    tc = min(4 * CHUNK, t)
    nb = GDN_BATCH_BLOCK if b % GDN_BATCH_BLOCK == 0 else 1
    consts = _gdn_constants(tc)
    body = functools.partial(_gdn_body, tc=tc)
    blk = lambda bi, j: (bi, j, 0)
    per_b3 = lambda bi, j: (bi, 0, 0)
    per_b4 = lambda bi, j: (bi, 0, 0, 0)
    const2 = lambda bi, j: (0, 0)
    return pl.pallas_call(
        body,
        grid=(b // nb, t // tc),
        in_specs=[pl.BlockSpec((nb, tc, w3), blk),
                  pl.BlockSpec((nb, tc, LANES), blk),
                  pl.BlockSpec((nb, tc, C_W), blk),
                  pl.BlockSpec((nb, CONV_W - 1, w3), per_b3),
                  pl.BlockSpec((nb, H_C, D_C, D_C), per_b4),
                  pl.BlockSpec(conv_w.shape, const2),
                  pl.BlockSpec((1, LANES), const2),
                  pl.BlockSpec((1, LANES), const2),
                  pl.BlockSpec((1, C_W), const2)]
                 + [pl.BlockSpec(c.shape, const2) for c in consts],
        out_specs=[pl.BlockSpec((nb, tc, C_W), blk),
                   pl.BlockSpec((nb, H_C, D_C, D_C), per_b4)],
        out_shape=[jax.ShapeDtypeStruct((b, t, C_W), F32),
                   jax.ShapeDtypeStruct((b, H_C, D_C, D_C), F32)],
        scratch_shapes=[pltpu.VMEM((nb, tc + 8, w3), F32), pltpu.VMEM((nb, H_C // 2, LANES, LANES), F32)],
        compiler_params=_params(("parallel", "arbitrary")),
        name="gdn",
    )(cqkv, small, gate, conv_prev, s0, conv_w, alog_vec, dtb_vec, onorm_tiled, *consts)


def _top2_gates(logits):
    lane = lax.broadcasted_iota(jnp.int32, logits.shape, 1).astype(F32)
    low = -3.0e38
    lg = jnp.where(lane < N_EXPERTS, logits, low)
    m1 = jnp.max(lg, axis=-1, keepdims=True)
    i1 = jnp.min(jnp.where(lg == m1, lane, float(LANES)), axis=-1, keepdims=True)
    lg2 = jnp.where(lane == i1, low, lg)
    m2 = jnp.max(lg2, axis=-1, keepdims=True)
    i2 = jnp.min(jnp.where(lg2 == m2, lane, float(LANES)), axis=-1, keepdims=True)
    e2 = jnp.exp(m2 - m1)
    den = 1.0 + e2
    return jnp.where(lane == i1, 1.0 / den, 0.0) + jnp.where(lane == i2, e2 / den, 0.0)


def _outproj_body(x_ref, oa_ref, ob_ref, oc_ref, w_ref, g_ref, *rest, with_router):
    if with_router:
        rhi_ref, rlo_ref, xo_ref, h_ref, gates_ref = rest
    else:
        xo_ref, h_ref = rest
    y = (_dot(oa_ref[...].astype(BF16), w_ref[0:HEAD_W, :])
         + _dot(ob_ref[...].astype(BF16), w_ref[HEAD_W:2 * HEAD_W, :])
         + _dot(oc_ref[...].astype(BF16), w_ref[2 * HEAD_W:, :]))
    x = x_ref[...] + y
    xo_ref[...] = x
    hf = _rms(x, g_ref[...])
    hb = hf.astype(BF16)
    h_ref[...] = hb
    if with_router:
        lo = (hf - hb.astype(F32)).astype(BF16)
        logits = _dot(hb, rhi_ref[...]) + _dot(lo, rhi_ref[...]) + _dot(hb, rlo_ref[...])
        gates_ref[...] = _top2_gates(logits)


def _outproj(x, oa, ob, oc, w_out, g, router=None):
    n, d = x.shape
    tm = min(ROW_TILE, n)
    row = lambda i: (i, 0)
    const = lambda i: (0, 0)
    in_specs = [pl.BlockSpec((tm, d), row), pl.BlockSpec((tm, HEAD_W), row), pl.BlockSpec((tm, HEAD_W), row),
                pl.BlockSpec((tm, C_W), row), pl.BlockSpec(w_out.shape, const), pl.BlockSpec((1, d), const)]
    out_specs = [pl.BlockSpec((tm, d), row), pl.BlockSpec((tm, d), row)]
    out_shape = [jax.ShapeDtypeStruct((n, d), F32), jax.ShapeDtypeStruct((n, d), BF16)]
    args = [x, oa, ob, oc, w_out, g]
    if router is not None:
        in_specs += [pl.BlockSpec(router[0].shape, const), pl.BlockSpec(router[1].shape, const)]
        out_specs.append(pl.BlockSpec((tm, LANES), row))
        out_shape.append(jax.ShapeDtypeStruct((n, LANES), F32))
        args += list(router)
    return pl.pallas_call(
        functools.partial(_outproj_body, with_router=router is not None),
        grid=(n // tm,),
        in_specs=in_specs, out_specs=out_specs, out_shape=out_shape,
        compiler_params=_params(("parallel",)),
        name="outproj",
    )(*args)


def _ffn_body(h_ref, x_ref, wg_ref, wu_ref, wd_ref, o_ref, acc_ref, *, ff_chunk):
    h = h_ref[...]
    d_ff = wg_ref.shape[1]
    acc_ref[...] = x_ref[...]
    for c0 in range(0, d_ff, ff_chunk):
        a = _silu(_dot(h, wg_ref[:, c0:c0 + ff_chunk])) * _dot(h, wu_ref[:, c0:c0 + ff_chunk])
        acc_ref[...] += _dot(a.astype(BF16), wd_ref[c0:c0 + ff_chunk, :])
    o_ref[...] = acc_ref[...]


def _ffn_dense(h, x, wg, wu, wd):
    n, d = x.shape
    d_ff = wg.shape[1]
    tm = min(ROW_TILE, n)
    ff_chunk = 256 if d_ff % 256 == 0 else LANES
    row = lambda i: (i, 0)
    const = lambda i: (0, 0)
    return pl.pallas_call(
        functools.partial(_ffn_body, ff_chunk=ff_chunk),
        grid=(n // tm,),
        in_specs=[pl.BlockSpec((tm, d), row), pl.BlockSpec((tm, d), row),
                  pl.BlockSpec(wg.shape, const), pl.BlockSpec(wu.shape, const), pl.BlockSpec(wd.shape, const)],
        out_specs=pl.BlockSpec((tm, d), row),
        out_shape=jax.ShapeDtypeStruct((n, d), F32),
        scratch_shapes=[pltpu.VMEM((tm, d), F32)],
        compiler_params=_params(("parallel",)),
        name="ffn_dense",
    )(h, x, wg, wu, wd)


def _moe_body(cnt_ref, h_ref, gates_ref, gates_t_ref, wg_ref, wu_ref, wd_ref, *rest, tg, n_f, final_norm, tiles):
    if final_norm:
        x_ref, gfin_ref, y_ref, rank_col, rank_row, hs, eo = rest
    else:
        y_ref, rank_col, rank_row, hs, eo = rest
    g = pl.program_id(0)
    e = pl.program_id(1)
    f = pl.program_id(2)
    tb = MOE_TOKEN_BLOCK
    n_tb = tg // tb
    unit = tiles[-1]
    n_units = (cnt_ref[g * N_EXPERTS + e] + unit - 1) // unit
    units_full = tiles[0] // unit
    n_full = n_units // units_full
    rem = n_units % units_full

    def for_tiles(fn):
        def full(t, carry):
            fn(pl.multiple_of(t * tiles[0], tiles[0]), tiles[0])
            return carry

        lax.fori_loop(0, n_full, full, 0)
        for tm in tiles[1:]:
            bit = tm // unit
            higher = (units_full - 1) & ~(2 * bit - 1)

            @pl.when((rem & bit) != 0)
            def _(tm=tm, higher=higher):
                fn(pl.multiple_of(n_full * tiles[0] + (rem & higher) * unit, unit), tm)

    @pl.when((e == 0) & (f == 0))
    def _():
        y_ref[...] = jnp.zeros_like(y_ref)
        ii = lax.broadcasted_iota(jnp.int32, (tb, tb), 0)
        jj = lax.broadcasted_iota(jnp.int32, (tb, tb), 1)
        lower = jnp.where(ii > jj, 1.0, 0.0).astype(BF16)
        upper = jnp.where(ii < jj, 1.0, 0.0).astype(BF16)
        carry_c = jnp.zeros((1, LANES), F32)
        carry_r = jnp.zeros((N_EXPERTS, 1), F32)
        for b in range(n_tb):
            mc = jnp.where(gates_ref[b * tb:(b + 1) * tb, :] > 0.0, 1.0, 0.0)
            rank_col[b * tb:(b + 1) * tb, :] = _dot(lower, mc.astype(BF16)) + carry_c
            carry_c = carry_c + jnp.sum(mc, axis=0, keepdims=True)
            mr = jnp.where(gates_t_ref[:, b * tb:(b + 1) * tb] > 0.0, 1.0, 0.0)
            rank_row[:, b * tb:(b + 1) * tb] = _dot(mr.astype(BF16), upper) + carry_r
            carry_r = carry_r + jnp.sum(mr, axis=1, keepdims=True)

    def gather(r0, tm):
        rows = (r0 + lax.broadcasted_iota(jnp.int32, (tm, tg), 0)).astype(F32)
        gr = gates_t_ref[pl.ds(e, 1), :]
        rr = jnp.where(gr > 0.0, rank_row[pl.ds(e, 1), :], -1.0)
        onehot = jnp.where(rr == rows, 1.0, 0.0).astype(BF16)
        hs[pl.ds(r0, tm), :] = _dot(onehot, h_ref[...]).astype(BF16)

    @pl.when(f == 0)
    def _():
        for_tiles(gather)

    def expert(r0, tm):
        x = hs[pl.ds(r0, tm), :]
        a = _silu(_dot(x, wg_ref[...])) * _dot(x, wu_ref[...])
        part = _dot(a.astype(BF16), wd_ref[...])

        @pl.when(f == 0)
        def _():
            eo[pl.ds(r0, tm), :] = part

        @pl.when(f > 0)
        def _():
            eo[pl.ds(r0, tm), :] += part

    for_tiles(expert)

    def scatter(r0, tm):
        slab = min(MOE_SCATTER_SLAB, tg)
        lane = lax.broadcasted_iota(jnp.int32, (slab, LANES), 1)
        cols = (r0 + lax.broadcasted_iota(jnp.int32, (slab, tm), 1)).astype(F32)
        out = eo[pl.ds(r0, tm), :].astype(BF16)
        for c in range(tg // slab):
            rows = slice(c * slab, (c + 1) * slab)
            gc = jnp.sum(jnp.where(lane == e, gates_ref[rows, :], 0.0), axis=-1, keepdims=True)
            rc = jnp.sum(jnp.where(lane == e, rank_col[rows, :], 0.0), axis=-1, keepdims=True)
            rc = jnp.where(gc > 0.0, rc, -1.0)
            weighted = jnp.where(rc == cols, gc, 0.0).astype(BF16)
            y_ref[rows, :] += _dot(weighted, out)

    @pl.when(f == n_f - 1)
    def _():
        for_tiles(scatter)

    if final_norm:
        @pl.when((e == pl.num_programs(1) - 1) & (f == n_f - 1))
        def _():
            for b in range(n_tb):
                rows = slice(b * tb, (b + 1) * tb)
                y_ref[rows, :] = _rms(x_ref[rows, :] + y_ref[rows, :], gfin_ref[...])


def _moe_counts(gates, tg):
    routed = (gates[:, :N_EXPERTS] > 0.0).astype(jnp.int32)
    return routed.reshape(gates.shape[0] // tg, tg, N_EXPERTS).sum(axis=1).reshape(-1)


def _moe(h, gates, wg, wu, wd, tg, x=None, g_final=None):
    n, d = h.shape
    n_e, _, d_ff = wg.shape
    n_f = 2
    ffh = d_ff // n_f
    final_norm = x is not None
    tiles = tuple(t for t in MOE_TILES if t <= tg)
    body = functools.partial(_moe_body, tg=tg, n_f=n_f, final_norm=final_norm, tiles=tiles)
    once = pl.Buffered(1)
    extra_specs, extra_args = [], []
    if final_norm:
        extra_specs = [pl.BlockSpec((tg, d), lambda g, e, f, c: (g, 0), pipeline_mode=once),
                       pl.BlockSpec((1, d), lambda g, e, f, c: (0, 0))]
        extra_args = [x, g_final]
    grid_spec = pltpu.PrefetchScalarGridSpec(
        num_scalar_prefetch=1,
        grid=(n // tg, n_e, n_f),
        in_specs=[pl.BlockSpec((tg, d), lambda g, e, f, c: (g, 0), pipeline_mode=once),
                  pl.BlockSpec((tg, LANES), lambda g, e, f, c: (g, 0), pipeline_mode=once),
                  pl.BlockSpec((N_EXPERTS, tg), lambda g, e, f, c: (0, g), pipeline_mode=once),
                  pl.BlockSpec((None, d, ffh), lambda g, e, f, c: (e, 0, f)),
                  pl.BlockSpec((None, d, ffh), lambda g, e, f, c: (e, 0, f)),
                  pl.BlockSpec((None, ffh, d), lambda g, e, f, c: (e, f, 0))] + extra_specs,
        out_specs=pl.BlockSpec((tg, d), lambda g, e, f, c: (g, 0), pipeline_mode=once),
        scratch_shapes=[pltpu.VMEM((tg, LANES), F32), pltpu.VMEM((N_EXPERTS, tg), F32),
                        pltpu.VMEM((tg, d), BF16), pltpu.VMEM((tg, d), F32)],
    )
    return pl.pallas_call(
        body,
        grid_spec=grid_spec,
        out_shape=jax.ShapeDtypeStruct((n, d), F32),
        compiler_params=_params(("parallel", "arbitrary", "arbitrary")),
        name="moe",
    )(_moe_counts(gates, tg), h, gates, gates[:, :N_EXPERTS].T, wg, wu, wd, *extra_args)


def _final_body(x_ref, y_ref, g_ref, o_ref):
    o_ref[...] = _rms(x_ref[...] + y_ref[...], g_ref[...])


def _final_norm(x, y, g):
    n, d = x.shape
    tm = min(ROW_TILE, n)
    row = lambda i: (i, 0)
    return pl.pallas_call(
        _final_body,
        grid=(n // tm,),
        in_specs=[pl.BlockSpec((tm, d), row), pl.BlockSpec((tm, d), row), pl.BlockSpec((1, d), lambda i: (0, 0))],
        out_specs=pl.BlockSpec((tm, d), row),
        out_shape=jax.ShapeDtypeStruct((n, d), F32),
        compiler_params=_params(("parallel",)),
        name="final_norm",
    )(x, y, g)


def _t5_bucket(rel):
    nb = T5_BUCKETS // 2
    max_exact = nb // 2
    ret = jnp.where(rel > 0, nb, 0)
    n = jnp.abs(rel)
    nf = jnp.maximum(n, 1).astype(F32)
    large = max_exact + (jnp.log(nf / max_exact) / math.log(T5_MAX_DIST / max_exact) * (nb - max_exact)).astype(jnp.int32)
    large = jnp.minimum(large, nb - 1)
    return ret + jnp.where(n < max_exact, n, large)


def _t5_table(t5_bias, q_pos, k_pos):
    bias = _lookup(t5_bias, _t5_bucket(k_pos[None, :] - q_pos[:, None]))
    mask = (k_pos[None, :] // CHUNK) <= (q_pos[:, None] // CHUNK)
    return bias, mask[None]


def _lookup(table, idx):
    onehot = jax.nn.one_hot(idx, table.shape[0], dtype=F32)
    return jnp.einsum("qkn,nh->hqk", onehot, table.astype(F32), precision=lax.Precision.HIGHEST)


def _band_table(rel_bias):
    qi = jnp.arange(CHUNK, dtype=jnp.int32)
    kj = jnp.arange(BAND, dtype=jnp.int32) - B_LEFT_CHUNKS * CHUNK
    rel = jnp.clip(kj[None, :] - qi[:, None], -B_REL_CLIP, B_REL_CLIP) + B_REL_CLIP
    base = jnp.swapaxes(_lookup(rel_bias, rel), 1, 2) * LOG2E
    halves = [jnp.pad(base, ((0, 0), (c * CHUNK, (1 - c) * CHUNK), (0, 0)), constant_values=NEG_INF)
              for c in range(2)]
    return jnp.concatenate(halves, axis=2)


def _lane_vec(v, offset):
    return jnp.zeros((1, LANES), F32).at[0, offset:offset + v.shape[0]].set(v.astype(F32))


def _moe_group(n):
    for tg in (2048, 1024, 512, 256):
        if n % tg == 0:
            return tg
    raise ValueError(f"token count {n} is not a multiple of 256")


def kernel(x_prompt, x_sample, cache_a_k, cache_a_v, cache_b_k, cache_b_v, cache_c_conv, state_c_ssm, w_in, w_out, norm_mix, norm_ffn, norm_final, lam_qk, subln_a, t5_bias, rel_bias_b, conv_c, a_log_c, dt_bias_c, onorm_c, ffn_gate, ffn_up, ffn_down, moe_router, moe_gate, moe_up, moe_down):
    depth = w_in.shape[0]
    bp, sp, d = x_prompt.shape
    bs, ss, _ = x_sample.shape
    past = cache_a_k.shape[2]
    nb_cache = cache_b_k.shape[2]
    assert ss == CHUNK and nb_cache == B_LEFT_CHUNKS * CHUNK and sp >= B_LEFT_CHUNKS * CHUNK

    xp = x_prompt.reshape(bp * sp, d)
    xs = x_sample.reshape(bs * ss, d)
    blk = min(A_BLOCK, sp)

    pos_blk = jnp.arange(blk, dtype=jnp.int32)
    far_bias = t5_bias[_t5_bucket(jnp.int32(-(blk + 1)))].astype(F32)
    bd, md = _t5_table(t5_bias, pos_blk, pos_blk)
    bias_diag = jnp.where(md, bd - far_bias[:, None, None], NEG_INF)
    bsub, _ = _t5_table(t5_bias, blk + pos_blk, pos_blk)
    bias_sub = bsub - far_bias[:, None, None]
    q_pos_s = past + jnp.arange(ss, dtype=jnp.int32)
    bfull, mfull = _t5_table(t5_bias, q_pos_s, jnp.arange(past + ss, dtype=jnp.int32))
    bias_s = jnp.where(mfull, bfull, NEG_INF)
    bias_s_t = jnp.swapaxes(bias_s, 1, 2) * LOG2E
    bias_s_cache = bias_s_t[:, :past]
    bias_s_new = jnp.pad(bias_s_t[:, past:], ((0, 0), (0, LANES - ss), (0, 0)), constant_values=NEG_INF)

    zeros_conv = jnp.zeros((bp, CONV_W - 1, 3 * C_W), F32)
    zeros_state = jnp.zeros((bp, H_C, D_C, D_C), F32)

    p_states = [[] for _ in range(6)]
    s_states = [[] for _ in range(6)]
    yp = ys = None
    g_fin = norm_final.reshape(1, d)
    finals = []
    for l in range(depth):
        if yp is not None:
            xp, xs, yp, ys = xp + yp, xs + ys, None, None
        w = w_in[l]
        n_main = 6 * HEAD_W + 3 * C_W
        w_main = jnp.concatenate([w[:, :n_main], w[:, n_main + 2 * H_C:]], axis=1).astype(BF16)
        w_small = jnp.zeros((d, LANES), F32).at[:, :2 * H_C].set(w[:, n_main:n_main + 2 * H_C]).astype(BF16)
        w_out_l = w_out[l].astype(BF16)
        g_mix = norm_mix[l].reshape(1, d)
        g_ffn = norm_ffn[l].reshape(1, d)
        lam_init = 0.8 - 0.6 * math.exp(-0.3 * l)
        lq = lam_qk[l].astype(F32)
        lam = (jnp.exp(jnp.sum(lq[0] * lq[1])) - jnp.exp(jnp.sum(lq[2] * lq[3])) + lam_init).reshape(1)
        subln = subln_a[l].reshape(1, DV_A)
        band_bias = _band_table(rel_bias_b[l])
        alog_vec = _lane_vec(a_log_c[l], _SMALL_DECAY0)
        dtb_vec = _lane_vec(dt_bias_c[l], _SMALL_DECAY0)
        onorm = jnp.tile(onorm_c[l].astype(F32), H_C).reshape(1, C_W)
        is_moe = l % 2 == 1
        if is_moe:
            r = jnp.zeros((d, LANES), F32).at[:, :N_EXPERTS].set(moe_router[l // 2])
            r_hi = r.astype(BF16)
            router = (r_hi, (r - r_hi.astype(F32)).astype(BF16))
            e_wg, e_wu, e_wd = (moe_gate[l // 2].astype(BF16), moe_up[l // 2].astype(BF16),
                                moe_down[l // 2].astype(BF16))
        else:
            router = None
            f_wg, f_wu, f_wd = (ffn_gate[l // 2].astype(BF16), ffn_up[l // 2].astype(BF16),
                                ffn_down[l // 2].astype(BF16))

        new_x = []
        for is_prompt, x in ((True, xp), (False, xs)):
            b, t = (bp, sp) if is_prompt else (bs, ss)
            aq, ak, av, bq, bk, bv, cqkv, cgate, csmall = _inproj(x, g_mix, w_main, w_small)
            r3 = lambda a: a.reshape(b, t, a.shape[-1])
            if is_prompt:
                oa = _attn_a_prompt(lam, r3(aq), r3(ak), r3(av), jnp.swapaxes(bias_diag, 1, 2) * LOG2E,
                                    jnp.swapaxes(bias_sub, 1, 2) * LOG2E, subln.reshape(DV_A, 1), 1.0 - lam_init)
                ob = _band_attn(r3(bq), r3(bk), r3(bv), band_bias, B_LEFT_CHUNKS * CHUNK)
                conv_prev, s0 = zeros_conv, zeros_state
            else:
                oa = _attn_a_sample(lam, r3(aq), cache_a_k[l].reshape(b, past, HEAD_W),
                                    cache_a_v[l].reshape(b, past, HEAD_W), r3(ak), r3(av),
                                    bias_s_cache, bias_s_new, subln.reshape(DV_A, 1), 1.0 - lam_init)
                kb = jnp.concatenate([cache_b_k[l].reshape(b, nb_cache, HEAD_W), r3(bk)], axis=1)
                vb = jnp.concatenate([cache_b_v[l].reshape(b, nb_cache, HEAD_W), r3(bv)], axis=1)
                ob = _band_attn(r3(bq), kb, vb, band_bias, 0)
                conv_prev, s0 = cache_c_conv[l], state_c_ssm[l]
            oc, s_new = _gdn(r3(cqkv), r3(csmall), r3(cgate), conv_prev, s0, conv_c[l], alog_vec, dtb_vec, onorm)
            n = b * t
            res = _outproj(x, oa.reshape(n, HEAD_W), ob.reshape(n, HEAD_W), oc.reshape(n, C_W), w_out_l, g_ffn,
                           router)
            if is_moe:
                x_new, h2, gates = res
                if l == depth - 1:
                    finals.append(_moe(h2, gates, e_wg, e_wu, e_wd, _moe_group(n), x=x_new, g_final=g_fin))
                    new_x.append((x_new, None))
                else:
                    new_x.append((x_new, _moe(h2, gates, e_wg, e_wu, e_wd, _moe_group(n))))
            else:
                x_new, h2 = res
                new_x.append((_ffn_dense(h2, x_new, f_wg, f_wu, f_wd), None))
            keep = min(B_LEFT_CHUNKS * CHUNK, t)
            states = (ak.reshape(b, t, H_A, 2 * DQK_A), av.reshape(b, t, H_A, DV_A),
                      bk.reshape(b, t, H_B, D_B)[:, t - keep:], bv.reshape(b, t, H_B, D_B)[:, t - keep:],
                      r3(cqkv)[:, t - (CONV_W - 1):], s_new)
            for i in range(6):
                (p_states if is_prompt else s_states)[i].append(states[i])
        (xp, yp), (xs, ys) = new_x

    if finals:
        y_prompt, y_sample = finals[0].reshape(bp, sp, d), finals[1].reshape(bs, ss, d)
    else:
        if yp is None:
            yp, ys = jnp.zeros_like(xp), jnp.zeros_like(xs)
        y_prompt = _final_norm(xp, yp, g_fin).reshape(bp, sp, d)
        y_sample = _final_norm(xs, ys, g_fin).reshape(bs, ss, d)
    p_out = [jnp.stack(s, axis=0) for s in p_states]
    s_out = [jnp.stack(s, axis=0) for s in s_states]
    return (y_prompt, y_sample, *p_out, *s_out)
```

```python
import functools
import math

import jax
import jax.numpy as jnp
from jax import lax
from jax.experimental import pallas as pl
from jax.experimental.pallas import tpu as pltpu

F32 = jnp.float32
BF16 = jnp.bfloat16

CHUNK = 64
H_A = 4
DQK_A = 32
DV_A = 64
H_B = 4
D_B = 64
B_LEFT_CHUNKS = 8
B_REL_CLIP = 128
H_C = 8
D_C = 64
CONV_W = 4
T5_BUCKETS = 32
T5_MAX_DIST = 128
N_EXPERTS = 8
NORM_EPS = 1e-6
L2_EPS = 1e-6
NEG_INF = -1e30
HEAD_W = 256
C_W = H_C * D_C
BAND = (B_LEFT_CHUNKS + 1) * CHUNK
LANES = 128
LOG2E = math.log2(math.e)
VMEM_LIMIT = 56 * 1024 * 1024

A_BLOCK = 256
ROW_TILE = 512
GDN_BATCH_BLOCK = 2
GDN_LEAF = 8
MOE_TILES = (512, 256, 128, 64)
MOE_TOKEN_BLOCK = 256
MOE_WINDOW = 96
MOE_SLACK = 128
MOE_F32_ROWS = 8
MOE_BF16_ROWS = 16


def _params(sem):
    return pltpu.CompilerParams(dimension_semantics=sem, vmem_limit_bytes=VMEM_LIMIT)


def _rms(x, g):
    ms = jnp.mean(x * x, axis=-1, keepdims=True)
    return x * lax.rsqrt(ms + NORM_EPS) * g


def _sigmoid(x):
    return 1.0 / (1.0 + jnp.exp(-x))


def _silu(x):
    return x * _sigmoid(x)


def _softplus(x):
    return jnp.maximum(x, 0.0) + jnp.log1p(jnp.exp(-jnp.abs(x)))


def _dot(a, b):
    return jnp.dot(a, b, preferred_element_type=F32)


def _dot_nt(a, b):
    return lax.dot_general(a, b, (((1,), (1,)), ((), ())), preferred_element_type=F32)


def _dot_tn(a, b):
    return lax.dot_general(a, b, (((0,), (0,)), ((), ())), preferred_element_type=F32)


def _split3(x):
    x1 = x.astype(BF16)
    r1 = x - x1.astype(F32)
    x2 = r1.astype(BF16)
    x3 = (r1 - x2.astype(F32)).astype(BF16)
    return x1, x2, x3


def _inproj_body(x_ref, g_ref, w_ref, ws_ref, aq_ref, ak_ref, av_ref, bq_ref, bk_ref, bv_ref,
                 cqkv_ref, cgate_ref, csmall_ref):
    h = _rms(x_ref[...], g_ref[...]).astype(BF16)
    col = 0
    for ref in (aq_ref, ak_ref, av_ref, bq_ref, bk_ref, bv_ref, cqkv_ref, cgate_ref):
        width = ref.shape[-1]
        ref[...] = _dot(h, w_ref[:, col:col + width])
        col += width
    csmall_ref[...] = _dot(h, ws_ref[...])


def _inproj(x, g, w_main, w_small):
    n, d = x.shape
    tm = min(ROW_TILE, n)
    widths = (HEAD_W,) * 6 + (3 * C_W, C_W, LANES)
    row = lambda i: (i, 0)
    const = lambda i: (0, 0)
    return pl.pallas_call(
        _inproj_body,
        grid=(n // tm,),
        in_specs=[pl.BlockSpec((tm, d), row), pl.BlockSpec((1, d), const),
                  pl.BlockSpec(w_main.shape, const), pl.BlockSpec(w_small.shape, const)],
        out_specs=[pl.BlockSpec((tm, w), row) for w in widths],
        out_shape=[jax.ShapeDtypeStruct((n, w), F32) for w in widths],
        compiler_params=_params(("parallel",)),
        name="inproj",
    )(x, g, w_main, w_small)


A_ACC_ROWS = DV_A + 16
A_GROUPS = LANES // DQK_A


def _attn_a_prompt_body(lam_ref, q_ref, k_ref, v_ref, bd_ref, bs_ref, subln_ref, o_ref,
                        kbf, vte, wq, m_scr, acc_scr, s_even, s_odd, *, blk, post_scale):
    i = pl.program_id(1)
    n_blk = k_ref.shape[0] // blk
    n_pair = H_A // 2

    @pl.when(i == 0)
    def _():
        ones_rows = jnp.where(lax.broadcasted_iota(jnp.int32, (A_ACC_ROWS - DV_A, blk), 0) == 0, 1.0, 0.0)
        for jb in range(n_blk):
            kbf[jb] = k_ref[jb * blk:(jb + 1) * blk, :].astype(BF16)
            vt = v_ref[jb * blk:(jb + 1) * blk, :].T
            for h in range(H_A):
                vte[jb, h, 0:DV_A, :] = vt[h * DV_A:(h + 1) * DV_A, :].astype(BF16)
                vte[jb, h, DV_A:, :] = ones_rows.astype(BF16)

    lam = lam_ref[0]
    qt = (q_ref[...] * (DQK_A ** -0.5 * LOG2E)).T
    grp = lax.broadcasted_iota(jnp.int32, (LANES, blk), 0) // DQK_A
    for p in range(n_pair):
        qtp = qt[p * LANES:(p + 1) * LANES, :]
        wq[p] = jnp.concatenate([jnp.where(grp == g, qtp, 0.0) for g in range(A_GROUPS)], axis=1).astype(BF16)
    m_scr[...] = jnp.full(m_scr.shape, NEG_INF, F32)
    acc_scr[...] = jnp.zeros(acc_scr.shape, F32)

    def scores(jb):
        return [_dot(kbf[jb, :, p * LANES:(p + 1) * LANES], wq[p]) for p in range(n_pair)]

    def consume(tile_of, jb, bias_ref, off):
        groups = range(n_pair * A_GROUPS)
        alphas, pts = [], []
        for g in groups:
            p, gi = divmod(g, A_GROUPS)
            s = tile_of(p, gi)
            if bias_ref is not None:
                s = s + bias_ref[2 * p + gi // 2]
            if off is not None:
                s = s + off
            m_old = m_scr[g:g + 1, :]
            m_new = jnp.maximum(m_old, jnp.max(s, axis=0, keepdims=True))
            alphas.append(jnp.exp2(m_old - m_new))
            pts.append(jnp.exp2(s - m_new).astype(BF16))
            m_scr[g:g + 1, :] = m_new
        pvs = [_dot(vte[jb, 2 * (g // A_GROUPS) + (g % A_GROUPS) // 2], pts[g]) for g in groups]
        for g in groups:
            acc_scr[g] = alphas[g] * acc_scr[g] + pvs[g]

    def tiles(jb, bias_ref, off):
        s_alls = scores(jb)
        consume(lambda p, gi: s_alls[p][:, gi * blk:(gi + 1) * blk], jb, bias_ref, off)

    tiles(i, bd_ref, None)
    tiles(jnp.maximum(i - 1, 0), bs_ref, jnp.where(i >= 1, 0.0, NEG_INF))

    n_far = jnp.maximum(i - 1, 0)

    def put_scores(dst, jb):
        s_alls = scores(jnp.minimum(jb, n_far - 1))
        for p in range(n_pair):
            dst[p] = s_alls[p]

    def from_scratch(src):
        return lambda p, gi: src[p, :, gi * blk:(gi + 1) * blk]

    @pl.when(n_far > 0)
    def _():
        put_scores(s_even, 0)

    def far(t, carry):
        jb = 2 * t
        put_scores(s_odd, jb + 1)
        consume(from_scratch(s_even), jb, None, None)

        @pl.when(jb + 1 < n_far)
        def _():
            put_scores(s_even, jb + 2)
            consume(from_scratch(s_odd), jb + 1, None, None)

        return carry

    lax.fori_loop(0, (n_far + 1) // 2, far, 0)

    outs = []
    for h in range(H_A):
        g1 = (h // 2) * A_GROUPS + (h % 2) * 2
        a1 = acc_scr[g1]
        a2 = acc_scr[g1 + 1]
        o = a1[:DV_A] / a1[DV_A:DV_A + 1] - lam * (a2[:DV_A] / a2[DV_A:DV_A + 1])
        ms = jnp.mean(o * o, axis=0, keepdims=True)
        outs.append(o * lax.rsqrt(ms + NORM_EPS) * subln_ref[...] * post_scale)
    o_ref[...] = jnp.concatenate(outs, axis=0).T


def _attn_a_prompt(lam, q, k, v, bias_diag_t, bias_sub_t, subln_col, post_scale):
    b, s, w = q.shape
    blk = min(A_BLOCK, s)
    n_blk = s // blk
    body = functools.partial(_attn_a_prompt_body, blk=blk, post_scale=post_scale)
    return pl.pallas_call(
        body,
        grid=(b, n_blk),
        in_specs=[pl.BlockSpec(memory_space=pltpu.SMEM),
                  pl.BlockSpec((None, blk, w), lambda bi, i: (bi, i, 0)),
                  pl.BlockSpec((None, s, w), lambda bi, i: (bi, 0, 0)),
                  pl.BlockSpec((None, s, w), lambda bi, i: (bi, 0, 0)),
                  pl.BlockSpec(bias_diag_t.shape, lambda bi, i: (0, 0, 0)),
                  pl.BlockSpec(bias_sub_t.shape, lambda bi, i: (0, 0, 0)),
                  pl.BlockSpec((DV_A, 1), lambda bi, i: (0, 0))],
        out_specs=pl.BlockSpec((None, blk, w), lambda bi, i: (bi, i, 0)),
        out_shape=jax.ShapeDtypeStruct((b, s, w), F32),
        scratch_shapes=[pltpu.VMEM((n_blk, blk, w), BF16),
                        pltpu.VMEM((n_blk, H_A, A_ACC_ROWS, blk), BF16),
                        pltpu.VMEM((H_A // 2, LANES, A_GROUPS * blk), BF16),
                        pltpu.VMEM((H_A * 2, blk), F32),
                        pltpu.VMEM((H_A * 2, A_ACC_ROWS, blk), F32),
                        pltpu.VMEM((H_A // 2, blk, A_GROUPS * blk), F32),
                        pltpu.VMEM((H_A // 2, blk, A_GROUPS * blk), F32)],
        compiler_params=_params(("parallel", "arbitrary")),
        name="attn_a_prompt",
    )(lam, q, k, v, bias_diag_t, bias_sub_t, subln_col)


def _attn_a_sample_body(lam_ref, q_ref, kc_ref, vc_ref, kn_ref, vn_ref, bc_ref, bn_ref, subln_ref, o_ref, vte_c,
                        *, post_scale):
    t = q_ref.shape[0]
    p_len = kc_ref.shape[0]
    n_pair = H_A // 2
    lam = lam_ref[0]

    ones_rows = jnp.where(lax.broadcasted_iota(jnp.int32, (A_ACC_ROWS - DV_A, LANES), 0) == 0, 1.0, 0.0).astype(BF16)
    for jb in range(p_len // LANES):
        cols = slice(jb * LANES, (jb + 1) * LANES)
        vt = vc_ref[cols, :].T
        for h in range(H_A):
            vte_c[h, 0:DV_A, cols] = vt[h * DV_A:(h + 1) * DV_A, :].astype(BF16)
            vte_c[h, DV_A:, cols] = ones_rows
    row_pad = jnp.zeros((LANES - t, q_ref.shape[1]), F32)
    vt_n = jnp.concatenate([vn_ref[...], row_pad], axis=0).T
    vte_n = [jnp.concatenate([vt_n[h * DV_A:(h + 1) * DV_A, :].astype(BF16), ones_rows], axis=0) for h in range(H_A)]
    kn_p = jnp.concatenate([kn_ref[...], row_pad], axis=0).astype(BF16)

    qt = jnp.concatenate([q_ref[...] * (DQK_A ** -0.5 * LOG2E), row_pad], axis=0).T
    grp = lax.broadcasted_iota(jnp.int32, (LANES, t), 0) // DQK_A
    s_c, s_n = [], []
    for p in range(n_pair):
        qtp = qt[p * LANES:(p + 1) * LANES, 0:t]
        w = jnp.concatenate([jnp.where(grp == g, qtp, 0.0) for g in range(A_GROUPS)], axis=1).astype(BF16)
        s_c.append(_dot(kc_ref[:, p * LANES:(p + 1) * LANES].astype(BF16), w))
        s_n.append(_dot(kn_p[:, p * LANES:(p + 1) * LANES], w))
    pcs, pns = [], []
    for h in range(H_A):
        cols = slice((h % 2) * 2 * t, (h % 2 + 1) * 2 * t)
        sc = s_c[h // 2][:, cols] + jnp.concatenate([bc_ref[h], bc_ref[h]], axis=1)
        sn = s_n[h // 2][:, cols] + jnp.concatenate([bn_ref[h], bn_ref[h]], axis=1)
        mx = jnp.maximum(jnp.max(sc, axis=0, keepdims=True), jnp.max(sn, axis=0, keepdims=True))
        pcs.append(jnp.exp2(sc - mx).astype(BF16))
        pns.append(jnp.exp2(sn - mx).astype(BF16))
    accs = [_dot(vte_c[h], pcs[h]) + _dot(vte_n[h], pns[h]) for h in range(H_A)]
    outs = []
    for a in accs:
        on = a[:DV_A] / a[DV_A:DV_A + 1]
        o = on[:, 0:t] - lam * on[:, t:2 * t]
        ms = jnp.mean(o * o, axis=0, keepdims=True)
        outs.append(o * lax.rsqrt(ms + NORM_EPS) * subln_ref[...] * post_scale)
    ot = jnp.concatenate(outs, axis=0)
    ot = jnp.concatenate([ot, jnp.zeros((ot.shape[0], LANES - t), F32)], axis=1)
    o_ref[...] = ot.T[0:t, :]


def _attn_a_sample(lam, q, kc, vc, kn, vn, bias_c_t, bias_n_t, subln_col, post_scale):
    b, t, w = q.shape
    p = kc.shape[1]
    assert 2 * t == LANES and p % LANES == 0
    body = functools.partial(_attn_a_sample_body, post_scale=post_scale)
    per_b = lambda bi: (bi, 0, 0)
    const3 = lambda bi: (0, 0, 0)
    return pl.pallas_call(
        body,
        grid=(b,),
        in_specs=[pl.BlockSpec(memory_space=pltpu.SMEM),
                  pl.BlockSpec((None, t, w), per_b),
                  pl.BlockSpec((None, p, w), per_b), pl.BlockSpec((None, p, w), per_b),
                  pl.BlockSpec((None, t, w), per_b), pl.BlockSpec((None, t, w), per_b),
                  pl.BlockSpec(bias_c_t.shape, const3, pipeline_mode=pl.Buffered(1)),
                  pl.BlockSpec(bias_n_t.shape, const3),
                  pl.BlockSpec((DV_A, 1), lambda bi: (0, 0))],
        out_specs=pl.BlockSpec((None, t, w), per_b),
        out_shape=jax.ShapeDtypeStruct((b, t, w), F32),
        scratch_shapes=[pltpu.VMEM((H_A, A_ACC_ROWS, p), BF16)],
        compiler_params=_params(("parallel",)),
        name="attn_a_sample",
    )(lam, q, kc, vc, kn, vn, bias_c_t, bias_n_t, subln_col)


B_PAIR = 2 * CHUNK
B_UNION = BAND + CHUNK
B_UNION_BLOCKS = B_UNION // LANES
B_ACC_ROWS = D_B + 16


def _band_body(q_ref, k_ref, v_ref, bias_ref, o_ref, kbf, vte, *, n_grp, n_invalid):
    i = pl.program_id(1)
    n_kblk = k_ref.shape[0] // LANES

    @pl.when(i == 0)
    def _():
        ones_rows = jnp.where(lax.broadcasted_iota(jnp.int32, (B_ACC_ROWS - D_B, LANES), 0) == 0, 1.0, 0.0)
        for jb in range(n_kblk):
            kbf[jb] = k_ref[jb * LANES:(jb + 1) * LANES, :].astype(BF16)
            vt = v_ref[jb * LANES:(jb + 1) * LANES, :].T
            for h in range(H_B):
                vte[jb, h, 0:D_B, :] = vt[h * D_B:(h + 1) * D_B, :].astype(BF16)
                vte[jb, h, D_B:, :] = ones_rows.astype(BF16)

    top = lax.broadcasted_iota(jnp.int32, (LANES, B_PAIR), 0) < D_B
    key_row = lax.broadcasted_iota(jnp.int32, (B_UNION, B_PAIR), 0)

    def group(gl, carry):
        g = i * n_grp + gl
        r0 = pl.multiple_of(gl * B_PAIR, B_PAIR)
        qt = (q_ref[pl.ds(r0, B_PAIR), :] * (D_B ** -0.5 * LOG2E)).T
        blocks = [jnp.maximum(g + t - n_invalid // LANES, 0) for t in range(B_UNION_BLOCKS)]
        kun = jnp.concatenate([kbf[jb] for jb in blocks], axis=0)
        s_pairs = []
        for p in range(H_B // 2):
            qtp = qt[p * LANES:(p + 1) * LANES, :]
            w = jnp.concatenate([jnp.where(top, qtp, 0.0), jnp.where(top, 0.0, qtp)], axis=1).astype(BF16)
            s_pairs.append(_dot(kun[:, p * LANES:(p + 1) * LANES], w))
        pts = []
        for h in range(H_B):
            s = s_pairs[h // 2][:, (h % 2) * B_PAIR:(h % 2 + 1) * B_PAIR] + bias_ref[h]
            if n_invalid:
                s = jnp.where(g * B_PAIR + key_row >= n_invalid, s, NEG_INF)
            pts.append(jnp.exp2(s - jnp.max(s, axis=0, keepdims=True)).astype(BF16))
        accs = [_dot(jnp.concatenate([vte[jb, h] for jb in blocks], axis=1), pts[h]) for h in range(H_B)]
        ot = jnp.concatenate([a[:D_B] / a[D_B:D_B + 1] for a in accs], axis=0)
        o_ref[pl.ds(r0, B_PAIR), :] = ot.T
        return carry

    lax.fori_loop(0, n_grp, group, 0)


def _band_attn(q, k, v, bias_t, n_invalid):
    b, t, w = q.shape
    assert n_invalid % LANES == 0
    tq = -(-t // B_PAIR) * B_PAIR
    if tq != t:
        q = jnp.pad(q, ((0, 0), (0, tq - t), (0, 0)))
    tk = tq + B_LEFT_CHUNKS * CHUNK - n_invalid
    if k.shape[1] != tk:
        k = jnp.pad(k, ((0, 0), (0, tk - k.shape[1]), (0, 0)))
        v = jnp.pad(v, ((0, 0), (0, tk - v.shape[1]), (0, 0)))
    qb = min(8 * CHUNK, tq)
    n_kblk = tk // LANES
    body = functools.partial(_band_body, n_grp=qb // B_PAIR, n_invalid=n_invalid)
    out = pl.pallas_call(
        body,
        grid=(b, tq // qb),
        in_specs=[pl.BlockSpec((None, qb, w), lambda bi, i: (bi, i, 0)),
                  pl.BlockSpec((None, tk, w), lambda bi, i: (bi, 0, 0)),
                  pl.BlockSpec((None, tk, w), lambda bi, i: (bi, 0, 0)),
                  pl.BlockSpec(bias_t.shape, lambda bi, i: (0, 0, 0))],
        out_specs=pl.BlockSpec((None, qb, w), lambda bi, i: (bi, i, 0)),
        out_shape=jax.ShapeDtypeStruct((b, tq, w), F32),
        scratch_shapes=[pltpu.VMEM((n_kblk, LANES, w), BF16),
                        pltpu.VMEM((n_kblk, H_B, B_ACC_ROWS, LANES), BF16)],
        compiler_params=_params(("parallel", "arbitrary")),
        name="band_attn",
    )(q, k, v, bias_t)
    return out[:, :t] if tq != t else out


_SMALL_DECAY0 = H_C


def _split2(x):
    hi = x.astype(BF16)
    return hi, (x - hi.astype(F32)).astype(BF16)


def _gdn_body(cqkv_ref, small_ref, gate_ref, cprev_ref, s0_ref, convw_ref, alog_ref, dtb_ref, onorm_ref,
              tri_ref, bo_ref, eb_ref, eg_ref, oc_ref, sout_ref, xext, s_scr, *, tc):
    j = pl.program_id(1)
    n_bb = cqkv_ref.shape[0]
    n_ch = tc // CHUNK
    n_pair = H_C // 2
    pad = 8
    lo = pad - (CONV_W - 1)

    @pl.when(j == 0)
    def _():
        z = jnp.zeros((D_C, D_C), F32)
        for bb in range(n_bb):
            xext[bb, lo:pad, :] = cprev_ref[bb]
            for p in range(n_pair):
                s_scr[bb, p] = jnp.concatenate([jnp.concatenate([s0_ref[bb, 2 * p], z], axis=1),
                                                jnp.concatenate([z, s0_ref[bb, 2 * p + 1]], axis=1)], axis=0)

    @pl.when(j > 0)
    def _():
        for bb in range(n_bb):
            xext[bb, lo:pad, :] = xext[bb, tc + lo:tc + pad, :]

    bo = bo_ref[...]
    tri = tri_ref[...]
    lane_s = lax.broadcasted_iota(jnp.int32, (tc, LANES), 1)
    is_dec = (lane_s >= _SMALL_DECAY0) & (lane_s < _SMALL_DECAY0 + H_C)

    def head_sumsq(x):
        hi, lo_ = _split2(x * x)
        return jnp.concatenate([_dot(hi[:, t * LANES:(t + 1) * LANES], bo) + _dot(lo_[:, t * LANES:(t + 1) * LANES], bo)
                                for t in range(n_pair)], axis=1)

    qn, kn, gc, bc, xk, vb, qe = [], [], [], [], [], [], []
    for bb in range(n_bb):
        xext[bb, pad:, :] = cqkv_ref[bb]
        y = convw_ref[0:1, :] * xext[bb, lo:lo + tc, :]
        for w in range(1, CONV_W):
            y = y + convw_ref[w:w + 1, :] * xext[bb, lo + w:lo + w + tc, :]
        y = _silu(y)
        yq, yk, yv = y[:, :C_W], y[:, C_W:2 * C_W], y[:, 2 * C_W:]
        qn.append(yq * lax.rsqrt(head_sumsq(yq) + L2_EPS) * (D_C ** -0.5))
        kn.append(yk * lax.rsqrt(head_sumsq(yk) + L2_EPS))
        sm = small_ref[bb]
        log_a = jnp.where(is_dec, -jnp.exp(alog_ref[...]) * _softplus(sm + dtb_ref[...]), 0.0)
        g_full = sum(_dot(tri, part) for part in _split3(log_a))
        gc.append(sum(_dot(part, eg_ref[...]) for part in _split3(g_full)))
        bc.append(sum(_dot(part, eb_ref[...]) for part in _split3(_sigmoid(sm))))
        egc = jnp.exp(gc[bb])
        xk.append(bc[bb] * egc * kn[bb])
        vb.append(bc[bb] * yv)
        qe.append(egc * qn[bb])

    ii = lax.broadcasted_iota(jnp.int32, (CHUNK, LANES), 0)
    ln = lax.broadcasted_iota(jnp.int32, (CHUNK, LANES), 1)
    jn = ln % D_C
    incl2 = ii >= jn
    strict2 = ii > jn
    eye2 = ii == jn
    left = ln < D_C
    eye2f = jnp.where(eye2, 1.0, 0.0)

    def same_block(n):
        return (ii // n) == (jn // n)
    r128 = lax.broadcasted_iota(jnp.int32, (LANES, LANES), 0)
    c128 = lax.broadcasted_iota(jnp.int32, (LANES, LANES), 1)
    on_diag_blocks = (r128 < D_C) == (c128 < D_C)
    eye128 = jnp.where(r128 == c128, 1.0, 0.0).astype(BF16)

    keep_left = jnp.where(left, 1.0, 0.0).astype(BF16)
    keep_right = jnp.where(left, 0.0, 1.0).astype(BF16)

    def bdiag(x):
        xb = x.astype(BF16)
        return jnp.concatenate([xb * keep_left, xb * keep_right], axis=0)

    units = [(bb, r, p) for bb in range(n_bb) for r in range(n_ch) for p in range(n_pair)]
    lanes_bp = [(bb, p) for bb in range(n_bb) for p in range(n_pair)]

    def rows_of(r):
        return slice(r * CHUNK, (r + 1) * CHUNK)

    def tile_of(p):
        return slice(p * LANES, (p + 1) * LANES)

    def sub(x, u):
        bb, r, p = u
        return x[bb][rows_of(r), tile_of(p)]

    g_last = {(bb, r): gc[bb][(r + 1) * CHUNK - 1:(r + 1) * CHUNK, :] for bb in range(n_bb) for r in range(n_ch)}
    kd = {br: kn[br[0]][rows_of(br[1])] * jnp.exp(g - gc[br[0]][rows_of(br[1])]) for br, g in g_last.items()}
    eg_last = {br: jnp.exp(g) for br, g in g_last.items()}
    dec, r_k, kdt = {}, {}, {}
    for u in units:
        gcp = sub(gc, u)
        g_row = jnp.sum(jnp.where(eye2, gcp, 0.0), axis=0, keepdims=True)
        dec[u] = jnp.where(incl2, jnp.exp(jnp.where(incl2, gcp - g_row, 0.0)), 0.0)
        kp = sub(kn, u)
        r_k[u] = _dot_nt(jnp.concatenate([sub(qn, u), kp], axis=0).astype(BF16), bdiag(kp).astype(BF16))
    for bb, r, p in units:
        kdt[bb, r, p] = _dot_nt(eye128, bdiag(kd[bb, r][:, tile_of(p)]).astype(BF16)).astype(BF16)
    a_mat = {u: sub(bc, u) * jnp.where(strict2, dec[u], 0.0) * r_k[u][CHUNK:] for u in units}
    leaf = GDN_LEAF
    n_leaf = {u: jnp.where(strict2 & same_block(leaf), -a_mat[u], 0.0) for u in units}
    pw = {u: _dot(n_leaf[u].astype(BF16), bdiag(n_leaf[u])) for u in units}
    tm_ = {u: eye2f + n_leaf[u] for u in units}
    st = {u: _dot(jnp.concatenate([tm_[u], pw[u]], axis=0).astype(BF16), bdiag(pw[u])) for u in units}
    tm_ = {u: tm_[u] + st[u][:CHUNK] for u in units}
    fin = {u: _dot(tm_[u].astype(BF16), bdiag(st[u][CHUNK:])) for u in units}
    tm_ = {u: tm_[u] + fin[u] for u in units}
    size = leaf
    while size < CHUNK:
        coupling = strict2 & same_block(2 * size) & jnp.logical_not(same_block(size))
        cd = {u: _dot(jnp.where(coupling, a_mat[u], 0.0).astype(BF16), bdiag(tm_[u])) for u in units}
        dcd = {u: _dot(tm_[u].astype(BF16), bdiag(cd[u])) for u in units}
        tm_ = {u: tm_[u] - dcd[u] for u in units}
        size *= 2
    wu = {u: _dot(tm_[u].astype(BF16),
                  jnp.concatenate([bdiag(sub(xk, u)), bdiag(sub(vb, u))], axis=1).astype(BF16)) for u in units}
    w_b = {u: wu[u][:, :LANES].astype(BF16) for u in units}
    qkd = {u: (dec[u] * r_k[u][:CHUNK]).astype(BF16) for u in units}

    states = {bp: s_scr[bp[0], bp[1]] for bp in lanes_bp}
    o_tiles = {}
    for r in range(n_ch):
        res = {(bb, p): _dot(jnp.concatenate([sub(qe, (bb, r, p)).astype(BF16), w_b[bb, r, p]], axis=0),
                             states[bb, p].astype(BF16)) for bb, p in lanes_bp}
        u = {(bb, p): wu[bb, r, p][:, LANES:] - res[bb, p][CHUNK:] for bb, p in lanes_bp}
        upd = {(bb, p): _dot(kdt[bb, r, p], jnp.concatenate([u[bb, p], u[bb, p]], axis=0).astype(BF16))
               for bb, p in lanes_bp}
        o_inner = {(bb, p): _dot(qkd[bb, r, p], bdiag(u[bb, p]).astype(BF16)) for bb, p in lanes_bp}
        states = {(bb, p): eg_last[bb, r][:, tile_of(p)] * states[bb, p] + jnp.where(on_diag_blocks, upd[bb, p], 0.0)
                  for bb, p in lanes_bp}
        for bb, p in lanes_bp:
            o_tiles[bb, r, p] = res[bb, p][:CHUNK] + o_inner[bb, p]
    for bb, p in lanes_bp:
        s_scr[bb, p] = states[bb, p]
    for bb in range(n_bb):
        o = jnp.concatenate([jnp.concatenate([o_tiles[bb, r, p] for p in range(n_pair)], axis=1)
                             for r in range(n_ch)], axis=0)
        ms = head_sumsq(o) * (1.0 / D_C)
        oc_ref[bb] = o * lax.rsqrt(ms + NORM_EPS) * onorm_ref[...] * _silu(gate_ref[bb])

    @pl.when(j == pl.num_programs(1) - 1)
    def _():
        for bb, p in lanes_bp:
            sout_ref[bb, 2 * p] = states[bb, p][:D_C, :D_C]
            sout_ref[bb, 2 * p + 1] = states[bb, p][D_C:, D_C:]


def _gdn_constants(tc):
    i = jnp.arange(tc, dtype=jnp.int32)
    tri = ((i[None, :] <= i[:, None]) & (i[None, :] // CHUNK == i[:, None] // CHUNK)).astype(BF16)
    l = jnp.arange(LANES, dtype=jnp.int32)
    block_ones = (l[:, None] // D_C == l[None, :] // D_C).astype(BF16)
    col_head = jnp.arange(C_W, dtype=jnp.int32)[None, :] // D_C
    e_beta = (l[:, None] == col_head).astype(BF16)
    e_g = (l[:, None] == col_head + _SMALL_DECAY0).astype(BF16)
    return tri, block_ones, e_beta, e_g


def _gdn(cqkv, small, gate, conv_prev, s0, conv_w, alog_vec, dtb_vec, onorm_tiled):
    b, t, w3 = cqkv.shape
    tc = min(4 * CHUNK, t)
    nb = GDN_BATCH_BLOCK if b % GDN_BATCH_BLOCK == 0 else 1
    consts = _gdn_constants(tc)
    body = functools.partial(_gdn_body, tc=tc)
    blk = lambda bi, j: (bi, j, 0)
    per_b3 = lambda bi, j: (bi, 0, 0)
    per_b4 = lambda bi, j: (bi, 0, 0, 0)
    const2 = lambda bi, j: (0, 0)
    return pl.pallas_call(
        body,
        grid=(b // nb, t // tc),
        in_specs=[pl.BlockSpec((nb, tc, w3), blk),
                  pl.BlockSpec((nb, tc, LANES), blk),
                  pl.BlockSpec((nb, tc, C_W), blk),
                  pl.BlockSpec((nb, CONV_W - 1, w3), per_b3),
                  pl.BlockSpec((nb, H_C, D_C, D_C), per_b4),
                  pl.BlockSpec(conv_w.shape, const2),
                  pl.BlockSpec((1, LANES), const2),
                  pl.BlockSpec((1, LANES), const2),
                  pl.BlockSpec((1, C_W), const2)]
                 + [pl.BlockSpec(c.shape, const2) for c in consts],
        out_specs=[pl.BlockSpec((nb, tc, C_W), blk),
                   pl.BlockSpec((nb, H_C, D_C, D_C), per_b4)],
        out_shape=[jax.ShapeDtypeStruct((b, t, C_W), F32),
                   jax.ShapeDtypeStruct((b, H_C, D_C, D_C), F32)],
        scratch_shapes=[pltpu.VMEM((nb, tc + 8, w3), F32), pltpu.VMEM((nb, H_C // 2, LANES, LANES), F32)],
        compiler_params=_params(("parallel", "arbitrary")),
        name="gdn",
    )(cqkv, small, gate, conv_prev, s0, conv_w, alog_vec, dtb_vec, onorm_tiled, *consts)


def _top2_gates(logits):
    lane = lax.broadcasted_iota(jnp.int32, logits.shape, 1).astype(F32)
    low = -3.0e38
    lg = jnp.where(lane < N_EXPERTS, logits, low)
    m1 = jnp.max(lg, axis=-1, keepdims=True)
    i1 = jnp.min(jnp.where(lg == m1, lane, float(LANES)), axis=-1, keepdims=True)
    lg2 = jnp.where(lane == i1, low, lg)
    m2 = jnp.max(lg2, axis=-1, keepdims=True)
    i2 = jnp.min(jnp.where(lg2 == m2, lane, float(LANES)), axis=-1, keepdims=True)
    e2 = jnp.exp(m2 - m1)
    den = 1.0 + e2
    return jnp.where(lane == i1, 1.0 / den, 0.0) + jnp.where(lane == i2, e2 / den, 0.0)


def _outproj_body(x_ref, oa_ref, ob_ref, oc_ref, w_ref, g_ref, *rest, with_router):
    if with_router:
        rhi_ref, rlo_ref, xo_ref, h_ref, gates_ref = rest
    else:
        xo_ref, h_ref = rest
    y = (_dot(oa_ref[...].astype(BF16), w_ref[0:HEAD_W, :])
         + _dot(ob_ref[...].astype(BF16), w_ref[HEAD_W:2 * HEAD_W, :])
         + _dot(oc_ref[...].astype(BF16), w_ref[2 * HEAD_W:, :]))
    x = x_ref[...] + y
    xo_ref[...] = x
    hf = _rms(x, g_ref[...])
    hb = hf.astype(BF16)
    h_ref[...] = hb
    if with_router:
        lo = (hf - hb.astype(F32)).astype(BF16)
        logits = _dot(hb, rhi_ref[...]) + _dot(lo, rhi_ref[...]) + _dot(hb, rlo_ref[...])
        gates_ref[...] = _top2_gates(logits)


def _outproj(x, oa, ob, oc, w_out, g, router=None):
    n, d = x.shape
    tm = min(ROW_TILE, n)
    row = lambda i: (i, 0)
    const = lambda i: (0, 0)
    in_specs = [pl.BlockSpec((tm, d), row), pl.BlockSpec((tm, HEAD_W), row), pl.BlockSpec((tm, HEAD_W), row),
                pl.BlockSpec((tm, C_W), row), pl.BlockSpec(w_out.shape, const), pl.BlockSpec((1, d), const)]
    out_specs = [pl.BlockSpec((tm, d), row), pl.BlockSpec((tm, d), row)]
    out_shape = [jax.ShapeDtypeStruct((n, d), F32), jax.ShapeDtypeStruct((n, d), BF16)]
    args = [x, oa, ob, oc, w_out, g]
    if router is not None:
        in_specs += [pl.BlockSpec(router[0].shape, const), pl.BlockSpec(router[1].shape, const)]
        out_specs.append(pl.BlockSpec((tm, LANES), row))
        out_shape.append(jax.ShapeDtypeStruct((n, LANES), F32))
        args += list(router)
    return pl.pallas_call(
        functools.partial(_outproj_body, with_router=router is not None),
        grid=(n // tm,),
        in_specs=in_specs, out_specs=out_specs, out_shape=out_shape,
        compiler_params=_params(("parallel",)),
        name="outproj",
    )(*args)


def _ffn_body(h_ref, x_ref, wg_ref, wu_ref, wd_ref, o_ref, acc_ref, *, ff_chunk):
    h = h_ref[...]
    d_ff = wg_ref.shape[1]
    acc_ref[...] = x_ref[...]
    for c0 in range(0, d_ff, ff_chunk):
        a = _silu(_dot(h, wg_ref[:, c0:c0 + ff_chunk])) * _dot(h, wu_ref[:, c0:c0 + ff_chunk])
        acc_ref[...] += _dot(a.astype(BF16), wd_ref[c0:c0 + ff_chunk, :])
    o_ref[...] = acc_ref[...]


def _ffn_dense(h, x, wg, wu, wd):
    n, d = x.shape
    d_ff = wg.shape[1]
    tm = min(ROW_TILE, n)
    ff_chunk = 256 if d_ff % 256 == 0 else LANES
    row = lambda i: (i, 0)
    const = lambda i: (0, 0)
    return pl.pallas_call(
        functools.partial(_ffn_body, ff_chunk=ff_chunk),
        grid=(n // tm,),
        in_specs=[pl.BlockSpec((tm, d), row), pl.BlockSpec((tm, d), row),
                  pl.BlockSpec(wg.shape, const), pl.BlockSpec(wu.shape, const), pl.BlockSpec(wd.shape, const)],
        out_specs=pl.BlockSpec((tm, d), row),
        out_shape=jax.ShapeDtypeStruct((n, d), F32),
        scratch_shapes=[pltpu.VMEM((tm, d), F32)],
        compiler_params=_params(("parallel",)),
        name="ffn_dense",
    )(h, x, wg, wu, wd)


def _moe_body(pre_ref, h_ref, gates_ref, gates_t_ref, wg_ref, wu_ref, wd_ref, *rest, tg, n_f, final_norm, tiles):
    if final_norm:
        x_ref, gfin_ref, y_ref, rank_col, rank_row, hs, eo = rest
    else:
        y_ref, rank_col, rank_row, hs, eo = rest
    g = pl.program_id(0)
    e = pl.program_id(1)
    f = pl.program_id(2)
    tb = MOE_TOKEN_BLOCK
    n_tb = tg // tb
    pre0 = (g * N_EXPERTS + e) * (n_tb + 1)
    unit = tiles[-1]
    n_units = (pre_ref[pre0 + n_tb] + unit - 1) // unit
    units_full = tiles[0] // unit
    n_full = n_units // units_full
    rem = n_units % units_full

    def for_tiles(fn):
        def full(t, carry):
            fn(pl.multiple_of(t * tiles[0], tiles[0]), tiles[0])
            return carry

        lax.fori_loop(0, n_full, full, 0)
        for tm in tiles[1:]:
            bit = tm // unit
            higher = (units_full - 1) & ~(2 * bit - 1)

            @pl.when((rem & bit) != 0)
            def _(tm=tm, higher=higher):
                fn(pl.multiple_of(n_full * tiles[0] + (rem & higher) * unit, unit), tm)

    @pl.when((e == 0) & (f == 0))
    def _():
        y_ref[...] = jnp.zeros_like(y_ref)
        ii = lax.broadcasted_iota(jnp.int32, (tb, tb), 0)
        jj = lax.broadcasted_iota(jnp.int32, (tb, tb), 1)
        lower = jnp.where(ii > jj, 1.0, 0.0).astype(BF16)
        upper = jnp.where(ii < jj, 1.0, 0.0).astype(BF16)
        carry_c = jnp.zeros((1, LANES), F32)
        carry_r = jnp.zeros((N_EXPERTS, 1), F32)
        for b in range(n_tb):
            mc = jnp.where(gates_ref[b * tb:(b + 1) * tb, :] > 0.0, 1.0, 0.0)
            rank_col[b * tb:(b + 1) * tb, :] = _dot(lower, mc.astype(BF16)) + carry_c
            carry_c = carry_c + jnp.sum(mc, axis=0, keepdims=True)
            mr = jnp.where(gates_t_ref[:, b * tb:(b + 1) * tb] > 0.0, 1.0, 0.0)
            rank_row[:, b * tb:(b + 1) * tb] = _dot(mr.astype(BF16), upper) + carry_r
            carry_r = carry_r + jnp.sum(mr, axis=1, keepdims=True)

    win = MOE_WINDOW

    def all_block_windows(align, product, apply):
        base = [(pre_ref[pre0 + b] // align) * align for b in range(n_tb)]
        n_win = [(pre_ref[pre0 + b + 1] - base[b] + win - 1) // win for b in range(n_tb)]
        first = [product(b, pl.multiple_of(base[b], align)) for b in range(n_tb)]
        for b in range(n_tb):
            apply(b, pl.multiple_of(base[b], align), first[b])
        for b in range(n_tb):
            def body(k, carry, b=b):
                r0 = pl.multiple_of(base[b] + k * win, align)
                apply(b, r0, product(b, r0))
                return carry

            lax.fori_loop(1, n_win[b], body, 0)

    @pl.when(f == 0)
    def _():
        def clear(k, carry):
            r0 = pl.multiple_of(k * unit, unit)
            hs[pl.ds(r0, unit), :] = jnp.zeros((unit, hs.shape[1]), BF16)
            return carry

        lax.fori_loop(0, n_units + MOE_SLACK // unit, clear, 0)
        tail = pl.multiple_of(n_units * unit, unit)
        eo[pl.ds(tail, MOE_SLACK), :] = jnp.zeros((MOE_SLACK, eo.shape[1]), F32)

        def gathered(b, r0):
            cols = slice(b * tb, (b + 1) * tb)
            gr = gates_t_ref[pl.ds(e, 1), cols]
            rr = jnp.where(gr > 0.0, rank_row[pl.ds(e, 1), cols], -1.0)
            rows = (r0 + lax.broadcasted_iota(jnp.int32, (win, tb), 0)).astype(F32)
            return _dot(jnp.where(rr == rows, 1.0, 0.0).astype(BF16), h_ref[cols, :])

        def add_rows(b, r0, got):
            hs[pl.ds(r0, win), :] = (hs[pl.ds(r0, win), :].astype(F32) + got).astype(BF16)

        all_block_windows(MOE_BF16_ROWS, gathered, add_rows)

    def expert(r0, tm):
        x = hs[pl.ds(r0, tm), :]
        a = _silu(_dot(x, wg_ref[...])) * _dot(x, wu_ref[...])
        part = _dot(a.astype(BF16), wd_ref[...])

        @pl.when(f == 0)
        def _():
            eo[pl.ds(r0, tm), :] = part

        @pl.when(f > 0)
        def _():
            eo[pl.ds(r0, tm), :] += part

    for_tiles(expert)

    @pl.when(f == n_f - 1)
    def _():
        lane = lax.broadcasted_iota(jnp.int32, (tb, LANES), 1)

        def scattered(b, r0):
            rows = slice(b * tb, (b + 1) * tb)
            gc = jnp.sum(jnp.where(lane == e, gates_ref[rows, :], 0.0), axis=-1, keepdims=True)
            rc = jnp.sum(jnp.where(lane == e, rank_col[rows, :], 0.0), axis=-1, keepdims=True)
            rc = jnp.where(gc > 0.0, rc, -1.0)
            cols = (r0 + lax.broadcasted_iota(jnp.int32, (tb, win), 1)).astype(F32)
            return _dot(jnp.where(rc == cols, gc, 0.0).astype(BF16), eo[pl.ds(r0, win), :].astype(BF16))

        def add_tokens(b, r0, got):
            y_ref[b * tb:(b + 1) * tb, :] += got

        all_block_windows(MOE_F32_ROWS, scattered, add_tokens)

    if final_norm:
        @pl.when((e == pl.num_programs(1) - 1) & (f == n_f - 1))
        def _():
            for b in range(n_tb):
                rows = slice(b * tb, (b + 1) * tb)
                y_ref[rows, :] = _rms(x_ref[rows, :] + y_ref[rows, :], gfin_ref[...])


def _moe_prefix(gates, tg):
    n = gates.shape[0]
    n_tb = tg // MOE_TOKEN_BLOCK
    routed = (gates[:, :N_EXPERTS] > 0.0).astype(jnp.int32)
    per_block = routed.reshape(n // tg, n_tb, MOE_TOKEN_BLOCK, N_EXPERTS).sum(axis=2)
    run = jnp.cumsum(per_block, axis=1)
    pre = jnp.concatenate([jnp.zeros_like(run[:, :1]), run], axis=1)
    return jnp.transpose(pre, (0, 2, 1)).reshape(-1)


def _moe(h, gates, wg, wu, wd, tg, x=None, g_final=None):
    n, d = h.shape
    n_e, _, d_ff = wg.shape
    n_f = 2
    ffh = d_ff // n_f
    final_norm = x is not None
    tiles = tuple(t for t in MOE_TILES if t <= tg)
    body = functools.partial(_moe_body, tg=tg, n_f=n_f, final_norm=final_norm, tiles=tiles)
    once = pl.Buffered(1)
    extra_specs, extra_args = [], []
    if final_norm:
        extra_specs = [pl.BlockSpec((tg, d), lambda g, e, f, c: (g, 0), pipeline_mode=once),
                       pl.BlockSpec((1, d), lambda g, e, f, c: (0, 0))]
        extra_args = [x, g_final]
    grid_spec = pltpu.PrefetchScalarGridSpec(
        num_scalar_prefetch=1,
        grid=(n // tg, n_e, n_f),
        in_specs=[pl.BlockSpec((tg, d), lambda g, e, f, c: (g, 0), pipeline_mode=once),
                  pl.BlockSpec((tg, LANES), lambda g, e, f, c: (g, 0), pipeline_mode=once),
                  pl.BlockSpec((N_EXPERTS, tg), lambda g, e, f, c: (0, g), pipeline_mode=once),
                  pl.BlockSpec((None, d, ffh), lambda g, e, f, c: (e, 0, f)),
                  pl.BlockSpec((None, d, ffh), lambda g, e, f, c: (e, 0, f)),
                  pl.BlockSpec((None, ffh, d), lambda g, e, f, c: (e, f, 0))] + extra_specs,
        out_specs=pl.BlockSpec((tg, d), lambda g, e, f, c: (g, 0), pipeline_mode=once),
        scratch_shapes=[pltpu.VMEM((tg, LANES), F32), pltpu.VMEM((N_EXPERTS, tg), F32),
                        pltpu.VMEM((tg + MOE_SLACK, d), BF16), pltpu.VMEM((tg + MOE_SLACK, d), F32)],
    )
    return pl.pallas_call(
        body,
        grid_spec=grid_spec,
        out_shape=jax.ShapeDtypeStruct((n, d), F32),
        compiler_params=_params(("parallel", "arbitrary", "arbitrary")),
        name="moe",
    )(_moe_prefix(gates, tg), h, gates, gates[:, :N_EXPERTS].T, wg, wu, wd, *extra_args)


def _final_body(x_ref, y_ref, g_ref, o_ref):
    o_ref[...] = _rms(x_ref[...] + y_ref[...], g_ref[...])


def _final_norm(x, y, g):
    n, d = x.shape
    tm = min(ROW_TILE, n)
    row = lambda i: (i, 0)
    return pl.pallas_call(
        _final_body,
        grid=(n // tm,),
        in_specs=[pl.BlockSpec((tm, d), row), pl.BlockSpec((tm, d), row), pl.BlockSpec((1, d), lambda i: (0, 0))],
        out_specs=pl.BlockSpec((tm, d), row),
        out_shape=jax.ShapeDtypeStruct((n, d), F32),
        compiler_params=_params(("parallel",)),
        name="final_norm",
    )(x, y, g)


def _t5_bucket(rel):
    nb = T5_BUCKETS // 2
    max_exact = nb // 2
    ret = jnp.where(rel > 0, nb, 0)
    n = jnp.abs(rel)
    nf = jnp.maximum(n, 1).astype(F32)
    large = max_exact + (jnp.log(nf / max_exact) / math.log(T5_MAX_DIST / max_exact) * (nb - max_exact)).astype(jnp.int32)
    large = jnp.minimum(large, nb - 1)
    return ret + jnp.where(n < max_exact, n, large)


def _t5_table(t5_bias, q_pos, k_pos):
    bias = _lookup(t5_bias, _t5_bucket(k_pos[None, :] - q_pos[:, None]))
    mask = (k_pos[None, :] // CHUNK) <= (q_pos[:, None] // CHUNK)
    return bias, mask[None]


def _lookup(table, idx):
    onehot = jax.nn.one_hot(idx, table.shape[0], dtype=F32)
    return jnp.einsum("qkn,nh->hqk", onehot, table.astype(F32), precision=lax.Precision.HIGHEST)


def _band_table(rel_bias):
    qi = jnp.arange(CHUNK, dtype=jnp.int32)
    kj = jnp.arange(BAND, dtype=jnp.int32) - B_LEFT_CHUNKS * CHUNK
    rel = jnp.clip(kj[None, :] - qi[:, None], -B_REL_CLIP, B_REL_CLIP) + B_REL_CLIP
    base = jnp.swapaxes(_lookup(rel_bias, rel), 1, 2) * LOG2E
    halves = [jnp.pad(base, ((0, 0), (c * CHUNK, (1 - c) * CHUNK), (0, 0)), constant_values=NEG_INF)
              for c in range(2)]
    return jnp.concatenate(halves, axis=2)


def _lane_vec(v, offset):
    return jnp.zeros((1, LANES), F32).at[0, offset:offset + v.shape[0]].set(v.astype(F32))


def _moe_group(n):
    for tg in (2048, 1024, 512, 256):
        if n % tg == 0:
            return tg
    raise ValueError(f"token count {n} is not a multiple of 256")


def kernel(x_prompt, x_sample, cache_a_k, cache_a_v, cache_b_k, cache_b_v, cache_c_conv, state_c_ssm, w_in, w_out, norm_mix, norm_ffn, norm_final, lam_qk, subln_a, t5_bias, rel_bias_b, conv_c, a_log_c, dt_bias_c, onorm_c, ffn_gate, ffn_up, ffn_down, moe_router, moe_gate, moe_up, moe_down):
    depth = w_in.shape[0]
    bp, sp, d = x_prompt.shape
    bs, ss, _ = x_sample.shape
    past = cache_a_k.shape[2]
    nb_cache = cache_b_k.shape[2]
    assert ss == CHUNK and nb_cache == B_LEFT_CHUNKS * CHUNK and sp >= B_LEFT_CHUNKS * CHUNK

    xp = x_prompt.reshape(bp * sp, d)
    xs = x_sample.reshape(bs * ss, d)
    blk = min(A_BLOCK, sp)

    pos_blk = jnp.arange(blk, dtype=jnp.int32)
    far_bias = t5_bias[_t5_bucket(jnp.int32(-(blk + 1)))].astype(F32)
    bd, md = _t5_table(t5_bias, pos_blk, pos_blk)
    bias_diag = jnp.where(md, bd - far_bias[:, None, None], NEG_INF)
    bsub, _ = _t5_table(t5_bias, blk + pos_blk, pos_blk)
    bias_sub = bsub - far_bias[:, None, None]
    q_pos_s = past + jnp.arange(ss, dtype=jnp.int32)
    bfull, mfull = _t5_table(t5_bias, q_pos_s, jnp.arange(past + ss, dtype=jnp.int32))
    bias_s = jnp.where(mfull, bfull, NEG_INF)
    bias_s_t = jnp.swapaxes(bias_s, 1, 2) * LOG2E
    bias_s_cache = bias_s_t[:, :past]
    bias_s_new = jnp.pad(bias_s_t[:, past:], ((0, 0), (0, LANES - ss), (0, 0)), constant_values=NEG_INF)

    zeros_conv = jnp.zeros((bp, CONV_W - 1, 3 * C_W), F32)
    zeros_state = jnp.zeros((bp, H_C, D_C, D_C), F32)

    p_states = [[] for _ in range(6)]
    s_states = [[] for _ in range(6)]
    yp = ys = None
    g_fin = norm_final.reshape(1, d)
    finals = []
    for l in range(depth):
        if yp is not None:
            xp, xs, yp, ys = xp + yp, xs + ys, None, None
        w = w_in[l]
        n_main = 6 * HEAD_W + 3 * C_W
        w_main = jnp.concatenate([w[:, :n_main], w[:, n_main + 2 * H_C:]], axis=1).astype(BF16)
        w_small = jnp.zeros((d, LANES), F32).at[:, :2 * H_C].set(w[:, n_main:n_main + 2 * H_C]).astype(BF16)
        w_out_l = w_out[l].astype(BF16)
        g_mix = norm_mix[l].reshape(1, d)
        g_ffn = norm_ffn[l].reshape(1, d)
        lam_init = 0.8 - 0.6 * math.exp(-0.3 * l)
        lq = lam_qk[l].astype(F32)
        lam = (jnp.exp(jnp.sum(lq[0] * lq[1])) - jnp.exp(jnp.sum(lq[2] * lq[3])) + lam_init).reshape(1)
        subln = subln_a[l].reshape(1, DV_A)
        band_bias = _band_table(rel_bias_b[l])
        alog_vec = _lane_vec(a_log_c[l], _SMALL_DECAY0)
        dtb_vec = _lane_vec(dt_bias_c[l], _SMALL_DECAY0)
        onorm = jnp.tile(onorm_c[l].astype(F32), H_C).reshape(1, C_W)
        is_moe = l % 2 == 1
        if is_moe:
            r = jnp.zeros((d, LANES), F32).at[:, :N_EXPERTS].set(moe_router[l // 2])
            r_hi = r.astype(BF16)
            router = (r_hi, (r - r_hi.astype(F32)).astype(BF16))
            e_wg, e_wu, e_wd = (moe_gate[l // 2].astype(BF16), moe_up[l // 2].astype(BF16),
                                moe_down[l // 2].astype(BF16))
        else:
            router = None
            f_wg, f_wu, f_wd = (ffn_gate[l // 2].astype(BF16), ffn_up[l // 2].astype(BF16),
                                ffn_down[l // 2].astype(BF16))

        new_x = []
        for is_prompt, x in ((True, xp), (False, xs)):
            b, t = (bp, sp) if is_prompt else (bs, ss)
            aq, ak, av, bq, bk, bv, cqkv, cgate, csmall = _inproj(x, g_mix, w_main, w_small)
            r3 = lambda a: a.reshape(b, t, a.shape[-1])
            if is_prompt:
                oa = _attn_a_prompt(lam, r3(aq), r3(ak), r3(av), jnp.swapaxes(bias_diag, 1, 2) * LOG2E,
                                    jnp.swapaxes(bias_sub, 1, 2) * LOG2E, subln.reshape(DV_A, 1), 1.0 - lam_init)
                ob = _band_attn(r3(bq), r3(bk), r3(bv), band_bias, B_LEFT_CHUNKS * CHUNK)
                conv_prev, s0 = zeros_conv, zeros_state
            else:
                oa = _attn_a_sample(lam, r3(aq), cache_a_k[l].reshape(b, past, HEAD_W),
                                    cache_a_v[l].reshape(b, past, HEAD_W), r3(ak), r3(av),
                                    bias_s_cache, bias_s_new, subln.reshape(DV_A, 1), 1.0 - lam_init)
                kb = jnp.concatenate([cache_b_k[l].reshape(b, nb_cache, HEAD_W), r3(bk)], axis=1)
                vb = jnp.concatenate([cache_b_v[l].reshape(b, nb_cache, HEAD_W), r3(bv)], axis=1)
                ob = _band_attn(r3(bq), kb, vb, band_bias, 0)
                conv_prev, s0 = cache_c_conv[l], state_c_ssm[l]
            oc, s_new = _gdn(r3(cqkv), r3(csmall), r3(cgate), conv_prev, s0, conv_c[l], alog_vec, dtb_vec, onorm)
            n = b * t
            res = _outproj(x, oa.reshape(n, HEAD_W), ob.reshape(n, HEAD_W), oc.reshape(n, C_W), w_out_l, g_ffn,
                           router)
            if is_moe:
                x_new, h2, gates = res
                if l == depth - 1:
                    finals.append(_moe(h2, gates, e_wg, e_wu, e_wd, _moe_group(n), x=x_new, g_final=g_fin))
                    new_x.append((x_new, None))
                else:
                    new_x.append((x_new, _moe(h2, gates, e_wg, e_wu, e_wd, _moe_group(n))))
            else:
                x_new, h2 = res
                new_x.append((_ffn_dense(h2, x_new, f_wg, f_wu, f_wd), None))
            keep = min(B_LEFT_CHUNKS * CHUNK, t)
            states = (ak.reshape(b, t, H_A, 2 * DQK_A), av.reshape(b, t, H_A, DV_A),
                      bk.reshape(b, t, H_B, D_B)[:, t - keep:], bv.reshape(b, t, H_B, D_B)[:, t - keep:],
                      r3(cqkv)[:, t - (CONV_W - 1):], s_new)
            for i in range(6):
                (p_states if is_prompt else s_states)[i].append(states[i])
        (xp, yp), (xs, ys) = new_x

    if finals:
        y_prompt, y_sample = finals[0].reshape(bp, sp, d), finals[1].reshape(bs, ss, d)
    else:
        if yp is None:
            yp, ys = jnp.zeros_like(xp), jnp.zeros_like(xs)
        y_prompt = _final_norm(xp, yp, g_fin).reshape(bp, sp, d)
        y_sample = _final_norm(xs, ys, g_fin).reshape(bs, ss, d)
    p_out = [jnp.stack(s, axis=0) for s in p_states]
    s_out = [jnp.stack(s, axis=0) for s in s_states]
    return (y_prompt, y_sample, *p_out, *s_out)
```

```python
import functools
import math

import jax
import jax.numpy as jnp
from jax import lax
from jax.experimental import pallas as pl
from jax.experimental.pallas import tpu as pltpu

F32 = jnp.float32
BF16 = jnp.bfloat16

CHUNK = 64
H_A = 4
DQK_A = 32
DV_A = 64
H_B = 4
D_B = 64
B_LEFT_CHUNKS = 8
B_REL_CLIP = 128
H_C = 8
D_C = 64
CONV_W = 4
T5_BUCKETS = 32
T5_MAX_DIST = 128
N_EXPERTS = 8
NORM_EPS = 1e-6
L2_EPS = 1e-6
NEG_INF = -1e30
HEAD_W = 256
C_W = H_C * D_C
BAND = (B_LEFT_CHUNKS + 1) * CHUNK
LANES = 128
LOG2E = math.log2(math.e)
VMEM_LIMIT = 56 * 1024 * 1024

A_BLOCK = 256
ROW_TILE = 512
GDN_BATCH_BLOCK = 2
GDN_LEAF = 8
MOE_TILES = (512, 256, 128, 64)
MOE_TOKEN_BLOCK = 256
MOE_WINDOW = 96
MOE_SLACK = 128
MOE_F32_ROWS = 8
MOE_BF16_ROWS = 16


def _params(sem):
    return pltpu.CompilerParams(dimension_semantics=sem, vmem_limit_bytes=VMEM_LIMIT)


def _rms(x, g):
    ms = jnp.mean(x * x, axis=-1, keepdims=True)
    return x * lax.rsqrt(ms + NORM_EPS) * g


def _sigmoid(x):
    return 1.0 / (1.0 + jnp.exp(-x))


def _silu(x):
    return x * _sigmoid(x)


def _softplus(x):
    return jnp.maximum(x, 0.0) + jnp.log1p(jnp.exp(-jnp.abs(x)))


def _dot(a, b):
    return jnp.dot(a, b, preferred_element_type=F32)


def _dot_nt(a, b):
    return lax.dot_general(a, b, (((1,), (1,)), ((), ())), preferred_element_type=F32)


def _dot_tn(a, b):
    return lax.dot_general(a, b, (((0,), (0,)), ((), ())), preferred_element_type=F32)


def _split3(x):
    x1 = x.astype(BF16)
    r1 = x - x1.astype(F32)
    x2 = r1.astype(BF16)
    x3 = (r1 - x2.astype(F32)).astype(BF16)
    return x1, x2, x3


def _inproj_body(x_ref, g_ref, w_ref, ws_ref, aq_ref, ak_ref, av_ref, bq_ref, bk_ref, bv_ref,
                 cqkv_ref, cgate_ref, csmall_ref):
    h = _rms(x_ref[...], g_ref[...]).astype(BF16)
    col = 0
    for ref in (aq_ref, ak_ref, av_ref, bq_ref, bk_ref, bv_ref, cqkv_ref, cgate_ref):
        width = ref.shape[-1]
        ref[...] = _dot(h, w_ref[:, col:col + width])
        col += width
    csmall_ref[...] = _dot(h, ws_ref[...])


def _inproj(x, g, w_main, w_small):
    n, d = x.shape
    tm = min(ROW_TILE, n)
    widths = (HEAD_W,) * 6 + (3 * C_W, C_W, LANES)
    row = lambda i: (i, 0)
    const = lambda i: (0, 0)
    return pl.pallas_call(
        _inproj_body,
        grid=(n // tm,),
        in_specs=[pl.BlockSpec((tm, d), row), pl.BlockSpec((1, d), const),
                  pl.BlockSpec(w_main.shape, const), pl.BlockSpec(w_small.shape, const)],
        out_specs=[pl.BlockSpec((tm, w), row) for w in widths],
        out_shape=[jax.ShapeDtypeStruct((n, w), F32) for w in widths],
        compiler_params=_params(("parallel",)),
        name="inproj",
    )(x, g, w_main, w_small)


A_ACC_ROWS = DV_A + 16
A_GROUPS = LANES // DQK_A


def _attn_a_prompt_body(lam_ref, q_ref, k_ref, v_ref, bd_ref, bs_ref, subln_ref, o_ref,
                        kbf, vte, wq, m_scr, acc_scr, s_even, s_odd, *, blk, post_scale):
    i = pl.program_id(1)
    n_blk = k_ref.shape[0] // blk
    n_pair = H_A // 2

    @pl.when(i == 0)
    def _():
        ones_rows = jnp.where(lax.broadcasted_iota(jnp.int32, (A_ACC_ROWS - DV_A, blk), 0) == 0, 1.0, 0.0)
        for jb in range(n_blk):
            kbf[jb] = k_ref[jb * blk:(jb + 1) * blk, :].astype(BF16)
            vt = v_ref[jb * blk:(jb + 1) * blk, :].T
            for h in range(H_A):
                vte[jb, h, 0:DV_A, :] = vt[h * DV_A:(h + 1) * DV_A, :].astype(BF16)
                vte[jb, h, DV_A:, :] = ones_rows.astype(BF16)

    lam = lam_ref[0]
    qt = (q_ref[...] * (DQK_A ** -0.5 * LOG2E)).T
    grp = lax.broadcasted_iota(jnp.int32, (LANES, blk), 0) // DQK_A
    for p in range(n_pair):
        qtp = qt[p * LANES:(p + 1) * LANES, :]
        wq[p] = jnp.concatenate([jnp.where(grp == g, qtp, 0.0) for g in range(A_GROUPS)], axis=1).astype(BF16)
    m_scr[...] = jnp.full(m_scr.shape, NEG_INF, F32)
    acc_scr[...] = jnp.zeros(acc_scr.shape, F32)

    def scores(jb):
        return [_dot(kbf[jb, :, p * LANES:(p + 1) * LANES], wq[p]) for p in range(n_pair)]

    groups = range(n_pair * A_GROUPS)

    def softmax_stage(tile_of, bias_ref, off):
        alphas, pts = [], []
        for g in groups:
            p, gi = divmod(g, A_GROUPS)
            s = tile_of(p, gi)
            if bias_ref is not None:
                s = s + bias_ref[2 * p + gi // 2]
            if off is not None:
                s = s + off
            m_old = m_scr[g:g + 1, :]
            m_new = jnp.maximum(m_old, jnp.max(s, axis=0, keepdims=True))
            alphas.append(jnp.exp2(m_old - m_new))
            pts.append(jnp.exp2(s - m_new).astype(BF16))
            m_scr[g:g + 1, :] = m_new
        return alphas, pts

    def value_stage(jb, alphas, pts):
        pvs = [_dot(vte[jb, 2 * (g // A_GROUPS) + (g % A_GROUPS) // 2], pts[g]) for g in groups]
        for g in groups:
            acc_scr[g] = alphas[g] * acc_scr[g] + pvs[g]

    def consume(tile_of, jb, bias_ref, off):
        value_stage(jb, *softmax_stage(tile_of, bias_ref, off))

    def tiles(jb, bias_ref, off):
        s_alls = scores(jb)
        consume(lambda p, gi: s_alls[p][:, gi * blk:(gi + 1) * blk], jb, bias_ref, off)

    tiles(i, bd_ref, None)
    tiles(jnp.maximum(i - 1, 0), bs_ref, jnp.where(i >= 1, 0.0, NEG_INF))

    n_far = jnp.maximum(i - 1, 0)

    def put_scores(dst, jb):
        s_alls = scores(jnp.minimum(jb, n_far - 1))
        for p in range(n_pair):
            dst[p] = s_alls[p]

    def from_scratch(src):
        return lambda p, gi: src[p, :, gi * blk:(gi + 1) * blk]

    @pl.when(n_far > 0)
    def _():
        put_scores(s_even, 0)
        put_scores(s_odd, 1)

    def far_pair(t, carry):
        jb = 2 * t
        soft_a = softmax_stage(from_scratch(s_even), None, None)
        put_scores(s_even, jb + 2)
        soft_b = softmax_stage(from_scratch(s_odd), None, None)
        value_stage(jb, *soft_a)
        put_scores(s_odd, jb + 3)
        value_stage(jb + 1, *soft_b)
        return carry

    lax.fori_loop(0, n_far // 2, far_pair, 0)

    @pl.when(n_far % 2 == 1)
    def _():
        consume(from_scratch(s_even), n_far - 1, None, None)

    outs = []
    for h in range(H_A):
        g1 = (h // 2) * A_GROUPS + (h % 2) * 2
        a1 = acc_scr[g1]
        a2 = acc_scr[g1 + 1]
        o = a1[:DV_A] / a1[DV_A:DV_A + 1] - lam * (a2[:DV_A] / a2[DV_A:DV_A + 1])
        ms = jnp.mean(o * o, axis=0, keepdims=True)
        outs.append(o * lax.rsqrt(ms + NORM_EPS) * subln_ref[...] * post_scale)
    o_ref[...] = jnp.concatenate(outs, axis=0).T


def _attn_a_prompt(lam, q, k, v, bias_diag_t, bias_sub_t, subln_col, post_scale):
    b, s, w = q.shape
    blk = min(A_BLOCK, s)
    n_blk = s // blk
    body = functools.partial(_attn_a_prompt_body, blk=blk, post_scale=post_scale)
    return pl.pallas_call(
        body,
        grid=(b, n_blk),
        in_specs=[pl.BlockSpec(memory_space=pltpu.SMEM),
                  pl.BlockSpec((None, blk, w), lambda bi, i: (bi, i, 0)),
                  pl.BlockSpec((None, s, w), lambda bi, i: (bi, 0, 0)),
                  pl.BlockSpec((None, s, w), lambda bi, i: (bi, 0, 0)),
                  pl.BlockSpec(bias_diag_t.shape, lambda bi, i: (0, 0, 0)),
                  pl.BlockSpec(bias_sub_t.shape, lambda bi, i: (0, 0, 0)),
                  pl.BlockSpec((DV_A, 1), lambda bi, i: (0, 0))],
        out_specs=pl.BlockSpec((None, blk, w), lambda bi, i: (bi, i, 0)),
        out_shape=jax.ShapeDtypeStruct((b, s, w), F32),
        scratch_shapes=[pltpu.VMEM((n_blk, blk, w), BF16),
                        pltpu.VMEM((n_blk, H_A, A_ACC_ROWS, blk), BF16),
                        pltpu.VMEM((H_A // 2, LANES, A_GROUPS * blk), BF16),
                        pltpu.VMEM((H_A * 2, blk), F32),
                        pltpu.VMEM((H_A * 2, A_ACC_ROWS, blk), F32),
                        pltpu.VMEM((H_A // 2, blk, A_GROUPS * blk), F32),
                        pltpu.VMEM((H_A // 2, blk, A_GROUPS * blk), F32)],
        compiler_params=_params(("parallel", "arbitrary")),
        name="attn_a_prompt",
    )(lam, q, k, v, bias_diag_t, bias_sub_t, subln_col)


def _attn_a_sample_body(lam_ref, q_ref, kc_ref, vc_ref, kn_ref, vn_ref, bc_ref, bn_ref, subln_ref, o_ref, vte_c,
                        *, post_scale):
    t = q_ref.shape[0]
    p_len = kc_ref.shape[0]
    n_pair = H_A // 2
    lam = lam_ref[0]

    ones_rows = jnp.where(lax.broadcasted_iota(jnp.int32, (A_ACC_ROWS - DV_A, LANES), 0) == 0, 1.0, 0.0).astype(BF16)
    for jb in range(p_len // LANES):
        cols = slice(jb * LANES, (jb + 1) * LANES)
        vt = vc_ref[cols, :].T
        for h in range(H_A):
            vte_c[h, 0:DV_A, cols] = vt[h * DV_A:(h + 1) * DV_A, :].astype(BF16)
            vte_c[h, DV_A:, cols] = ones_rows
    row_pad = jnp.zeros((LANES - t, q_ref.shape[1]), F32)
    vt_n = jnp.concatenate([vn_ref[...], row_pad], axis=0).T
    vte_n = [jnp.concatenate([vt_n[h * DV_A:(h + 1) * DV_A, :].astype(BF16), ones_rows], axis=0) for h in range(H_A)]
    kn_p = jnp.concatenate([kn_ref[...], row_pad], axis=0).astype(BF16)

    qt = jnp.concatenate([q_ref[...] * (DQK_A ** -0.5 * LOG2E), row_pad], axis=0).T
    grp = lax.broadcasted_iota(jnp.int32, (LANES, t), 0) // DQK_A
    s_c, s_n = [], []
    for p in range(n_pair):
        qtp = qt[p * LANES:(p + 1) * LANES, 0:t]
        w = jnp.concatenate([jnp.where(grp == g, qtp, 0.0) for g in range(A_GROUPS)], axis=1).astype(BF16)
        s_c.append(_dot(kc_ref[:, p * LANES:(p + 1) * LANES].astype(BF16), w))
        s_n.append(_dot(kn_p[:, p * LANES:(p + 1) * LANES], w))
    pcs, pns = [], []
    for h in range(H_A):
        cols = slice((h % 2) * 2 * t, (h % 2 + 1) * 2 * t)
        sc = s_c[h // 2][:, cols] + jnp.concatenate([bc_ref[h], bc_ref[h]], axis=1)
        sn = s_n[h // 2][:, cols] + jnp.concatenate([bn_ref[h], bn_ref[h]], axis=1)
        mx = jnp.maximum(jnp.max(sc, axis=0, keepdims=True), jnp.max(sn, axis=0, keepdims=True))
        pcs.append(jnp.exp2(sc - mx).astype(BF16))
        pns.append(jnp.exp2(sn - mx).astype(BF16))
    accs = [_dot(vte_c[h], pcs[h]) + _dot(vte_n[h], pns[h]) for h in range(H_A)]
    outs = []
    for a in accs:
        on = a[:DV_A] / a[DV_A:DV_A + 1]
        o = on[:, 0:t] - lam * on[:, t:2 * t]
        ms = jnp.mean(o * o, axis=0, keepdims=True)
        outs.append(o * lax.rsqrt(ms + NORM_EPS) * subln_ref[...] * post_scale)
    ot = jnp.concatenate(outs, axis=0)
    ot = jnp.concatenate([ot, jnp.zeros((ot.shape[0], LANES - t), F32)], axis=1)
    o_ref[...] = ot.T[0:t, :]


def _attn_a_sample(lam, q, kc, vc, kn, vn, bias_c_t, bias_n_t, subln_col, post_scale):
    b, t, w = q.shape
    p = kc.shape[1]
    assert 2 * t == LANES and p % LANES == 0
    body = functools.partial(_attn_a_sample_body, post_scale=post_scale)
    per_b = lambda bi: (bi, 0, 0)
    const3 = lambda bi: (0, 0, 0)
    return pl.pallas_call(
        body,
        grid=(b,),
        in_specs=[pl.BlockSpec(memory_space=pltpu.SMEM),
                  pl.BlockSpec((None, t, w), per_b),
                  pl.BlockSpec((None, p, w), per_b), pl.BlockSpec((None, p, w), per_b),
                  pl.BlockSpec((None, t, w), per_b), pl.BlockSpec((None, t, w), per_b),
                  pl.BlockSpec(bias_c_t.shape, const3, pipeline_mode=pl.Buffered(1)),
                  pl.BlockSpec(bias_n_t.shape, const3),
                  pl.BlockSpec((DV_A, 1), lambda bi: (0, 0))],
        out_specs=pl.BlockSpec((None, t, w), per_b),
        out_shape=jax.ShapeDtypeStruct((b, t, w), F32),
        scratch_shapes=[pltpu.VMEM((H_A, A_ACC_ROWS, p), BF16)],
        compiler_params=_params(("parallel",)),
        name="attn_a_sample",
    )(lam, q, kc, vc, kn, vn, bias_c_t, bias_n_t, subln_col)


B_PAIR = 2 * CHUNK
B_UNION = BAND + CHUNK
B_UNION_BLOCKS = B_UNION // LANES
B_ACC_ROWS = D_B + 16


def _band_body(q_ref, k_ref, v_ref, bias_ref, o_ref, kbf, vte, *, n_grp, n_invalid):
    i = pl.program_id(1)
    n_kblk = k_ref.shape[0] // LANES

    @pl.when(i == 0)
    def _():
        ones_rows = jnp.where(lax.broadcasted_iota(jnp.int32, (B_ACC_ROWS - D_B, LANES), 0) == 0, 1.0, 0.0)
        for jb in range(n_kblk):
            kbf[jb] = k_ref[jb * LANES:(jb + 1) * LANES, :].astype(BF16)
            vt = v_ref[jb * LANES:(jb + 1) * LANES, :].T
            for h in range(H_B):
                vte[jb, h, 0:D_B, :] = vt[h * D_B:(h + 1) * D_B, :].astype(BF16)
                vte[jb, h, D_B:, :] = ones_rows.astype(BF16)

    top = lax.broadcasted_iota(jnp.int32, (LANES, B_PAIR), 0) < D_B
    key_row = lax.broadcasted_iota(jnp.int32, (B_UNION, B_PAIR), 0)

    unroll = 2 if n_grp % 2 == 0 else 1

    def groups(gt, carry):
        gl = [gt * unroll + k for k in range(unroll)]
        g = [i * n_grp + x for x in gl]
        r0 = [pl.multiple_of(x * B_PAIR, B_PAIR) for x in gl]
        blocks = [[jnp.maximum(x + t - n_invalid // LANES, 0) for t in range(B_UNION_BLOCKS)] for x in g]
        s_pairs = []
        for k in range(unroll):
            qt = (q_ref[pl.ds(r0[k], B_PAIR), :] * (D_B ** -0.5 * LOG2E)).T
            kun = jnp.concatenate([kbf[jb] for jb in blocks[k]], axis=0)
            for p in range(H_B // 2):
                qtp = qt[p * LANES:(p + 1) * LANES, :]
                w = jnp.concatenate([jnp.where(top, qtp, 0.0), jnp.where(top, 0.0, qtp)], axis=1).astype(BF16)
                s_pairs.append(_dot(kun[:, p * LANES:(p + 1) * LANES], w))
        pts = []
        for k in range(unroll):
            for h in range(H_B):
                s = s_pairs[k * (H_B // 2) + h // 2][:, (h % 2) * B_PAIR:(h % 2 + 1) * B_PAIR] + bias_ref[h]
                if n_invalid:
                    s = jnp.where(g[k] * B_PAIR + key_row >= n_invalid, s, NEG_INF)
                pts.append(jnp.exp2(s - jnp.max(s, axis=0, keepdims=True)).astype(BF16))
        accs = [_dot(jnp.concatenate([vte[jb, h] for jb in blocks[k]], axis=1), pts[k * H_B + h])
                for k in range(unroll) for h in range(H_B)]
        for k in range(unroll):
            ot = jnp.concatenate([a[:D_B] / a[D_B:D_B + 1] for a in accs[k * H_B:(k + 1) * H_B]], axis=0)
            o_ref[pl.ds(r0[k], B_PAIR), :] = ot.T
        return carry

    lax.fori_loop(0, n_grp // unroll, groups, 0)


def _band_attn(q, k, v, bias_t, n_invalid):
    b, t, w = q.shape
    assert n_invalid % LANES == 0
    tq = -(-t // B_PAIR) * B_PAIR
    if tq != t:
        q = jnp.pad(q, ((0, 0), (0, tq - t), (0, 0)))
    tk = tq + B_LEFT_CHUNKS * CHUNK - n_invalid
    if k.shape[1] != tk:
        k = jnp.pad(k, ((0, 0), (0, tk - k.shape[1]), (0, 0)))
        v = jnp.pad(v, ((0, 0), (0, tk - v.shape[1]), (0, 0)))
    qb = min(8 * CHUNK, tq)
    n_kblk = tk // LANES
    body = functools.partial(_band_body, n_grp=qb // B_PAIR, n_invalid=n_invalid)
    out = pl.pallas_call(
        body,
        grid=(b, tq // qb),
        in_specs=[pl.BlockSpec((None, qb, w), lambda bi, i: (bi, i, 0)),
                  pl.BlockSpec((None, tk, w), lambda bi, i: (bi, 0, 0)),
                  pl.BlockSpec((None, tk, w), lambda bi, i: (bi, 0, 0)),
                  pl.BlockSpec(bias_t.shape, lambda bi, i: (0, 0, 0))],
        out_specs=pl.BlockSpec((None, qb, w), lambda bi, i: (bi, i, 0)),
        out_shape=jax.ShapeDtypeStruct((b, tq, w), F32),
        scratch_shapes=[pltpu.VMEM((n_kblk, LANES, w), BF16),
                        pltpu.VMEM((n_kblk, H_B, B_ACC_ROWS, LANES), BF16)],
        compiler_params=_params(("parallel", "arbitrary")),
        name="band_attn",
    )(q, k, v, bias_t)
    return out[:, :t] if tq != t else out


_SMALL_DECAY0 = H_C


def _split2(x):
    hi = x.astype(BF16)
    return hi, (x - hi.astype(F32)).astype(BF16)


def _gdn_body(cqkv_ref, small_ref, gate_ref, cprev_ref, s0_ref, convw_ref, alog_ref, dtb_ref, onorm_ref,
              tri_ref, bo_ref, eb_ref, eg_ref, oc_ref, sout_ref, xext, s_scr, *, tc):
    j = pl.program_id(1)
    n_bb = cqkv_ref.shape[0]
    n_ch = tc // CHUNK
    n_pair = H_C // 2
    pad = 8
    lo = pad - (CONV_W - 1)

    @pl.when(j == 0)
    def _():
        z = jnp.zeros((D_C, D_C), F32)
        for bb in range(n_bb):
            xext[bb, lo:pad, :] = cprev_ref[bb]
            for p in range(n_pair):
                s_scr[bb, p] = jnp.concatenate([jnp.concatenate([s0_ref[bb, 2 * p], z], axis=1),
                                                jnp.concatenate([z, s0_ref[bb, 2 * p + 1]], axis=1)], axis=0)

    @pl.when(j > 0)
    def _():
        for bb in range(n_bb):
            xext[bb, lo:pad, :] = xext[bb, tc + lo:tc + pad, :]

    bo = bo_ref[...]
    tri = tri_ref[...]
    lane_s = lax.broadcasted_iota(jnp.int32, (tc, LANES), 1)
    is_dec = (lane_s >= _SMALL_DECAY0) & (lane_s < _SMALL_DECAY0 + H_C)

    def head_sumsq(x):
        hi, lo_ = _split2(x * x)
        return jnp.concatenate([_dot(hi[:, t * LANES:(t + 1) * LANES], bo) + _dot(lo_[:, t * LANES:(t + 1) * LANES], bo)
                                for t in range(n_pair)], axis=1)

    qn, kn, gc, bc, xk, vb, qe = [], [], [], [], [], [], []
    for bb in range(n_bb):
        xext[bb, pad:, :] = cqkv_ref[bb]
        y = convw_ref[0:1, :] * xext[bb, lo:lo + tc, :]
        for w in range(1, CONV_W):
            y = y + convw_ref[w:w + 1, :] * xext[bb, lo + w:lo + w + tc, :]
        y = _silu(y)
        yq, yk, yv = y[:, :C_W], y[:, C_W:2 * C_W], y[:, 2 * C_W:]
        qn.append(yq * lax.rsqrt(head_sumsq(yq) + L2_EPS) * (D_C ** -0.5))
        kn.append(yk * lax.rsqrt(head_sumsq(yk) + L2_EPS))
        sm = small_ref[bb]
        log_a = jnp.where(is_dec, -jnp.exp(alog_ref[...]) * _softplus(sm + dtb_ref[...]), 0.0)
        g_full = sum(_dot(tri, part) for part in _split3(log_a))
        gc.append(sum(_dot(part, eg_ref[...]) for part in _split3(g_full)))
        bc.append(sum(_dot(part, eb_ref[...]) for part in _split3(_sigmoid(sm))))
        egc = jnp.exp(gc[bb])
        xk.append(bc[bb] * egc * kn[bb])
        vb.append(bc[bb] * yv)
        qe.append(egc * qn[bb])

    ii = lax.broadcasted_iota(jnp.int32, (CHUNK, LANES), 0)
    ln = lax.broadcasted_iota(jnp.int32, (CHUNK, LANES), 1)
    jn = ln % D_C
    incl2 = ii >= jn
    strict2 = ii > jn
    eye2 = ii == jn
    left = ln < D_C
    eye2f = jnp.where(eye2, 1.0, 0.0)

    def same_block(n):
        return (ii // n) == (jn // n)
    r128 = lax.broadcasted_iota(jnp.int32, (LANES, LANES), 0)
    c128 = lax.broadcasted_iota(jnp.int32, (LANES, LANES), 1)
    on_diag_blocks = (r128 < D_C) == (c128 < D_C)
    eye128 = jnp.where(r128 == c128, 1.0, 0.0).astype(BF16)

    keep_left = jnp.where(left, 1.0, 0.0).astype(BF16)
    keep_right = jnp.where(left, 0.0, 1.0).astype(BF16)

    def bdiag(x):
        xb = x.astype(BF16)
        return jnp.concatenate([xb * keep_left, xb * keep_right], axis=0)

    units = [(bb, r, p) for bb in range(n_bb) for r in range(n_ch) for p in range(n_pair)]
    lanes_bp = [(bb, p) for bb in range(n_bb) for p in range(n_pair)]

    def rows_of(r):
        return slice(r * CHUNK, (r + 1) * CHUNK)

    def tile_of(p):
        return slice(p * LANES, (p + 1) * LANES)

    def sub(x, u):
        bb, r, p = u
        return x[bb][rows_of(r), tile_of(p)]

    g_last = {(bb, r): gc[bb][(r + 1) * CHUNK - 1:(r + 1) * CHUNK, :] for bb in range(n_bb) for r in range(n_ch)}
    kd = {br: kn[br[0]][rows_of(br[1])] * jnp.exp(g - gc[br[0]][rows_of(br[1])]) for br, g in g_last.items()}
    eg_last = {br: jnp.exp(g) for br, g in g_last.items()}
    dec, r_k, kdt = {}, {}, {}
    for u in units:
        gcp = sub(gc, u)
        g_row = jnp.sum(jnp.where(eye2, gcp, 0.0), axis=0, keepdims=True)
        dec[u] = jnp.where(incl2, jnp.exp(jnp.where(incl2, gcp - g_row, 0.0)), 0.0)
        kp = sub(kn, u)
        r_k[u] = _dot_nt(jnp.concatenate([sub(qn, u), kp], axis=0).astype(BF16), bdiag(kp).astype(BF16))
    for bb, r, p in units:
        kdt[bb, r, p] = _dot_nt(eye128, bdiag(kd[bb, r][:, tile_of(p)]).astype(BF16)).astype(BF16)
    a_mat = {u: sub(bc, u) * jnp.where(strict2, dec[u], 0.0) * r_k[u][CHUNK:] for u in units}
    leaf = GDN_LEAF
    n_leaf = {u: jnp.where(strict2 & same_block(leaf), -a_mat[u], 0.0) for u in units}
    pw = {u: _dot(n_leaf[u].astype(BF16), bdiag(n_leaf[u])) for u in units}
    tm_ = {u: eye2f + n_leaf[u] for u in units}
    st = {u: _dot(jnp.concatenate([tm_[u], pw[u]], axis=0).astype(BF16), bdiag(pw[u])) for u in units}
    tm_ = {u: tm_[u] + st[u][:CHUNK] for u in units}
    fin = {u: _dot(tm_[u].astype(BF16), bdiag(st[u][CHUNK:])) for u in units}
    tm_ = {u: tm_[u] + fin[u] for u in units}
    size = leaf
    while size < CHUNK:
        coupling = strict2 & same_block(2 * size) & jnp.logical_not(same_block(size))
        cd = {u: _dot(jnp.where(coupling, a_mat[u], 0.0).astype(BF16), bdiag(tm_[u])) for u in units}
        dcd = {u: _dot(tm_[u].astype(BF16), bdiag(cd[u])) for u in units}
        tm_ = {u: tm_[u] - dcd[u] for u in units}
        size *= 2
    wu = {u: _dot(tm_[u].astype(BF16),
                  jnp.concatenate([bdiag(sub(xk, u)), bdiag(sub(vb, u))], axis=1).astype(BF16)) for u in units}
    w_b = {u: wu[u][:, :LANES].astype(BF16) for u in units}
    qkd = {u: (dec[u] * r_k[u][:CHUNK]).astype(BF16) for u in units}

    states = {bp: s_scr[bp[0], bp[1]] for bp in lanes_bp}
    o_tiles = {}
    for r in range(n_ch):
        res = {(bb, p): _dot(jnp.concatenate([sub(qe, (bb, r, p)).astype(BF16), w_b[bb, r, p]], axis=0),
                             states[bb, p].astype(BF16)) for bb, p in lanes_bp}
        u = {(bb, p): wu[bb, r, p][:, LANES:] - res[bb, p][CHUNK:] for bb, p in lanes_bp}
        upd = {(bb, p): _dot(kdt[bb, r, p], jnp.concatenate([u[bb, p], u[bb, p]], axis=0).astype(BF16))
               for bb, p in lanes_bp}
        o_inner = {(bb, p): _dot(qkd[bb, r, p], bdiag(u[bb, p]).astype(BF16)) for bb, p in lanes_bp}
        states = {(bb, p): eg_last[bb, r][:, tile_of(p)] * states[bb, p] + jnp.where(on_diag_blocks, upd[bb, p], 0.0)
                  for bb, p in lanes_bp}
        for bb, p in lanes_bp:
            o_tiles[bb, r, p] = res[bb, p][:CHUNK] + o_inner[bb, p]
    for bb, p in lanes_bp:
        s_scr[bb, p] = states[bb, p]
    for bb in range(n_bb):
        o = jnp.concatenate([jnp.concatenate([o_tiles[bb, r, p] for p in range(n_pair)], axis=1)
                             for r in range(n_ch)], axis=0)
        ms = head_sumsq(o) * (1.0 / D_C)
        oc_ref[bb] = o * lax.rsqrt(ms + NORM_EPS) * onorm_ref[...] * _silu(gate_ref[bb])

    @pl.when(j == pl.num_programs(1) - 1)
    def _():
        for bb, p in lanes_bp:
            sout_ref[bb, 2 * p] = states[bb, p][:D_C, :D_C]
            sout_ref[bb, 2 * p + 1] = states[bb, p][D_C:, D_C:]


def _gdn_constants(tc):
    i = jnp.arange(tc, dtype=jnp.int32)
    tri = ((i[None, :] <= i[:, None]) & (i[None, :] // CHUNK == i[:, None] // CHUNK)).astype(BF16)
    l = jnp.arange(LANES, dtype=jnp.int32)
    block_ones = (l[:, None] // D_C == l[None, :] // D_C).astype(BF16)
    col_head = jnp.arange(C_W, dtype=jnp.int32)[None, :] // D_C
    e_beta = (l[:, None] == col_head).astype(BF16)
    e_g = (l[:, None] == col_head + _SMALL_DECAY0).astype(BF16)
    return tri, block_ones, e_beta, e_g


def _gdn(cqkv, small, gate, conv_prev, s0, conv_w, alog_vec, dtb_vec, onorm_tiled):
    b, t, w3 = cqkv.shape
    tc = min(4 * CHUNK, t)
    nb = GDN_BATCH_BLOCK if b % GDN_BATCH_BLOCK == 0 else 1
    consts = _gdn_constants(tc)
    body = functools.partial(_gdn_body, tc=tc)
    blk = lambda bi, j: (bi, j, 0)
    per_b3 = lambda bi, j: (bi, 0, 0)
    per_b4 = lambda bi, j: (bi, 0, 0, 0)
    const2 = lambda bi, j: (0, 0)
    return pl.pallas_call(
        body,
        grid=(b // nb, t // tc),
        in_specs=[pl.BlockSpec((nb, tc, w3), blk),
                  pl.BlockSpec((nb, tc, LANES), blk),
                  pl.BlockSpec((nb, tc, C_W), blk),
                  pl.BlockSpec((nb, CONV_W - 1, w3), per_b3),
                  pl.BlockSpec((nb, H_C, D_C, D_C), per_b4),
                  pl.BlockSpec(conv_w.shape, const2),
                  pl.BlockSpec((1, LANES), const2),
                  pl.BlockSpec((1, LANES), const2),
                  pl.BlockSpec((1, C_W), const2)]
                 + [pl.BlockSpec(c.shape, const2) for c in consts],
        out_specs=[pl.BlockSpec((nb, tc, C_W), blk),
                   pl.BlockSpec((nb, H_C, D_C, D_C), per_b4)],
        out_shape=[jax.ShapeDtypeStruct((b, t, C_W), F32),
                   jax.ShapeDtypeStruct((b, H_C, D_C, D_C), F32)],
        scratch_shapes=[pltpu.VMEM((nb, tc + 8, w3), F32), pltpu.VMEM((nb, H_C // 2, LANES, LANES), F32)],
        compiler_params=_params(("parallel", "arbitrary")),
        name="gdn",
    )(cqkv, small, gate, conv_prev, s0, conv_w, alog_vec, dtb_vec, onorm_tiled, *consts)


def _top2_gates(logits):
    lane = lax.broadcasted_iota(jnp.int32, logits.shape, 1).astype(F32)
    low = -3.0e38
    lg = jnp.where(lane < N_EXPERTS, logits, low)
    m1 = jnp.max(lg, axis=-1, keepdims=True)
    i1 = jnp.min(jnp.where(lg == m1, lane, float(LANES)), axis=-1, keepdims=True)
    lg2 = jnp.where(lane == i1, low, lg)
    m2 = jnp.max(lg2, axis=-1, keepdims=True)
    i2 = jnp.min(jnp.where(lg2 == m2, lane, float(LANES)), axis=-1, keepdims=True)
    e2 = jnp.exp(m2 - m1)
    den = 1.0 + e2
    return jnp.where(lane == i1, 1.0 / den, 0.0) + jnp.where(lane == i2, e2 / den, 0.0)


def _outproj_body(x_ref, oa_ref, ob_ref, oc_ref, w_ref, g_ref, *rest, with_router):
    if with_router:
        rhi_ref, rlo_ref, xo_ref, h_ref, gates_ref = rest
    else:
        xo_ref, h_ref = rest
    y = (_dot(oa_ref[...].astype(BF16), w_ref[0:HEAD_W, :])
         + _dot(ob_ref[...].astype(BF16), w_ref[HEAD_W:2 * HEAD_W, :])
         + _dot(oc_ref[...].astype(BF16), w_ref[2 * HEAD_W:, :]))
    x = x_ref[...] + y
    xo_ref[...] = x
    hf = _rms(x, g_ref[...])
    hb = hf.astype(BF16)
    h_ref[...] = hb
    if with_router:
        lo = (hf - hb.astype(F32)).astype(BF16)
        logits = _dot(hb, rhi_ref[...]) + _dot(lo, rhi_ref[...]) + _dot(hb, rlo_ref[...])
        gates_ref[...] = _top2_gates(logits)


def _outproj(x, oa, ob, oc, w_out, g, router=None):
    n, d = x.shape
    tm = min(ROW_TILE, n)
    row = lambda i: (i, 0)
    const = lambda i: (0, 0)
    in_specs = [pl.BlockSpec((tm, d), row), pl.BlockSpec((tm, HEAD_W), row), pl.BlockSpec((tm, HEAD_W), row),
                pl.BlockSpec((tm, C_W), row), pl.BlockSpec(w_out.shape, const), pl.BlockSpec((1, d), const)]
    out_specs = [pl.BlockSpec((tm, d), row), pl.BlockSpec((tm, d), row)]
    out_shape = [jax.ShapeDtypeStruct((n, d), F32), jax.ShapeDtypeStruct((n, d), BF16)]
    args = [x, oa, ob, oc, w_out, g]
    if router is not None:
        in_specs += [pl.BlockSpec(router[0].shape, const), pl.BlockSpec(router[1].shape, const)]
        out_specs.append(pl.BlockSpec((tm, LANES), row))
        out_shape.append(jax.ShapeDtypeStruct((n, LANES), F32))
        args += list(router)
    return pl.pallas_call(
        functools.partial(_outproj_body, with_router=router is not None),
        grid=(n // tm,),
        in_specs=in_specs, out_specs=out_specs, out_shape=out_shape,
        compiler_params=_params(("parallel",)),
        name="outproj",
    )(*args)


def _ffn_body(h_ref, x_ref, wg_ref, wu_ref, wd_ref, o_ref, acc_ref, *, ff_chunk):
    h = h_ref[...]
    d_ff = wg_ref.shape[1]
    acc_ref[...] = x_ref[...]
    for c0 in range(0, d_ff, ff_chunk):
        a = _silu(_dot(h, wg_ref[:, c0:c0 + ff_chunk])) * _dot(h, wu_ref[:, c0:c0 + ff_chunk])
        acc_ref[...] += _dot(a.astype(BF16), wd_ref[c0:c0 + ff_chunk, :])
    o_ref[...] = acc_ref[...]


def _ffn_dense(h, x, wg, wu, wd):
    n, d = x.shape
    d_ff = wg.shape[1]
    tm = min(ROW_TILE, n)
    ff_chunk = 256 if d_ff % 256 == 0 else LANES
    row = lambda i: (i, 0)
    const = lambda i: (0, 0)
    return pl.pallas_call(
        functools.partial(_ffn_body, ff_chunk=ff_chunk),
        grid=(n // tm,),
        in_specs=[pl.BlockSpec((tm, d), row), pl.BlockSpec((tm, d), row),
                  pl.BlockSpec(wg.shape, const), pl.BlockSpec(wu.shape, const), pl.BlockSpec(wd.shape, const)],
        out_specs=pl.BlockSpec((tm, d), row),
        out_shape=jax.ShapeDtypeStruct((n, d), F32),
        scratch_shapes=[pltpu.VMEM((tm, d), F32)],
        compiler_params=_params(("parallel",)),
        name="ffn_dense",
    )(h, x, wg, wu, wd)


def _moe_body(pre_ref, h_ref, gates_ref, gates_t_ref, wg_ref, wu_ref, wd_ref, *rest, tg, n_f, final_norm, tiles):
    if final_norm:
        x_ref, gfin_ref, y_ref, rank_col, rank_row, hs, eo = rest
    else:
        y_ref, rank_col, rank_row, hs, eo = rest
    g = pl.program_id(0)
    e = pl.program_id(1)
    f = pl.program_id(2)
    tb = MOE_TOKEN_BLOCK
    n_tb = tg // tb
    pre0 = (g * N_EXPERTS + e) * (n_tb + 1)
    unit = tiles[-1]
    n_units = (pre_ref[pre0 + n_tb] + unit - 1) // unit
    units_full = tiles[0] // unit
    n_full = n_units // units_full
    rem = n_units % units_full

    def for_tiles(fn):
        def full(t, carry):
            fn(pl.multiple_of(t * tiles[0], tiles[0]), tiles[0])
            return carry

        lax.fori_loop(0, n_full, full, 0)
        for tm in tiles[1:]:
            bit = tm // unit
            higher = (units_full - 1) & ~(2 * bit - 1)

            @pl.when((rem & bit) != 0)
            def _(tm=tm, higher=higher):
                fn(pl.multiple_of(n_full * tiles[0] + (rem & higher) * unit, unit), tm)

    @pl.when((e == 0) & (f == 0))
    def _():
        y_ref[...] = jnp.zeros_like(y_ref)
        ii = lax.broadcasted_iota(jnp.int32, (tb, tb), 0)
        jj = lax.broadcasted_iota(jnp.int32, (tb, tb), 1)
        lower = jnp.where(ii > jj, 1.0, 0.0).astype(BF16)
        upper = jnp.where(ii < jj, 1.0, 0.0).astype(BF16)
        carry_c = jnp.zeros((1, LANES), F32)
        carry_r = jnp.zeros((N_EXPERTS, 1), F32)
        for b in range(n_tb):
            mc = jnp.where(gates_ref[b * tb:(b + 1) * tb, :] > 0.0, 1.0, 0.0)
            rank_col[b * tb:(b + 1) * tb, :] = _dot(lower, mc.astype(BF16)) + carry_c
            carry_c = carry_c + jnp.sum(mc, axis=0, keepdims=True)
            mr = jnp.where(gates_t_ref[:, b * tb:(b + 1) * tb] > 0.0, 1.0, 0.0)
            rank_row[:, b * tb:(b + 1) * tb] = _dot(mr.astype(BF16), upper) + carry_r
            carry_r = carry_r + jnp.sum(mr, axis=1, keepdims=True)

    win = MOE_WINDOW

    def all_block_windows(align, product, apply):
        base = [(pre_ref[pre0 + b] // align) * align for b in range(n_tb)]
        n_win = [(pre_ref[pre0 + b + 1] - base[b] + win - 1) // win for b in range(n_tb)]
        first = [product(b, pl.multiple_of(base[b], align)) for b in range(n_tb)]
        for b in range(n_tb):
            apply(b, pl.multiple_of(base[b], align), first[b])
        for b in range(n_tb):
            def body(k, carry, b=b):
                r0 = pl.multiple_of(base[b] + k * win, align)
                apply(b, r0, product(b, r0))
                return carry

            lax.fori_loop(1, n_win[b], body, 0)

    @pl.when(f == 0)
    def _():
        def clear(k, carry):
            r0 = pl.multiple_of(k * unit, unit)
            hs[pl.ds(r0, unit), :] = jnp.zeros((unit, hs.shape[1]), BF16)
            return carry

        lax.fori_loop(0, n_units + MOE_SLACK // unit, clear, 0)
        tail = pl.multiple_of(n_units * unit, unit)
        eo[pl.ds(tail, MOE_SLACK), :] = jnp.zeros((MOE_SLACK, eo.shape[1]), F32)

        def gathered(b, r0):
            cols = slice(b * tb, (b + 1) * tb)
            gr = gates_t_ref[pl.ds(e, 1), cols]
            rr = jnp.where(gr > 0.0, rank_row[pl.ds(e, 1), cols], -1.0)
            rows = (r0 + lax.broadcasted_iota(jnp.int32, (win, tb), 0)).astype(F32)
            return _dot(jnp.where(rr == rows, 1.0, 0.0).astype(BF16), h_ref[cols, :])

        def add_rows(b, r0, got):
            hs[pl.ds(r0, win), :] = (hs[pl.ds(r0, win), :].astype(F32) + got).astype(BF16)

        all_block_windows(MOE_BF16_ROWS, gathered, add_rows)

    def expert(r0, tm):
        x = hs[pl.ds(r0, tm), :]
        a = _silu(_dot(x, wg_ref[...])) * _dot(x, wu_ref[...])
        part = _dot(a.astype(BF16), wd_ref[...])

        @pl.when(f == 0)
        def _():
            eo[pl.ds(r0, tm), :] = part

        @pl.when(f > 0)
        def _():
            eo[pl.ds(r0, tm), :] += part

    for_tiles(expert)

    @pl.when(f == n_f - 1)
    def _():
        lane = lax.broadcasted_iota(jnp.int32, (tb, LANES), 1)

        def scattered(b, r0):
            rows = slice(b * tb, (b + 1) * tb)
            gc = jnp.sum(jnp.where(lane == e, gates_ref[rows, :], 0.0), axis=-1, keepdims=True)
            rc = jnp.sum(jnp.where(lane == e, rank_col[rows, :], 0.0), axis=-1, keepdims=True)
            rc = jnp.where(gc > 0.0, rc, -1.0)
            cols = (r0 + lax.broadcasted_iota(jnp.int32, (tb, win), 1)).astype(F32)
            return _dot(jnp.where(rc == cols, gc, 0.0).astype(BF16), eo[pl.ds(r0, win), :].astype(BF16))

        def add_tokens(b, r0, got):
            y_ref[b * tb:(b + 1) * tb, :] += got

        all_block_windows(MOE_F32_ROWS, scattered, add_tokens)

    if final_norm:
        @pl.when((e == pl.num_programs(1) - 1) & (f == n_f - 1))
        def _():
            for b in range(n_tb):
                rows = slice(b * tb, (b + 1) * tb)
                y_ref[rows, :] = _rms(x_ref[rows, :] + y_ref[rows, :], gfin_ref[...])


def _moe_prefix(gates, tg):
    n = gates.shape[0]
    n_tb = tg // MOE_TOKEN_BLOCK
    routed = (gates[:, :N_EXPERTS] > 0.0).astype(jnp.int32)
    per_block = routed.reshape(n // tg, n_tb, MOE_TOKEN_BLOCK, N_EXPERTS).sum(axis=2)
    run = jnp.cumsum(per_block, axis=1)
    pre = jnp.concatenate([jnp.zeros_like(run[:, :1]), run], axis=1)
    return jnp.transpose(pre, (0, 2, 1)).reshape(-1)


def _moe(h, gates, wg, wu, wd, tg, x=None, g_final=None):
    n, d = h.shape
    n_e, _, d_ff = wg.shape
    n_f = 2
    ffh = d_ff // n_f
    final_norm = x is not None
    tiles = tuple(t for t in MOE_TILES if t <= tg)
    body = functools.partial(_moe_body, tg=tg, n_f=n_f, final_norm=final_norm, tiles=tiles)
    once = pl.Buffered(1)
    extra_specs, extra_args = [], []
    if final_norm:
        extra_specs = [pl.BlockSpec((tg, d), lambda g, e, f, c: (g, 0), pipeline_mode=once),
                       pl.BlockSpec((1, d), lambda g, e, f, c: (0, 0))]
        extra_args = [x, g_final]
    grid_spec = pltpu.PrefetchScalarGridSpec(
        num_scalar_prefetch=1,
        grid=(n // tg, n_e, n_f),
        in_specs=[pl.BlockSpec((tg, d), lambda g, e, f, c: (g, 0), pipeline_mode=once),
                  pl.BlockSpec((tg, LANES), lambda g, e, f, c: (g, 0), pipeline_mode=once),
                  pl.BlockSpec((N_EXPERTS, tg), lambda g, e, f, c: (0, g), pipeline_mode=once),
                  pl.BlockSpec((None, d, ffh), lambda g, e, f, c: (e, 0, f)),
                  pl.BlockSpec((None, d, ffh), lambda g, e, f, c: (e, 0, f)),
                  pl.BlockSpec((None, ffh, d), lambda g, e, f, c: (e, f, 0))] + extra_specs,
        out_specs=pl.BlockSpec((tg, d), lambda g, e, f, c: (g, 0), pipeline_mode=once),
        scratch_shapes=[pltpu.VMEM((tg, LANES), F32), pltpu.VMEM((N_EXPERTS, tg), F32),
                        pltpu.VMEM((tg + MOE_SLACK, d), BF16), pltpu.VMEM((tg + MOE_SLACK, d), F32)],
    )
    return pl.pallas_call(
        body,
        grid_spec=grid_spec,
        out_shape=jax.ShapeDtypeStruct((n, d), F32),
        compiler_params=_params(("parallel", "arbitrary", "arbitrary")),
        name="moe",
    )(_moe_prefix(gates, tg), h, gates, gates[:, :N_EXPERTS].T, wg, wu, wd, *extra_args)


def _final_body(x_ref, y_ref, g_ref, o_ref):
    o_ref[...] = _rms(x_ref[...] + y_ref[...], g_ref[...])


def _final_norm(x, y, g):
    n, d = x.shape
    tm = min(ROW_TILE, n)
    row = lambda i: (i, 0)
    return pl.pallas_call(
        _final_body,
        grid=(n // tm,),
        in_specs=[pl.BlockSpec((tm, d), row), pl.BlockSpec((tm, d), row), pl.BlockSpec((1, d), lambda i: (0, 0))],
        out_specs=pl.BlockSpec((tm, d), row),
        out_shape=jax.ShapeDtypeStruct((n, d), F32),
        compiler_params=_params(("parallel",)),
        name="final_norm",
    )(x, y, g)


def _t5_bucket(rel):
    nb = T5_BUCKETS // 2
    max_exact = nb // 2
    ret = jnp.where(rel > 0, nb, 0)
    n = jnp.abs(rel)
    nf = jnp.maximum(n, 1).astype(F32)
    large = max_exact + (jnp.log(nf / max_exact) / math.log(T5_MAX_DIST / max_exact) * (nb - max_exact)).astype(jnp.int32)
    large = jnp.minimum(large, nb - 1)
    return ret + jnp.where(n < max_exact, n, large)


def _t5_table(t5_bias, q_pos, k_pos):
    bias = _lookup(t5_bias, _t5_bucket(k_pos[None, :] - q_pos[:, None]))
    mask = (k_pos[None, :] // CHUNK) <= (q_pos[:, None] // CHUNK)
    return bias, mask[None]


def _lookup(table, idx):
    onehot = jax.nn.one_hot(idx, table.shape[0], dtype=F32)
    return jnp.einsum("qkn,nh->hqk", onehot, table.astype(F32), precision=lax.Precision.HIGHEST)


def _band_table(rel_bias):
    qi = jnp.arange(CHUNK, dtype=jnp.int32)
    kj = jnp.arange(BAND, dtype=jnp.int32) - B_LEFT_CHUNKS * CHUNK
    rel = jnp.clip(kj[None, :] - qi[:, None], -B_REL_CLIP, B_REL_CLIP) + B_REL_CLIP
    base = jnp.swapaxes(_lookup(rel_bias, rel), 1, 2) * LOG2E
    halves = [jnp.pad(base, ((0, 0), (c * CHUNK, (1 - c) * CHUNK), (0, 0)), constant_values=NEG_INF)
              for c in range(2)]
    return jnp.concatenate(halves, axis=2)


def _lane_vec(v, offset):
    return jnp.zeros((1, LANES), F32).at[0, offset:offset + v.shape[0]].set(v.astype(F32))


def _moe_group(n):
    for tg in (2048, 1024, 512, 256):
        if n % tg == 0:
            return tg
    raise ValueError(f"token count {n} is not a multiple of 256")


def kernel(x_prompt, x_sample, cache_a_k, cache_a_v, cache_b_k, cache_b_v, cache_c_conv, state_c_ssm, w_in, w_out, norm_mix, norm_ffn, norm_final, lam_qk, subln_a, t5_bias, rel_bias_b, conv_c, a_log_c, dt_bias_c, onorm_c, ffn_gate, ffn_up, ffn_down, moe_router, moe_gate, moe_up, moe_down):
    depth = w_in.shape[0]
    bp, sp, d = x_prompt.shape
    bs, ss, _ = x_sample.shape
    past = cache_a_k.shape[2]
    nb_cache = cache_b_k.shape[2]
    assert ss == CHUNK and nb_cache == B_LEFT_CHUNKS * CHUNK and sp >= B_LEFT_CHUNKS * CHUNK

    xp = x_prompt.reshape(bp * sp, d)
    xs = x_sample.reshape(bs * ss, d)
    blk = min(A_BLOCK, sp)

    pos_blk = jnp.arange(blk, dtype=jnp.int32)
    far_bias = t5_bias[_t5_bucket(jnp.int32(-(blk + 1)))].astype(F32)
    bd, md = _t5_table(t5_bias, pos_blk, pos_blk)
    bias_diag = jnp.where(md, bd - far_bias[:, None, None], NEG_INF)
    bsub, _ = _t5_table(t5_bias, blk + pos_blk, pos_blk)
    bias_sub = bsub - far_bias[:, None, None]
    q_pos_s = past + jnp.arange(ss, dtype=jnp.int32)
    bfull, mfull = _t5_table(t5_bias, q_pos_s, jnp.arange(past + ss, dtype=jnp.int32))
    bias_s = jnp.where(mfull, bfull, NEG_INF)
    bias_s_t = jnp.swapaxes(bias_s, 1, 2) * LOG2E
    bias_s_cache = bias_s_t[:, :past]
    bias_s_new = jnp.pad(bias_s_t[:, past:], ((0, 0), (0, LANES - ss), (0, 0)), constant_values=NEG_INF)

    zeros_conv = jnp.zeros((bp, CONV_W - 1, 3 * C_W), F32)
    zeros_state = jnp.zeros((bp, H_C, D_C, D_C), F32)

    p_states = [[] for _ in range(6)]
    s_states = [[] for _ in range(6)]
    yp = ys = None
    g_fin = norm_final.reshape(1, d)
    finals = []
    for l in range(depth):
        if yp is not None:
            xp, xs, yp, ys = xp + yp, xs + ys, None, None
        w = w_in[l]
        n_main = 6 * HEAD_W + 3 * C_W
        w_main = jnp.concatenate([w[:, :n_main], w[:, n_main + 2 * H_C:]], axis=1).astype(BF16)
        w_small = jnp.zeros((d, LANES), F32).at[:, :2 * H_C].set(w[:, n_main:n_main + 2 * H_C]).astype(BF16)
        w_out_l = w_out[l].astype(BF16)
        g_mix = norm_mix[l].reshape(1, d)
        g_ffn = norm_ffn[l].reshape(1, d)
        lam_init = 0.8 - 0.6 * math.exp(-0.3 * l)
        lq = lam_qk[l].astype(F32)
        lam = (jnp.exp(jnp.sum(lq[0] * lq[1])) - jnp.exp(jnp.sum(lq[2] * lq[3])) + lam_init).reshape(1)
        subln = subln_a[l].reshape(1, DV_A)
        band_bias = _band_table(rel_bias_b[l])
        alog_vec = _lane_vec(a_log_c[l], _SMALL_DECAY0)
        dtb_vec = _lane_vec(dt_bias_c[l], _SMALL_DECAY0)
        onorm = jnp.tile(onorm_c[l].astype(F32), H_C).reshape(1, C_W)
        is_moe = l % 2 == 1
        if is_moe:
            r = jnp.zeros((d, LANES), F32).at[:, :N_EXPERTS].set(moe_router[l // 2])
            r_hi = r.astype(BF16)
            router = (r_hi, (r - r_hi.astype(F32)).astype(BF16))
            e_wg, e_wu, e_wd = (moe_gate[l // 2].astype(BF16), moe_up[l // 2].astype(BF16),
                                moe_down[l // 2].astype(BF16))
        else:
            router = None
            f_wg, f_wu, f_wd = (ffn_gate[l // 2].astype(BF16), ffn_up[l // 2].astype(BF16),
                                ffn_down[l // 2].astype(BF16))

        new_x = []
        for is_prompt, x in ((True, xp), (False, xs)):
            b, t = (bp, sp) if is_prompt else (bs, ss)
            aq, ak, av, bq, bk, bv, cqkv, cgate, csmall = _inproj(x, g_mix, w_main, w_small)
            r3 = lambda a: a.reshape(b, t, a.shape[-1])
            if is_prompt:
                oa = _attn_a_prompt(lam, r3(aq), r3(ak), r3(av), jnp.swapaxes(bias_diag, 1, 2) * LOG2E,
                                    jnp.swapaxes(bias_sub, 1, 2) * LOG2E, subln.reshape(DV_A, 1), 1.0 - lam_init)
                ob = _band_attn(r3(bq), r3(bk), r3(bv), band_bias, B_LEFT_CHUNKS * CHUNK)
                conv_prev, s0 = zeros_conv, zeros_state
            else:
                oa = _attn_a_sample(lam, r3(aq), cache_a_k[l].reshape(b, past, HEAD_W),
                                    cache_a_v[l].reshape(b, past, HEAD_W), r3(ak), r3(av),
                                    bias_s_cache, bias_s_new, subln.reshape(DV_A, 1), 1.0 - lam_init)
                kb = jnp.concatenate([cache_b_k[l].reshape(b, nb_cache, HEAD_W), r3(bk)], axis=1)
                vb = jnp.concatenate([cache_b_v[l].reshape(b, nb_cache, HEAD_W), r3(bv)], axis=1)
                ob = _band_attn(r3(bq), kb, vb, band_bias, 0)
                conv_prev, s0 = cache_c_conv[l], state_c_ssm[l]
            oc, s_new = _gdn(r3(cqkv), r3(csmall), r3(cgate), conv_prev, s0, conv_c[l], alog_vec, dtb_vec, onorm)
            n = b * t
            res = _outproj(x, oa.reshape(n, HEAD_W), ob.reshape(n, HEAD_W), oc.reshape(n, C_W), w_out_l, g_ffn,
                           router)
            if is_moe:
                x_new, h2, gates = res
                if l == depth - 1:
                    finals.append(_moe(h2, gates, e_wg, e_wu, e_wd, _moe_group(n), x=x_new, g_final=g_fin))
                    new_x.append((x_new, None))
                else:
                    new_x.append((x_new, _moe(h2, gates, e_wg, e_wu, e_wd, _moe_group(n))))
            else:
                x_new, h2 = res
                new_x.append((_ffn_dense(h2, x_new, f_wg, f_wu, f_wd), None))
            keep = min(B_LEFT_CHUNKS * CHUNK, t)
            states = (ak.reshape(b, t, H_A, 2 * DQK_A), av.reshape(b, t, H_A, DV_A),
                      bk.reshape(b, t, H_B, D_B)[:, t - keep:], bv.reshape(b, t, H_B, D_B)[:, t - keep:],
                      r3(cqkv)[:, t - (CONV_W - 1):], s_new)
            for i in range(6):
                (p_states if is_prompt else s_states)[i].append(states[i])
        (xp, yp), (xs, ys) = new_x

    if finals:
        y_prompt, y_sample = finals[0].reshape(bp, sp, d), finals[1].reshape(bs, ss, d)
    else:
        if yp is None:
            yp, ys = jnp.zeros_like(xp), jnp.zeros_like(xs)
        y_prompt = _final_norm(xp, yp, g_fin).reshape(bp, sp, d)
        y_sample = _final_norm(xs, ys, g_fin).reshape(bs, ss, d)
    p_out = [jnp.stack(s, axis=0) for s in p_states]
    s_out = [jnp.stack(s, axis=0) for s in s_states]
    return (y_prompt, y_sample, *p_out, *s_out)
```

```python
import functools
import math

import jax
import jax.numpy as jnp
from jax import lax
from jax.experimental import pallas as pl
from jax.experimental.pallas import tpu as pltpu

F32 = jnp.float32
BF16 = jnp.bfloat16

CHUNK = 64
H_A = 4
DQK_A = 32
DV_A = 64
H_B = 4
D_B = 64
B_LEFT_CHUNKS = 8
B_REL_CLIP = 128
H_C = 8
D_C = 64
CONV_W = 4
T5_BUCKETS = 32
T5_MAX_DIST = 128
N_EXPERTS = 8
NORM_EPS = 1e-6
L2_EPS = 1e-6
NEG_INF = -1e30
HEAD_W = 256
C_W = H_C * D_C
BAND = (B_LEFT_CHUNKS + 1) * CHUNK
LANES = 128
LOG2E = math.log2(math.e)
VMEM_LIMIT = 56 * 1024 * 1024

A_BLOCK = 256
ROW_TILE = 512
GDN_BATCH_BLOCK = 2
GDN_LEAF = 8
MOE_TILES = (512, 256, 128, 64)
MOE_TOKEN_BLOCK = 256
MOE_WINDOW = 96
MOE_SLACK = 128
MOE_F32_ROWS = 8
MOE_BF16_ROWS = 16


def _params(sem):
    return pltpu.CompilerParams(dimension_semantics=sem, vmem_limit_bytes=VMEM_LIMIT)


def _rms(x, g):
    ms = jnp.mean(x * x, axis=-1, keepdims=True)
    return x * lax.rsqrt(ms + NORM_EPS) * g


def _sigmoid(x):
    return 1.0 / (1.0 + jnp.exp(-x))


def _silu(x):
    return x * _sigmoid(x)


def _softplus(x):
    return jnp.maximum(x, 0.0) + jnp.log1p(jnp.exp(-jnp.abs(x)))


def _dot(a, b):
    return jnp.dot(a, b, preferred_element_type=F32)


def _dot_nt(a, b):
    return lax.dot_general(a, b, (((1,), (1,)), ((), ())), preferred_element_type=F32)


def _dot_tn(a, b):
    return lax.dot_general(a, b, (((0,), (0,)), ((), ())), preferred_element_type=F32)


def _split3(x):
    x1 = x.astype(BF16)
    r1 = x - x1.astype(F32)
    x2 = r1.astype(BF16)
    x3 = (r1 - x2.astype(F32)).astype(BF16)
    return x1, x2, x3


def _inproj_body(x_ref, g_ref, w_ref, ws_ref, aq_ref, ak_ref, av_ref, bq_ref, bk_ref, bv_ref,
                 cqkv_ref, cgate_ref, csmall_ref):
    h = _rms(x_ref[...], g_ref[...]).astype(BF16)
    col = 0
    for ref in (aq_ref, ak_ref, av_ref, bq_ref, bk_ref, bv_ref, cqkv_ref, cgate_ref):
        width = ref.shape[-1]
        ref[...] = _dot(h, w_ref[:, col:col + width])
        col += width
    csmall_ref[...] = _dot(h, ws_ref[...])


def _inproj(x, g, w_main, w_small):
    n, d = x.shape
    tm = min(ROW_TILE, n)
    widths = (HEAD_W,) * 6 + (3 * C_W, C_W, LANES)
    row = lambda i: (i, 0)
    const = lambda i: (0, 0)
    return pl.pallas_call(
        _inproj_body,
        grid=(n // tm,),
        in_specs=[pl.BlockSpec((tm, d), row), pl.BlockSpec((1, d), const),
                  pl.BlockSpec(w_main.shape, const), pl.BlockSpec(w_small.shape, const)],
        out_specs=[pl.BlockSpec((tm, w), row) for w in widths],
        out_shape=[jax.ShapeDtypeStruct((n, w), F32) for w in widths],
        compiler_params=_params(("parallel",)),
        name="inproj",
    )(x, g, w_main, w_small)


A_ACC_ROWS = DV_A + 16
A_GROUPS = LANES // DQK_A


def _attn_a_prompt_body(lam_ref, q_ref, k_ref, v_ref, bd_ref, bs_ref, subln_ref, o_ref,
                        kbf, vte, wq, m_scr, acc_scr, s_even, s_odd, *, blk, post_scale):
    i = pl.program_id(1)
    n_blk = k_ref.shape[0] // blk
    n_pair = H_A // 2

    @pl.when(i == 0)
    def _():
        ones_rows = jnp.where(lax.broadcasted_iota(jnp.int32, (A_ACC_ROWS - DV_A, blk), 0) == 0, 1.0, 0.0)
        for jb in range(n_blk):
            kbf[jb] = k_ref[jb * blk:(jb + 1) * blk, :].astype(BF16)
            vt = v_ref[jb * blk:(jb + 1) * blk, :].T
            for h in range(H_A):
                vte[jb, h, 0:DV_A, :] = vt[h * DV_A:(h + 1) * DV_A, :].astype(BF16)
                vte[jb, h, DV_A:, :] = ones_rows.astype(BF16)

    lam = lam_ref[0]
    qt = (q_ref[...] * (DQK_A ** -0.5 * LOG2E)).T
    grp = lax.broadcasted_iota(jnp.int32, (LANES, blk), 0) // DQK_A
    for p in range(n_pair):
        qtp = qt[p * LANES:(p + 1) * LANES, :]
        wq[p] = jnp.concatenate([jnp.where(grp == g, qtp, 0.0) for g in range(A_GROUPS)], axis=1).astype(BF16)
    m_scr[...] = jnp.full(m_scr.shape, NEG_INF, F32)
    acc_scr[...] = jnp.zeros(acc_scr.shape, F32)

    def scores(jb):
        return [_dot(kbf[jb, :, p * LANES:(p + 1) * LANES], wq[p]) for p in range(n_pair)]

    groups = range(n_pair * A_GROUPS)

    def softmax_stage(tile_of, bias_ref, off):
        alphas, pts = [], []
        for g in groups:
            p, gi = divmod(g, A_GROUPS)
            s = tile_of(p, gi)
            if bias_ref is not None:
                s = s + bias_ref[2 * p + gi // 2]
            if off is not None:
                s = s + off
            m_old = m_scr[g:g + 1, :]
            m_new = jnp.maximum(m_old, jnp.max(s, axis=0, keepdims=True))
            alphas.append(jnp.exp2(m_old - m_new))
            pts.append(jnp.exp2(s - m_new).astype(BF16))
            m_scr[g:g + 1, :] = m_new
        return alphas, pts

    def value_stage(jb, alphas, pts):
        pvs = [_dot(vte[jb, 2 * (g // A_GROUPS) + (g % A_GROUPS) // 2], pts[g]) for g in groups]
        for g in groups:
            acc_scr[g] = alphas[g] * acc_scr[g] + pvs[g]

    def consume(tile_of, jb, bias_ref, off):
        value_stage(jb, *softmax_stage(tile_of, bias_ref, off))

    def tiles(jb, bias_ref, off):
        s_alls = scores(jb)
        consume(lambda p, gi: s_alls[p][:, gi * blk:(gi + 1) * blk], jb, bias_ref, off)

    tiles(i, bd_ref, None)
    tiles(jnp.maximum(i - 1, 0), bs_ref, jnp.where(i >= 1, 0.0, NEG_INF))

    n_far = jnp.maximum(i - 1, 0)

    def put_scores(dst, jb):
        s_alls = scores(jnp.minimum(jb, n_far - 1))
        for p in range(n_pair):
            dst[p] = s_alls[p]

    def from_scratch(src):
        return lambda p, gi: src[p, :, gi * blk:(gi + 1) * blk]

    @pl.when(n_far > 0)
    def _():
        put_scores(s_even, 0)
        put_scores(s_odd, 1)

    def far_pair(t, carry):
        jb = 2 * t
        soft_a = softmax_stage(from_scratch(s_even), None, None)
        put_scores(s_even, jb + 2)
        soft_b = softmax_stage(from_scratch(s_odd), None, None)
        value_stage(jb, *soft_a)
        put_scores(s_odd, jb + 3)
        value_stage(jb + 1, *soft_b)
        return carry

    lax.fori_loop(0, n_far // 2, far_pair, 0)

    @pl.when(n_far % 2 == 1)
    def _():
        consume(from_scratch(s_even), n_far - 1, None, None)

    outs = []
    for h in range(H_A):
        g1 = (h // 2) * A_GROUPS + (h % 2) * 2
        a1 = acc_scr[g1]
        a2 = acc_scr[g1 + 1]
        o = a1[:DV_A] / a1[DV_A:DV_A + 1] - lam * (a2[:DV_A] / a2[DV_A:DV_A + 1])
        ms = jnp.mean(o * o, axis=0, keepdims=True)
        outs.append(o * lax.rsqrt(ms + NORM_EPS) * subln_ref[...] * post_scale)
    o_ref[...] = jnp.concatenate(outs, axis=0).T


def _attn_a_prompt(lam, q, k, v, bias_diag_t, bias_sub_t, subln_col, post_scale):
    b, s, w = q.shape
    blk = min(A_BLOCK, s)
    n_blk = s // blk
    body = functools.partial(_attn_a_prompt_body, blk=blk, post_scale=post_scale)
    return pl.pallas_call(
        body,
        grid=(b, n_blk),
        in_specs=[pl.BlockSpec(memory_space=pltpu.SMEM),
                  pl.BlockSpec((None, blk, w), lambda bi, i: (bi, i, 0)),
                  pl.BlockSpec((None, s, w), lambda bi, i: (bi, 0, 0)),
                  pl.BlockSpec((None, s, w), lambda bi, i: (bi, 0, 0)),
                  pl.BlockSpec(bias_diag_t.shape, lambda bi, i: (0, 0, 0)),
                  pl.BlockSpec(bias_sub_t.shape, lambda bi, i: (0, 0, 0)),
                  pl.BlockSpec((DV_A, 1), lambda bi, i: (0, 0))],
        out_specs=pl.BlockSpec((None, blk, w), lambda bi, i: (bi, i, 0)),
        out_shape=jax.ShapeDtypeStruct((b, s, w), F32),
        scratch_shapes=[pltpu.VMEM((n_blk, blk, w), BF16),
                        pltpu.VMEM((n_blk, H_A, A_ACC_ROWS, blk), BF16),
                        pltpu.VMEM((H_A // 2, LANES, A_GROUPS * blk), BF16),
                        pltpu.VMEM((H_A * 2, blk), F32),
                        pltpu.VMEM((H_A * 2, A_ACC_ROWS, blk), F32),
                        pltpu.VMEM((H_A // 2, blk, A_GROUPS * blk), F32),
                        pltpu.VMEM((H_A // 2, blk, A_GROUPS * blk), F32)],
        compiler_params=_params(("parallel", "arbitrary")),
        name="attn_a_prompt",
    )(lam, q, k, v, bias_diag_t, bias_sub_t, subln_col)


def _attn_a_sample_body(lam_ref, q_ref, kct_ref, vct_ref, kn_ref, vn_ref, bc_ref, bn_ref, subln_ref, o_ref,
                        *, post_scale):
    t = q_ref.shape[0]
    p_len = kct_ref.shape[1]
    n_pair = H_A // 2
    lam = lam_ref[0]

    ones_rows = jnp.where(lax.broadcasted_iota(jnp.int32, (A_ACC_ROWS - DV_A, LANES), 0) == 0, 1.0, 0.0).astype(BF16)
    ones_cache = jnp.where(lax.broadcasted_iota(jnp.int32, (A_ACC_ROWS - DV_A, p_len), 0) == 0, 1.0, 0.0).astype(BF16)
    vte_c = [jnp.concatenate([vct_ref[h * DV_A:(h + 1) * DV_A, :].astype(BF16), ones_cache], axis=0)
             for h in range(H_A)]
    row_pad = jnp.zeros((LANES - t, q_ref.shape[1]), F32)
    vt_n = jnp.concatenate([vn_ref[...], row_pad], axis=0).T
    vte_n = [jnp.concatenate([vt_n[h * DV_A:(h + 1) * DV_A, :].astype(BF16), ones_rows], axis=0) for h in range(H_A)]
    kn_p = jnp.concatenate([kn_ref[...], row_pad], axis=0).astype(BF16)

    qt = jnp.concatenate([q_ref[...] * (DQK_A ** -0.5 * LOG2E), row_pad], axis=0).T
    grp = lax.broadcasted_iota(jnp.int32, (LANES, t), 0) // DQK_A
    s_c, s_n = [], []
    for p in range(n_pair):
        qtp = qt[p * LANES:(p + 1) * LANES, 0:t]
        w = jnp.concatenate([jnp.where(grp == g, qtp, 0.0) for g in range(A_GROUPS)], axis=1).astype(BF16)
        s_c.append(_dot(kct_ref[p * LANES:(p + 1) * LANES, :].T.astype(BF16), w))
        s_n.append(_dot(kn_p[:, p * LANES:(p + 1) * LANES], w))
    pcs, pns = [], []
    for h in range(H_A):
        cols = slice((h % 2) * 2 * t, (h % 2 + 1) * 2 * t)
        sc = s_c[h // 2][:, cols] + jnp.concatenate([bc_ref[h], bc_ref[h]], axis=1)
        sn = s_n[h // 2][:, cols] + jnp.concatenate([bn_ref[h], bn_ref[h]], axis=1)
        mx = jnp.maximum(jnp.max(sc, axis=0, keepdims=True), jnp.max(sn, axis=0, keepdims=True))
        pcs.append(jnp.exp2(sc - mx).astype(BF16))
        pns.append(jnp.exp2(sn - mx).astype(BF16))
    accs = [_dot(vte_c[h], pcs[h]) + _dot(vte_n[h], pns[h]) for h in range(H_A)]
    outs = []
    for a in accs:
        on = a[:DV_A] / a[DV_A:DV_A + 1]
        o = on[:, 0:t] - lam * on[:, t:2 * t]
        ms = jnp.mean(o * o, axis=0, keepdims=True)
        outs.append(o * lax.rsqrt(ms + NORM_EPS) * subln_ref[...] * post_scale)
    ot = jnp.concatenate(outs, axis=0)
    ot = jnp.concatenate([ot, jnp.zeros((ot.shape[0], LANES - t), F32)], axis=1)
    o_ref[...] = ot.T[0:t, :]


def _attn_a_sample(lam, q, kct, vct, layer, kn, vn, bias_c_t, bias_n_t, subln_col, post_scale):
    b, t, w = q.shape
    p = kct.shape[3]
    assert 2 * t == LANES and p % LANES == 0
    cache_spec = pl.BlockSpec((None, None, w, p), lambda bi: (layer, bi, 0, 0))
    body = functools.partial(_attn_a_sample_body, post_scale=post_scale)
    per_b = lambda bi: (bi, 0, 0)
    const3 = lambda bi: (0, 0, 0)
    return pl.pallas_call(
        body,
        grid=(b,),
        in_specs=[pl.BlockSpec(memory_space=pltpu.SMEM),
                  pl.BlockSpec((None, t, w), per_b),
                  cache_spec, cache_spec,
                  pl.BlockSpec((None, t, w), per_b), pl.BlockSpec((None, t, w), per_b),
                  pl.BlockSpec(bias_c_t.shape, const3, pipeline_mode=pl.Buffered(1)),
                  pl.BlockSpec(bias_n_t.shape, const3),
                  pl.BlockSpec((DV_A, 1), lambda bi: (0, 0))],
        out_specs=pl.BlockSpec((None, t, w), per_b),
        out_shape=jax.ShapeDtypeStruct((b, t, w), F32),
        compiler_params=_params(("parallel",)),
        name="attn_a_sample",
    )(lam, q, kct, vct, kn, vn, bias_c_t, bias_n_t, subln_col)


B_PAIR = 2 * CHUNK
B_UNION = BAND + CHUNK
B_UNION_BLOCKS = B_UNION // LANES
B_ACC_ROWS = D_B + 16


def _band_body(q_ref, k_ref, v_ref, bias_ref, o_ref, kbf, vte, *, n_grp, n_invalid):
    i = pl.program_id(1)
    n_kblk = k_ref.shape[0] // LANES

    @pl.when(i == 0)
    def _():
        ones_rows = jnp.where(lax.broadcasted_iota(jnp.int32, (B_ACC_ROWS - D_B, LANES), 0) == 0, 1.0, 0.0)
        for jb in range(n_kblk):
            kbf[jb] = k_ref[jb * LANES:(jb + 1) * LANES, :].astype(BF16)
            vt = v_ref[jb * LANES:(jb + 1) * LANES, :].T
            for h in range(H_B):
                vte[jb, h, 0:D_B, :] = vt[h * D_B:(h + 1) * D_B, :].astype(BF16)
                vte[jb, h, D_B:, :] = ones_rows.astype(BF16)

    top = lax.broadcasted_iota(jnp.int32, (LANES, B_PAIR), 0) < D_B
    key_row = lax.broadcasted_iota(jnp.int32, (B_UNION, B_PAIR), 0)

    unroll = 2 if n_grp % 2 == 0 else 1

    def groups(gt, carry):
        gl = [gt * unroll + k for k in range(unroll)]
        g = [i * n_grp + x for x in gl]
        r0 = [pl.multiple_of(x * B_PAIR, B_PAIR) for x in gl]
        blocks = [[jnp.maximum(x + t - n_invalid // LANES, 0) for t in range(B_UNION_BLOCKS)] for x in g]
        s_pairs = []
        for k in range(unroll):
            qt = (q_ref[pl.ds(r0[k], B_PAIR), :] * (D_B ** -0.5 * LOG2E)).T
            kun = jnp.concatenate([kbf[jb] for jb in blocks[k]], axis=0)
            for p in range(H_B // 2):
                qtp = qt[p * LANES:(p + 1) * LANES, :]
                w = jnp.concatenate([jnp.where(top, qtp, 0.0), jnp.where(top, 0.0, qtp)], axis=1).astype(BF16)
                s_pairs.append(_dot(kun[:, p * LANES:(p + 1) * LANES], w))
        pts = []
        for k in range(unroll):
            for h in range(H_B):
                s = s_pairs[k * (H_B // 2) + h // 2][:, (h % 2) * B_PAIR:(h % 2 + 1) * B_PAIR] + bias_ref[h]
                if n_invalid:
                    s = jnp.where(g[k] * B_PAIR + key_row >= n_invalid, s, NEG_INF)
                pts.append(jnp.exp2(s - jnp.max(s, axis=0, keepdims=True)).astype(BF16))
        accs = [_dot(jnp.concatenate([vte[jb, h] for jb in blocks[k]], axis=1), pts[k * H_B + h])
                for k in range(unroll) for h in range(H_B)]
        for k in range(unroll):
            ot = jnp.concatenate([a[:D_B] / a[D_B:D_B + 1] for a in accs[k * H_B:(k + 1) * H_B]], axis=0)
            o_ref[pl.ds(r0[k], B_PAIR), :] = ot.T
        return carry

    lax.fori_loop(0, n_grp // unroll, groups, 0)


def _band_attn(q, k, v, bias_t, n_invalid):
    b, t, w = q.shape
    assert n_invalid % LANES == 0
    tq = -(-t // B_PAIR) * B_PAIR
    if tq != t:
        q = jnp.pad(q, ((0, 0), (0, tq - t), (0, 0)))
    tk = tq + B_LEFT_CHUNKS * CHUNK - n_invalid
    if k.shape[1] != tk:
        k = jnp.pad(k, ((0, 0), (0, tk - k.shape[1]), (0, 0)))
        v = jnp.pad(v, ((0, 0), (0, tk - v.shape[1]), (0, 0)))
    qb = min(8 * CHUNK, tq)
    n_kblk = tk // LANES
    body = functools.partial(_band_body, n_grp=qb // B_PAIR, n_invalid=n_invalid)
    out = pl.pallas_call(
        body,
        grid=(b, tq // qb),
        in_specs=[pl.BlockSpec((None, qb, w), lambda bi, i: (bi, i, 0)),
                  pl.BlockSpec((None, tk, w), lambda bi, i: (bi, 0, 0)),
                  pl.BlockSpec((None, tk, w), lambda bi, i: (bi, 0, 0)),
                  pl.BlockSpec(bias_t.shape, lambda bi, i: (0, 0, 0))],
        out_specs=pl.BlockSpec((None, qb, w), lambda bi, i: (bi, i, 0)),
        out_shape=jax.ShapeDtypeStruct((b, tq, w), F32),
        scratch_shapes=[pltpu.VMEM((n_kblk, LANES, w), BF16),
                        pltpu.VMEM((n_kblk, H_B, B_ACC_ROWS, LANES), BF16)],
        compiler_params=_params(("parallel", "arbitrary")),
        name="band_attn",
    )(q, k, v, bias_t)
    return out[:, :t] if tq != t else out


_SMALL_DECAY0 = H_C


def _split2(x):
    hi = x.astype(BF16)
    return hi, (x - hi.astype(F32)).astype(BF16)


def _gdn_body(cqkv_ref, small_ref, gate_ref, cprev_ref, s0_ref, convw_ref, alog_ref, dtb_ref, onorm_ref,
              tri_ref, bo_ref, eb_ref, eg_ref, oc_ref, sout_ref, xext, s_scr, *, tc):
    j = pl.program_id(1)
    n_bb = cqkv_ref.shape[0]
    n_ch = tc // CHUNK
    n_pair = H_C // 2
    pad = 8
    lo = pad - (CONV_W - 1)

    @pl.when(j == 0)
    def _():
        z = jnp.zeros((D_C, D_C), F32)
        for bb in range(n_bb):
            xext[bb, lo:pad, :] = cprev_ref[bb]
            for p in range(n_pair):
                s_scr[bb, p] = jnp.concatenate([jnp.concatenate([s0_ref[bb, 2 * p], z], axis=1),
                                                jnp.concatenate([z, s0_ref[bb, 2 * p + 1]], axis=1)], axis=0)

    @pl.when(j > 0)
    def _():
        for bb in range(n_bb):
            xext[bb, lo:pad, :] = xext[bb, tc + lo:tc + pad, :]

    bo = bo_ref[...]
    tri = tri_ref[...]
    lane_s = lax.broadcasted_iota(jnp.int32, (tc, LANES), 1)
    is_dec = (lane_s >= _SMALL_DECAY0) & (lane_s < _SMALL_DECAY0 + H_C)

    def head_sumsq(x):
        hi, lo_ = _split2(x * x)
        return jnp.concatenate([_dot(hi[:, t * LANES:(t + 1) * LANES], bo) + _dot(lo_[:, t * LANES:(t + 1) * LANES], bo)
                                for t in range(n_pair)], axis=1)

    qn, kn, gc, bc, xk, vb, qe = [], [], [], [], [], [], []
    for bb in range(n_bb):
        xext[bb, pad:, :] = cqkv_ref[bb]
        y = convw_ref[0:1, :] * xext[bb, lo:lo + tc, :]
        for w in range(1, CONV_W):
            y = y + convw_ref[w:w + 1, :] * xext[bb, lo + w:lo + w + tc, :]
        y = _silu(y)
        yq, yk, yv = y[:, :C_W], y[:, C_W:2 * C_W], y[:, 2 * C_W:]
        qn.append(yq * lax.rsqrt(head_sumsq(yq) + L2_EPS) * (D_C ** -0.5))
        kn.append(yk * lax.rsqrt(head_sumsq(yk) + L2_EPS))
        sm = small_ref[bb]
        log_a = jnp.where(is_dec, -jnp.exp(alog_ref[...]) * _softplus(sm + dtb_ref[...]), 0.0)
        g_full = sum(_dot(tri, part) for part in _split3(log_a))
        gc.append(sum(_dot(part, eg_ref[...]) for part in _split3(g_full)))
        bc.append(sum(_dot(part, eb_ref[...]) for part in _split3(_sigmoid(sm))))
        egc = jnp.exp(gc[bb])
        xk.append(bc[bb] * egc * kn[bb])
        vb.append(bc[bb] * yv)
        qe.append(egc * qn[bb])

    ii = lax.broadcasted_iota(jnp.int32, (CHUNK, LANES), 0)
    ln = lax.broadcasted_iota(jnp.int32, (CHUNK, LANES), 1)
    jn = ln % D_C
    incl2 = ii >= jn
    strict2 = ii > jn
    eye2 = ii == jn
    left = ln < D_C
    eye2f = jnp.where(eye2, 1.0, 0.0)

    def same_block(n):
        return (ii // n) == (jn // n)
    r128 = lax.broadcasted_iota(jnp.int32, (LANES, LANES), 0)
    c128 = lax.broadcasted_iota(jnp.int32, (LANES, LANES), 1)
    on_diag_blocks = (r128 < D_C) == (c128 < D_C)
    eye128 = jnp.where(r128 == c128, 1.0, 0.0).astype(BF16)

    keep_left = jnp.where(left, 1.0, 0.0).astype(BF16)
    keep_right = jnp.where(left, 0.0, 1.0).astype(BF16)

    def bdiag(x):
        xb = x.astype(BF16)
        return jnp.concatenate([xb * keep_left, xb * keep_right], axis=0)

    units = [(bb, r, p) for bb in range(n_bb) for r in range(n_ch) for p in range(n_pair)]
    lanes_bp = [(bb, p) for bb in range(n_bb) for p in range(n_pair)]

    def rows_of(r):
        return slice(r * CHUNK, (r + 1) * CHUNK)

    def tile_of(p):
        return slice(p * LANES, (p + 1) * LANES)

    def sub(x, u):
        bb, r, p = u
        return x[bb][rows_of(r), tile_of(p)]

    g_last = {(bb, r): gc[bb][(r + 1) * CHUNK - 1:(r + 1) * CHUNK, :] for bb in range(n_bb) for r in range(n_ch)}
    kd = {br: kn[br[0]][rows_of(br[1])] * jnp.exp(g - gc[br[0]][rows_of(br[1])]) for br, g in g_last.items()}
    eg_last = {br: jnp.exp(g) for br, g in g_last.items()}
    dec, r_k, kdt = {}, {}, {}
    for u in units:
        gcp = sub(gc, u)
        g_row = jnp.sum(jnp.where(eye2, gcp, 0.0), axis=0, keepdims=True)
        dec[u] = jnp.where(incl2, jnp.exp(jnp.where(incl2, gcp - g_row, 0.0)), 0.0)
        kp = sub(kn, u)
        r_k[u] = _dot_nt(jnp.concatenate([sub(qn, u), kp], axis=0).astype(BF16), bdiag(kp).astype(BF16))
    for bb, r, p in units:
        kdt[bb, r, p] = _dot_nt(eye128, bdiag(kd[bb, r][:, tile_of(p)]).astype(BF16)).astype(BF16)
    a_mat = {u: sub(bc, u) * jnp.where(strict2, dec[u], 0.0) * r_k[u][CHUNK:] for u in units}
    leaf = GDN_LEAF
    n_leaf = {u: jnp.where(strict2 & same_block(leaf), -a_mat[u], 0.0) for u in units}
    pw = {u: _dot(n_leaf[u].astype(BF16), bdiag(n_leaf[u])) for u in units}
    tm_ = {u: eye2f + n_leaf[u] for u in units}
    st = {u: _dot(jnp.concatenate([tm_[u], pw[u]], axis=0).astype(BF16), bdiag(pw[u])) for u in units}
    tm_ = {u: tm_[u] + st[u][:CHUNK] for u in units}
    fin = {u: _dot(tm_[u].astype(BF16), bdiag(st[u][CHUNK:])) for u in units}
    tm_ = {u: tm_[u] + fin[u] for u in units}
    size = leaf
    while size < CHUNK:
        coupling = strict2 & same_block(2 * size) & jnp.logical_not(same_block(size))
        cd = {u: _dot(jnp.where(coupling, a_mat[u], 0.0).astype(BF16), bdiag(tm_[u])) for u in units}
        dcd = {u: _dot(tm_[u].astype(BF16), bdiag(cd[u])) for u in units}
        tm_ = {u: tm_[u] - dcd[u] for u in units}
        size *= 2
    wu = {u: _dot(tm_[u].astype(BF16),
                  jnp.concatenate([bdiag(sub(xk, u)), bdiag(sub(vb, u))], axis=1).astype(BF16)) for u in units}
    w_b = {u: wu[u][:, :LANES].astype(BF16) for u in units}
    qkd = {u: (dec[u] * r_k[u][:CHUNK]).astype(BF16) for u in units}

    states = {bp: s_scr[bp[0], bp[1]] for bp in lanes_bp}
    o_tiles = {}
    for r in range(n_ch):
        res = {(bb, p): _dot(jnp.concatenate([sub(qe, (bb, r, p)).astype(BF16), w_b[bb, r, p]], axis=0),
                             states[bb, p].astype(BF16)) for bb, p in lanes_bp}
        u = {(bb, p): wu[bb, r, p][:, LANES:] - res[bb, p][CHUNK:] for bb, p in lanes_bp}
        upd = {(bb, p): _dot(kdt[bb, r, p], jnp.concatenate([u[bb, p], u[bb, p]], axis=0).astype(BF16))
               for bb, p in lanes_bp}
        o_inner = {(bb, p): _dot(qkd[bb, r, p], bdiag(u[bb, p]).astype(BF16)) for bb, p in lanes_bp}
        states = {(bb, p): eg_last[bb, r][:, tile_of(p)] * states[bb, p] + jnp.where(on_diag_blocks, upd[bb, p], 0.0)
                  for bb, p in lanes_bp}
        for bb, p in lanes_bp:
            o_tiles[bb, r, p] = res[bb, p][:CHUNK] + o_inner[bb, p]
    for bb, p in lanes_bp:
        s_scr[bb, p] = states[bb, p]
    for bb in range(n_bb):
        o = jnp.concatenate([jnp.concatenate([o_tiles[bb, r, p] for p in range(n_pair)], axis=1)
                             for r in range(n_ch)], axis=0)
        ms = head_sumsq(o) * (1.0 / D_C)
        oc_ref[bb] = o * lax.rsqrt(ms + NORM_EPS) * onorm_ref[...] * _silu(gate_ref[bb])

    @pl.when(j == pl.num_programs(1) - 1)
    def _():
        for bb, p in lanes_bp:
            sout_ref[bb, 2 * p] = states[bb, p][:D_C, :D_C]
            sout_ref[bb, 2 * p + 1] = states[bb, p][D_C:, D_C:]


def _gdn_constants(tc):
    i = jnp.arange(tc, dtype=jnp.int32)
    tri = ((i[None, :] <= i[:, None]) & (i[None, :] // CHUNK == i[:, None] // CHUNK)).astype(BF16)
    l = jnp.arange(LANES, dtype=jnp.int32)
    block_ones = (l[:, None] // D_C == l[None, :] // D_C).astype(BF16)
    col_head = jnp.arange(C_W, dtype=jnp.int32)[None, :] // D_C
    e_beta = (l[:, None] == col_head).astype(BF16)
    e_g = (l[:, None] == col_head + _SMALL_DECAY0).astype(BF16)
    return tri, block_ones, e_beta, e_g


def _gdn(cqkv, small, gate, conv_prev, s0, conv_w, alog_vec, dtb_vec, onorm_tiled):
    b, t, w3 = cqkv.shape
    tc = min(4 * CHUNK, t)
    nb = GDN_BATCH_BLOCK if b % GDN_BATCH_BLOCK == 0 else 1
    consts = _gdn_constants(tc)
    body = functools.partial(_gdn_body, tc=tc)
    blk = lambda bi, j: (bi, j, 0)
    per_b3 = lambda bi, j: (bi, 0, 0)
    per_b4 = lambda bi, j: (bi, 0, 0, 0)
    const2 = lambda bi, j: (0, 0)
    return pl.pallas_call(
        body,
        grid=(b // nb, t // tc),
        in_specs=[pl.BlockSpec((nb, tc, w3), blk),
                  pl.BlockSpec((nb, tc, LANES), blk),
                  pl.BlockSpec((nb, tc, C_W), blk),
                  pl.BlockSpec((nb, CONV_W - 1, w3), per_b3),
                  pl.BlockSpec((nb, H_C, D_C, D_C), per_b4),
                  pl.BlockSpec(conv_w.shape, const2),
                  pl.BlockSpec((1, LANES), const2),
                  pl.BlockSpec((1, LANES), const2),
                  pl.BlockSpec((1, C_W), const2)]
                 + [pl.BlockSpec(c.shape, const2) for c in consts],
        out_specs=[pl.BlockSpec((nb, tc, C_W), blk),
                   pl.BlockSpec((nb, H_C, D_C, D_C), per_b4)],
        out_shape=[jax.ShapeDtypeStruct((b, t, C_W), F32),
                   jax.ShapeDtypeStruct((b, H_C, D_C, D_C), F32)],
        scratch_shapes=[pltpu.VMEM((nb, tc + 8, w3), F32), pltpu.VMEM((nb, H_C // 2, LANES, LANES), F32)],
        compiler_params=_params(("parallel", "arbitrary")),
        name="gdn",
    )(cqkv, small, gate, conv_prev, s0, conv_w, alog_vec, dtb_vec, onorm_tiled, *consts)


def _top2_gates(logits):
    lane = lax.broadcasted_iota(jnp.int32, logits.shape, 1).astype(F32)
    low = -3.0e38
    lg = jnp.where(lane < N_EXPERTS, logits, low)
    m1 = jnp.max(lg, axis=-1, keepdims=True)
    i1 = jnp.min(jnp.where(lg == m1, lane, float(LANES)), axis=-1, keepdims=True)
    lg2 = jnp.where(lane == i1, low, lg)
    m2 = jnp.max(lg2, axis=-1, keepdims=True)
    i2 = jnp.min(jnp.where(lg2 == m2, lane, float(LANES)), axis=-1, keepdims=True)
    e2 = jnp.exp(m2 - m1)
    den = 1.0 + e2
    return jnp.where(lane == i1, 1.0 / den, 0.0) + jnp.where(lane == i2, e2 / den, 0.0)


def _outproj_body(x_ref, oa_ref, ob_ref, oc_ref, w_ref, g_ref, *rest, with_router):
    if with_router:
        rhi_ref, rlo_ref, xo_ref, h_ref, gates_ref = rest
    else:
        xo_ref, h_ref = rest
    y = (_dot(oa_ref[...].astype(BF16), w_ref[0:HEAD_W, :])
         + _dot(ob_ref[...].astype(BF16), w_ref[HEAD_W:2 * HEAD_W, :])
         + _dot(oc_ref[...].astype(BF16), w_ref[2 * HEAD_W:, :]))
    x = x_ref[...] + y
    xo_ref[...] = x
    hf = _rms(x, g_ref[...])
    hb = hf.astype(BF16)
    h_ref[...] = hb
    if with_router:
        lo = (hf - hb.astype(F32)).astype(BF16)
        logits = _dot(hb, rhi_ref[...]) + _dot(lo, rhi_ref[...]) + _dot(hb, rlo_ref[...])
        gates_ref[...] = _top2_gates(logits)


def _outproj(x, oa, ob, oc, w_out, g, router=None):
    n, d = x.shape
    tm = min(ROW_TILE, n)
    row = lambda i: (i, 0)
    const = lambda i: (0, 0)
    in_specs = [pl.BlockSpec((tm, d), row), pl.BlockSpec((tm, HEAD_W), row), pl.BlockSpec((tm, HEAD_W), row),
                pl.BlockSpec((tm, C_W), row), pl.BlockSpec(w_out.shape, const), pl.BlockSpec((1, d), const)]
    out_specs = [pl.BlockSpec((tm, d), row), pl.BlockSpec((tm, d), row)]
    out_shape = [jax.ShapeDtypeStruct((n, d), F32), jax.ShapeDtypeStruct((n, d), BF16)]
    args = [x, oa, ob, oc, w_out, g]
    if router is not None:
        in_specs += [pl.BlockSpec(router[0].shape, const), pl.BlockSpec(router[1].shape, const)]
        out_specs.append(pl.BlockSpec((tm, LANES), row))
        out_shape.append(jax.ShapeDtypeStruct((n, LANES), F32))
        args += list(router)
    return pl.pallas_call(
        functools.partial(_outproj_body, with_router=router is not None),
        grid=(n // tm,),
        in_specs=in_specs, out_specs=out_specs, out_shape=out_shape,
        compiler_params=_params(("parallel",)),
        name="outproj",
    )(*args)


def _ffn_body(h_ref, x_ref, wg_ref, wu_ref, wd_ref, o_ref, acc_ref, *, ff_chunk):
    h = h_ref[...]
    d_ff = wg_ref.shape[1]
    acc_ref[...] = x_ref[...]
    for c0 in range(0, d_ff, ff_chunk):
        a = _silu(_dot(h, wg_ref[:, c0:c0 + ff_chunk])) * _dot(h, wu_ref[:, c0:c0 + ff_chunk])
        acc_ref[...] += _dot(a.astype(BF16), wd_ref[c0:c0 + ff_chunk, :])
    o_ref[...] = acc_ref[...]


def _ffn_dense(h, x, wg, wu, wd):
    n, d = x.shape
    d_ff = wg.shape[1]
    tm = min(ROW_TILE, n)
    ff_chunk = 256 if d_ff % 256 == 0 else LANES
    row = lambda i: (i, 0)
    const = lambda i: (0, 0)
    return pl.pallas_call(
        functools.partial(_ffn_body, ff_chunk=ff_chunk),
        grid=(n // tm,),
        in_specs=[pl.BlockSpec((tm, d), row), pl.BlockSpec((tm, d), row),
                  pl.BlockSpec(wg.shape, const), pl.BlockSpec(wu.shape, const), pl.BlockSpec(wd.shape, const)],
        out_specs=pl.BlockSpec((tm, d), row),
        out_shape=jax.ShapeDtypeStruct((n, d), F32),
        scratch_shapes=[pltpu.VMEM((tm, d), F32)],
        compiler_params=_params(("parallel",)),
        name="ffn_dense",
    )(h, x, wg, wu, wd)


def _moe_body(pre_ref, h_ref, gates_ref, gates_t_ref, wg_ref, wu_ref, wd_ref, *rest, tg, n_f, final_norm, tiles):
    if final_norm:
        x_ref, gfin_ref, y_ref, rank_col, rank_row, hs, eo = rest
    else:
        y_ref, rank_col, rank_row, hs, eo = rest
    g = pl.program_id(0)
    e = pl.program_id(1)
    f = pl.program_id(2)
    tb = MOE_TOKEN_BLOCK
    n_tb = tg // tb
    pre0 = (g * N_EXPERTS + e) * (n_tb + 1)
    unit = tiles[-1]
    n_units = (pre_ref[pre0 + n_tb] + unit - 1) // unit
    units_full = tiles[0] // unit
    n_full = n_units // units_full
    rem = n_units % units_full

    def for_tiles(fn):
        def full(t, carry):
            fn(pl.multiple_of(t * tiles[0], tiles[0]), tiles[0])
            return carry

        lax.fori_loop(0, n_full, full, 0)
        for tm in tiles[1:]:
            bit = tm // unit
            higher = (units_full - 1) & ~(2 * bit - 1)

            @pl.when((rem & bit) != 0)
            def _(tm=tm, higher=higher):
                fn(pl.multiple_of(n_full * tiles[0] + (rem & higher) * unit, unit), tm)

    @pl.when((e == 0) & (f == 0))
    def _():
        y_ref[...] = jnp.zeros_like(y_ref)
        ii = lax.broadcasted_iota(jnp.int32, (tb, tb), 0)
        jj = lax.broadcasted_iota(jnp.int32, (tb, tb), 1)
        lower = jnp.where(ii > jj, 1.0, 0.0).astype(BF16)
        upper = jnp.where(ii < jj, 1.0, 0.0).astype(BF16)
        carry_c = jnp.zeros((1, LANES), F32)
        carry_r = jnp.zeros((N_EXPERTS, 1), F32)
        for b in range(n_tb):
            mc = jnp.where(gates_ref[b * tb:(b + 1) * tb, :] > 0.0, 1.0, 0.0)
            rank_col[b * tb:(b + 1) * tb, :] = _dot(lower, mc.astype(BF16)) + carry_c
            carry_c = carry_c + jnp.sum(mc, axis=0, keepdims=True)
            mr = jnp.where(gates_t_ref[:, b * tb:(b + 1) * tb] > 0.0, 1.0, 0.0)
            rank_row[:, b * tb:(b + 1) * tb] = _dot(mr.astype(BF16), upper) + carry_r
            carry_r = carry_r + jnp.sum(mr, axis=1, keepdims=True)

    win = MOE_WINDOW

    def all_block_windows(align, product, apply):
        base = [(pre_ref[pre0 + b] // align) * align for b in range(n_tb)]
        n_win = [(pre_ref[pre0 + b + 1] - base[b] + win - 1) // win for b in range(n_tb)]
        first = [product(b, pl.multiple_of(base[b], align)) for b in range(n_tb)]
        for b in range(n_tb):
            apply(b, pl.multiple_of(base[b], align), first[b])
        for b in range(n_tb):
            def body(k, carry, b=b):
                r0 = pl.multiple_of(base[b] + k * win, align)
                apply(b, r0, product(b, r0))
                return carry

            lax.fori_loop(1, n_win[b], body, 0)

    @pl.when(f == 0)
    def _():
        def clear(k, carry):
            r0 = pl.multiple_of(k * unit, unit)
            hs[pl.ds(r0, unit), :] = jnp.zeros((unit, hs.shape[1]), BF16)
            return carry

        lax.fori_loop(0, n_units + MOE_SLACK // unit, clear, 0)
        tail = pl.multiple_of(n_units * unit, unit)
        eo[pl.ds(tail, MOE_SLACK), :] = jnp.zeros((MOE_SLACK, eo.shape[1]), F32)

        def gathered(b, r0):
            cols = slice(b * tb, (b + 1) * tb)
            gr = gates_t_ref[pl.ds(e, 1), cols]
            rr = jnp.where(gr > 0.0, rank_row[pl.ds(e, 1), cols], -1.0)
            rows = (r0 + lax.broadcasted_iota(jnp.int32, (win, tb), 0)).astype(F32)
            return _dot(jnp.where(rr == rows, 1.0, 0.0).astype(BF16), h_ref[cols, :])

        def add_rows(b, r0, got):
            hs[pl.ds(r0, win), :] = (hs[pl.ds(r0, win), :].astype(F32) + got).astype(BF16)

        all_block_windows(MOE_BF16_ROWS, gathered, add_rows)

    def expert(r0, tm):
        x = hs[pl.ds(r0, tm), :]
        a = _silu(_dot(x, wg_ref[...])) * _dot(x, wu_ref[...])
        part = _dot(a.astype(BF16), wd_ref[...])

        @pl.when(f == 0)
        def _():
            eo[pl.ds(r0, tm), :] = part

        @pl.when(f > 0)
        def _():
            eo[pl.ds(r0, tm), :] += part

    for_tiles(expert)

    @pl.when(f == n_f - 1)
    def _():
        lane = lax.broadcasted_iota(jnp.int32, (tb, LANES), 1)

        def scattered(b, r0):
            rows = slice(b * tb, (b + 1) * tb)
            gc = jnp.sum(jnp.where(lane == e, gates_ref[rows, :], 0.0), axis=-1, keepdims=True)
            rc = jnp.sum(jnp.where(lane == e, rank_col[rows, :], 0.0), axis=-1, keepdims=True)
            rc = jnp.where(gc > 0.0, rc, -1.0)
            cols = (r0 + lax.broadcasted_iota(jnp.int32, (tb, win), 1)).astype(F32)
            return _dot(jnp.where(rc == cols, gc, 0.0).astype(BF16), eo[pl.ds(r0, win), :].astype(BF16))

        def add_tokens(b, r0, got):
            y_ref[b * tb:(b + 1) * tb, :] += got

        all_block_windows(MOE_F32_ROWS, scattered, add_tokens)

    if final_norm:
        @pl.when((e == pl.num_programs(1) - 1) & (f == n_f - 1))
        def _():
            for b in range(n_tb):
                rows = slice(b * tb, (b + 1) * tb)
                y_ref[rows, :] = _rms(x_ref[rows, :] + y_ref[rows, :], gfin_ref[...])


def _moe_prefix(gates, tg):
    n = gates.shape[0]
    n_tb = tg // MOE_TOKEN_BLOCK
    routed = (gates[:, :N_EXPERTS] > 0.0).astype(jnp.int32)
    per_block = routed.reshape(n // tg, n_tb, MOE_TOKEN_BLOCK, N_EXPERTS).sum(axis=2)
    run = jnp.cumsum(per_block, axis=1)
    pre = jnp.concatenate([jnp.zeros_like(run[:, :1]), run], axis=1)
    return jnp.transpose(pre, (0, 2, 1)).reshape(-1)


def _moe(h, gates, wg, wu, wd, tg, x=None, g_final=None):
    n, d = h.shape
    n_e, _, d_ff = wg.shape
    n_f = 2
    ffh = d_ff // n_f
    final_norm = x is not None
    tiles = tuple(t for t in MOE_TILES if t <= tg)
    body = functools.partial(_moe_body, tg=tg, n_f=n_f, final_norm=final_norm, tiles=tiles)
    once = pl.Buffered(1)
    extra_specs, extra_args = [], []
    if final_norm:
        extra_specs = [pl.BlockSpec((tg, d), lambda g, e, f, c: (g, 0), pipeline_mode=once),
                       pl.BlockSpec((1, d), lambda g, e, f, c: (0, 0))]
        extra_args = [x, g_final]
    grid_spec = pltpu.PrefetchScalarGridSpec(
        num_scalar_prefetch=1,
        grid=(n // tg, n_e, n_f),
        in_specs=[pl.BlockSpec((tg, d), lambda g, e, f, c: (g, 0), pipeline_mode=once),
                  pl.BlockSpec((tg, LANES), lambda g, e, f, c: (g, 0), pipeline_mode=once),
                  pl.BlockSpec((N_EXPERTS, tg), lambda g, e, f, c: (0, g), pipeline_mode=once),
                  pl.BlockSpec((None, d, ffh), lambda g, e, f, c: (e, 0, f)),
                  pl.BlockSpec((None, d, ffh), lambda g, e, f, c: (e, 0, f)),
                  pl.BlockSpec((None, ffh, d), lambda g, e, f, c: (e, f, 0))] + extra_specs,
        out_specs=pl.BlockSpec((tg, d), lambda g, e, f, c: (g, 0), pipeline_mode=once),
        scratch_shapes=[pltpu.VMEM((tg, LANES), F32), pltpu.VMEM((N_EXPERTS, tg), F32),
                        pltpu.VMEM((tg + MOE_SLACK, d), BF16), pltpu.VMEM((tg + MOE_SLACK, d), F32)],
    )
    return pl.pallas_call(
        body,
        grid_spec=grid_spec,
        out_shape=jax.ShapeDtypeStruct((n, d), F32),
        compiler_params=_params(("parallel", "arbitrary", "arbitrary")),
        name="moe",
    )(_moe_prefix(gates, tg), h, gates, gates[:, :N_EXPERTS].T, wg, wu, wd, *extra_args)


def _final_body(x_ref, y_ref, g_ref, o_ref):
    o_ref[...] = _rms(x_ref[...] + y_ref[...], g_ref[...])


def _final_norm(x, y, g):
    n, d = x.shape
    tm = min(ROW_TILE, n)
    row = lambda i: (i, 0)
    return pl.pallas_call(
        _final_body,
        grid=(n // tm,),
        in_specs=[pl.BlockSpec((tm, d), row), pl.BlockSpec((tm, d), row), pl.BlockSpec((1, d), lambda i: (0, 0))],
        out_specs=pl.BlockSpec((tm, d), row),
        out_shape=jax.ShapeDtypeStruct((n, d), F32),
        compiler_params=_params(("parallel",)),
        name="final_norm",
    )(x, y, g)


def _t5_bucket(rel):
    nb = T5_BUCKETS // 2
    max_exact = nb // 2
    ret = jnp.where(rel > 0, nb, 0)
    n = jnp.abs(rel)
    nf = jnp.maximum(n, 1).astype(F32)
    large = max_exact + (jnp.log(nf / max_exact) / math.log(T5_MAX_DIST / max_exact) * (nb - max_exact)).astype(jnp.int32)
    large = jnp.minimum(large, nb - 1)
    return ret + jnp.where(n < max_exact, n, large)


def _t5_table(t5_bias, q_pos, k_pos):
    bias = _lookup(t5_bias, _t5_bucket(k_pos[None, :] - q_pos[:, None]))
    mask = (k_pos[None, :] // CHUNK) <= (q_pos[:, None] // CHUNK)
    return bias, mask[None]


def _lookup(table, idx):
    onehot = jax.nn.one_hot(idx, table.shape[0], dtype=F32)
    return jnp.einsum("qkn,nh->hqk", onehot, table.astype(F32), precision=lax.Precision.HIGHEST)


def _band_table(rel_bias):
    qi = jnp.arange(CHUNK, dtype=jnp.int32)
    kj = jnp.arange(BAND, dtype=jnp.int32) - B_LEFT_CHUNKS * CHUNK
    rel = jnp.clip(kj[None, :] - qi[:, None], -B_REL_CLIP, B_REL_CLIP) + B_REL_CLIP
    base = jnp.swapaxes(_lookup(rel_bias, rel), 1, 2) * LOG2E
    halves = [jnp.pad(base, ((0, 0), (c * CHUNK, (1 - c) * CHUNK), (0, 0)), constant_values=NEG_INF)
              for c in range(2)]
    return jnp.concatenate(halves, axis=2)


def _lane_vec(v, offset):
    return jnp.zeros((1, LANES), F32).at[0, offset:offset + v.shape[0]].set(v.astype(F32))


def _moe_group(n):
    for tg in (2048, 1024, 512, 256):
        if n % tg == 0:
            return tg
    raise ValueError(f"token count {n} is not a multiple of 256")


def kernel(x_prompt, x_sample, cache_a_k, cache_a_v, cache_b_k, cache_b_v, cache_c_conv, state_c_ssm, w_in, w_out, norm_mix, norm_ffn, norm_final, lam_qk, subln_a, t5_bias, rel_bias_b, conv_c, a_log_c, dt_bias_c, onorm_c, ffn_gate, ffn_up, ffn_down, moe_router, moe_gate, moe_up, moe_down):
    depth = w_in.shape[0]
    bp, sp, d = x_prompt.shape
    bs, ss, _ = x_sample.shape
    past = cache_a_k.shape[2]
    nb_cache = cache_b_k.shape[2]
    assert ss == CHUNK and nb_cache == B_LEFT_CHUNKS * CHUNK and sp >= B_LEFT_CHUNKS * CHUNK

    xp = x_prompt.reshape(bp * sp, d)
    xs = x_sample.reshape(bs * ss, d)
    blk = min(A_BLOCK, sp)

    pos_blk = jnp.arange(blk, dtype=jnp.int32)
    far_bias = t5_bias[_t5_bucket(jnp.int32(-(blk + 1)))].astype(F32)
    bd, md = _t5_table(t5_bias, pos_blk, pos_blk)
    bias_diag = jnp.where(md, bd - far_bias[:, None, None], NEG_INF)
    bsub, _ = _t5_table(t5_bias, blk + pos_blk, pos_blk)
    bias_sub = bsub - far_bias[:, None, None]
    q_pos_s = past + jnp.arange(ss, dtype=jnp.int32)
    bfull, mfull = _t5_table(t5_bias, q_pos_s, jnp.arange(past + ss, dtype=jnp.int32))
    bias_s = jnp.where(mfull, bfull, NEG_INF)
    bias_s_t = jnp.swapaxes(bias_s, 1, 2) * LOG2E
    bias_s_cache = bias_s_t[:, :past]
    bias_s_new = jnp.pad(bias_s_t[:, past:], ((0, 0), (0, LANES - ss), (0, 0)), constant_values=NEG_INF)

    feature_major = lambda c: jnp.transpose(c, (0, 1, 3, 4, 2)).reshape(depth, bs, HEAD_W, past)
    cache_a_kt, cache_a_vt = feature_major(cache_a_k), feature_major(cache_a_v)

    zeros_conv = jnp.zeros((bp, CONV_W - 1, 3 * C_W), F32)
    zeros_state = jnp.zeros((bp, H_C, D_C, D_C), F32)

    p_states = [[] for _ in range(6)]
    s_states = [[] for _ in range(6)]
    yp = ys = None
    g_fin = norm_final.reshape(1, d)
    finals = []
    for l in range(depth):
        if yp is not None:
            xp, xs, yp, ys = xp + yp, xs + ys, None, None
        w = w_in[l]
        n_main = 6 * HEAD_W + 3 * C_W
        w_main = jnp.concatenate([w[:, :n_main], w[:, n_main + 2 * H_C:]], axis=1).astype(BF16)
        w_small = jnp.zeros((d, LANES), F32).at[:, :2 * H_C].set(w[:, n_main:n_main + 2 * H_C]).astype(BF16)
        w_out_l = w_out[l].astype(BF16)
        g_mix = norm_mix[l].reshape(1, d)
        g_ffn = norm_ffn[l].reshape(1, d)
        lam_init = 0.8 - 0.6 * math.exp(-0.3 * l)
        lq = lam_qk[l].astype(F32)
        lam = (jnp.exp(jnp.sum(lq[0] * lq[1])) - jnp.exp(jnp.sum(lq[2] * lq[3])) + lam_init).reshape(1)
        subln = subln_a[l].reshape(1, DV_A)
        band_bias = _band_table(rel_bias_b[l])
        alog_vec = _lane_vec(a_log_c[l], _SMALL_DECAY0)
        dtb_vec = _lane_vec(dt_bias_c[l], _SMALL_DECAY0)
        onorm = jnp.tile(onorm_c[l].astype(F32), H_C).reshape(1, C_W)
        is_moe = l % 2 == 1
        if is_moe:
            r = jnp.zeros((d, LANES), F32).at[:, :N_EXPERTS].set(moe_router[l // 2])
            r_hi = r.astype(BF16)
            router = (r_hi, (r - r_hi.astype(F32)).astype(BF16))
            e_wg, e_wu, e_wd = (moe_gate[l // 2].astype(BF16), moe_up[l // 2].astype(BF16),
                                moe_down[l // 2].astype(BF16))
        else:
            router = None
            f_wg, f_wu, f_wd = (ffn_gate[l // 2].astype(BF16), ffn_up[l // 2].astype(BF16),
                                ffn_down[l // 2].astype(BF16))

        new_x = []
        for is_prompt, x in ((True, xp), (False, xs)):
            b, t = (bp, sp) if is_prompt else (bs, ss)
            aq, ak, av, bq, bk, bv, cqkv, cgate, csmall = _inproj(x, g_mix, w_main, w_small)
            r3 = lambda a: a.reshape(b, t, a.shape[-1])
            if is_prompt:
                oa = _attn_a_prompt(lam, r3(aq), r3(ak), r3(av), jnp.swapaxes(bias_diag, 1, 2) * LOG2E,
                                    jnp.swapaxes(bias_sub, 1, 2) * LOG2E, subln.reshape(DV_A, 1), 1.0 - lam_init)
                ob = _band_attn(r3(bq), r3(bk), r3(bv), band_bias, B_LEFT_CHUNKS * CHUNK)
                conv_prev, s0 = zeros_conv, zeros_state
            else:
                oa = _attn_a_sample(lam, r3(aq), cache_a_kt, cache_a_vt, l, r3(ak), r3(av),
                                    bias_s_cache, bias_s_new, subln.reshape(DV_A, 1), 1.0 - lam_init)
                kb = jnp.concatenate([cache_b_k[l].reshape(b, nb_cache, HEAD_W), r3(bk)], axis=1)
                vb = jnp.concatenate([cache_b_v[l].reshape(b, nb_cache, HEAD_W), r3(bv)], axis=1)
                ob = _band_attn(r3(bq), kb, vb, band_bias, 0)
                conv_prev, s0 = cache_c_conv[l], state_c_ssm[l]
            oc, s_new = _gdn(r3(cqkv), r3(csmall), r3(cgate), conv_prev, s0, conv_c[l], alog_vec, dtb_vec, onorm)
            n = b * t
            res = _outproj(x, oa.reshape(n, HEAD_W), ob.reshape(n, HEAD_W), oc.reshape(n, C_W), w_out_l, g_ffn,
                           router)
            if is_moe:
                x_new, h2, gates = res
                if l == depth - 1:
                    finals.append(_moe(h2, gates, e_wg, e_wu, e_wd, _moe_group(n), x=x_new, g_final=g_fin))
                    new_x.append((x_new, None))
                else:
                    new_x.append((x_new, _moe(h2, gates, e_wg, e_wu, e_wd, _moe_group(n))))
            else:
                x_new, h2 = res
                new_x.append((_ffn_dense(h2, x_new, f_wg, f_wu, f_wd), None))
            keep = min(B_LEFT_CHUNKS * CHUNK, t)
            states = (ak.reshape(b, t, H_A, 2 * DQK_A), av.reshape(b, t, H_A, DV_A),
                      bk.reshape(b, t, H_B, D_B)[:, t - keep:], bv.reshape(b, t, H_B, D_B)[:, t - keep:],
                      r3(cqkv)[:, t - (CONV_W - 1):], s_new)
            for i in range(6):
                (p_states if is_prompt else s_states)[i].append(states[i])
        (xp, yp), (xs, ys) = new_x

    if finals:
        y_prompt, y_sample = finals[0].reshape(bp, sp, d), finals[1].reshape(bs, ss, d)
    else:
        if yp is None:
            yp, ys = jnp.zeros_like(xp), jnp.zeros_like(xs)
        y_prompt = _final_norm(xp, yp, g_fin).reshape(bp, sp, d)
        y_sample = _final_norm(xs, ys, g_fin).reshape(bs, ss, d)
    p_out = [jnp.stack(s, axis=0) for s in p_states]
    s_out = [jnp.stack(s, axis=0) for s in s_states]
    return (y_prompt, y_sample, *p_out, *s_out)
```

```python
import functools
import math

import jax
import jax.numpy as jnp
from jax import lax
from jax.experimental import pallas as pl
from jax.experimental.pallas import tpu as pltpu

F32 = jnp.float32
BF16 = jnp.bfloat16

CHUNK = 64
H_A = 4
DQK_A = 32
DV_A = 64
H_B = 4
D_B = 64
B_LEFT_CHUNKS = 8
B_REL_CLIP = 128
H_C = 8
D_C = 64
CONV_W = 4
T5_BUCKETS = 32
T5_MAX_DIST = 128
N_EXPERTS = 8
NORM_EPS = 1e-6
L2_EPS = 1e-6
NEG_INF = -1e30
HEAD_W = 256
C_W = H_C * D_C
BAND = (B_LEFT_CHUNKS + 1) * CHUNK
LANES = 128
LOG2E = math.log2(math.e)
VMEM_LIMIT = 56 * 1024 * 1024

A_BLOCK = 256
ROW_TILE = 512
GDN_BATCH_BLOCK = 2
GDN_LEAF = 8
MOE_TILES = (512, 256, 128, 64)
MOE_TOKEN_BLOCK = 256
MOE_WINDOW = 96
MOE_SLACK = 128
MOE_F32_ROWS = 8
MOE_BF16_ROWS = 16


def _params(sem):
    return pltpu.CompilerParams(dimension_semantics=sem, vmem_limit_bytes=VMEM_LIMIT)


def _rms(x, g):
    ms = jnp.mean(x * x, axis=-1, keepdims=True)
    return x * lax.rsqrt(ms + NORM_EPS) * g


def _sigmoid(x):
    return 1.0 / (1.0 + jnp.exp(-x))


def _silu(x):
    return x * _sigmoid(x)


def _softplus(x):
    return jnp.maximum(x, 0.0) + jnp.log1p(jnp.exp(-jnp.abs(x)))


def _dot(a, b):
    return jnp.dot(a, b, preferred_element_type=F32)


def _dot_nt(a, b):
    return lax.dot_general(a, b, (((1,), (1,)), ((), ())), preferred_element_type=F32)


def _split3(x):
    x1 = x.astype(BF16)
    r1 = x - x1.astype(F32)
    x2 = r1.astype(BF16)
    x3 = (r1 - x2.astype(F32)).astype(BF16)
    return x1, x2, x3


def _inproj_body(x_ref, g_ref, w_ref, ws_ref, aq_ref, ak_ref, av_ref, bq_ref, bk_ref, bv_ref,
                 cqkv_ref, cgate_ref, csmall_ref):
    h = _rms(x_ref[...], g_ref[...]).astype(BF16)
    col = 0
    for ref in (aq_ref, ak_ref, av_ref, bq_ref, bk_ref, bv_ref, cqkv_ref, cgate_ref):
        width = ref.shape[-1]
        ref[...] = _dot(h, w_ref[:, col:col + width])
        col += width
    csmall_ref[...] = _dot(h, ws_ref[...])


def _inproj(x, g, w_main, w_small):
    n, d = x.shape
    tm = min(ROW_TILE, n)
    widths = (HEAD_W,) * 6 + (3 * C_W, C_W, LANES)
    row = lambda i: (i, 0)
    const = lambda i: (0, 0)
    return pl.pallas_call(
        _inproj_body,
        grid=(n // tm,),
        in_specs=[pl.BlockSpec((tm, d), row), pl.BlockSpec((1, d), const),
                  pl.BlockSpec(w_main.shape, const), pl.BlockSpec(w_small.shape, const)],
        out_specs=[pl.BlockSpec((tm, w), row) for w in widths],
        out_shape=[jax.ShapeDtypeStruct((n, w), F32) for w in widths],
        compiler_params=_params(("parallel",)),
        name="inproj",
    )(x, g, w_main, w_small)


A_ACC_ROWS = DV_A + 16
A_GROUPS = LANES // DQK_A


def _attn_a_prompt_body(lam_ref, q_ref, k_ref, v_ref, bd_ref, bs_ref, subln_ref, o_ref,
                        kbf, vte, wq, m_scr, acc_scr, s_even, s_odd, *, blk, post_scale):
    i = pl.program_id(1)
    n_blk = k_ref.shape[0] // blk
    n_pair = H_A // 2

    @pl.when(i == 0)
    def _():
        ones_rows = jnp.where(lax.broadcasted_iota(jnp.int32, (A_ACC_ROWS - DV_A, blk), 0) == 0, 1.0, 0.0)
        for jb in range(n_blk):
            kbf[jb] = k_ref[jb * blk:(jb + 1) * blk, :].astype(BF16)
            vt = v_ref[jb * blk:(jb + 1) * blk, :].T
            for h in range(H_A):
                vte[jb, h, 0:DV_A, :] = vt[h * DV_A:(h + 1) * DV_A, :].astype(BF16)
                vte[jb, h, DV_A:, :] = ones_rows.astype(BF16)

    lam = lam_ref[0]
    qt = (q_ref[...] * (DQK_A ** -0.5 * LOG2E)).T
    grp = lax.broadcasted_iota(jnp.int32, (LANES, blk), 0) // DQK_A
    for p in range(n_pair):
        qtp = qt[p * LANES:(p + 1) * LANES, :]
        wq[p] = jnp.concatenate([jnp.where(grp == g, qtp, 0.0) for g in range(A_GROUPS)], axis=1).astype(BF16)
    m_scr[...] = jnp.full(m_scr.shape, NEG_INF, F32)
    acc_scr[...] = jnp.zeros(acc_scr.shape, F32)

    def scores(jb):
        return [_dot(kbf[jb, :, p * LANES:(p + 1) * LANES], wq[p]) for p in range(n_pair)]

    groups = range(n_pair * A_GROUPS)

    def softmax_stage(tile_of, bias_ref, off):
        alphas, pts = [], []
        for g in groups:
            p, gi = divmod(g, A_GROUPS)
            s = tile_of(p, gi)
            if bias_ref is not None:
                s = s + bias_ref[2 * p + gi // 2]
            if off is not None:
                s = s + off
            m_old = m_scr[g:g + 1, :]
            m_new = jnp.maximum(m_old, jnp.max(s, axis=0, keepdims=True))
            alphas.append(jnp.exp2(m_old - m_new))
            pts.append(jnp.exp2(s - m_new).astype(BF16))
            m_scr[g:g + 1, :] = m_new
        return alphas, pts

    def value_stage(jb, alphas, pts):
        pvs = [_dot(vte[jb, 2 * (g // A_GROUPS) + (g % A_GROUPS) // 2], pts[g]) for g in groups]
        for g in groups:
            acc_scr[g] = alphas[g] * acc_scr[g] + pvs[g]

    def consume(tile_of, jb, bias_ref, off):
        value_stage(jb, *softmax_stage(tile_of, bias_ref, off))

    def tiles(jb, bias_ref, off):
        s_alls = scores(jb)
        consume(lambda p, gi: s_alls[p][:, gi * blk:(gi + 1) * blk], jb, bias_ref, off)

    tiles(i, bd_ref, None)
    tiles(jnp.maximum(i - 1, 0), bs_ref, jnp.where(i >= 1, 0.0, NEG_INF))

    n_far = jnp.maximum(i - 1, 0)

    def put_scores(dst, jb):
        s_alls = scores(jnp.minimum(jb, n_far - 1))
        for p in range(n_pair):
            dst[p] = s_alls[p]

    def from_scratch(src):
        return lambda p, gi: src[p, :, gi * blk:(gi + 1) * blk]

    @pl.when(n_far > 0)
    def _():
        put_scores(s_even, 0)
        put_scores(s_odd, 1)

    def far_pair(t, carry):
        jb = 2 * t
        soft_a = softmax_stage(from_scratch(s_even), None, None)
        put_scores(s_even, jb + 2)
        soft_b = softmax_stage(from_scratch(s_odd), None, None)
        value_stage(jb, *soft_a)
        put_scores(s_odd, jb + 3)
        value_stage(jb + 1, *soft_b)
        return carry

    lax.fori_loop(0, n_far // 2, far_pair, 0)

    @pl.when(n_far % 2 == 1)
    def _():
        consume(from_scratch(s_even), n_far - 1, None, None)

    outs = []
    for h in range(H_A):
        g1 = (h // 2) * A_GROUPS + (h % 2) * 2
        a1 = acc_scr[g1]
        a2 = acc_scr[g1 + 1]
        o = a1[:DV_A] / a1[DV_A:DV_A + 1] - lam * (a2[:DV_A] / a2[DV_A:DV_A + 1])
        ms = jnp.mean(o * o, axis=0, keepdims=True)
        outs.append(o * lax.rsqrt(ms + NORM_EPS) * subln_ref[...] * post_scale)
    o_ref[...] = jnp.concatenate(outs, axis=0).T


def _attn_a_prompt(lam, q, k, v, bias_diag_t, bias_sub_t, subln_col, post_scale):
    b, s, w = q.shape
    blk = min(A_BLOCK, s)
    n_blk = s // blk
    body = functools.partial(_attn_a_prompt_body, blk=blk, post_scale=post_scale)
    return pl.pallas_call(
        body,
        grid=(b, n_blk),
        in_specs=[pl.BlockSpec(memory_space=pltpu.SMEM),
                  pl.BlockSpec((None, blk, w), lambda bi, i: (bi, i, 0)),
                  pl.BlockSpec((None, s, w), lambda bi, i: (bi, 0, 0)),
                  pl.BlockSpec((None, s, w), lambda bi, i: (bi, 0, 0)),
                  pl.BlockSpec(bias_diag_t.shape, lambda bi, i: (0, 0, 0)),
                  pl.BlockSpec(bias_sub_t.shape, lambda bi, i: (0, 0, 0)),
                  pl.BlockSpec((DV_A, 1), lambda bi, i: (0, 0))],
        out_specs=pl.BlockSpec((None, blk, w), lambda bi, i: (bi, i, 0)),
        out_shape=jax.ShapeDtypeStruct((b, s, w), F32),
        scratch_shapes=[pltpu.VMEM((n_blk, blk, w), BF16),
                        pltpu.VMEM((n_blk, H_A, A_ACC_ROWS, blk), BF16),
                        pltpu.VMEM((H_A // 2, LANES, A_GROUPS * blk), BF16),
                        pltpu.VMEM((H_A * 2, blk), F32),
                        pltpu.VMEM((H_A * 2, A_ACC_ROWS, blk), F32),
                        pltpu.VMEM((H_A // 2, blk, A_GROUPS * blk), F32),
                        pltpu.VMEM((H_A // 2, blk, A_GROUPS * blk), F32)],
        compiler_params=_params(("parallel", "arbitrary")),
        name="attn_a_prompt",
    )(lam, q, k, v, bias_diag_t, bias_sub_t, subln_col)


def _attn_a_sample_body(lam_ref, q_ref, kct_ref, vct_ref, kn_ref, vn_ref, bc_ref, bn_ref, subln_ref, o_ref,
                        *, post_scale):
    t = q_ref.shape[0]
    p_len = kct_ref.shape[1]
    n_pair = H_A // 2
    lam = lam_ref[0]

    ones_rows = jnp.where(lax.broadcasted_iota(jnp.int32, (A_ACC_ROWS - DV_A, LANES), 0) == 0, 1.0, 0.0).astype(BF16)
    ones_cache = jnp.where(lax.broadcasted_iota(jnp.int32, (A_ACC_ROWS - DV_A, p_len), 0) == 0, 1.0, 0.0).astype(BF16)
    vte_c = [jnp.concatenate([vct_ref[h * DV_A:(h + 1) * DV_A, :].astype(BF16), ones_cache], axis=0)
             for h in range(H_A)]
    row_pad = jnp.zeros((LANES - t, q_ref.shape[1]), F32)
    vt_n = jnp.concatenate([vn_ref[...], row_pad], axis=0).T
    vte_n = [jnp.concatenate([vt_n[h * DV_A:(h + 1) * DV_A, :].astype(BF16), ones_rows], axis=0) for h in range(H_A)]
    kn_p = jnp.concatenate([kn_ref[...], row_pad], axis=0).astype(BF16)

    qt = jnp.concatenate([q_ref[...] * (DQK_A ** -0.5 * LOG2E), row_pad], axis=0).T
    grp = lax.broadcasted_iota(jnp.int32, (LANES, t), 0) // DQK_A
    s_c, s_n = [], []
    for p in range(n_pair):
        qtp = qt[p * LANES:(p + 1) * LANES, 0:t]
        w = jnp.concatenate([jnp.where(grp == g, qtp, 0.0) for g in range(A_GROUPS)], axis=1).astype(BF16)
        s_c.append(_dot(kct_ref[p * LANES:(p + 1) * LANES, :].T.astype(BF16), w))
        s_n.append(_dot(kn_p[:, p * LANES:(p + 1) * LANES], w))
    pcs, pns = [], []
    for h in range(H_A):
        cols = slice((h % 2) * 2 * t, (h % 2 + 1) * 2 * t)
        sc = s_c[h // 2][:, cols] + jnp.concatenate([bc_ref[h], bc_ref[h]], axis=1)
        sn = s_n[h // 2][:, cols] + jnp.concatenate([bn_ref[h], bn_ref[h]], axis=1)
        mx = jnp.maximum(jnp.max(sc, axis=0, keepdims=True), jnp.max(sn, axis=0, keepdims=True))
        pcs.append(jnp.exp2(sc - mx).astype(BF16))
        pns.append(jnp.exp2(sn - mx).astype(BF16))
    accs = [_dot(vte_c[h], pcs[h]) + _dot(vte_n[h], pns[h]) for h in range(H_A)]
    outs = []
    for a in accs:
        on = a[:DV_A] / a[DV_A:DV_A + 1]
        o = on[:, 0:t] - lam * on[:, t:2 * t]
        ms = jnp.mean(o * o, axis=0, keepdims=True)
        outs.append(o * lax.rsqrt(ms + NORM_EPS) * subln_ref[...] * post_scale)
    ot = jnp.concatenate(outs, axis=0)
    ot = jnp.concatenate([ot, jnp.zeros((ot.shape[0], LANES - t), F32)], axis=1)
    o_ref[...] = ot.T[0:t, :]


def _attn_a_sample(lam, q, kct, vct, layer, kn, vn, bias_c_t, bias_n_t, subln_col, post_scale):
    b, t, w = q.shape
    p = kct.shape[3]
    assert 2 * t == LANES and p % LANES == 0
    cache_spec = pl.BlockSpec((None, None, w, p), lambda bi: (layer, bi, 0, 0))
    body = functools.partial(_attn_a_sample_body, post_scale=post_scale)
    per_b = lambda bi: (bi, 0, 0)
    const3 = lambda bi: (0, 0, 0)
    return pl.pallas_call(
        body,
        grid=(b,),
        in_specs=[pl.BlockSpec(memory_space=pltpu.SMEM),
                  pl.BlockSpec((None, t, w), per_b),
                  cache_spec, cache_spec,
                  pl.BlockSpec((None, t, w), per_b), pl.BlockSpec((None, t, w), per_b),
                  pl.BlockSpec(bias_c_t.shape, const3, pipeline_mode=pl.Buffered(1)),
                  pl.BlockSpec(bias_n_t.shape, const3),
                  pl.BlockSpec((DV_A, 1), lambda bi: (0, 0))],
        out_specs=pl.BlockSpec((None, t, w), per_b),
        out_shape=jax.ShapeDtypeStruct((b, t, w), F32),
        compiler_params=_params(("parallel",)),
        name="attn_a_sample",
    )(lam, q, kct, vct, kn, vn, bias_c_t, bias_n_t, subln_col)


B_PAIR = 2 * CHUNK
B_UNION = BAND + CHUNK
B_UNION_BLOCKS = B_UNION // LANES
B_ACC_ROWS = D_B + 16


def _band_body(q_ref, k_ref, v_ref, bias_ref, o_ref, kbf, vte, *, n_grp, n_invalid):
    i = pl.program_id(1)
    n_kblk = k_ref.shape[0] // LANES

    @pl.when(i == 0)
    def _():
        ones_rows = jnp.where(lax.broadcasted_iota(jnp.int32, (B_ACC_ROWS - D_B, LANES), 0) == 0, 1.0, 0.0)
        for jb in range(n_kblk):
            kbf[jb] = k_ref[jb * LANES:(jb + 1) * LANES, :].astype(BF16)
            vt = v_ref[jb * LANES:(jb + 1) * LANES, :].T
            for h in range(H_B):
                vte[jb, h, 0:D_B, :] = vt[h * D_B:(h + 1) * D_B, :].astype(BF16)
                vte[jb, h, D_B:, :] = ones_rows.astype(BF16)

    top = lax.broadcasted_iota(jnp.int32, (LANES, B_PAIR), 0) < D_B
    key_row = lax.broadcasted_iota(jnp.int32, (B_UNION, B_PAIR), 0)

    unroll = 2 if n_grp % 2 == 0 else 1

    def groups(gt, carry):
        gl = [gt * unroll + k for k in range(unroll)]
        g = [i * n_grp + x for x in gl]
        r0 = [pl.multiple_of(x * B_PAIR, B_PAIR) for x in gl]
        blocks = [[jnp.maximum(x + t - n_invalid // LANES, 0) for t in range(B_UNION_BLOCKS)] for x in g]
        s_pairs = []
        for k in range(unroll):
            qt = (q_ref[pl.ds(r0[k], B_PAIR), :] * (D_B ** -0.5 * LOG2E)).T
            kun = jnp.concatenate([kbf[jb] for jb in blocks[k]], axis=0)
            for p in range(H_B // 2):
                qtp = qt[p * LANES:(p + 1) * LANES, :]
                w = jnp.concatenate([jnp.where(top, qtp, 0.0), jnp.where(top, 0.0, qtp)], axis=1).astype(BF16)
                s_pairs.append(_dot(kun[:, p * LANES:(p + 1) * LANES], w))
        pts = []
        for k in range(unroll):
            for h in range(H_B):
                s = s_pairs[k * (H_B // 2) + h // 2][:, (h % 2) * B_PAIR:(h % 2 + 1) * B_PAIR] + bias_ref[h]
                if n_invalid:
                    s = jnp.where(g[k] * B_PAIR + key_row >= n_invalid, s, NEG_INF)
                pts.append(jnp.exp2(s - jnp.max(s, axis=0, keepdims=True)).astype(BF16))
        accs = [_dot(jnp.concatenate([vte[jb, h] for jb in blocks[k]], axis=1), pts[k * H_B + h])
                for k in range(unroll) for h in range(H_B)]
        for k in range(unroll):
            ot = jnp.concatenate([a[:D_B] / a[D_B:D_B + 1] for a in accs[k * H_B:(k + 1) * H_B]], axis=0)
            o_ref[pl.ds(r0[k], B_PAIR), :] = ot.T
        return carry

    lax.fori_loop(0, n_grp // unroll, groups, 0)


def _band_attn(q, k, v, bias_t, n_invalid):
    b, t, w = q.shape
    assert n_invalid % LANES == 0
    tq = -(-t // B_PAIR) * B_PAIR
    if tq != t:
        q = jnp.pad(q, ((0, 0), (0, tq - t), (0, 0)))
    tk = tq + B_LEFT_CHUNKS * CHUNK - n_invalid
    if k.shape[1] != tk:
        k = jnp.pad(k, ((0, 0), (0, tk - k.shape[1]), (0, 0)))
        v = jnp.pad(v, ((0, 0), (0, tk - v.shape[1]), (0, 0)))
    qb = min(8 * CHUNK, tq)
    n_kblk = tk // LANES
    body = functools.partial(_band_body, n_grp=qb // B_PAIR, n_invalid=n_invalid)
    out = pl.pallas_call(
        body,
        grid=(b, tq // qb),
        in_specs=[pl.BlockSpec((None, qb, w), lambda bi, i: (bi, i, 0)),
                  pl.BlockSpec((None, tk, w), lambda bi, i: (bi, 0, 0)),
                  pl.BlockSpec((None, tk, w), lambda bi, i: (bi, 0, 0)),
                  pl.BlockSpec(bias_t.shape, lambda bi, i: (0, 0, 0))],
        out_specs=pl.BlockSpec((None, qb, w), lambda bi, i: (bi, i, 0)),
        out_shape=jax.ShapeDtypeStruct((b, tq, w), F32),
        scratch_shapes=[pltpu.VMEM((n_kblk, LANES, w), BF16),
                        pltpu.VMEM((n_kblk, H_B, B_ACC_ROWS, LANES), BF16)],
        compiler_params=_params(("parallel", "arbitrary")),
        name="band_attn",
    )(q, k, v, bias_t)
    return out[:, :t] if tq != t else out


_SMALL_DECAY0 = H_C


def _gdn_body(cqkv_ref, small_ref, gate_ref, cprev_ref, s0_ref, convw_ref, alog_ref, dtb_ref, onorm_ref,
              tri_ref, bo_ref, eb_ref, eg_ref, oc_ref, sout_ref, xext, s_scr, *, tc):
    j = pl.program_id(1)
    n_bb = cqkv_ref.shape[0]
    n_ch = tc // CHUNK
    n_pair = H_C // 2
    pad = 8
    lo = pad - (CONV_W - 1)

    @pl.when(j == 0)
    def _():
        z = jnp.zeros((D_C, D_C), F32)
        for bb in range(n_bb):
            xext[bb, lo:pad, :] = cprev_ref[bb]
            for p in range(n_pair):
                s_scr[bb, p] = jnp.concatenate([jnp.concatenate([s0_ref[bb, 2 * p], z], axis=1),
                                                jnp.concatenate([z, s0_ref[bb, 2 * p + 1]], axis=1)], axis=0)

    @pl.when(j > 0)
    def _():
        for bb in range(n_bb):
            xext[bb, lo:pad, :] = xext[bb, tc + lo:tc + pad, :]

    bo = bo_ref[...]
    tri = tri_ref[...]
    lane_s = lax.broadcasted_iota(jnp.int32, (tc, LANES), 1)
    is_dec = (lane_s >= _SMALL_DECAY0) & (lane_s < _SMALL_DECAY0 + H_C)

    def head_sumsq(x):
        x2 = (x * x).astype(BF16)
        return jnp.concatenate([_dot(x2[:, t * LANES:(t + 1) * LANES], bo) for t in range(n_pair)], axis=1)

    qn, kn, gc, bc, xk, vb, qe = [], [], [], [], [], [], []
    for bb in range(n_bb):
        xext[bb, pad:, :] = cqkv_ref[bb]
        y = convw_ref[0:1, :] * xext[bb, lo:lo + tc, :]
        for w in range(1, CONV_W):
            y = y + convw_ref[w:w + 1, :] * xext[bb, lo + w:lo + w + tc, :]
        y = _silu(y)
        yq, yk, yv = y[:, :C_W], y[:, C_W:2 * C_W], y[:, 2 * C_W:]
        qn.append(yq * lax.rsqrt(head_sumsq(yq) + L2_EPS) * (D_C ** -0.5))
        kn.append(yk * lax.rsqrt(head_sumsq(yk) + L2_EPS))
        sm = small_ref[bb]
        log_a = jnp.where(is_dec, -jnp.exp(alog_ref[...]) * _softplus(sm + dtb_ref[...]), 0.0)
        g_full = sum(_dot(tri, part) for part in _split3(log_a))
        gc.append(sum(_dot(part, eg_ref[...]) for part in _split3(g_full)))
        bc.append(sum(_dot(part, eb_ref[...]) for part in _split3(_sigmoid(sm))))
        egc = jnp.exp(gc[bb])
        xk.append(bc[bb] * egc * kn[bb])
        vb.append(bc[bb] * yv)
        qe.append(egc * qn[bb])

    ii = lax.broadcasted_iota(jnp.int32, (CHUNK, LANES), 0)
    ln = lax.broadcasted_iota(jnp.int32, (CHUNK, LANES), 1)
    jn = ln % D_C
    incl2 = ii >= jn
    strict2 = ii > jn
    eye2 = ii == jn
    left = ln < D_C
    eye2f = jnp.where(eye2, 1.0, 0.0)

    def same_block(n):
        return (ii // n) == (jn // n)
    r128 = lax.broadcasted_iota(jnp.int32, (LANES, LANES), 0)
    c128 = lax.broadcasted_iota(jnp.int32, (LANES, LANES), 1)
    on_diag_blocks = (r128 < D_C) == (c128 < D_C)
    eye128 = jnp.where(r128 == c128, 1.0, 0.0).astype(BF16)

    keep_left = jnp.where(left, 1.0, 0.0).astype(BF16)
    keep_right = jnp.where(left, 0.0, 1.0).astype(BF16)

    def bdiag(x):
        xb = x.astype(BF16)
        return jnp.concatenate([xb * keep_left, xb * keep_right], axis=0)

    units = [(bb, r, p) for bb in range(n_bb) for r in range(n_ch) for p in range(n_pair)]
    lanes_bp = [(bb, p) for bb in range(n_bb) for p in range(n_pair)]

    def rows_of(r):
        return slice(r * CHUNK, (r + 1) * CHUNK)

    def tile_of(p):
        return slice(p * LANES, (p + 1) * LANES)

    def sub(x, u):
        bb, r, p = u
        return x[bb][rows_of(r), tile_of(p)]

    g_last = {(bb, r): gc[bb][(r + 1) * CHUNK - 1:(r + 1) * CHUNK, :] for bb in range(n_bb) for r in range(n_ch)}
    kd = {br: kn[br[0]][rows_of(br[1])] * jnp.exp(g - gc[br[0]][rows_of(br[1])]) for br, g in g_last.items()}
    eg_last = {br: jnp.exp(g) for br, g in g_last.items()}
    dec, r_k, kdt = {}, {}, {}
    for u in units:
        gcp = sub(gc, u)
        g_row = jnp.sum(jnp.where(eye2, gcp, 0.0), axis=0, keepdims=True)
        dec[u] = jnp.where(incl2, jnp.exp(jnp.where(incl2, gcp - g_row, 0.0)), 0.0)
        kp = sub(kn, u)
        r_k[u] = _dot_nt(jnp.concatenate([sub(qn, u), kp], axis=0).astype(BF16), bdiag(kp).astype(BF16))
    for bb, r, p in units:
        kdt[bb, r, p] = _dot_nt(eye128, bdiag(kd[bb, r][:, tile_of(p)]).astype(BF16)).astype(BF16)
    a_mat = {u: sub(bc, u) * jnp.where(strict2, dec[u], 0.0) * r_k[u][CHUNK:] for u in units}
    leaf = GDN_LEAF
    n_leaf = {u: jnp.where(strict2 & same_block(leaf), -a_mat[u], 0.0) for u in units}
    pw = {u: _dot(n_leaf[u].astype(BF16), bdiag(n_leaf[u])) for u in units}
    tm_ = {u: eye2f + n_leaf[u] for u in units}
    st = {u: _dot(jnp.concatenate([tm_[u], pw[u]], axis=0).astype(BF16), bdiag(pw[u])) for u in units}
    tm_ = {u: tm_[u] + st[u][:CHUNK] for u in units}
    fin = {u: _dot(tm_[u].astype(BF16), bdiag(st[u][CHUNK:])) for u in units}
    tm_ = {u: tm_[u] + fin[u] for u in units}
    size = leaf
    while size < CHUNK:
        coupling = strict2 & same_block(2 * size) & jnp.logical_not(same_block(size))
        cd = {u: _dot(jnp.where(coupling, a_mat[u], 0.0).astype(BF16), bdiag(tm_[u])) for u in units}
        dcd = {u: _dot(tm_[u].astype(BF16), bdiag(cd[u])) for u in units}
        tm_ = {u: tm_[u] - dcd[u] for u in units}
        size *= 2
    wu = {u: _dot(tm_[u].astype(BF16),
                  jnp.concatenate([bdiag(sub(xk, u)), bdiag(sub(vb, u))], axis=1).astype(BF16)) for u in units}
    w_b = {u: wu[u][:, :LANES].astype(BF16) for u in units}
    qkd = {u: (dec[u] * r_k[u][:CHUNK]).astype(BF16) for u in units}

    states = {bp: s_scr[bp[0], bp[1]] for bp in lanes_bp}
    o_tiles = {}
    for r in range(n_ch):
        res = {(bb, p): _dot(jnp.concatenate([sub(qe, (bb, r, p)).astype(BF16), w_b[bb, r, p]], axis=0),
                             states[bb, p].astype(BF16)) for bb, p in lanes_bp}
        u = {(bb, p): wu[bb, r, p][:, LANES:] - res[bb, p][CHUNK:] for bb, p in lanes_bp}
        upd = {(bb, p): _dot(kdt[bb, r, p], jnp.concatenate([u[bb, p], u[bb, p]], axis=0).astype(BF16))
               for bb, p in lanes_bp}
        o_inner = {(bb, p): _dot(qkd[bb, r, p], bdiag(u[bb, p]).astype(BF16)) for bb, p in lanes_bp}
        states = {(bb, p): eg_last[bb, r][:, tile_of(p)] * states[bb, p] + jnp.where(on_diag_blocks, upd[bb, p], 0.0)
                  for bb, p in lanes_bp}
        for bb, p in lanes_bp:
            o_tiles[bb, r, p] = res[bb, p][:CHUNK] + o_inner[bb, p]
    for bb, p in lanes_bp:
        s_scr[bb, p] = states[bb, p]
    for bb in range(n_bb):
        o = jnp.concatenate([jnp.concatenate([o_tiles[bb, r, p] for p in range(n_pair)], axis=1)
                             for r in range(n_ch)], axis=0)
        ms = head_sumsq(o) * (1.0 / D_C)
        oc_ref[bb] = o * lax.rsqrt(ms + NORM_EPS) * onorm_ref[...] * _silu(gate_ref[bb])

    @pl.when(j == pl.num_programs(1) - 1)
    def _():
        for bb, p in lanes_bp:
            sout_ref[bb, 2 * p] = states[bb, p][:D_C, :D_C]
            sout_ref[bb, 2 * p + 1] = states[bb, p][D_C:, D_C:]


def _gdn_constants(tc):
    i = jnp.arange(tc, dtype=jnp.int32)
    tri = ((i[None, :] <= i[:, None]) & (i[None, :] // CHUNK == i[:, None] // CHUNK)).astype(BF16)
    l = jnp.arange(LANES, dtype=jnp.int32)
    block_ones = (l[:, None] // D_C == l[None, :] // D_C).astype(BF16)
    col_head = jnp.arange(C_W, dtype=jnp.int32)[None, :] // D_C
    e_beta = (l[:, None] == col_head).astype(BF16)
    e_g = (l[:, None] == col_head + _SMALL_DECAY0).astype(BF16)
    return tri, block_ones, e_beta, e_g


def _gdn(cqkv, small, gate, conv_prev, s0, conv_w, alog_vec, dtb_vec, onorm_tiled):
    b, t, w3 = cqkv.shape
    tc = min(4 * CHUNK, t)
    nb = GDN_BATCH_BLOCK if b % GDN_BATCH_BLOCK == 0 else 1
    consts = _gdn_constants(tc)
    body = functools.partial(_gdn_body, tc=tc)
    blk = lambda bi, j: (bi, j, 0)
    per_b3 = lambda bi, j: (bi, 0, 0)
    per_b4 = lambda bi, j: (bi, 0, 0, 0)
    const2 = lambda bi, j: (0, 0)
    return pl.pallas_call(
        body,
        grid=(b // nb, t // tc),
        in_specs=[pl.BlockSpec((nb, tc, w3), blk),
                  pl.BlockSpec((nb, tc, LANES), blk),
                  pl.BlockSpec((nb, tc, C_W), blk),
                  pl.BlockSpec((nb, CONV_W - 1, w3), per_b3),
                  pl.BlockSpec((nb, H_C, D_C, D_C), per_b4),
                  pl.BlockSpec(conv_w.shape, const2),
                  pl.BlockSpec((1, LANES), const2),
                  pl.BlockSpec((1, LANES), const2),
                  pl.BlockSpec((1, C_W), const2)]
                 + [pl.BlockSpec(c.shape, const2) for c in consts],
        out_specs=[pl.BlockSpec((nb, tc, C_W), blk),
                   pl.BlockSpec((nb, H_C, D_C, D_C), per_b4)],
        out_shape=[jax.ShapeDtypeStruct((b, t, C_W), F32),
                   jax.ShapeDtypeStruct((b, H_C, D_C, D_C), F32)],
        scratch_shapes=[pltpu.VMEM((nb, tc + 8, w3), F32), pltpu.VMEM((nb, H_C // 2, LANES, LANES), F32)],
        compiler_params=_params(("parallel", "arbitrary")),
        name="gdn",
    )(cqkv, small, gate, conv_prev, s0, conv_w, alog_vec, dtb_vec, onorm_tiled, *consts)


def _top2_gates(logits):
    lane = lax.broadcasted_iota(jnp.int32, logits.shape, 1).astype(F32)
    low = -3.0e38
    lg = jnp.where(lane < N_EXPERTS, logits, low)
    m1 = jnp.max(lg, axis=-1, keepdims=True)
    i1 = jnp.min(jnp.where(lg == m1, lane, float(LANES)), axis=-1, keepdims=True)
    lg2 = jnp.where(lane == i1, low, lg)
    m2 = jnp.max(lg2, axis=-1, keepdims=True)
    i2 = jnp.min(jnp.where(lg2 == m2, lane, float(LANES)), axis=-1, keepdims=True)
    e2 = jnp.exp(m2 - m1)
    den = 1.0 + e2
    return jnp.where(lane == i1, 1.0 / den, 0.0) + jnp.where(lane == i2, e2 / den, 0.0)


def _outproj_body(x_ref, oa_ref, ob_ref, oc_ref, w_ref, g_ref, *rest, with_router):
    if with_router:
        rhi_ref, rlo_ref, xo_ref, h_ref, gates_ref = rest
    else:
        xo_ref, h_ref = rest
    y = (_dot(oa_ref[...].astype(BF16), w_ref[0:HEAD_W, :])
         + _dot(ob_ref[...].astype(BF16), w_ref[HEAD_W:2 * HEAD_W, :])
         + _dot(oc_ref[...].astype(BF16), w_ref[2 * HEAD_W:, :]))
    x = x_ref[...] + y
    xo_ref[...] = x
    hf = _rms(x, g_ref[...])
    hb = hf.astype(BF16)
    h_ref[...] = hb
    if with_router:
        lo = (hf - hb.astype(F32)).astype(BF16)
        logits = _dot(hb, rhi_ref[...]) + _dot(lo, rhi_ref[...]) + _dot(hb, rlo_ref[...])
        gates_ref[...] = _top2_gates(logits)


def _outproj(x, oa, ob, oc, w_out, g, router=None):
    n, d = x.shape
    tm = min(ROW_TILE, n)
    row = lambda i: (i, 0)
    const = lambda i: (0, 0)
    in_specs = [pl.BlockSpec((tm, d), row), pl.BlockSpec((tm, HEAD_W), row), pl.BlockSpec((tm, HEAD_W), row),
                pl.BlockSpec((tm, C_W), row), pl.BlockSpec(w_out.shape, const), pl.BlockSpec((1, d), const)]
    out_specs = [pl.BlockSpec((tm, d), row), pl.BlockSpec((tm, d), row)]
    out_shape = [jax.ShapeDtypeStruct((n, d), F32), jax.ShapeDtypeStruct((n, d), BF16)]
    args = [x, oa, ob, oc, w_out, g]
    if router is not None:
        in_specs += [pl.BlockSpec(router[0].shape, const), pl.BlockSpec(router[1].shape, const)]
        out_specs.append(pl.BlockSpec((tm, LANES), row))
        out_shape.append(jax.ShapeDtypeStruct((n, LANES), F32))
        args += list(router)
    return pl.pallas_call(
        functools.partial(_outproj_body, with_router=router is not None),
        grid=(n // tm,),
        in_specs=in_specs, out_specs=out_specs, out_shape=out_shape,
        compiler_params=_params(("parallel",)),
        name="outproj",
    )(*args)


def _ffn_body(h_ref, x_ref, wg_ref, wu_ref, wd_ref, o_ref, acc_ref, *, ff_chunk):
    h = h_ref[...]
    d_ff = wg_ref.shape[1]
    acc_ref[...] = x_ref[...]
    for c0 in range(0, d_ff, ff_chunk):
        a = _silu(_dot(h, wg_ref[:, c0:c0 + ff_chunk])) * _dot(h, wu_ref[:, c0:c0 + ff_chunk])
        acc_ref[...] += _dot(a.astype(BF16), wd_ref[c0:c0 + ff_chunk, :])
    o_ref[...] = acc_ref[...]


def _ffn_dense(h, x, wg, wu, wd):
    n, d = x.shape
    d_ff = wg.shape[1]
    tm = min(ROW_TILE, n)
    ff_chunk = 256 if d_ff % 256 == 0 else LANES
    row = lambda i: (i, 0)
    const = lambda i: (0, 0)
    return pl.pallas_call(
        functools.partial(_ffn_body, ff_chunk=ff_chunk),
        grid=(n // tm,),
        in_specs=[pl.BlockSpec((tm, d), row), pl.BlockSpec((tm, d), row),
                  pl.BlockSpec(wg.shape, const), pl.BlockSpec(wu.shape, const), pl.BlockSpec(wd.shape, const)],
        out_specs=pl.BlockSpec((tm, d), row),
        out_shape=jax.ShapeDtypeStruct((n, d), F32),
        scratch_shapes=[pltpu.VMEM((tm, d), F32)],
        compiler_params=_params(("parallel",)),
        name="ffn_dense",
    )(h, x, wg, wu, wd)


def _moe_body(pre_ref, h_ref, gates_ref, gates_t_ref, wg_ref, wu_ref, wd_ref, *rest, tg, n_f, final_norm, tiles):
    if final_norm:
        x_ref, gfin_ref, y_ref, rank_col, rank_row, hs, eo = rest
    else:
        y_ref, rank_col, rank_row, hs, eo = rest
    g = pl.program_id(0)
    e = pl.program_id(1)
    f = pl.program_id(2)
    tb = MOE_TOKEN_BLOCK
    n_tb = tg // tb
    pre0 = (g * N_EXPERTS + e) * (n_tb + 1)
    unit = tiles[-1]
    n_units = (pre_ref[pre0 + n_tb] + unit - 1) // unit
    units_full = tiles[0] // unit
    n_full = n_units // units_full
    rem = n_units % units_full

    def for_tiles(fn):
        def full(t, carry):
            fn(pl.multiple_of(t * tiles[0], tiles[0]), tiles[0])
            return carry

        lax.fori_loop(0, n_full, full, 0)
        for tm in tiles[1:]:
            bit = tm // unit
            higher = (units_full - 1) & ~(2 * bit - 1)

            @pl.when((rem & bit) != 0)
            def _(tm=tm, higher=higher):
                fn(pl.multiple_of(n_full * tiles[0] + (rem & higher) * unit, unit), tm)

    @pl.when((e == 0) & (f == 0))
    def _():
        y_ref[...] = jnp.zeros_like(y_ref)
        ii = lax.broadcasted_iota(jnp.int32, (tb, tb), 0)
        jj = lax.broadcasted_iota(jnp.int32, (tb, tb), 1)
        lower = jnp.where(ii > jj, 1.0, 0.0).astype(BF16)
        upper = jnp.where(ii < jj, 1.0, 0.0).astype(BF16)
        carry_c = jnp.zeros((1, LANES), F32)
        carry_r = jnp.zeros((N_EXPERTS, 1), F32)
        for b in range(n_tb):
            mc = jnp.where(gates_ref[b * tb:(b + 1) * tb, :] > 0.0, 1.0, 0.0)
            rank_col[b * tb:(b + 1) * tb, :] = _dot(lower, mc.astype(BF16)) + carry_c
            carry_c = carry_c + jnp.sum(mc, axis=0, keepdims=True)
            mr = jnp.where(gates_t_ref[:, b * tb:(b + 1) * tb] > 0.0, 1.0, 0.0)
            rank_row[:, b * tb:(b + 1) * tb] = _dot(mr.astype(BF16), upper) + carry_r
            carry_r = carry_r + jnp.sum(mr, axis=1, keepdims=True)

    win = MOE_WINDOW

    def all_block_windows(align, product, apply):
        base = [(pre_ref[pre0 + b] // align) * align for b in range(n_tb)]
        n_win = [(pre_ref[pre0 + b + 1] - base[b] + win - 1) // win for b in range(n_tb)]
        first = [product(b, pl.multiple_of(base[b], align)) for b in range(n_tb)]
        for b in range(n_tb):
            apply(b, pl.multiple_of(base[b], align), first[b])
        for b in range(n_tb):
            def body(k, carry, b=b):
                r0 = pl.multiple_of(base[b] + k * win, align)
                apply(b, r0, product(b, r0))
                return carry

            lax.fori_loop(1, n_win[b], body, 0)

    @pl.when(f == 0)
    def _():
        def clear(k, carry):
            r0 = pl.multiple_of(k * unit, unit)
            hs[pl.ds(r0, unit), :] = jnp.zeros((unit, hs.shape[1]), BF16)
            return carry

        lax.fori_loop(0, n_units + MOE_SLACK // unit, clear, 0)
        tail = pl.multiple_of(n_units * unit, unit)
        eo[pl.ds(tail, MOE_SLACK), :] = jnp.zeros((MOE_SLACK, eo.shape[1]), F32)

        def gathered(b, r0):
            cols = slice(b * tb, (b + 1) * tb)
            gr = gates_t_ref[pl.ds(e, 1), cols]
            rr = jnp.where(gr > 0.0, rank_row[pl.ds(e, 1), cols], -1.0)
            rows = (r0 + lax.broadcasted_iota(jnp.int32, (win, tb), 0)).astype(F32)
            return _dot(jnp.where(rr == rows, 1.0, 0.0).astype(BF16), h_ref[cols, :])

        def add_rows(b, r0, got):
            hs[pl.ds(r0, win), :] = (hs[pl.ds(r0, win), :].astype(F32) + got).astype(BF16)

        all_block_windows(MOE_BF16_ROWS, gathered, add_rows)

    def expert(r0, tm):
        x = hs[pl.ds(r0, tm), :]
        a = _silu(_dot(x, wg_ref[...])) * _dot(x, wu_ref[...])
        part = _dot(a.astype(BF16), wd_ref[...])

        @pl.when(f == 0)
        def _():
            eo[pl.ds(r0, tm), :] = part

        @pl.when(f > 0)
        def _():
            eo[pl.ds(r0, tm), :] += part

    for_tiles(expert)

    @pl.when(f == n_f - 1)
    def _():
        lane = lax.broadcasted_iota(jnp.int32, (tb, LANES), 1)

        def scattered(b, r0):
            rows = slice(b * tb, (b + 1) * tb)
            gc = jnp.sum(jnp.where(lane == e, gates_ref[rows, :], 0.0), axis=-1, keepdims=True)
            rc = jnp.sum(jnp.where(lane == e, rank_col[rows, :], 0.0), axis=-1, keepdims=True)
            rc = jnp.where(gc > 0.0, rc, -1.0)
            cols = (r0 + lax.broadcasted_iota(jnp.int32, (tb, win), 1)).astype(F32)
            return _dot(jnp.where(rc == cols, gc, 0.0).astype(BF16), eo[pl.ds(r0, win), :].astype(BF16))

        def add_tokens(b, r0, got):
            y_ref[b * tb:(b + 1) * tb, :] += got

        all_block_windows(MOE_F32_ROWS, scattered, add_tokens)

    if final_norm:
        @pl.when((e == pl.num_programs(1) - 1) & (f == n_f - 1))
        def _():
            for b in range(n_tb):
                rows = slice(b * tb, (b + 1) * tb)
                y_ref[rows, :] = _rms(x_ref[rows, :] + y_ref[rows, :], gfin_ref[...])


def _moe_prefix(gates, tg):
    n = gates.shape[0]
    n_tb = tg // MOE_TOKEN_BLOCK
    routed = (gates[:, :N_EXPERTS] > 0.0).astype(jnp.int32)
    per_block = routed.reshape(n // tg, n_tb, MOE_TOKEN_BLOCK, N_EXPERTS).sum(axis=2)
    run = jnp.cumsum(per_block, axis=1)
    pre = jnp.concatenate([jnp.zeros_like(run[:, :1]), run], axis=1)
    return jnp.transpose(pre, (0, 2, 1)).reshape(-1)


def _moe(h, gates, wg, wu, wd, tg, x=None, g_final=None):
    n, d = h.shape
    n_e, _, d_ff = wg.shape
    n_f = 2
    ffh = d_ff // n_f
    final_norm = x is not None
    tiles = tuple(t for t in MOE_TILES if t <= tg)
    body = functools.partial(_moe_body, tg=tg, n_f=n_f, final_norm=final_norm, tiles=tiles)
    once = pl.Buffered(1)
    extra_specs, extra_args = [], []
    if final_norm:
        extra_specs = [pl.BlockSpec((tg, d), lambda g, e, f, c: (g, 0), pipeline_mode=once),
                       pl.BlockSpec((1, d), lambda g, e, f, c: (0, 0))]
        extra_args = [x, g_final]
    grid_spec = pltpu.PrefetchScalarGridSpec(
        num_scalar_prefetch=1,
        grid=(n // tg, n_e, n_f),
        in_specs=[pl.BlockSpec((tg, d), lambda g, e, f, c: (g, 0), pipeline_mode=once),
                  pl.BlockSpec((tg, LANES), lambda g, e, f, c: (g, 0), pipeline_mode=once),
                  pl.BlockSpec((N_EXPERTS, tg), lambda g, e, f, c: (0, g), pipeline_mode=once),
                  pl.BlockSpec((None, d, ffh), lambda g, e, f, c: (e, 0, f)),
                  pl.BlockSpec((None, d, ffh), lambda g, e, f, c: (e, 0, f)),
                  pl.BlockSpec((None, ffh, d), lambda g, e, f, c: (e, f, 0))] + extra_specs,
        out_specs=pl.BlockSpec((tg, d), lambda g, e, f, c: (g, 0), pipeline_mode=once),
        scratch_shapes=[pltpu.VMEM((tg, LANES), F32), pltpu.VMEM((N_EXPERTS, tg), F32),
                        pltpu.VMEM((tg + MOE_SLACK, d), BF16), pltpu.VMEM((tg + MOE_SLACK, d), F32)],
    )
    return pl.pallas_call(
        body,
        grid_spec=grid_spec,
        out_shape=jax.ShapeDtypeStruct((n, d), F32),
        compiler_params=_params(("parallel", "arbitrary", "arbitrary")),
        name="moe",
    )(_moe_prefix(gates, tg), h, gates, gates[:, :N_EXPERTS].T, wg, wu, wd, *extra_args)


def _final_body(x_ref, y_ref, g_ref, o_ref):
    o_ref[...] = _rms(x_ref[...] + y_ref[...], g_ref[...])


def _final_norm(x, y, g):
    n, d = x.shape
    tm = min(ROW_TILE, n)
    row = lambda i: (i, 0)
    return pl.pallas_call(
        _final_body,
        grid=(n // tm,),
        in_specs=[pl.BlockSpec((tm, d), row), pl.BlockSpec((tm, d), row), pl.BlockSpec((1, d), lambda i: (0, 0))],
        out_specs=pl.BlockSpec((tm, d), row),
        out_shape=jax.ShapeDtypeStruct((n, d), F32),
        compiler_params=_params(("parallel",)),
        name="final_norm",
    )(x, y, g)


def _t5_bucket(rel):
    nb = T5_BUCKETS // 2
    max_exact = nb // 2
    ret = jnp.where(rel > 0, nb, 0)
    n = jnp.abs(rel)
    nf = jnp.maximum(n, 1).astype(F32)
    large = max_exact + (jnp.log(nf / max_exact) / math.log(T5_MAX_DIST / max_exact) * (nb - max_exact)).astype(jnp.int32)
    large = jnp.minimum(large, nb - 1)
    return ret + jnp.where(n < max_exact, n, large)


def _t5_table(t5_bias, q_pos, k_pos):
    bias = _lookup(t5_bias, _t5_bucket(k_pos[None, :] - q_pos[:, None]))
    mask = (k_pos[None, :] // CHUNK) <= (q_pos[:, None] // CHUNK)
    return bias, mask[None]


def _lookup(table, idx):
    onehot = jax.nn.one_hot(idx, table.shape[0], dtype=F32)
    return jnp.einsum("qkn,nh->hqk", onehot, table.astype(F32), precision=lax.Precision.HIGHEST)


def _band_table(rel_bias):
    qi = jnp.arange(CHUNK, dtype=jnp.int32)
    kj = jnp.arange(BAND, dtype=jnp.int32) - B_LEFT_CHUNKS * CHUNK
    rel = jnp.clip(kj[None, :] - qi[:, None], -B_REL_CLIP, B_REL_CLIP) + B_REL_CLIP
    base = jnp.swapaxes(_lookup(rel_bias, rel), 1, 2) * LOG2E
    halves = [jnp.pad(base, ((0, 0), (c * CHUNK, (1 - c) * CHUNK), (0, 0)), constant_values=NEG_INF)
              for c in range(2)]
    return jnp.concatenate(halves, axis=2)


def _lane_vec(v, offset):
    return jnp.zeros((1, LANES), F32).at[0, offset:offset + v.shape[0]].set(v.astype(F32))


def _moe_group(n):
    for tg in (2048, 1024, 512, 256):
        if n % tg == 0:
            return tg
    raise ValueError(f"token count {n} is not a multiple of 256")


def kernel(x_prompt, x_sample, cache_a_k, cache_a_v, cache_b_k, cache_b_v, cache_c_conv, state_c_ssm, w_in, w_out, norm_mix, norm_ffn, norm_final, lam_qk, subln_a, t5_bias, rel_bias_b, conv_c, a_log_c, dt_bias_c, onorm_c, ffn_gate, ffn_up, ffn_down, moe_router, moe_gate, moe_up, moe_down):
    depth = w_in.shape[0]
    bp, sp, d = x_prompt.shape
    bs, ss, _ = x_sample.shape
    past = cache_a_k.shape[2]
    nb_cache = cache_b_k.shape[2]
    assert ss == CHUNK and nb_cache == B_LEFT_CHUNKS * CHUNK and sp >= B_LEFT_CHUNKS * CHUNK

    xp = x_prompt.reshape(bp * sp, d)
    xs = x_sample.reshape(bs * ss, d)
    blk = min(A_BLOCK, sp)

    pos_blk = jnp.arange(blk, dtype=jnp.int32)
    far_bias = t5_bias[_t5_bucket(jnp.int32(-(blk + 1)))].astype(F32)
    bd, md = _t5_table(t5_bias, pos_blk, pos_blk)
    bias_diag = jnp.where(md, bd - far_bias[:, None, None], NEG_INF)
    bsub, _ = _t5_table(t5_bias, blk + pos_blk, pos_blk)
    bias_sub = bsub - far_bias[:, None, None]
    q_pos_s = past + jnp.arange(ss, dtype=jnp.int32)
    bfull, mfull = _t5_table(t5_bias, q_pos_s, jnp.arange(past + ss, dtype=jnp.int32))
    bias_s = jnp.where(mfull, bfull, NEG_INF)
    bias_s_t = jnp.swapaxes(bias_s, 1, 2) * LOG2E
    bias_s_cache = bias_s_t[:, :past]
    bias_s_new = jnp.pad(bias_s_t[:, past:], ((0, 0), (0, LANES - ss), (0, 0)), constant_values=NEG_INF)

    feature_major = lambda c: jnp.transpose(c, (0, 1, 3, 4, 2)).reshape(depth, bs, HEAD_W, past)
    cache_a_kt, cache_a_vt = feature_major(cache_a_k), feature_major(cache_a_v)

    zeros_conv = jnp.zeros((bp, CONV_W - 1, 3 * C_W), F32)
    zeros_state = jnp.zeros((bp, H_C, D_C, D_C), F32)

    p_states = [[] for _ in range(6)]
    s_states = [[] for _ in range(6)]
    yp = ys = None
    g_fin = norm_final.reshape(1, d)
    finals = []
    for l in range(depth):
        if yp is not None:
            xp, xs, yp, ys = xp + yp, xs + ys, None, None
        w = w_in[l]
        n_main = 6 * HEAD_W + 3 * C_W
        w_main = jnp.concatenate([w[:, :n_main], w[:, n_main + 2 * H_C:]], axis=1).astype(BF16)
        w_small = jnp.zeros((d, LANES), F32).at[:, :2 * H_C].set(w[:, n_main:n_main + 2 * H_C]).astype(BF16)
        w_out_l = w_out[l].astype(BF16)
        g_mix = norm_mix[l].reshape(1, d)
        g_ffn = norm_ffn[l].reshape(1, d)
        lam_init = 0.8 - 0.6 * math.exp(-0.3 * l)
        lq = lam_qk[l].astype(F32)
        lam = (jnp.exp(jnp.sum(lq[0] * lq[1])) - jnp.exp(jnp.sum(lq[2] * lq[3])) + lam_init).reshape(1)
        subln = subln_a[l].reshape(1, DV_A)
        band_bias = _band_table(rel_bias_b[l])
        alog_vec = _lane_vec(a_log_c[l], _SMALL_DECAY0)
        dtb_vec = _lane_vec(dt_bias_c[l], _SMALL_DECAY0)
        onorm = jnp.tile(onorm_c[l].astype(F32), H_C).reshape(1, C_W)
        is_moe = l % 2 == 1
        if is_moe:
            r = jnp.zeros((d, LANES), F32).at[:, :N_EXPERTS].set(moe_router[l // 2])
            r_hi = r.astype(BF16)
            router = (r_hi, (r - r_hi.astype(F32)).astype(BF16))
            e_wg, e_wu, e_wd = (moe_gate[l // 2].astype(BF16), moe_up[l // 2].astype(BF16),
                                moe_down[l // 2].astype(BF16))
        else:
            router = None
            f_wg, f_wu, f_wd = (ffn_gate[l // 2].astype(BF16), ffn_up[l // 2].astype(BF16),
                                ffn_down[l // 2].astype(BF16))

        new_x = []
        for is_prompt, x in ((True, xp), (False, xs)):
            b, t = (bp, sp) if is_prompt else (bs, ss)
            aq, ak, av, bq, bk, bv, cqkv, cgate, csmall = _inproj(x, g_mix, w_main, w_small)
            r3 = lambda a: a.reshape(b, t, a.shape[-1])
            if is_prompt:
                oa = _attn_a_prompt(lam, r3(aq), r3(ak), r3(av), jnp.swapaxes(bias_diag, 1, 2) * LOG2E,
                                    jnp.swapaxes(bias_sub, 1, 2) * LOG2E, subln.reshape(DV_A, 1), 1.0 - lam_init)
                ob = _band_attn(r3(bq), r3(bk), r3(bv), band_bias, B_LEFT_CHUNKS * CHUNK)
                conv_prev, s0 = zeros_conv, zeros_state
            else:
                oa = _attn_a_sample(lam, r3(aq), cache_a_kt, cache_a_vt, l, r3(ak), r3(av),
                                    bias_s_cache, bias_s_new, subln.reshape(DV_A, 1), 1.0 - lam_init)
                kb = jnp.concatenate([cache_b_k[l].reshape(b, nb_cache, HEAD_W), r3(bk)], axis=1)
                vb = jnp.concatenate([cache_b_v[l].reshape(b, nb_cache, HEAD_W), r3(bv)], axis=1)
                ob = _band_attn(r3(bq), kb, vb, band_bias, 0)
                conv_prev, s0 = cache_c_conv[l], state_c_ssm[l]
            oc, s_new = _gdn(r3(cqkv), r3(csmall), r3(cgate), conv_prev, s0, conv_c[l], alog_vec, dtb_vec, onorm)
            n = b * t
            res = _outproj(x, oa.reshape(n, HEAD_W), ob.reshape(n, HEAD_W), oc.reshape(n, C_W), w_out_l, g_ffn,
                           router)
            if is_moe:
                x_new, h2, gates = res
                if l == depth - 1:
                    finals.append(_moe(h2, gates, e_wg, e_wu, e_wd, _moe_group(n), x=x_new, g_final=g_fin))
                    new_x.append((x_new, None))
                else:
                    new_x.append((x_new, _moe(h2, gates, e_wg, e_wu, e_wd, _moe_group(n))))
            else:
                x_new, h2 = res
                new_x.append((_ffn_dense(h2, x_new, f_wg, f_wu, f_wd), None))
            keep = min(B_LEFT_CHUNKS * CHUNK, t)
            states = (ak.reshape(b, t, H_A, 2 * DQK_A), av.reshape(b, t, H_A, DV_A),
                      bk.reshape(b, t, H_B, D_B)[:, t - keep:], bv.reshape(b, t, H_B, D_B)[:, t - keep:],
                      r3(cqkv)[:, t - (CONV_W - 1):], s_new)
            for i in range(6):
                (p_states if is_prompt else s_states)[i].append(states[i])
        (xp, yp), (xs, ys) = new_x

    if finals:
        y_prompt, y_sample = finals[0].reshape(bp, sp, d), finals[1].reshape(bs, ss, d)
    else:
        if yp is None:
            yp, ys = jnp.zeros_like(xp), jnp.zeros_like(xs)
        y_prompt = _final_norm(xp, yp, g_fin).reshape(bp, sp, d)
        y_sample = _final_norm(xs, ys, g_fin).reshape(bs, ss, d)
    p_out = [jnp.stack(s, axis=0) for s in p_states]
    s_out = [jnp.stack(s, axis=0) for s in s_states]
    return (y_prompt, y_sample, *p_out, *s_out)
```

```python
import functools
import math

import jax
import jax.numpy as jnp
from jax import lax
from jax.experimental import pallas as pl
from jax.experimental.pallas import tpu as pltpu

F32 = jnp.float32
BF16 = jnp.bfloat16

CHUNK = 64
H_A = 4
DQK_A = 32
DV_A = 64
H_B = 4
D_B = 64
B_LEFT_CHUNKS = 8
B_REL_CLIP = 128
H_C = 8
D_C = 64
CONV_W = 4
T5_BUCKETS = 32
T5_MAX_DIST = 128
N_EXPERTS = 8
NORM_EPS = 1e-6
L2_EPS = 1e-6
NEG_INF = -1e30
HEAD_W = 256
C_W = H_C * D_C
BAND = (B_LEFT_CHUNKS + 1) * CHUNK
LANES = 128
LOG2E = math.log2(math.e)
VMEM_LIMIT = 56 * 1024 * 1024

A_BLOCK = 256
ROW_TILE = 512
GDN_BATCH_BLOCK = 2
GDN_LEAF = 8
MOE_TILES = (512, 256, 128, 64)
MOE_TOKEN_BLOCK = 256
MOE_WINDOW = 96
MOE_SLACK = 128
MOE_F32_ROWS = 8
MOE_BF16_ROWS = 16


def _params(sem):
    return pltpu.CompilerParams(dimension_semantics=sem, vmem_limit_bytes=VMEM_LIMIT)


def _rms(x, g):
    ms = jnp.mean(x * x, axis=-1, keepdims=True)
    return x * lax.rsqrt(ms + NORM_EPS) * g


def _sigmoid(x):
    return 1.0 / (1.0 + jnp.exp(-x))


def _silu(x):
    return x * _sigmoid(x)


def _softplus(x):
    return jnp.maximum(x, 0.0) + jnp.log1p(jnp.exp(-jnp.abs(x)))


def _dot(a, b):
    return jnp.dot(a, b, preferred_element_type=F32)


def _dot_nt(a, b):
    return lax.dot_general(a, b, (((1,), (1,)), ((), ())), preferred_element_type=F32)


def _split3(x):
    x1 = x.astype(BF16)
    r1 = x - x1.astype(F32)
    x2 = r1.astype(BF16)
    x3 = (r1 - x2.astype(F32)).astype(BF16)
    return x1, x2, x3


def _inproj_body(x_ref, g_ref, w_ref, ws_ref, aq_ref, ak_ref, av_ref, bq_ref, bk_ref, bv_ref,
                 cqkv_ref, cgate_ref, csmall_ref):
    h = _rms(x_ref[...], g_ref[...]).astype(BF16)
    col = 0
    for ref in (aq_ref, ak_ref, av_ref, bq_ref, bk_ref, bv_ref, cqkv_ref, cgate_ref):
        width = ref.shape[-1]
        ref[...] = _dot(h, w_ref[:, col:col + width])
        col += width
    csmall_ref[...] = _dot(h, ws_ref[...])


def _inproj(x, g, w_main, w_small):
    n, d = x.shape
    tm = min(ROW_TILE, n)
    widths = (HEAD_W,) * 6 + (3 * C_W, C_W, LANES)
    row = lambda i: (i, 0)
    const = lambda i: (0, 0)
    return pl.pallas_call(
        _inproj_body,
        grid=(n // tm,),
        in_specs=[pl.BlockSpec((tm, d), row), pl.BlockSpec((1, d), const),
                  pl.BlockSpec(w_main.shape, const), pl.BlockSpec(w_small.shape, const)],
        out_specs=[pl.BlockSpec((tm, w), row) for w in widths],
        out_shape=[jax.ShapeDtypeStruct((n, w), F32) for w in widths],
        compiler_params=_params(("parallel",)),
        name="inproj",
    )(x, g, w_main, w_small)


A_ACC_ROWS = DV_A + 16
A_GROUPS = LANES // DQK_A


def _attn_a_prompt_body(lam_ref, q_ref, k_ref, v_ref, bd_ref, bs_ref, subln_ref, o_ref,
                        kbf, vte, wq, m_scr, acc_scr, s_even, s_odd, *, blk, post_scale):
    i = pl.program_id(1)
    n_blk = k_ref.shape[0] // blk
    n_pair = H_A // 2

    @pl.when(i == 0)
    def _():
        ones_rows = jnp.where(lax.broadcasted_iota(jnp.int32, (A_ACC_ROWS - DV_A, blk), 0) == 0, 1.0, 0.0)
        for jb in range(n_blk):
            kbf[jb] = k_ref[jb * blk:(jb + 1) * blk, :].astype(BF16)
            vt = v_ref[jb * blk:(jb + 1) * blk, :].T
            for h in range(H_A):
                vte[jb, h, 0:DV_A, :] = vt[h * DV_A:(h + 1) * DV_A, :].astype(BF16)
                vte[jb, h, DV_A:, :] = ones_rows.astype(BF16)

    lam = lam_ref[0]
    qt = (q_ref[...] * (DQK_A ** -0.5 * LOG2E)).T
    grp = lax.broadcasted_iota(jnp.int32, (LANES, blk), 0) // DQK_A
    for p in range(n_pair):
        qtp = qt[p * LANES:(p + 1) * LANES, :]
        wq[p] = jnp.concatenate([jnp.where(grp == g, qtp, 0.0) for g in range(A_GROUPS)], axis=1).astype(BF16)
    m_scr[...] = jnp.full(m_scr.shape, NEG_INF, F32)
    acc_scr[...] = jnp.zeros(acc_scr.shape, F32)

    def scores(jb):
        return [_dot(kbf[jb, :, p * LANES:(p + 1) * LANES], wq[p]) for p in range(n_pair)]

    groups = range(n_pair * A_GROUPS)

    def softmax_stage(tile_of, bias_ref, off):
        alphas, pts = [], []
        for g in groups:
            p, gi = divmod(g, A_GROUPS)
            s = tile_of(p, gi)
            if bias_ref is not None:
                s = s + bias_ref[2 * p + gi // 2]
            if off is not None:
                s = s + off
            m_old = m_scr[g:g + 1, :]
            m_new = jnp.maximum(m_old, jnp.max(s, axis=0, keepdims=True))
            alphas.append(jnp.exp2(m_old - m_new))
            pts.append(jnp.exp2(s - m_new).astype(BF16))
            m_scr[g:g + 1, :] = m_new
        return alphas, pts

    def value_stage(jb, alphas, pts):
        pvs = [_dot(vte[jb, 2 * (g // A_GROUPS) + (g % A_GROUPS) // 2], pts[g]) for g in groups]
        for g in groups:
            acc_scr[g] = alphas[g] * acc_scr[g] + pvs[g]

    def consume(tile_of, jb, bias_ref, off):
        value_stage(jb, *softmax_stage(tile_of, bias_ref, off))

    def tiles(jb, bias_ref, off):
        s_alls = scores(jb)
        consume(lambda p, gi: s_alls[p][:, gi * blk:(gi + 1) * blk], jb, bias_ref, off)

    tiles(i, bd_ref, None)
    tiles(jnp.maximum(i - 1, 0), bs_ref, jnp.where(i >= 1, 0.0, NEG_INF))

    n_far = jnp.maximum(i - 1, 0)

    def put_scores(dst, jb):
        s_alls = scores(jnp.minimum(jb, n_far - 1))
        for p in range(n_pair):
            dst[p] = s_alls[p]

    def from_scratch(src):
        return lambda p, gi: src[p, :, gi * blk:(gi + 1) * blk]

    @pl.when(n_far > 0)
    def _():
        put_scores(s_even, 0)
        put_scores(s_odd, 1)

    def far_pair(t, carry):
        jb = 2 * t
        soft_a = softmax_stage(from_scratch(s_even), None, None)
        put_scores(s_even, jb + 2)
        soft_b = softmax_stage(from_scratch(s_odd), None, None)
        value_stage(jb, *soft_a)
        put_scores(s_odd, jb + 3)
        value_stage(jb + 1, *soft_b)
        return carry

    lax.fori_loop(0, n_far // 2, far_pair, 0)

    @pl.when(n_far % 2 == 1)
    def _():
        consume(from_scratch(s_even), n_far - 1, None, None)

    outs = []
    for h in range(H_A):
        g1 = (h // 2) * A_GROUPS + (h % 2) * 2
        a1 = acc_scr[g1]
        a2 = acc_scr[g1 + 1]
        o = a1[:DV_A] / a1[DV_A:DV_A + 1] - lam * (a2[:DV_A] / a2[DV_A:DV_A + 1])
        ms = jnp.mean(o * o, axis=0, keepdims=True)
        outs.append(o * lax.rsqrt(ms + NORM_EPS) * subln_ref[...] * post_scale)
    o_ref[...] = jnp.concatenate(outs, axis=0).T


def _attn_a_prompt(lam, q, k, v, bias_diag_t, bias_sub_t, subln_col, post_scale):
    b, s, w = q.shape
    blk = min(A_BLOCK, s)
    n_blk = s // blk
    body = functools.partial(_attn_a_prompt_body, blk=blk, post_scale=post_scale)
    return pl.pallas_call(
        body,
        grid=(b, n_blk),
        in_specs=[pl.BlockSpec(memory_space=pltpu.SMEM),
                  pl.BlockSpec((None, blk, w), lambda bi, i: (bi, i, 0)),
                  pl.BlockSpec((None, s, w), lambda bi, i: (bi, 0, 0)),
                  pl.BlockSpec((None, s, w), lambda bi, i: (bi, 0, 0)),
                  pl.BlockSpec(bias_diag_t.shape, lambda bi, i: (0, 0, 0)),
                  pl.BlockSpec(bias_sub_t.shape, lambda bi, i: (0, 0, 0)),
                  pl.BlockSpec((DV_A, 1), lambda bi, i: (0, 0))],
        out_specs=pl.BlockSpec((None, blk, w), lambda bi, i: (bi, i, 0)),
        out_shape=jax.ShapeDtypeStruct((b, s, w), F32),
        scratch_shapes=[pltpu.VMEM((n_blk, blk, w), BF16),
                        pltpu.VMEM((n_blk, H_A, A_ACC_ROWS, blk), BF16),
                        pltpu.VMEM((H_A // 2, LANES, A_GROUPS * blk), BF16),
                        pltpu.VMEM((H_A * 2, blk), F32),
                        pltpu.VMEM((H_A * 2, A_ACC_ROWS, blk), F32),
                        pltpu.VMEM((H_A // 2, blk, A_GROUPS * blk), F32),
                        pltpu.VMEM((H_A // 2, blk, A_GROUPS * blk), F32)],
        compiler_params=_params(("parallel", "arbitrary")),
        name="attn_a_prompt",
    )(lam, q, k, v, bias_diag_t, bias_sub_t, subln_col)


def _attn_a_sample_body(lam_ref, q_ref, kct_ref, vct_ref, kn_ref, vn_ref, bc_ref, bn_ref, subln_ref, o_ref,
                        *, post_scale):
    t = q_ref.shape[0]
    p_len = kct_ref.shape[1]
    n_pair = H_A // 2
    lam = lam_ref[0]

    ones_rows = jnp.where(lax.broadcasted_iota(jnp.int32, (A_ACC_ROWS - DV_A, LANES), 0) == 0, 1.0, 0.0).astype(BF16)
    ones_cache = jnp.where(lax.broadcasted_iota(jnp.int32, (A_ACC_ROWS - DV_A, p_len), 0) == 0, 1.0, 0.0).astype(BF16)
    vte_c = [jnp.concatenate([vct_ref[h * DV_A:(h + 1) * DV_A, :].astype(BF16), ones_cache], axis=0)
             for h in range(H_A)]
    row_pad = jnp.zeros((LANES - t, q_ref.shape[1]), F32)
    vt_n = jnp.concatenate([vn_ref[...], row_pad], axis=0).T
    vte_n = [jnp.concatenate([vt_n[h * DV_A:(h + 1) * DV_A, :].astype(BF16), ones_rows], axis=0) for h in range(H_A)]
    kn_p = jnp.concatenate([kn_ref[...], row_pad], axis=0).astype(BF16)

    qt = jnp.concatenate([q_ref[...] * (DQK_A ** -0.5 * LOG2E), row_pad], axis=0).T
    grp = lax.broadcasted_iota(jnp.int32, (LANES, t), 0) // DQK_A
    s_c, s_n = [], []
    for p in range(n_pair):
        qtp = qt[p * LANES:(p + 1) * LANES, 0:t]
        w = jnp.concatenate([jnp.where(grp == g, qtp, 0.0) for g in range(A_GROUPS)], axis=1).astype(BF16)
        s_c.append(_dot(kct_ref[p * LANES:(p + 1) * LANES, :].T.astype(BF16), w))
        s_n.append(_dot(kn_p[:, p * LANES:(p + 1) * LANES], w))
    pcs, pns = [], []
    for h in range(H_A):
        cols = slice((h % 2) * 2 * t, (h % 2 + 1) * 2 * t)
        sc = s_c[h // 2][:, cols] + jnp.concatenate([bc_ref[h], bc_ref[h]], axis=1)
        sn = s_n[h // 2][:, cols] + jnp.concatenate([bn_ref[h], bn_ref[h]], axis=1)
        mx = jnp.maximum(jnp.max(sc, axis=0, keepdims=True), jnp.max(sn, axis=0, keepdims=True))
        pcs.append(jnp.exp2(sc - mx).astype(BF16))
        pns.append(jnp.exp2(sn - mx).astype(BF16))
    accs = [_dot(vte_c[h], pcs[h]) + _dot(vte_n[h], pns[h]) for h in range(H_A)]
    outs = []
    for a in accs:
        on = a[:DV_A] / a[DV_A:DV_A + 1]
        o = on[:, 0:t] - lam * on[:, t:2 * t]
        ms = jnp.mean(o * o, axis=0, keepdims=True)
        outs.append(o * lax.rsqrt(ms + NORM_EPS) * subln_ref[...] * post_scale)
    ot = jnp.concatenate(outs, axis=0)
    ot = jnp.concatenate([ot, jnp.zeros((ot.shape[0], LANES - t), F32)], axis=1)
    o_ref[...] = ot.T[0:t, :]


def _attn_a_sample(lam, q, kct, vct, layer, kn, vn, bias_c_t, bias_n_t, subln_col, post_scale):
    b, t, w = q.shape
    p = kct.shape[3]
    assert 2 * t == LANES and p % LANES == 0
    cache_spec = pl.BlockSpec((None, None, w, p), lambda bi: (layer, bi, 0, 0))
    body = functools.partial(_attn_a_sample_body, post_scale=post_scale)
    per_b = lambda bi: (bi, 0, 0)
    const3 = lambda bi: (0, 0, 0)
    return pl.pallas_call(
        body,
        grid=(b,),
        in_specs=[pl.BlockSpec(memory_space=pltpu.SMEM),
                  pl.BlockSpec((None, t, w), per_b),
                  cache_spec, cache_spec,
                  pl.BlockSpec((None, t, w), per_b), pl.BlockSpec((None, t, w), per_b),
                  pl.BlockSpec(bias_c_t.shape, const3, pipeline_mode=pl.Buffered(1)),
                  pl.BlockSpec(bias_n_t.shape, const3),
                  pl.BlockSpec((DV_A, 1), lambda bi: (0, 0))],
        out_specs=pl.BlockSpec((None, t, w), per_b),
        out_shape=jax.ShapeDtypeStruct((b, t, w), F32),
        compiler_params=_params(("parallel",)),
        name="attn_a_sample",
    )(lam, q, kct, vct, kn, vn, bias_c_t, bias_n_t, subln_col)


B_PAIR = 2 * CHUNK
B_UNION = BAND + CHUNK
B_UNION_BLOCKS = B_UNION // LANES
B_ACC_ROWS = D_B + 16


def _band_body(q_ref, k_ref, v_ref, bias_ref, o_ref, kbf, vte, *, n_grp, n_invalid):
    i = pl.program_id(1)
    n_kblk = k_ref.shape[0] // LANES

    @pl.when(i == 0)
    def _():
        ones_rows = jnp.where(lax.broadcasted_iota(jnp.int32, (B_ACC_ROWS - D_B, LANES), 0) == 0, 1.0, 0.0)
        for jb in range(n_kblk):
            kbf[jb] = k_ref[jb * LANES:(jb + 1) * LANES, :].astype(BF16)
            vt = v_ref[jb * LANES:(jb + 1) * LANES, :].T
            for h in range(H_B):
                vte[jb, h, 0:D_B, :] = vt[h * D_B:(h + 1) * D_B, :].astype(BF16)
                vte[jb, h, D_B:, :] = ones_rows.astype(BF16)

    top = lax.broadcasted_iota(jnp.int32, (LANES, B_PAIR), 0) < D_B
    key_row = lax.broadcasted_iota(jnp.int32, (B_UNION, B_PAIR), 0)

    unroll = next(u for u in (4, 2, 1) if n_grp % u == 0)

    def groups(gt, carry):
        gl = [gt * unroll + k for k in range(unroll)]
        g = [i * n_grp + x for x in gl]
        r0 = [pl.multiple_of(x * B_PAIR, B_PAIR) for x in gl]
        blocks = [[jnp.maximum(x + t - n_invalid // LANES, 0) for t in range(B_UNION_BLOCKS)] for x in g]
        s_pairs = []
        for k in range(unroll):
            qt = (q_ref[pl.ds(r0[k], B_PAIR), :] * (D_B ** -0.5 * LOG2E)).T
            kun = jnp.concatenate([kbf[jb] for jb in blocks[k]], axis=0)
            for p in range(H_B // 2):
                qtp = qt[p * LANES:(p + 1) * LANES, :]
                w = jnp.concatenate([jnp.where(top, qtp, 0.0), jnp.where(top, 0.0, qtp)], axis=1).astype(BF16)
                s_pairs.append(_dot(kun[:, p * LANES:(p + 1) * LANES], w))
        pts = []
        for k in range(unroll):
            for h in range(H_B):
                s = s_pairs[k * (H_B // 2) + h // 2][:, (h % 2) * B_PAIR:(h % 2 + 1) * B_PAIR] + bias_ref[h]
                if n_invalid:
                    s = jnp.where(g[k] * B_PAIR + key_row >= n_invalid, s, NEG_INF)
                pts.append(jnp.exp2(s - jnp.max(s, axis=0, keepdims=True)).astype(BF16))
        accs = [_dot(jnp.concatenate([vte[jb, h] for jb in blocks[k]], axis=1), pts[k * H_B + h])
                for k in range(unroll) for h in range(H_B)]
        for k in range(unroll):
            ot = jnp.concatenate([a[:D_B] / a[D_B:D_B + 1] for a in accs[k * H_B:(k + 1) * H_B]], axis=0)
            o_ref[pl.ds(r0[k], B_PAIR), :] = ot.T
        return carry

    lax.fori_loop(0, n_grp // unroll, groups, 0)


def _band_attn(q, k, v, bias_t, n_invalid):
    b, t, w = q.shape
    assert n_invalid % LANES == 0
    tq = -(-t // B_PAIR) * B_PAIR
    if tq != t:
        q = jnp.pad(q, ((0, 0), (0, tq - t), (0, 0)))
    tk = tq + B_LEFT_CHUNKS * CHUNK - n_invalid
    if k.shape[1] != tk:
        k = jnp.pad(k, ((0, 0), (0, tk - k.shape[1]), (0, 0)))
        v = jnp.pad(v, ((0, 0), (0, tk - v.shape[1]), (0, 0)))
    qb = min(8 * CHUNK, tq)
    n_kblk = tk // LANES
    body = functools.partial(_band_body, n_grp=qb // B_PAIR, n_invalid=n_invalid)
    out = pl.pallas_call(
        body,
        grid=(b, tq // qb),
        in_specs=[pl.BlockSpec((None, qb, w), lambda bi, i: (bi, i, 0)),
                  pl.BlockSpec((None, tk, w), lambda bi, i: (bi, 0, 0)),
                  pl.BlockSpec((None, tk, w), lambda bi, i: (bi, 0, 0)),
                  pl.BlockSpec(bias_t.shape, lambda bi, i: (0, 0, 0))],
        out_specs=pl.BlockSpec((None, qb, w), lambda bi, i: (bi, i, 0)),
        out_shape=jax.ShapeDtypeStruct((b, tq, w), F32),
        scratch_shapes=[pltpu.VMEM((n_kblk, LANES, w), BF16),
                        pltpu.VMEM((n_kblk, H_B, B_ACC_ROWS, LANES), BF16)],
        compiler_params=_params(("parallel", "arbitrary")),
        name="band_attn",
    )(q, k, v, bias_t)
    return out[:, :t] if tq != t else out


_SMALL_DECAY0 = H_C


def _gdn_body(cqkv_ref, small_ref, gate_ref, cprev_ref, s0_ref, convw_ref, alog_ref, dtb_ref, onorm_ref,
              tri_ref, bo_ref, eb_ref, eg_ref, oc_ref, sout_ref, xext, s_scr, *, tc):
    j = pl.program_id(1)
    n_bb = cqkv_ref.shape[0]
    n_ch = tc // CHUNK
    n_pair = H_C // 2
    pad = 8
    lo = pad - (CONV_W - 1)

    @pl.when(j == 0)
    def _():
        z = jnp.zeros((D_C, D_C), F32)
        for bb in range(n_bb):
            xext[bb, lo:pad, :] = cprev_ref[bb]
            for p in range(n_pair):
                s_scr[bb, p] = jnp.concatenate([jnp.concatenate([s0_ref[bb, 2 * p], z], axis=1),
                                                jnp.concatenate([z, s0_ref[bb, 2 * p + 1]], axis=1)], axis=0)

    @pl.when(j > 0)
    def _():
        for bb in range(n_bb):
            xext[bb, lo:pad, :] = xext[bb, tc + lo:tc + pad, :]

    bo = bo_ref[...]
    tri = tri_ref[...]
    lane_s = lax.broadcasted_iota(jnp.int32, (tc, LANES), 1)
    is_dec = (lane_s >= _SMALL_DECAY0) & (lane_s < _SMALL_DECAY0 + H_C)

    def head_sumsq(x):
        x2 = (x * x).astype(BF16)
        return jnp.concatenate([_dot(x2[:, t * LANES:(t + 1) * LANES], bo) for t in range(n_pair)], axis=1)

    qn, kn, gc, bc, xk, vb, qe = [], [], [], [], [], [], []
    for bb in range(n_bb):
        xext[bb, pad:, :] = cqkv_ref[bb]
        y = convw_ref[0:1, :] * xext[bb, lo:lo + tc, :]
        for w in range(1, CONV_W):
            y = y + convw_ref[w:w + 1, :] * xext[bb, lo + w:lo + w + tc, :]
        y = _silu(y)
        yq, yk, yv = y[:, :C_W], y[:, C_W:2 * C_W], y[:, 2 * C_W:]
        qn.append(yq * lax.rsqrt(head_sumsq(yq) + L2_EPS) * (D_C ** -0.5))
        kn.append(yk * lax.rsqrt(head_sumsq(yk) + L2_EPS))
        sm = small_ref[bb]
        log_a = jnp.where(is_dec, -jnp.exp(alog_ref[...]) * _softplus(sm + dtb_ref[...]), 0.0)
        g_full = sum(_dot(tri, part) for part in _split3(log_a))
        gc.append(sum(_dot(part, eg_ref[...]) for part in _split3(g_full)))
        bc.append(sum(_dot(part, eb_ref[...]) for part in _split3(_sigmoid(sm))))
        egc = jnp.exp(gc[bb])
        xk.append(bc[bb] * egc * kn[bb])
        vb.append(bc[bb] * yv)
        qe.append(egc * qn[bb])

    ii = lax.broadcasted_iota(jnp.int32, (CHUNK, LANES), 0)
    ln = lax.broadcasted_iota(jnp.int32, (CHUNK, LANES), 1)
    jn = ln % D_C
    incl2 = ii >= jn
    strict2 = ii > jn
    eye2 = ii == jn
    left = ln < D_C
    eye2f = jnp.where(eye2, 1.0, 0.0)

    def same_block(n):
        return (ii // n) == (jn // n)
    r128 = lax.broadcasted_iota(jnp.int32, (LANES, LANES), 0)
    c128 = lax.broadcasted_iota(jnp.int32, (LANES, LANES), 1)
    on_diag_blocks = (r128 < D_C) == (c128 < D_C)
    eye128 = jnp.where(r128 == c128, 1.0, 0.0).astype(BF16)

    keep_left = jnp.where(left, 1.0, 0.0).astype(BF16)
    keep_right = jnp.where(left, 0.0, 1.0).astype(BF16)

    def bdiag(x):
        xb = x.astype(BF16)
        return jnp.concatenate([xb * keep_left, xb * keep_right], axis=0)

    units = [(bb, r, p) for bb in range(n_bb) for r in range(n_ch) for p in range(n_pair)]
    lanes_bp = [(bb, p) for bb in range(n_bb) for p in range(n_pair)]

    def rows_of(r):
        return slice(r * CHUNK, (r + 1) * CHUNK)

    def tile_of(p):
        return slice(p * LANES, (p + 1) * LANES)

    def sub(x, u):
        bb, r, p = u
        return x[bb][rows_of(r), tile_of(p)]

    g_last = {(bb, r): gc[bb][(r + 1) * CHUNK - 1:(r + 1) * CHUNK, :] for bb in range(n_bb) for r in range(n_ch)}
    kd = {br: kn[br[0]][rows_of(br[1])] * jnp.exp(g - gc[br[0]][rows_of(br[1])]) for br, g in g_last.items()}
    eg_last = {br: jnp.exp(g) for br, g in g_last.items()}
    dec, r_k, kdt = {}, {}, {}
    for u in units:
        gcp = sub(gc, u)
        g_row = jnp.sum(jnp.where(eye2, gcp, 0.0), axis=0, keepdims=True)
        dec[u] = jnp.where(incl2, jnp.exp(jnp.where(incl2, gcp - g_row, 0.0)), 0.0)
        kp = sub(kn, u)
        r_k[u] = _dot_nt(jnp.concatenate([sub(qn, u), kp], axis=0).astype(BF16), bdiag(kp).astype(BF16))
    for bb, r, p in units:
        kdt[bb, r, p] = _dot_nt(eye128, bdiag(kd[bb, r][:, tile_of(p)]).astype(BF16)).astype(BF16)
    a_mat = {u: sub(bc, u) * jnp.where(strict2, dec[u], 0.0) * r_k[u][CHUNK:] for u in units}
    leaf = GDN_LEAF
    n_leaf = {u: jnp.where(strict2 & same_block(leaf), -a_mat[u], 0.0) for u in units}
    pw = {u: _dot(n_leaf[u].astype(BF16), bdiag(n_leaf[u])) for u in units}
    tm_ = {u: eye2f + n_leaf[u] for u in units}
    st = {u: _dot(jnp.concatenate([tm_[u], pw[u]], axis=0).astype(BF16), bdiag(pw[u])) for u in units}
    tm_ = {u: tm_[u] + st[u][:CHUNK] for u in units}
    fin = {u: _dot(tm_[u].astype(BF16), bdiag(st[u][CHUNK:])) for u in units}
    tm_ = {u: tm_[u] + fin[u] for u in units}
    size = leaf
    while size < CHUNK:
        coupling = strict2 & same_block(2 * size) & jnp.logical_not(same_block(size))
        cd = {u: _dot(jnp.where(coupling, a_mat[u], 0.0).astype(BF16), bdiag(tm_[u])) for u in units}
        dcd = {u: _dot(tm_[u].astype(BF16), bdiag(cd[u])) for u in units}
        tm_ = {u: tm_[u] - dcd[u] for u in units}
        size *= 2
    wu = {u: _dot(tm_[u].astype(BF16),
                  jnp.concatenate([bdiag(sub(xk, u)), bdiag(sub(vb, u))], axis=1).astype(BF16)) for u in units}
    w_b = {u: wu[u][:, :LANES].astype(BF16) for u in units}
    qkd = {u: (dec[u] * r_k[u][:CHUNK]).astype(BF16) for u in units}

    states = {bp: s_scr[bp[0], bp[1]] for bp in lanes_bp}
    o_tiles = {}
    for r in range(n_ch):
        res = {(bb, p): _dot(jnp.concatenate([sub(qe, (bb, r, p)).astype(BF16), w_b[bb, r, p]], axis=0),
                             states[bb, p].astype(BF16)) for bb, p in lanes_bp}
        u = {(bb, p): wu[bb, r, p][:, LANES:] - res[bb, p][CHUNK:] for bb, p in lanes_bp}
        upd = {(bb, p): _dot(kdt[bb, r, p], jnp.concatenate([u[bb, p], u[bb, p]], axis=0).astype(BF16))
               for bb, p in lanes_bp}
        o_inner = {(bb, p): _dot(qkd[bb, r, p], bdiag(u[bb, p]).astype(BF16)) for bb, p in lanes_bp}
        states = {(bb, p): eg_last[bb, r][:, tile_of(p)] * states[bb, p] + jnp.where(on_diag_blocks, upd[bb, p], 0.0)
                  for bb, p in lanes_bp}
        for bb, p in lanes_bp:
            o_tiles[bb, r, p] = res[bb, p][:CHUNK] + o_inner[bb, p]
    for bb, p in lanes_bp:
        s_scr[bb, p] = states[bb, p]
    for bb in range(n_bb):
        o = jnp.concatenate([jnp.concatenate([o_tiles[bb, r, p] for p in range(n_pair)], axis=1)
                             for r in range(n_ch)], axis=0)
        ms = head_sumsq(o) * (1.0 / D_C)
        oc_ref[bb] = o * lax.rsqrt(ms + NORM_EPS) * onorm_ref[...] * _silu(gate_ref[bb])

    @pl.when(j == pl.num_programs(1) - 1)
    def _():
        for bb, p in lanes_bp:
            sout_ref[bb, 2 * p] = states[bb, p][:D_C, :D_C]
            sout_ref[bb, 2 * p + 1] = states[bb, p][D_C:, D_C:]


def _gdn_constants(tc):
    i = jnp.arange(tc, dtype=jnp.int32)
    tri = ((i[None, :] <= i[:, None]) & (i[None, :] // CHUNK == i[:, None] // CHUNK)).astype(BF16)
    l = jnp.arange(LANES, dtype=jnp.int32)
    block_ones = (l[:, None] // D_C == l[None, :] // D_C).astype(BF16)
    col_head = jnp.arange(C_W, dtype=jnp.int32)[None, :] // D_C
    e_beta = (l[:, None] == col_head).astype(BF16)
    e_g = (l[:, None] == col_head + _SMALL_DECAY0).astype(BF16)
    return tri, block_ones, e_beta, e_g


def _gdn(cqkv, small, gate, conv_prev, s0, conv_w, alog_vec, dtb_vec, onorm_tiled):
    b, t, w3 = cqkv.shape
    tc = min(4 * CHUNK, t)
    nb = GDN_BATCH_BLOCK if b % GDN_BATCH_BLOCK == 0 else 1
    consts = _gdn_constants(tc)
    body = functools.partial(_gdn_body, tc=tc)
    blk = lambda bi, j: (bi, j, 0)
    per_b3 = lambda bi, j: (bi, 0, 0)
    per_b4 = lambda bi, j: (bi, 0, 0, 0)
    const2 = lambda bi, j: (0, 0)
    return pl.pallas_call(
        body,
        grid=(b // nb, t // tc),
        in_specs=[pl.BlockSpec((nb, tc, w3), blk),
                  pl.BlockSpec((nb, tc, LANES), blk),
                  pl.BlockSpec((nb, tc, C_W), blk),
                  pl.BlockSpec((nb, CONV_W - 1, w3), per_b3),
                  pl.BlockSpec((nb, H_C, D_C, D_C), per_b4),
                  pl.BlockSpec(conv_w.shape, const2),
                  pl.BlockSpec((1, LANES), const2),
                  pl.BlockSpec((1, LANES), const2),
                  pl.BlockSpec((1, C_W), const2)]
                 + [pl.BlockSpec(c.shape, const2) for c in consts],
        out_specs=[pl.BlockSpec((nb, tc, C_W), blk),
                   pl.BlockSpec((nb, H_C, D_C, D_C), per_b4)],
        out_shape=[jax.ShapeDtypeStruct((b, t, C_W), F32),
                   jax.ShapeDtypeStruct((b, H_C, D_C, D_C), F32)],
        scratch_shapes=[pltpu.VMEM((nb, tc + 8, w3), F32), pltpu.VMEM((nb, H_C // 2, LANES, LANES), F32)],
        compiler_params=_params(("parallel", "arbitrary")),
        name="gdn",
    )(cqkv, small, gate, conv_prev, s0, conv_w, alog_vec, dtb_vec, onorm_tiled, *consts)


def _top2_gates(logits):
    lane = lax.broadcasted_iota(jnp.int32, logits.shape, 1).astype(F32)
    low = -3.0e38
    lg = jnp.where(lane < N_EXPERTS, logits, low)
    m1 = jnp.max(lg, axis=-1, keepdims=True)
    i1 = jnp.min(jnp.where(lg == m1, lane, float(LANES)), axis=-1, keepdims=True)
    lg2 = jnp.where(lane == i1, low, lg)
    m2 = jnp.max(lg2, axis=-1, keepdims=True)
    i2 = jnp.min(jnp.where(lg2 == m2, lane, float(LANES)), axis=-1, keepdims=True)
    e2 = jnp.exp(m2 - m1)
    den = 1.0 + e2
    return jnp.where(lane == i1, 1.0 / den, 0.0) + jnp.where(lane == i2, e2 / den, 0.0)


def _outproj_body(x_ref, oa_ref, ob_ref, oc_ref, w_ref, g_ref, *rest, with_router):
    if with_router:
        rhi_ref, rlo_ref, xo_ref, h_ref, gates_ref = rest
    else:
        xo_ref, h_ref = rest
    y = (_dot(oa_ref[...].astype(BF16), w_ref[0:HEAD_W, :])
         + _dot(ob_ref[...].astype(BF16), w_ref[HEAD_W:2 * HEAD_W, :])
         + _dot(oc_ref[...].astype(BF16), w_ref[2 * HEAD_W:, :]))
    x = x_ref[...] + y
    xo_ref[...] = x
    hf = _rms(x, g_ref[...])
    hb = hf.astype(BF16)
    h_ref[...] = hb
    if with_router:
        lo = (hf - hb.astype(F32)).astype(BF16)
        logits = _dot(hb, rhi_ref[...]) + _dot(lo, rhi_ref[...]) + _dot(hb, rlo_ref[...])
        gates_ref[...] = _top2_gates(logits)


def _outproj(x, oa, ob, oc, w_out, g, router=None):
    n, d = x.shape
    tm = min(ROW_TILE, n)
    row = lambda i: (i, 0)
    const = lambda i: (0, 0)
    in_specs = [pl.BlockSpec((tm, d), row), pl.BlockSpec((tm, HEAD_W), row), pl.BlockSpec((tm, HEAD_W), row),
                pl.BlockSpec((tm, C_W), row), pl.BlockSpec(w_out.shape, const), pl.BlockSpec((1, d), const)]
    out_specs = [pl.BlockSpec((tm, d), row), pl.BlockSpec((tm, d), row)]
    out_shape = [jax.ShapeDtypeStruct((n, d), F32), jax.ShapeDtypeStruct((n, d), BF16)]
    args = [x, oa, ob, oc, w_out, g]
    if router is not None:
        in_specs += [pl.BlockSpec(router[0].shape, const), pl.BlockSpec(router[1].shape, const)]
        out_specs.append(pl.BlockSpec((tm, LANES), row))
        out_shape.append(jax.ShapeDtypeStruct((n, LANES), F32))
        args += list(router)
    return pl.pallas_call(
        functools.partial(_outproj_body, with_router=router is not None),
        grid=(n // tm,),
        in_specs=in_specs, out_specs=out_specs, out_shape=out_shape,
        compiler_params=_params(("parallel",)),
        name="outproj",
    )(*args)


def _ffn_body(h_ref, x_ref, wg_ref, wu_ref, wd_ref, o_ref, acc_ref, *, ff_chunk):
    h = h_ref[...]
    d_ff = wg_ref.shape[1]
    acc_ref[...] = x_ref[...]
    for c0 in range(0, d_ff, ff_chunk):
        a = _silu(_dot(h, wg_ref[:, c0:c0 + ff_chunk])) * _dot(h, wu_ref[:, c0:c0 + ff_chunk])
        acc_ref[...] += _dot(a.astype(BF16), wd_ref[c0:c0 + ff_chunk, :])
    o_ref[...] = acc_ref[...]


def _ffn_dense(h, x, wg, wu, wd):
    n, d = x.shape
    d_ff = wg.shape[1]
    tm = min(ROW_TILE, n)
    ff_chunk = 256 if d_ff % 256 == 0 else LANES
    row = lambda i: (i, 0)
    const = lambda i: (0, 0)
    return pl.pallas_call(
        functools.partial(_ffn_body, ff_chunk=ff_chunk),
        grid=(n // tm,),
        in_specs=[pl.BlockSpec((tm, d), row), pl.BlockSpec((tm, d), row),
                  pl.BlockSpec(wg.shape, const), pl.BlockSpec(wu.shape, const), pl.BlockSpec(wd.shape, const)],
        out_specs=pl.BlockSpec((tm, d), row),
        out_shape=jax.ShapeDtypeStruct((n, d), F32),
        scratch_shapes=[pltpu.VMEM((tm, d), F32)],
        compiler_params=_params(("parallel",)),
        name="ffn_dense",
    )(h, x, wg, wu, wd)


def _moe_body(pre_ref, h_ref, gates_ref, gates_t_ref, wg_ref, wu_ref, wd_ref, *rest, tg, n_f, final_norm, tiles):
    if final_norm:
        x_ref, gfin_ref, y_ref, rank_col, rank_row, hs, eo = rest
    else:
        y_ref, rank_col, rank_row, hs, eo = rest
    g = pl.program_id(0)
    e = pl.program_id(1)
    f = pl.program_id(2)
    tb = MOE_TOKEN_BLOCK
    n_tb = tg // tb
    pre0 = (g * N_EXPERTS + e) * (n_tb + 1)
    unit = tiles[-1]
    n_units = (pre_ref[pre0 + n_tb] + unit - 1) // unit
    units_full = tiles[0] // unit
    n_full = n_units // units_full
    rem = n_units % units_full

    def for_tiles(fn):
        def full(t, carry):
            fn(pl.multiple_of(t * tiles[0], tiles[0]), tiles[0])
            return carry

        lax.fori_loop(0, n_full, full, 0)
        for tm in tiles[1:]:
            bit = tm // unit
            higher = (units_full - 1) & ~(2 * bit - 1)

            @pl.when((rem & bit) != 0)
            def _(tm=tm, higher=higher):
                fn(pl.multiple_of(n_full * tiles[0] + (rem & higher) * unit, unit), tm)

    @pl.when((e == 0) & (f == 0))
    def _():
        y_ref[...] = jnp.zeros_like(y_ref)
        ii = lax.broadcasted_iota(jnp.int32, (tb, tb), 0)
        jj = lax.broadcasted_iota(jnp.int32, (tb, tb), 1)
        lower = jnp.where(ii > jj, 1.0, 0.0).astype(BF16)
        upper = jnp.where(ii < jj, 1.0, 0.0).astype(BF16)
        carry_c = jnp.zeros((1, LANES), F32)
        carry_r = jnp.zeros((N_EXPERTS, 1), F32)
        for b in range(n_tb):
            mc = jnp.where(gates_ref[b * tb:(b + 1) * tb, :] > 0.0, 1.0, 0.0)
            rank_col[b * tb:(b + 1) * tb, :] = _dot(lower, mc.astype(BF16)) + carry_c
            carry_c = carry_c + jnp.sum(mc, axis=0, keepdims=True)
            mr = jnp.where(gates_t_ref[:, b * tb:(b + 1) * tb] > 0.0, 1.0, 0.0)
            rank_row[:, b * tb:(b + 1) * tb] = _dot(mr.astype(BF16), upper) + carry_r
            carry_r = carry_r + jnp.sum(mr, axis=1, keepdims=True)

    win = MOE_WINDOW

    def all_block_windows(align, product, apply):
        base = [(pre_ref[pre0 + b] // align) * align for b in range(n_tb)]
        n_win = [(pre_ref[pre0 + b + 1] - base[b] + win - 1) // win for b in range(n_tb)]
        first = [product(b, pl.multiple_of(base[b], align)) for b in range(n_tb)]
        for b in range(n_tb):
            apply(b, pl.multiple_of(base[b], align), first[b])
        for b in range(n_tb):
            def body(k, carry, b=b):
                r0 = pl.multiple_of(base[b] + k * win, align)
                apply(b, r0, product(b, r0))
                return carry

            lax.fori_loop(1, n_win[b], body, 0)

    @pl.when(f == 0)
    def _():
        def clear(k, carry):
            r0 = pl.multiple_of(k * unit, unit)
            hs[pl.ds(r0, unit), :] = jnp.zeros((unit, hs.shape[1]), BF16)
            return carry

        lax.fori_loop(0, n_units + MOE_SLACK // unit, clear, 0)
        tail = pl.multiple_of(n_units * unit, unit)
        eo[pl.ds(tail, MOE_SLACK), :] = jnp.zeros((MOE_SLACK, eo.shape[1]), F32)

        def gathered(b, r0):
            cols = slice(b * tb, (b + 1) * tb)
            gr = gates_t_ref[pl.ds(e, 1), cols]
            rr = jnp.where(gr > 0.0, rank_row[pl.ds(e, 1), cols], -1.0)
            rows = (r0 + lax.broadcasted_iota(jnp.int32, (win, tb), 0)).astype(F32)
            return _dot(jnp.where(rr == rows, 1.0, 0.0).astype(BF16), h_ref[cols, :])

        def add_rows(b, r0, got):
            hs[pl.ds(r0, win), :] = (hs[pl.ds(r0, win), :].astype(F32) + got).astype(BF16)

        all_block_windows(MOE_BF16_ROWS, gathered, add_rows)

    def expert(r0, tm):
        x = hs[pl.ds(r0, tm), :]
        a = _silu(_dot(x, wg_ref[...])) * _dot(x, wu_ref[...])
        part = _dot(a.astype(BF16), wd_ref[...])

        @pl.when(f == 0)
        def _():
            eo[pl.ds(r0, tm), :] = part

        @pl.when(f > 0)
        def _():
            eo[pl.ds(r0, tm), :] += part

    for_tiles(expert)

    @pl.when(f == n_f - 1)
    def _():
        lane = lax.broadcasted_iota(jnp.int32, (tb, LANES), 1)

        def scattered(b, r0):
            rows = slice(b * tb, (b + 1) * tb)
            gc = jnp.sum(jnp.where(lane == e, gates_ref[rows, :], 0.0), axis=-1, keepdims=True)
            rc = jnp.sum(jnp.where(lane == e, rank_col[rows, :], 0.0), axis=-1, keepdims=True)
            rc = jnp.where(gc > 0.0, rc, -1.0)
            cols = (r0 + lax.broadcasted_iota(jnp.int32, (tb, win), 1)).astype(F32)
            return _dot(jnp.where(rc == cols, gc, 0.0).astype(BF16), eo[pl.ds(r0, win), :].astype(BF16))

        def add_tokens(b, r0, got):
            y_ref[b * tb:(b + 1) * tb, :] += got

        all_block_windows(MOE_F32_ROWS, scattered, add_tokens)

    if final_norm:
        @pl.when((e == pl.num_programs(1) - 1) & (f == n_f - 1))
        def _():
            for b in range(n_tb):
                rows = slice(b * tb, (b + 1) * tb)
                y_ref[rows, :] = _rms(x_ref[rows, :] + y_ref[rows, :], gfin_ref[...])


def _moe_prefix(gates, tg):
    n = gates.shape[0]
    n_tb = tg // MOE_TOKEN_BLOCK
    routed = (gates[:, :N_EXPERTS] > 0.0).astype(jnp.int32)
    per_block = routed.reshape(n // tg, n_tb, MOE_TOKEN_BLOCK, N_EXPERTS).sum(axis=2)
    run = jnp.cumsum(per_block, axis=1)
    pre = jnp.concatenate([jnp.zeros_like(run[:, :1]), run], axis=1)
    return jnp.transpose(pre, (0, 2, 1)).reshape(-1)


def _moe(h, gates, wg, wu, wd, tg, x=None, g_final=None):
    n, d = h.shape
    n_e, _, d_ff = wg.shape
    n_f = 2
    ffh = d_ff // n_f
    final_norm = x is not None
    tiles = tuple(t for t in MOE_TILES if t <= tg)
    body = functools.partial(_moe_body, tg=tg, n_f=n_f, final_norm=final_norm, tiles=tiles)
    once = pl.Buffered(1)
    extra_specs, extra_args = [], []
    if final_norm:
        extra_specs = [pl.BlockSpec((tg, d), lambda g, e, f, c: (g, 0), pipeline_mode=once),
                       pl.BlockSpec((1, d), lambda g, e, f, c: (0, 0))]
        extra_args = [x, g_final]
    grid_spec = pltpu.PrefetchScalarGridSpec(
        num_scalar_prefetch=1,
        grid=(n // tg, n_e, n_f),
        in_specs=[pl.BlockSpec((tg, d), lambda g, e, f, c: (g, 0), pipeline_mode=once),
                  pl.BlockSpec((tg, LANES), lambda g, e, f, c: (g, 0), pipeline_mode=once),
                  pl.BlockSpec((N_EXPERTS, tg), lambda g, e, f, c: (0, g), pipeline_mode=once),
                  pl.BlockSpec((None, d, ffh), lambda g, e, f, c: (e, 0, f)),
                  pl.BlockSpec((None, d, ffh), lambda g, e, f, c: (e, 0, f)),
                  pl.BlockSpec((None, ffh, d), lambda g, e, f, c: (e, f, 0))] + extra_specs,
        out_specs=pl.BlockSpec((tg, d), lambda g, e, f, c: (g, 0), pipeline_mode=once),
        scratch_shapes=[pltpu.VMEM((tg, LANES), F32), pltpu.VMEM((N_EXPERTS, tg), F32),
                        pltpu.VMEM((tg + MOE_SLACK, d), BF16), pltpu.VMEM((tg + MOE_SLACK, d), F32)],
    )
    return pl.pallas_call(
        body,
        grid_spec=grid_spec,
        out_shape=jax.ShapeDtypeStruct((n, d), F32),
        compiler_params=_params(("parallel", "arbitrary", "arbitrary")),
        name="moe",
    )(_moe_prefix(gates, tg), h, gates, gates[:, :N_EXPERTS].T, wg, wu, wd, *extra_args)


def _final_body(x_ref, y_ref, g_ref, o_ref):
    o_ref[...] = _rms(x_ref[...] + y_ref[...], g_ref[...])


def _final_norm(x, y, g):
    n, d = x.shape
    tm = min(ROW_TILE, n)
    row = lambda i: (i, 0)
    return pl.pallas_call(
        _final_body,
        grid=(n // tm,),
        in_specs=[pl.BlockSpec((tm, d), row), pl.BlockSpec((tm, d), row), pl.BlockSpec((1, d), lambda i: (0, 0))],
        out_specs=pl.BlockSpec((tm, d), row),
        out_shape=jax.ShapeDtypeStruct((n, d), F32),
        compiler_params=_params(("parallel",)),
        name="final_norm",
    )(x, y, g)


def _t5_bucket(rel):
    nb = T5_BUCKETS // 2
    max_exact = nb // 2
    ret = jnp.where(rel > 0, nb, 0)
    n = jnp.abs(rel)
    nf = jnp.maximum(n, 1).astype(F32)
    large = max_exact + (jnp.log(nf / max_exact) / math.log(T5_MAX_DIST / max_exact) * (nb - max_exact)).astype(jnp.int32)
    large = jnp.minimum(large, nb - 1)
    return ret + jnp.where(n < max_exact, n, large)


def _t5_table(t5_bias, q_pos, k_pos):
    bias = _lookup(t5_bias, _t5_bucket(k_pos[None, :] - q_pos[:, None]))
    mask = (k_pos[None, :] // CHUNK) <= (q_pos[:, None] // CHUNK)
    return bias, mask[None]


def _lookup(table, idx):
    onehot = jax.nn.one_hot(idx, table.shape[0], dtype=F32)
    return jnp.einsum("qkn,nh->hqk", onehot, table.astype(F32), precision=lax.Precision.HIGHEST)


def _band_table(rel_bias):
    qi = jnp.arange(CHUNK, dtype=jnp.int32)
    kj = jnp.arange(BAND, dtype=jnp.int32) - B_LEFT_CHUNKS * CHUNK
    rel = jnp.clip(kj[None, :] - qi[:, None], -B_REL_CLIP, B_REL_CLIP) + B_REL_CLIP
    base = jnp.swapaxes(_lookup(rel_bias, rel), 1, 2) * LOG2E
    halves = [jnp.pad(base, ((0, 0), (c * CHUNK, (1 - c) * CHUNK), (0, 0)), constant_values=NEG_INF)
              for c in range(2)]
    return jnp.concatenate(halves, axis=2)


def _lane_vec(v, offset):
    return jnp.zeros((1, LANES), F32).at[0, offset:offset + v.shape[0]].set(v.astype(F32))


def _moe_group(n):
    for tg in (2048, 1024, 512, 256):
        if n % tg == 0:
            return tg
    raise ValueError(f"token count {n} is not a multiple of 256")


def kernel(x_prompt, x_sample, cache_a_k, cache_a_v, cache_b_k, cache_b_v, cache_c_conv, state_c_ssm, w_in, w_out, norm_mix, norm_ffn, norm_final, lam_qk, subln_a, t5_bias, rel_bias_b, conv_c, a_log_c, dt_bias_c, onorm_c, ffn_gate, ffn_up, ffn_down, moe_router, moe_gate, moe_up, moe_down):
    depth = w_in.shape[0]
    bp, sp, d = x_prompt.shape
    bs, ss, _ = x_sample.shape
    past = cache_a_k.shape[2]
    nb_cache = cache_b_k.shape[2]
    assert ss == CHUNK and nb_cache == B_LEFT_CHUNKS * CHUNK and sp >= B_LEFT_CHUNKS * CHUNK

    xp = x_prompt.reshape(bp * sp, d)
    xs = x_sample.reshape(bs * ss, d)
    blk = min(A_BLOCK, sp)

    pos_blk = jnp.arange(blk, dtype=jnp.int32)
    far_bias = t5_bias[_t5_bucket(jnp.int32(-(blk + 1)))].astype(F32)
    bd, md = _t5_table(t5_bias, pos_blk, pos_blk)
    bias_diag = jnp.where(md, bd - far_bias[:, None, None], NEG_INF)
    bsub, _ = _t5_table(t5_bias, blk + pos_blk, pos_blk)
    bias_sub = bsub - far_bias[:, None, None]
    q_pos_s = past + jnp.arange(ss, dtype=jnp.int32)
    bfull, mfull = _t5_table(t5_bias, q_pos_s, jnp.arange(past + ss, dtype=jnp.int32))
    bias_s = jnp.where(mfull, bfull, NEG_INF)
    bias_s_t = jnp.swapaxes(bias_s, 1, 2) * LOG2E
    bias_s_cache = bias_s_t[:, :past]
    bias_s_new = jnp.pad(bias_s_t[:, past:], ((0, 0), (0, LANES - ss), (0, 0)), constant_values=NEG_INF)

    feature_major = lambda c: jnp.transpose(c, (0, 1, 3, 4, 2)).reshape(depth, bs, HEAD_W, past)
    cache_a_kt, cache_a_vt = feature_major(cache_a_k), feature_major(cache_a_v)

    zeros_conv = jnp.zeros((bp, CONV_W - 1, 3 * C_W), F32)
    zeros_state = jnp.zeros((bp, H_C, D_C, D_C), F32)

    p_states = [[] for _ in range(6)]
    s_states = [[] for _ in range(6)]
    yp = ys = None
    g_fin = norm_final.reshape(1, d)
    finals = []
    for l in range(depth):
        if yp is not None:
            xp, xs, yp, ys = xp + yp, xs + ys, None, None
        w = w_in[l]
        n_main = 6 * HEAD_W + 3 * C_W
        w_main = jnp.concatenate([w[:, :n_main], w[:, n_main + 2 * H_C:]], axis=1).astype(BF16)
        w_small = jnp.zeros((d, LANES), F32).at[:, :2 * H_C].set(w[:, n_main:n_main + 2 * H_C]).astype(BF16)
        w_out_l = w_out[l].astype(BF16)
        g_mix = norm_mix[l].reshape(1, d)
        g_ffn = norm_ffn[l].reshape(1, d)
        lam_init = 0.8 - 0.6 * math.exp(-0.3 * l)
        lq = lam_qk[l].astype(F32)
        lam = (jnp.exp(jnp.sum(lq[0] * lq[1])) - jnp.exp(jnp.sum(lq[2] * lq[3])) + lam_init).reshape(1)
        subln = subln_a[l].reshape(1, DV_A)
        band_bias = _band_table(rel_bias_b[l])
        alog_vec = _lane_vec(a_log_c[l], _SMALL_DECAY0)
        dtb_vec = _lane_vec(dt_bias_c[l], _SMALL_DECAY0)
        onorm = jnp.tile(onorm_c[l].astype(F32), H_C).reshape(1, C_W)
        is_moe = l % 2 == 1
        if is_moe:
            r = jnp.zeros((d, LANES), F32).at[:, :N_EXPERTS].set(moe_router[l // 2])
            r_hi = r.astype(BF16)
            router = (r_hi, (r - r_hi.astype(F32)).astype(BF16))
            e_wg, e_wu, e_wd = (moe_gate[l // 2].astype(BF16), moe_up[l // 2].astype(BF16),
                                moe_down[l // 2].astype(BF16))
        else:
            router = None
            f_wg, f_wu, f_wd = (ffn_gate[l // 2].astype(BF16), ffn_up[l // 2].astype(BF16),
                                ffn_down[l // 2].astype(BF16))

        new_x = []
        for is_prompt, x in ((True, xp), (False, xs)):
            b, t = (bp, sp) if is_prompt else (bs, ss)
            aq, ak, av, bq, bk, bv, cqkv, cgate, csmall = _inproj(x, g_mix, w_main, w_small)
            r3 = lambda a: a.reshape(b, t, a.shape[-1])
            if is_prompt:
                oa = _attn_a_prompt(lam, r3(aq), r3(ak), r3(av), jnp.swapaxes(bias_diag, 1, 2) * LOG2E,
                                    jnp.swapaxes(bias_sub, 1, 2) * LOG2E, subln.reshape(DV_A, 1), 1.0 - lam_init)
                ob = _band_attn(r3(bq), r3(bk), r3(bv), band_bias, B_LEFT_CHUNKS * CHUNK)
                conv_prev, s0 = zeros_conv, zeros_state
            else:
                oa = _attn_a_sample(lam, r3(aq), cache_a_kt, cache_a_vt, l, r3(ak), r3(av),
                                    bias_s_cache, bias_s_new, subln.reshape(DV_A, 1), 1.0 - lam_init)
                kb = jnp.concatenate([cache_b_k[l].reshape(b, nb_cache, HEAD_W), r3(bk)], axis=1)
                vb = jnp.concatenate([cache_b_v[l].reshape(b, nb_cache, HEAD_W), r3(bv)], axis=1)
                ob = _band_attn(r3(bq), kb, vb, band_bias, 0)
                conv_prev, s0 = cache_c_conv[l], state_c_ssm[l]
            oc, s_new = _gdn(r3(cqkv), r3(csmall), r3(cgate), conv_prev, s0, conv_c[l], alog_vec, dtb_vec, onorm)
            n = b * t
            res = _outproj(x, oa.reshape(n, HEAD_W), ob.reshape(n, HEAD_W), oc.reshape(n, C_W), w_out_l, g_ffn,
                           router)
            if is_moe:
                x_new, h2, gates = res
                if l == depth - 1:
                    finals.append(_moe(h2, gates, e_wg, e_wu, e_wd, _moe_group(n), x=x_new, g_final=g_fin))
                    new_x.append((x_new, None))
                else:
                    new_x.append((x_new, _moe(h2, gates, e_wg, e_wu, e_wd, _moe_group(n))))
            else:
                x_new, h2 = res
                new_x.append((_ffn_dense(h2, x_new, f_wg, f_wu, f_wd), None))
            keep = min(B_LEFT_CHUNKS * CHUNK, t)
            states = (ak.reshape(b, t, H_A, 2 * DQK_A), av.reshape(b, t, H_A, DV_A),
                      bk.reshape(b, t, H_B, D_B)[:, t - keep:], bv.reshape(b, t, H_B, D_B)[:, t - keep:],
                      r3(cqkv)[:, t - (CONV_W - 1):], s_new)
            for i in range(6):
                (p_states if is_prompt else s_states)[i].append(states[i])
        (xp, yp), (xs, ys) = new_x

    if finals:
        y_prompt, y_sample = finals[0].reshape(bp, sp, d), finals[1].reshape(bs, ss, d)
    else:
        if yp is None:
            yp, ys = jnp.zeros_like(xp), jnp.zeros_like(xs)
        y_prompt = _final_norm(xp, yp, g_fin).reshape(bp, sp, d)
        y_sample = _final_norm(xs, ys, g_fin).reshape(bs, ss, d)
    p_out = [jnp.stack(s, axis=0) for s in p_states]
    s_out = [jnp.stack(s, axis=0) for s in s_states]
    return (y_prompt, y_sample, *p_out, *s_out)
```
